```python
import jax, jax.numpy as jnp
from jax import lax
import numpy as np

D_MODEL = 1024
BATCH = 8
SEQ = 2048
DEPTH = 1
DEC_BATCH = 32
DEC_SEQ = 1
PAST_LEN = 8192
PAGE_SIZE = 128

ATT_GROUPS = ((128, 1), (512, 4), (2048, 16))
N_GROUPS = 3
ATT_HEADS = 4
ATT_HEAD_DIM = 128
ATT_WIDTH = ATT_HEADS * ATT_HEAD_DIM
ATT_BLOCK = 128
HG_HEADS = 4
HG_KEY_DIM = 128
HG_VAL_DIM = 128
HG_KW = HG_HEADS * HG_KEY_DIM
HG_VW = HG_HEADS * HG_VAL_DIM
HG_CHUNK = 64
MEM_LEN = 256
MEM_HEADS = 4
MEM_HEAD_DIM = 128
MEM_WIDTH = MEM_HEADS * MEM_HEAD_DIM
N_BRANCH = 3
D_FF = ((8 * D_MODEL // 3 + 127) // 128) * 128
EPS = 1e-6
IN_WIDTH = 3 * N_GROUPS * ATT_WIDTH + 2 * HG_KW + HG_VW + MEM_WIDTH + N_BRANCH * D_MODEL

kernel_name = 'hybrid_dilated_hgrn2_memory_decoder_step'


def rmsnorm(x, g):
    xf = x.astype(jnp.float32)
    y = xf * lax.rsqrt(jnp.mean(xf * xf, axis=-1, keepdims=True) + EPS)
    return (y * g.astype(jnp.float32)).astype(x.dtype)


def half_ffn(x, g, w_gate, w_up, w_down):
    h = rmsnorm(x, g)
    return (jax.nn.silu(h @ w_gate) * (h @ w_up)) @ w_down


def split_in(z):
    sizes = (N_GROUPS * ATT_WIDTH,) * 3 + (HG_KW, HG_KW, HG_VW, MEM_WIDTH, N_BRANCH * D_MODEL)
    idx, acc = [], 0
    for s in sizes[:-1]:
        acc += s
        idx.append(acc)
    return jnp.split(z, idx, axis=-1)


def att_heads(a):
    B, T, _ = a.shape
    return a.reshape(B, T, N_GROUPS, ATT_HEADS, ATT_HEAD_DIM)


def mem_heads(a):
    B, T, _ = a.shape
    return a.reshape(B, T, MEM_HEADS, MEM_HEAD_DIM)


def dilated_group_prompt(q, k, v, window, dil):
    B, T, H, Dh = q.shape
    span = window // dil
    L = T // dil
    Lp = -(-L // ATT_BLOCK) * ATT_BLOCK
    nb = Lp // ATT_BLOCK
    N = B * dil

    def to_blocks(a):
        a = a.reshape(B, L, dil, H, Dh).transpose(0, 2, 1, 3, 4).reshape(N, L, H, Dh)
        a = jnp.pad(a, ((0, 0), (0, Lp - L), (0, 0), (0, 0)))
        return a.reshape(N, nb, ATT_BLOCK, H, Dh)

    def with_prev(a):
        prev = jnp.pad(a[:, :-1], ((0, 0), (1, 0), (0, 0), (0, 0), (0, 0)))
        return jnp.concatenate([prev, a], axis=2)

    qb = to_blocks(q)
    kk = with_prev(to_blocks(k))
    vv = with_prev(to_blocks(v))
    s = jnp.einsum('nbqhd,nbkhd->nbhqk', qb, kk, preferred_element_type=jnp.float32) * (Dh ** -0.5)
    qi = np.arange(ATT_BLOCK)[:, None]
    ki = np.arange(2 * ATT_BLOCK)[None, :]
    dist = ATT_BLOCK + qi - ki
    band = (dist >= 0) & (dist <= span)
    valid = np.where((np.arange(nb) == 0)[:, None, None], band & (ki >= ATT_BLOCK), band)
    s = jnp.where(valid[None, :, None], s, -jnp.inf)
    lse = jax.nn.logsumexp(s, axis=-1)
    p = jnp.exp(s - lse[..., None])
    o = jnp.einsum('nbhqk,nbkhd->nbqhd', p, vv.astype(jnp.float32))
    o = o.reshape(N, Lp, H, Dh)[:, :L].reshape(B, dil, L, H, Dh).transpose(0, 2, 1, 3, 4).reshape(B, T, H, Dh)
    lse = lse.transpose(0, 1, 3, 2).reshape(N, Lp, H)[:, :L].reshape(B, dil, L, H).transpose(0, 2, 1, 3).reshape(B, T, H)
    return o, lse


def dilated_group_sample(q, k_new, v_new, k_buf, v_buf, window, dil):
    W, S, Dh = k_buf.shape[1], q.shape[1], q.shape[-1]
    span = window // dil
    kk = jnp.concatenate([k_buf, k_new.astype(k_buf.dtype)], axis=1)
    vv = jnp.concatenate([v_buf, v_new.astype(v_buf.dtype)], axis=1)
    rows = W + np.arange(S)[:, None] - dil * np.arange(span + 1)[None, :]
    valid = rows >= 0
    rows = np.clip(rows, 0, None)
    kg = kk[:, rows]
    vg = vv[:, rows]
    s = jnp.einsum('bshd,bsjhd->bshj', q, kg, preferred_element_type=jnp.float32) * (Dh ** -0.5)
    s = jnp.where(valid[None, :, None, :], s, -jnp.inf)
    lse = jax.nn.logsumexp(s, axis=-1)
    p = jnp.exp(s - lse[..., None])
    o = jnp.einsum('bshj,bsjhd->bshd', p, vg.astype(jnp.float32))
    return o, lse, kk[:, S:], vv[:, S:]


def combine_groups(outs, lses):
    o = jnp.stack(outs, axis=0)
    a = jax.nn.softmax(jnp.stack(lses, axis=0), axis=0)
    return jnp.sum(a[..., None] * o, axis=0)


def hgrn2_gates(zq, zf, zi, lb):
    B, T, _ = zq.shape

    def heads(a, d):
        return a.astype(jnp.float32).reshape(B, T, HG_HEADS, d).transpose(0, 2, 1, 3)

    lbh = lb.reshape(HG_HEADS, 1, HG_KEY_DIM)
    q = jax.nn.sigmoid(heads(zq, HG_KEY_DIM))
    k = (1.0 - lbh) * jax.nn.sigmoid(heads(zf, HG_KEY_DIM))
    logf = jnp.log(lbh + k)
    v = heads(zi, HG_VAL_DIM)
    return q, k, v, logf


def hgrn2_chunk(S0, q, k, v, logf):
    C = q.shape[2]
    b = jnp.cumsum(logf, axis=2)
    diff = b[:, :, :, None, :] - b[:, :, None, :, :]
    causal = np.tril(np.ones((C, C), dtype=bool))[None, None, :, :, None]
    decay = jnp.where(causal, jnp.exp(jnp.minimum(diff, 0.0)), 0.0)
    A = jnp.einsum('bhtk,bhsk,bhtsk->bhts', q, k, decay)
    o = jnp.einsum('bhts,bhsv->bhtv', A, v) + jnp.einsum('bhtk,bhkv->bhtv', q * jnp.exp(b), S0)
    b_end = b[:, :, -1:, :]
    S1 = jnp.exp(b_end[:, :, 0, :, None]) * S0 + jnp.einsum('bhsk,bhsv->bhkv', k * jnp.exp(b_end - b), v)
    return S1, o


def hgrn2_prompt(q, k, v, logf):
    B, H, T, K = q.shape
    nc = T // HG_CHUNK

    def chunks(a):
        return a.reshape(B, H, nc, HG_CHUNK, a.shape[-1]).transpose(2, 0, 1, 3, 4)

    S0 = jnp.zeros((B, H, K, HG_VAL_DIM), jnp.float32)

    def step(S, inp):
        return hgrn2_chunk(S, *inp)

    S, o = lax.scan(step, S0, (chunks(q), chunks(k), chunks(v), chunks(logf)))
    o = o.transpose(1, 2, 0, 3, 4).reshape(B, H, T, HG_VAL_DIM)
    return o, S


def head_rmsnorm(o, g):
    B, H, T, V = o.shape
    return rmsnorm(o.transpose(0, 2, 1, 3), g.reshape(H, V)).reshape(B, T, H * V)


def mem_kv(mem, g_mem, w_mk, w_mv):
    B, M, _ = mem.shape
    m = rmsnorm(mem, g_mem)
    return ((m @ w_mk).reshape(B, M, MEM_HEADS, MEM_HEAD_DIM),
            (m @ w_mv).reshape(B, M, MEM_HEADS, MEM_HEAD_DIM))


def mem_attend(q, k, v):
    s = jnp.einsum('bthd,bmhd->bhtm', q, k, preferred_element_type=jnp.float32) * (MEM_HEAD_DIM ** -0.5)
    p = jax.nn.softmax(s, axis=-1)
    return jnp.einsum('bhtm,bmhd->bthd', p, v.astype(jnp.float32))


def merge_branches(att, hgo, memo, zg, w_ba, w_bb, w_bc, w_o):
    B, T, _ = zg.shape
    dt = zg.dtype
    gates = jax.nn.sigmoid(zg.astype(jnp.float32)).reshape(B, T, N_BRANCH, D_MODEL)
    m = (gates[:, :, 0] * (att.reshape(B, T, -1).astype(dt) @ w_ba)
         + gates[:, :, 1] * (hgo.astype(dt) @ w_bb)
         + gates[:, :, 2] * (memo.reshape(B, T, -1).astype(dt) @ w_bc))
    return m.astype(dt) @ w_o


def setup_inputs(seed: int = 0) -> dict:
    key = jax.random.key(seed)
    ks = iter(jax.random.split(key, 48))

    def nrm(shape, scale):
        return scale * jax.random.normal(next(ks), shape, jnp.float32)

    def gain(shape):
        return 1.0 + 0.01 * jax.random.normal(next(ks), shape, jnp.float32)

    wl = [min(w, PAST_LEN) for (w, d) in ATT_GROUPS]
    hd = (ATT_HEADS, ATT_HEAD_DIM)
    return {
        'x_prompt': nrm((BATCH, SEQ, D_MODEL), 1.0),
        'x_sample': nrm((DEC_BATCH, DEC_SEQ, D_MODEL), 1.0),
        'mem_prompt': nrm((BATCH, MEM_LEN, D_MODEL), 1.0),
        'cache_win1_k': nrm((DEPTH, DEC_BATCH, wl[0]) + hd, 1.0),
        'cache_win1_v': nrm((DEPTH, DEC_BATCH, wl[0]) + hd, 1.0),
        'cache_win4_k': nrm((DEPTH, DEC_BATCH, wl[1]) + hd, 1.0),
        'cache_win4_v': nrm((DEPTH, DEC_BATCH, wl[1]) + hd, 1.0),
        'cache_win16_k': nrm((DEPTH, DEC_BATCH, wl[2]) + hd, 1.0),
        'cache_win16_v': nrm((DEPTH, DEC_BATCH, wl[2]) + hd, 1.0),
        'cache_mem_k': nrm((DEPTH, DEC_BATCH, MEM_LEN, MEM_HEADS, MEM_HEAD_DIM), 1.0),
        'cache_mem_v': nrm((DEPTH, DEC_BATCH, MEM_LEN, MEM_HEADS, MEM_HEAD_DIM), 1.0),
        'state_hgrn': nrm((DEPTH, DEC_BATCH, HG_HEADS, HG_KEY_DIM, HG_VAL_DIM), 0.5),
        'g_ff1': gain((DEPTH, D_MODEL)),
        'w_ff1_gate': nrm((DEPTH, D_MODEL, D_FF), D_MODEL ** -0.5),
        'w_ff1_up': nrm((DEPTH, D_MODEL, D_FF), D_MODEL ** -0.5),
        'w_ff1_down': nrm((DEPTH, D_FF, D_MODEL), D_FF ** -0.5),
        'g_mix': gain((DEPTH, D_MODEL)),
        'w_in': nrm((DEPTH, D_MODEL, IN_WIDTH), D_MODEL ** -0.5),
        'hg_lb_logits': nrm((DEPTH + 1, HG_KW), 0.5),
        'g_hg_out': gain((DEPTH, HG_VW)),
        'g_mem': gain((DEPTH, D_MODEL)),
        'w_mem_k': nrm((DEPTH, D_MODEL, MEM_WIDTH), D_MODEL ** -0.5),
        'w_mem_v': nrm((DEPTH, D_MODEL, MEM_WIDTH), D_MODEL ** -0.5),
        'w_branch_att': nrm((DEPTH, ATT_WIDTH, D_MODEL), ATT_WIDTH ** -0.5),
        'w_branch_hg': nrm((DEPTH, HG_VW, D_MODEL), HG_VW ** -0.5),
        'w_branch_mem': nrm((DEPTH, MEM_WIDTH, D_MODEL), MEM_WIDTH ** -0.5),
        'w_out': nrm((DEPTH, D_MODEL, D_MODEL), D_MODEL ** -0.5),
        'g_ff2': gain((DEPTH, D_MODEL)),
        'w_ff2_gate': nrm((DEPTH, D_MODEL, D_FF), D_MODEL ** -0.5),
        'w_ff2_up': nrm((DEPTH, D_MODEL, D_FF), D_MODEL ** -0.5),
        'w_ff2_down': nrm((DEPTH, D_FF, D_MODEL), D_FF ** -0.5),
        'g_final': gain((D_MODEL,)),
    }


def reference(x_prompt, x_sample, mem_prompt,
              cache_win1_k, cache_win1_v, cache_win4_k, cache_win4_v, cache_win16_k, cache_win16_v,
              cache_mem_k, cache_mem_v, state_hgrn,
              g_ff1, w_ff1_gate, w_ff1_up, w_ff1_down,
              g_mix, w_in, hg_lb_logits, g_hg_out, g_mem, w_mem_k, w_mem_v,
              w_branch_att, w_branch_hg, w_branch_mem, w_out,
              g_ff2, w_ff2_gate, w_ff2_up, w_ff2_down, g_final):
    T = x_prompt.shape[1]
    lb_all = jnp.cumsum(jax.nn.softmax(hg_lb_logits.astype(jnp.float32), axis=0), axis=0)
    win_k = (cache_win1_k, cache_win4_k, cache_win16_k)
    win_v = (cache_win1_v, cache_win4_v, cache_win16_v)
    pk = [[] for _ in ATT_GROUPS]
    pv = [[] for _ in ATT_GROUPS]
    sk = [[] for _ in ATT_GROUPS]
    sv = [[] for _ in ATT_GROUPS]
    p_mk, p_mv, p_hs, s_hs = [], [], [], []
    xp, xs = x_prompt, x_sample
    for l in range(DEPTH):
        lb = lb_all[l]
        ffn1 = (g_ff1[l], w_ff1_gate[l], w_ff1_up[l], w_ff1_down[l])
        ffn2 = (g_ff2[l], w_ff2_gate[l], w_ff2_up[l], w_ff2_down[l])
        bw = (w_branch_att[l], w_branch_hg[l], w_branch_mem[l], w_out[l])
        xp = xp + 0.5 * half_ffn(xp, *ffn1)
        xs = xs + 0.5 * half_ffn(xs, *ffn1)

        qa, ka, va, hq, hf, hi, mq, zg = split_in(rmsnorm(xp, g_mix[l]) @ w_in[l])
        qa, ka, va = att_heads(qa), att_heads(ka), att_heads(va)
        outs, lses = [], []
        for g, (win, dil) in enumerate(ATT_GROUPS):
            o, s = dilated_group_prompt(qa[:, :, g], ka[:, :, g], va[:, :, g], win, dil)
            outs.append(o)
            lses.append(s)
            keep = min(win, T)
            pk[g].append(ka[:, T - keep:, g])
            pv[g].append(va[:, T - keep:, g])
        att = combine_groups(outs, lses)
        q, k, v, logf = hgrn2_gates(hq, hf, hi, lb)
        o_hg, S = hgrn2_prompt(q, k, v, logf)
        mk, mv = mem_kv(mem_prompt, g_mem[l], w_mem_k[l], w_mem_v[l])
        o_mem = mem_attend(mem_heads(mq), mk, mv)
        xp = xp + merge_branches(att, head_rmsnorm(o_hg, g_hg_out[l]), o_mem, zg, *bw)
        p_mk.append(mk)
        p_mv.append(mv)
        p_hs.append(S)

        qa, ka, va, hq, hf, hi, mq, zg = split_in(rmsnorm(xs, g_mix[l]) @ w_in[l])
        qa, ka, va = att_heads(qa), att_heads(ka), att_heads(va)
        outs, lses = [], []
        for g, (win, dil) in enumerate(ATT_GROUPS):
            o, s, nk, nv = dilated_group_sample(qa[:, :, g], ka[:, :, g], va[:, :, g],
                                                win_k[g][l], win_v[g][l], win, dil)
            outs.append(o)
            lses.append(s)
            sk[g].append(nk)
            sv[g].append(nv)
        att = combine_groups(outs, lses)
        q, k, v, logf = hgrn2_gates(hq, hf, hi, lb)
        S, o_hg = hgrn2_chunk(state_hgrn[l].astype(jnp.float32), q, k, v, logf)
        o_mem = mem_attend(mem_heads(mq), cache_mem_k[l], cache_mem_v[l])
        xs = xs + merge_branches(att, head_rmsnorm(o_hg, g_hg_out[l]), o_mem, zg, *bw)
        s_hs.append(S)

        xp = xp + 0.5 * half_ffn(xp, *ffn2)
        xs = xs + 0.5 * half_ffn(xs, *ffn2)

    y_prompt = rmsnorm(xp, g_final)
    y_sample = rmsnorm(xs, g_final)
    p_win1_k, p_win1_v = jnp.stack(pk[0]), jnp.stack(pv[0])
    p_win4_k, p_win4_v = jnp.stack(pk[1]), jnp.stack(pv[1])
    p_win16_k, p_win16_v = jnp.stack(pk[2]), jnp.stack(pv[2])
    p_mem_k, p_mem_v, p_hgrn = jnp.stack(p_mk), jnp.stack(p_mv), jnp.stack(p_hs)
    s_win1_k, s_win1_v = jnp.stack(sk[0]), jnp.stack(sv[0])
    s_win4_k, s_win4_v = jnp.stack(sk[1]), jnp.stack(sv[1])
    s_win16_k, s_win16_v = jnp.stack(sk[2]), jnp.stack(sv[2])
    s_hgrn = jnp.stack(s_hs)
    return (y_prompt, y_sample,
            p_win1_k, p_win1_v, p_win4_k, p_win4_v, p_win16_k, p_win16_v,
            p_mem_k, p_mem_v, p_hgrn,
            s_win1_k, s_win1_v, s_win4_k, s_win4_v, s_win16_k, s_win16_v,
            s_hgrn)
```

```python
import functools

import numpy as np
import jax
import jax.numpy as jnp
from jax import lax
from jax.experimental import pallas as pl
from jax.experimental.pallas import tpu as pltpu

F32 = jnp.float32
BF16 = jnp.bfloat16

D_MODEL = 1024
D_FF = 2816
EPS = 1e-6
HEADS = 4
HEAD_DIM = 128
WIDTH = HEADS * HEAD_DIM
ATT_GROUPS = ((128, 1), (512, 4), (2048, 16))
N_GROUPS = 3
BLOCK = 128
MEM_LEN = 256
N_BRANCH = 3
LSE_REP = 32
QK_SCALE = HEAD_DIM ** -0.5
VMEM_LIMIT_BYTES = 56 * 1024 * 1024

COL_Q, COL_K, COL_V, COL_HQ, COL_HF, COL_HI, COL_MQ, COL_GATE = 0, 3, 6, 9, 10, 11, 12, 13
IN_WIDTH = (COL_GATE + N_BRANCH * D_MODEL // WIDTH) * WIDTH


def _params(*sem):
    return pltpu.CompilerParams(dimension_semantics=sem, vmem_limit_bytes=VMEM_LIMIT_BYTES)


def _const_spec(shape):
    nd = len(shape)
    return pl.BlockSpec(shape, lambda *_: (0,) * nd, pipeline_mode=pl.Buffered(1))


def _rms(x, g):
    return x * lax.rsqrt(jnp.mean(x * x, axis=-1, keepdims=True) + EPS) * g


def _dot(a, b):
    return jnp.dot(a, b, preferred_element_type=F32)


def _dot_nt(a, b):
    return lax.dot_general(a, b, (((1,), (1,)), ((), ())), preferred_element_type=F32)


def _ffn_body(x_ref, g_ref, wg_ref, wu_ref, wd_ref, gf_ref, o_ref, *, final):
    x = x_ref[...]
    h = _rms(x, g_ref[...]).astype(BF16)
    a = _dot(h, wg_ref[...])
    u = _dot(h, wu_ref[...])
    act = (a * jax.nn.sigmoid(a) * u).astype(BF16)
    y = x + 0.5 * _dot(act, wd_ref[...])
    if final:
        y = _rms(y, gf_ref[...])
    o_ref[...] = y


def _ffn(x, g, wg, wu, wd, g_final, *, tm, final):
    m = x.shape[0]
    return pl.pallas_call(
        functools.partial(_ffn_body, final=final),
        grid=(m // tm,),
        in_specs=[pl.BlockSpec((tm, D_MODEL), lambda i: (i, 0)),
                  _const_spec((1, D_MODEL)),
                  _const_spec((D_MODEL, D_FF)), _const_spec((D_MODEL, D_FF)), _const_spec((D_FF, D_MODEL)),
                  _const_spec((1, D_MODEL))],
        out_specs=pl.BlockSpec((tm, D_MODEL), lambda i: (i, 0)),
        out_shape=jax.ShapeDtypeStruct((m, D_MODEL), F32),
        compiler_params=_params("parallel"),
        name="ffn_final" if final else "ffn",
    )(x, g, wg, wu, wd, g_final)


def _store_heads(ref, z):
    for h in range(HEADS):
        ref[:, h, :] = z[:, h * HEAD_DIM:(h + 1) * HEAD_DIM]


def _qkv_body(x_ref, g_ref, w_ref, *out_refs, tm, n_tiles):
    q_refs, k_refs, v_refs = out_refs[0:3], out_refs[3:6], out_refs[6:9]
    pk_refs, pv_refs = out_refs[9:12], out_refs[12:15]
    i = pl.program_id(1)
    h = _rms(x_ref[...], g_ref[...]).astype(BF16)

    def window_out(ref, z, g):
        keep = min(ATT_GROUPS[g][0], tm * n_tiles)
        if keep == tm * n_tiles:
            _store_heads(ref, z)
        else:
            @pl.when(i == n_tiles - 1)
            def _():
                _store_heads(ref, z[tm - keep:, :])

    for g in range(N_GROUPS):
        zq = _dot(h, w_ref[:, (COL_Q + g) * WIDTH:(COL_Q + g + 1) * WIDTH])
        q_refs[g][...] = (zq * QK_SCALE).astype(BF16)
        zk = _dot(h, w_ref[:, (COL_K + g) * WIDTH:(COL_K + g + 1) * WIDTH])
        k_refs[g][...] = zk.astype(BF16)
        window_out(pk_refs[g], zk, g)
        zv = _dot(h, w_ref[:, (COL_V + g) * WIDTH:(COL_V + g + 1) * WIDTH])
        v_refs[g][...] = zv.astype(BF16)
        window_out(pv_refs[g], zv, g)


def _qkv_proj(x, g_mix, w_qkv, *, tm):
    b, t, _ = x.shape
    n_tiles = t // tm
    row_spec = pl.BlockSpec((None, tm, WIDTH), lambda bi, i: (bi, i, 0))
    bf_shape = jax.ShapeDtypeStruct((b, t, WIDTH), BF16)
    win_specs, win_shapes = [], []
    for win, _ in ATT_GROUPS:
        keep = min(win, t)
        assert keep == t or keep <= tm
        if keep == t:
            win_specs.append(pl.BlockSpec((None, None, tm, HEADS, HEAD_DIM), lambda bi, i: (0, bi, i, 0, 0)))
        else:
            win_specs.append(pl.BlockSpec((None, None, keep, HEADS, HEAD_DIM), lambda bi, i: (0, bi, 0, 0, 0)))
        win_shapes.append(jax.ShapeDtypeStruct((1, b, keep, HEADS, HEAD_DIM), F32))
    return pl.pallas_call(
        functools.partial(_qkv_body, tm=tm, n_tiles=n_tiles),
        grid=(b, n_tiles),
        in_specs=[pl.BlockSpec((None, tm, D_MODEL), lambda bi, i: (bi, i, 0)),
                  _const_spec((1, D_MODEL)),
                  _const_spec((D_MODEL, 3 * N_GROUPS * WIDTH))],
        out_specs=[row_spec] * 9 + win_specs + win_specs,
        out_shape=[bf_shape] * 9 + win_shapes + win_shapes,
        compiler_params=_params("parallel", "arbitrary"),
        name="qkv_proj",
    )(x, g_mix, w_qkv)


def _lower_bound(logits_ref):
    lg = logits_ref[...]
    e = jnp.exp(lg - jnp.max(lg, axis=0, keepdims=True))
    return e[0:1, :] / jnp.sum(e, axis=0, keepdims=True)


def _rest_body(x_ref, g_ref, w_ref, lb_ref, hq_ref, hk_ref, lf_ref, hv_ref, mq_ref, gate_ref):
    h = _rms(x_ref[...], g_ref[...]).astype(BF16)
    lb = _lower_bound(lb_ref)
    hq_ref[...] = jax.nn.sigmoid(_dot(h, w_ref[:, 0:WIDTH])).astype(BF16)
    kk = (1.0 - lb) * jax.nn.sigmoid(_dot(h, w_ref[:, WIDTH:2 * WIDTH]))
    hk_ref[...] = kk.astype(BF16)
    lf_ref[...] = jnp.log(lb + kk)
    hv_ref[...] = _dot(h, w_ref[:, 2 * WIDTH:3 * WIDTH]).astype(BF16)
    mq_ref[...] = (_dot(h, w_ref[:, 3 * WIDTH:4 * WIDTH]) * QK_SCALE).astype(BF16)
    for c in range(N_BRANCH * D_MODEL // WIDTH):
        z = _dot(h, w_ref[:, (4 + c) * WIDTH:(5 + c) * WIDTH])
        gate_ref[:, c * WIDTH:(c + 1) * WIDTH] = jax.nn.sigmoid(z).astype(BF16)


def _rest_proj(x, g_mix, w_rest, lb_logits, *, tm):
    m = x.shape[0]
    spec = pl.BlockSpec((tm, WIDTH), lambda i: (i, 0))
    bf = jax.ShapeDtypeStruct((m, WIDTH), BF16)
    return pl.pallas_call(
        _rest_body,
        grid=(m // tm,),
        in_specs=[pl.BlockSpec((tm, D_MODEL), lambda i: (i, 0)),
                  _const_spec((1, D_MODEL)),
                  _const_spec(w_rest.shape),
                  _const_spec(lb_logits.shape)],
        out_specs=[spec, spec, spec, spec, spec, pl.BlockSpec((tm, N_BRANCH * D_MODEL), lambda i: (i, 0))],
        out_shape=[bf, bf, jax.ShapeDtypeStruct((m, WIDTH), F32), bf, bf,
                   jax.ShapeDtypeStruct((m, N_BRANCH * D_MODEL), BF16)],
        compiler_params=_params("parallel"),
        name="rest_proj",
    )(x, g_mix, w_rest, lb_logits)


def _att_body(q_ref, ko_ref, kp_ref, vo_ref, vp_ref, o_ref, lse_ref):
    i = pl.program_id(2)
    qi = lax.broadcasted_iota(jnp.int32, (BLOCK, BLOCK), 0)
    ki = lax.broadcasted_iota(jnp.int32, (BLOCK, BLOCK), 1)
    own_ok = ki <= qi
    prev_ok = jnp.logical_and(ki >= qi, i > 0)
    for h in range(HEADS):
        sl = slice(h * HEAD_DIM, (h + 1) * HEAD_DIM)
        q = q_ref[:, sl]
        s_o = jnp.where(own_ok, _dot_nt(q, ko_ref[:, sl]), -jnp.inf)
        s_p = jnp.where(prev_ok, _dot_nt(q, kp_ref[:, sl]), -jnp.inf)
        m = jnp.maximum(jnp.max(s_o, axis=-1, keepdims=True), jnp.max(s_p, axis=-1, keepdims=True))
        p_o = jnp.exp(s_o - m)
        p_p = jnp.exp(s_p - m)
        l = jnp.sum(p_o, axis=-1, keepdims=True) + jnp.sum(p_p, axis=-1, keepdims=True)
        acc = _dot(p_o.astype(BF16), vo_ref[:, sl]) + _dot(p_p.astype(BF16), vp_ref[:, sl])
        o_ref[:, sl] = (acc / l).astype(BF16)
        lse_ref[:, h * LSE_REP:(h + 1) * LSE_REP] = jnp.broadcast_to(m + jnp.log(l), (BLOCK, LSE_REP))


def _att_group(q, k, v, dil):
    b, t, _ = q.shape
    sub_len = t // dil
    nb = sub_len // BLOCK
    view = lambda a: a.reshape(b, sub_len, dil * WIDTH)
    own = pl.BlockSpec((None, BLOCK, WIDTH), lambda bi, r, i: (bi, i, r))
    prev = pl.BlockSpec((None, BLOCK, WIDTH), lambda bi, r, i: (bi, jnp.maximum(i - 1, 0), r))
    o, lse = pl.pallas_call(
        _att_body,
        grid=(b, dil, nb),
        in_specs=[own, own, prev, own, prev],
        out_specs=[own, pl.BlockSpec((None, BLOCK, HEADS * LSE_REP), lambda bi, r, i: (bi, i, r))],
        out_shape=[jax.ShapeDtypeStruct((b, sub_len, dil * WIDTH), BF16),
                   jax.ShapeDtypeStruct((b, sub_len, dil * HEADS * LSE_REP), F32)],
        compiler_params=_params("parallel", "parallel", "arbitrary"),
        name=f"att_dil{dil}",
    )(view(q), view(k), view(k), view(v), view(v))
    return o.reshape(b, t, WIDTH), lse.reshape(b, t, HEADS * LSE_REP)


N_LEVELS = 7
ROW_CUM, ROW_REM = N_LEVELS, N_LEVELS + 1


def _hgrn_constants():
    r = np.arange(BLOCK)[:, None]
    s = np.arange(BLOCK)[None, :]
    mats = []
    for lvl in range(N_LEVELS):
        half = 1 << lvl
        mid = (r // (2 * half)) * 2 * half + half
        upper = (r % (2 * half)) >= half
        mats.append(np.where(upper, (s >= mid) & (s <= r), (s > r) & (s <= mid - 1)))
    mats.append(s <= r)
    mats.append(s > r)
    range_mat = np.concatenate(mats, axis=0).astype(np.float32)
    x = r ^ s
    level = np.where(s < r, np.floor(np.log2(np.maximum(x, 1))).astype(np.int32), np.where(s == r, N_LEVELS, -1))
    return jnp.asarray(range_mat, BF16), jnp.asarray(level, jnp.int32)


def _hgrn_body(q_ref, k_ref, lf_ref, v_ref, rng_ref, lvl_ref, gain_ref, o_ref, s_ref, st_ref, *, n_chunks):
    c = pl.program_id(1)

    @pl.when(c == 0)
    def _():
        st_ref[...] = jnp.zeros_like(st_ref)

    lf = lf_ref[...]
    lf_hi = lf.astype(BF16)
    lf_lo = (lf - lf_hi.astype(F32)).astype(BF16)
    rng = rng_ref[...]
    z = _dot(rng, lf_hi) + _dot(rng, lf_lo)
    q = q_ref[...].astype(F32)
    kk = k_ref[...].astype(F32)
    lhs, rhs = [], []
    for lvl in range(N_LEVELS):
        x = jnp.exp(z[lvl * BLOCK:(lvl + 1) * BLOCK])
        lhs.append((q * x).astype(BF16))
        rhs.append((kk * x).astype(BF16))
    lhs.append(q_ref[...])
    rhs.append(k_ref[...])
    b_cum = z[ROW_CUM * BLOCK:(ROW_CUM + 1) * BLOCK]
    q_in = (q * jnp.exp(b_cum)).astype(BF16)
    k_out = (kk * jnp.exp(z[ROW_REM * BLOCK:(ROW_REM + 1) * BLOCK])).astype(BF16)
    decay = jnp.exp(b_cum[BLOCK - 1:BLOCK, :])
    level = lvl_ref[...]
    gain = gain_ref[...]
    for h in range(HEADS):
        sl = slice(h * HEAD_DIM, (h + 1) * HEAD_DIM)
        a = jnp.zeros((BLOCK, BLOCK), F32)
        for lvl in range(N_LEVELS + 1):
            a = jnp.where(level == lvl, _dot_nt(lhs[lvl][:, sl], rhs[lvl][:, sl]), a)
        v = v_ref[:, sl]
        st = st_ref[h]
        o = _dot(a.astype(BF16), v) + _dot_nt(q_in[:, sl], st.astype(BF16))
        o_ref[:, sl] = _rms(o, gain[:, sl]).astype(BF16)
        v_t = v.astype(F32).T.astype(BF16)
        st_new = st * decay[:, sl] + _dot(v_t, k_out[:, sl])
        st_ref[h] = st_new

        @pl.when(c == n_chunks - 1)
        def _():
            s_ref[h] = st_new.T


def _hgrn_prompt(hq, hk, lf, hv, gain, b, t):
    n_chunks = t // BLOCK
    range_mat, level = _hgrn_constants()
    spec = pl.BlockSpec((BLOCK, WIDTH), lambda bi, c: (bi * n_chunks + c, 0))
    return pl.pallas_call(
        functools.partial(_hgrn_body, n_chunks=n_chunks),
        grid=(b, n_chunks),
        in_specs=[spec, spec, spec, spec, _const_spec(range_mat.shape), _const_spec(level.shape),
                  _const_spec((1, WIDTH))],
        out_specs=[spec, pl.BlockSpec((None, None, HEADS, HEAD_DIM, HEAD_DIM), lambda bi, c: (0, bi, 0, 0, 0))],
        out_shape=[jax.ShapeDtypeStruct((b * t, WIDTH), BF16),
                   jax.ShapeDtypeStruct((1, b, HEADS, HEAD_DIM, HEAD_DIM), F32)],
        scratch_shapes=[pltpu.VMEM((HEADS, HEAD_DIM, HEAD_DIM), F32)],
        compiler_params=_params("parallel", "arbitrary"),
        name="hgrn_prompt",
    )(hq, hk, lf, hv, range_mat, level, gain)


def _memkv_body(m_ref, g_ref, wk_ref, wv_ref, pk_ref, pv_ref, kb_ref, vb_ref):
    h = _rms(m_ref[...], g_ref[...]).astype(BF16)
    k = _dot(h, wk_ref[...])
    v = _dot(h, wv_ref[...])
    _store_heads(pk_ref, k)
    _store_heads(pv_ref, v)
    kb_ref[...] = k.astype(BF16)
    vb_ref[...] = v.astype(BF16)


def _mem_kv(mem, g_mem, w_mk, w_mv):
    b = mem.shape[0]
    head_spec = pl.BlockSpec((None, None, MEM_LEN, HEADS, HEAD_DIM), lambda bi: (0, bi, 0, 0, 0))
    flat_spec = pl.BlockSpec((None, MEM_LEN, WIDTH), lambda bi: (bi, 0, 0))
    head_shape = jax.ShapeDtypeStruct((1, b, MEM_LEN, HEADS, HEAD_DIM), F32)
    flat_shape = jax.ShapeDtypeStruct((b, MEM_LEN, WIDTH), BF16)
    return pl.pallas_call(
        _memkv_body,
        grid=(b,),
        in_specs=[pl.BlockSpec((None, MEM_LEN, D_MODEL), lambda bi: (bi, 0, 0)),
                  _const_spec((1, D_MODEL)), _const_spec((D_MODEL, WIDTH)), _const_spec((D_MODEL, WIDTH))],
        out_specs=[head_spec, head_spec, flat_spec, flat_spec],
        out_shape=[head_shape, head_shape, flat_shape, flat_shape],
        compiler_params=_params("parallel"),
        name="mem_kv",
    )(mem, g_mem, w_mk, w_mv)


def _combine_body(o0_ref, o1_ref, o2_ref, l0_ref, l1_ref, l2_ref, mq_ref, mk_ref, mv_ref, att_ref, mem_ref):
    o_refs = (o0_ref, o1_ref, o2_ref)
    l_refs = (l0_ref, l1_ref, l2_ref)
    for h in range(HEADS):
        sl = slice(h * HEAD_DIM, (h + 1) * HEAD_DIM)
        lse = [r[:, h * LSE_REP:h * LSE_REP + 1] for r in l_refs]
        m = jnp.maximum(jnp.maximum(lse[0], lse[1]), lse[2])
        e = [jnp.exp(x - m) for x in lse]
        inv = 1.0 / (e[0] + e[1] + e[2])
        att = sum((e[g] * inv) * o_refs[g][:, sl].astype(F32) for g in range(N_GROUPS))
        att_ref[:, sl] = att.astype(BF16)
        s = _dot_nt(mq_ref[:, sl], mk_ref[:, sl])
        p = jnp.exp(s - jnp.max(s, axis=-1, keepdims=True))
        o = _dot(p.astype(BF16), mv_ref[:, sl]) / jnp.sum(p, axis=-1, keepdims=True)
        mem_ref[:, sl] = o.astype(BF16)


def _combine(outs, lses, mq, mk, mv, b, t, *, tm):
    n_tiles = t // tm
    row = pl.BlockSpec((tm, WIDTH), lambda bi, i: (bi * n_tiles + i, 0))
    lrow = pl.BlockSpec((tm, HEADS * LSE_REP), lambda bi, i: (bi * n_tiles + i, 0))
    memspec = pl.BlockSpec((None, MEM_LEN, WIDTH), lambda bi, i: (bi, 0, 0))
    shape = jax.ShapeDtypeStruct((b * t, WIDTH), BF16)
    return pl.pallas_call(
        _combine_body,
        grid=(b, n_tiles),
        in_specs=[row, row, row, lrow, lrow, lrow, row, memspec, memspec],
        out_specs=[row, row],
        out_shape=[shape, shape],
        compiler_params=_params("parallel", "parallel"),
        name="combine",
    )(*outs, *lses, mq, mk, mv)


def _merge_body(x_ref, att_ref, hg_ref, mem_ref, gate_ref, wa_ref, wb_ref, wc_ref, wo_ref, o_ref):
    def gate(j):
        return gate_ref[:, j * D_MODEL:(j + 1) * D_MODEL].astype(F32)

    m = (gate(0) * _dot(att_ref[...].astype(BF16), wa_ref[...])
         + gate(1) * _dot(hg_ref[...].astype(BF16), wb_ref[...])
         + gate(2) * _dot(mem_ref[...].astype(BF16), wc_ref[...]))
    o_ref[...] = x_ref[...] + _dot(m.astype(BF16), wo_ref[...])


def _merge(x, att, hg, mem, gates, wa, wb, wc, wo, *, tm):
    m = x.shape[0]
    row = pl.BlockSpec((tm, WIDTH), lambda i: (i, 0))
    wide = pl.BlockSpec((tm, D_MODEL), lambda i: (i, 0))
    return pl.pallas_call(
        _merge_body,
        grid=(m // tm,),
        in_specs=[wide, row, row, row, pl.BlockSpec((tm, N_BRANCH * D_MODEL), lambda i: (i, 0)),
                  _const_spec((WIDTH, D_MODEL)), _const_spec((WIDTH, D_MODEL)), _const_spec((WIDTH, D_MODEL)),
                  _const_spec((D_MODEL, D_MODEL))],
        out_specs=wide,
        out_shape=jax.ShapeDtypeStruct((m, D_MODEL), F32),
        compiler_params=_params("parallel"),
        name="merge",
    )(x, att, hg, mem, gates, wa, wb, wc, wo)


def _norm_matmul_body(x_ref, g_ref, w_ref, o_ref):
    o_ref[...] = _dot(_rms(x_ref[...], g_ref[...]).astype(BF16), w_ref[...])


def _sample_in_proj(x, g_mix, w_in):
    m = x.shape[0]
    n = w_in.shape[1]
    return pl.pallas_call(
        _norm_matmul_body,
        grid=(n // WIDTH,),
        in_specs=[_const_spec((m, D_MODEL)), _const_spec((1, D_MODEL)),
                  pl.BlockSpec((D_MODEL, WIDTH), lambda j: (0, j))],
        out_specs=pl.BlockSpec((m, WIDTH), lambda j: (0, j)),
        out_shape=jax.ShapeDtypeStruct((m, n), F32),
        compiler_params=_params("parallel"),
        name="sample_in_proj",
    )(x, g_mix, w_in)


def _one_query_attend(q, k_rows, v_rows, k_new=None, v_new=None):
    s = jnp.sum(k_rows * q, axis=-1, keepdims=True)
    m = jnp.max(s, axis=0, keepdims=True)
    if k_new is not None:
        s_new = jnp.sum(k_new * q, axis=-1, keepdims=True)
        m = jnp.maximum(m, s_new)
    p = jnp.exp(s - m)
    l = jnp.sum(p, axis=0, keepdims=True)
    acc = jnp.sum(p * v_rows, axis=0, keepdims=True)
    if k_new is not None:
        p_new = jnp.exp(s_new - m)
        l = l + p_new
        acc = acc + p_new * v_new
    return acc / l, m + jnp.log(l)


def _sample_att_body(q_ref, kn_ref, vn_ref, mq_ref, k1_ref, v1_ref, k4_ref, v4_ref, k16_ref, v16_ref,
                     mk_ref, mv_ref, att_ref, mem_ref):
    k_refs = (k1_ref, k4_ref, k16_ref)
    v_refs = (v1_ref, v4_ref, v16_ref)
    for h in range(HEADS):
        sl = slice(h * HEAD_DIM, (h + 1) * HEAD_DIM)
        outs, lses = [], []
        for g in range(N_GROUPS):
            gs = slice(g * WIDTH + h * HEAD_DIM, g * WIDTH + (h + 1) * HEAD_DIM)
            o, lse = _one_query_attend(q_ref[:, gs] * QK_SCALE, k_refs[g][:, h, :], v_refs[g][:, h, :],
                                       kn_ref[:, gs], vn_ref[:, gs])
            outs.append(o)
            lses.append(lse)
        m = jnp.maximum(jnp.maximum(lses[0], lses[1]), lses[2])
        e = [jnp.exp(x - m) for x in lses]
        inv = 1.0 / (e[0] + e[1] + e[2])
        att_ref[:, sl] = sum((e[g] * inv) * outs[g] for g in range(N_GROUPS))
        o, _ = _one_query_attend(mq_ref[:, sl] * QK_SCALE, mk_ref[:, h, :], mv_ref[:, h, :])
        mem_ref[:, sl] = o


def _sample_attend(z3, caches_k, caches_v, cache_mk, cache_mv):
    bd = z3.shape[0]
    zspec = lambda col, n: pl.BlockSpec((None, 1, n * WIDTH), lambda b: (b, 0, col // n))
    cache_specs, cache_args = [], []
    for (win, dil), ck, cv in zip(ATT_GROUPS, caches_k, caches_v):
        rows = ck.shape[2]
        assert rows == win and rows // dil == BLOCK
        spec = pl.BlockSpec((None, None, BLOCK, None, HEADS, HEAD_DIM), lambda b: (0, b, 0, 0, 0, 0))
        for c in (ck, cv):
            cache_specs.append(spec)
            cache_args.append(c.reshape(1, bd, BLOCK, dil, HEADS, HEAD_DIM))
    mem_spec = pl.BlockSpec((None, None, MEM_LEN, HEADS, HEAD_DIM), lambda b: (0, b, 0, 0, 0))
    out_spec = pl.BlockSpec((None, 1, WIDTH), lambda b: (b, 0, 0))
    out_shape = jax.ShapeDtypeStruct((bd, 1, WIDTH), F32)
    return pl.pallas_call(
        _sample_att_body,
        grid=(bd,),
        in_specs=[zspec(COL_Q, 3), zspec(COL_K, 3), zspec(COL_V, 3), zspec(COL_MQ, 1)] + cache_specs
                 + [mem_spec, mem_spec],
        out_specs=[out_spec, out_spec],
        out_shape=[out_shape, out_shape],
        compiler_params=_params("parallel"),
        name="sample_attend",
    )(z3, z3, z3, z3, *cache_args, cache_mk, cache_mv)


def _cache_shift_body(*refs):
    n = len(refs) // 3
    cache_refs, new_refs, out_refs, sem = refs[:n], refs[n:2 * n], refs[2 * n:3 * n], refs[3 * n]
    copies = []
    for j in range(n):
        rows = cache_refs[j].shape[2]
        copies.append(pltpu.make_async_copy(cache_refs[j].at[:, :, pl.ds(1, rows - 1)],
                                            out_refs[j].at[:, :, pl.ds(0, rows - 1)], sem.at[2 * j]))
        copies.append(pltpu.make_async_copy(new_refs[j], out_refs[j].at[:, :, pl.ds(rows - 1, 1)],
                                            sem.at[2 * j + 1]))
    for cp in copies:
        cp.start()
    for cp in copies:
        cp.wait()


def _cache_shift(caches, new_rows):
    n = len(caches)
    any_spec = pl.BlockSpec(memory_space=pl.ANY)
    return pl.pallas_call(
        _cache_shift_body,
        in_specs=[any_spec] * (2 * n),
        out_specs=[any_spec] * n,
        out_shape=[jax.ShapeDtypeStruct(c.shape, c.dtype) for c in caches],
        scratch_shapes=[pltpu.SemaphoreType.DMA((2 * n,))],
        name="cache_shift",
    )(*caches, *new_rows)


SAMPLE_GROUP = 8


def _sample_hgrn_body(zq_ref, zf_ref, zi_ref, lb_ref, gain_ref, s_ref, o_ref, so_ref):
    lb = _lower_bound(lb_ref)
    gain = gain_ref[...]
    q = jax.nn.sigmoid(zq_ref[...])
    kk = (1.0 - lb) * jax.nn.sigmoid(zf_ref[...])
    f = lb + kk
    v = zi_ref[...]
    pad = jnp.zeros((HEAD_DIM - SAMPLE_GROUP, HEAD_DIM), F32)

    def columns(a, h):
        return jnp.concatenate([a[:, h * HEAD_DIM:(h + 1) * HEAD_DIM], pad], axis=0).T

    for h in range(HEADS):
        sl = slice(h * HEAD_DIM, (h + 1) * HEAD_DIM)
        q_t, k_t, f_t = columns(q, h), columns(kk, h), columns(f, h)
        for j in range(SAMPLE_GROUP):
            s1 = f_t[:, j:j + 1] * s_ref[j, h] + k_t[:, j:j + 1] * v[j:j + 1, sl]
            so_ref[j, h] = s1
            o = jnp.sum(q_t[:, j:j + 1] * s1, axis=0, keepdims=True)
            o_ref[j:j + 1, sl] = _rms(o, gain[:, sl])


def _sample_hgrn(z, lb_logits, gain, state):
    bd = z.shape[0]
    zspec = lambda col: pl.BlockSpec((SAMPLE_GROUP, WIDTH), lambda i: (i, col))
    sspec = pl.BlockSpec((None, SAMPLE_GROUP, HEADS, HEAD_DIM, HEAD_DIM), lambda i: (0, i, 0, 0, 0))
    return pl.pallas_call(
        _sample_hgrn_body,
        grid=(bd // SAMPLE_GROUP,),
        in_specs=[zspec(COL_HQ), zspec(COL_HF), zspec(COL_HI), _const_spec(lb_logits.shape),
                  _const_spec((1, WIDTH)), sspec],
        out_specs=[pl.BlockSpec((SAMPLE_GROUP, WIDTH), lambda i: (i, 0)), sspec],
        out_shape=[jax.ShapeDtypeStruct((bd, WIDTH), F32), jax.ShapeDtypeStruct(state.shape, F32)],
        compiler_params=_params("parallel"),
        name="sample_hgrn",
    )(z, z, z, lb_logits, gain, state)


def _gate_body(z_ref, o_ref):
    o_ref[...] = jax.nn.sigmoid(z_ref[...]).astype(BF16)


def _sample_gates(z):
    bd = z.shape[0]
    n = N_BRANCH * D_MODEL
    return pl.pallas_call(
        _gate_body,
        grid=(n // WIDTH,),
        in_specs=[pl.BlockSpec((bd, WIDTH), lambda j: (0, COL_GATE + j))],
        out_specs=pl.BlockSpec((bd, WIDTH), lambda j: (0, j)),
        out_shape=jax.ShapeDtypeStruct((bd, n), BF16),
        name="sample_gates",
    )(z)


def kernel(x_prompt, x_sample, mem_prompt, cache_win1_k, cache_win1_v, cache_win4_k, cache_win4_v, cache_win16_k, cache_win16_v, cache_mem_k, cache_mem_v, state_hgrn, g_ff1, w_ff1_gate, w_ff1_up, w_ff1_down, g_mix, w_in, hg_lb_logits, g_hg_out, g_mem, w_mem_k, w_mem_v, w_branch_att, w_branch_hg, w_branch_mem, w_out, g_ff2, w_ff2_gate, w_ff2_up, w_ff2_down, g_final):
    b, t, _ = x_prompt.shape
    bd = x_sample.shape[0]
    assert g_ff1.shape[0] == 1 and x_sample.shape[1] == 1 and w_in.shape[2] == IN_WIDTH
    bf = lambda w: w[0].astype(BF16)
    ffn1 = (g_ff1, bf(w_ff1_gate), bf(w_ff1_up), bf(w_ff1_down), g_final.reshape(1, D_MODEL))
    ffn2 = (g_ff2, bf(w_ff2_gate), bf(w_ff2_up), bf(w_ff2_down), g_final.reshape(1, D_MODEL))
    w_in_b = bf(w_in)
    w_qkv, w_rest = w_in_b[:, :COL_HQ * WIDTH], w_in_b[:, COL_HQ * WIDTH:]
    branch_w = (bf(w_branch_att), bf(w_branch_hg), bf(w_branch_mem), bf(w_out))

    xs = _ffn(x_sample.reshape(bd, D_MODEL), *ffn1, tm=bd, final=False)
    zs = _sample_in_proj(xs, g_mix, w_in_b)
    caches_k = (cache_win1_k, cache_win4_k, cache_win16_k)
    caches_v = (cache_win1_v, cache_win4_v, cache_win16_v)
    att_s, mem_s = _sample_attend(zs.reshape(bd, 1, IN_WIDTH), caches_k, caches_v, cache_mem_k, cache_mem_v)
    new_rows = []
    for g in range(N_GROUPS):
        for col in (COL_K, COL_V):
            new_rows.append(zs[:, (col + g) * WIDTH:(col + g + 1) * WIDTH].reshape(1, bd, 1, HEADS, HEAD_DIM))
    s_wins = _cache_shift([c for pair in zip(caches_k, caches_v) for c in pair], new_rows)
    hg_s, s_hgrn = _sample_hgrn(zs, hg_lb_logits, g_hg_out, state_hgrn)
    xs = _merge(xs, att_s.reshape(bd, WIDTH), hg_s, mem_s.reshape(bd, WIDTH), _sample_gates(zs), *branch_w, tm=bd)
    y_sample = _ffn(xs, *ffn2, tm=bd, final=True).reshape(bd, 1, D_MODEL)

    xp = _ffn(x_prompt.reshape(b * t, D_MODEL), *ffn1, tm=512, final=False)
    qkv = _qkv_proj(xp.reshape(b, t, D_MODEL), g_mix, w_qkv, tm=512)
    q_g, k_g, v_g, p_k, p_v = qkv[0:3], qkv[3:6], qkv[6:9], qkv[9:12], qkv[12:15]
    hq, hk, lf, hv, mq, gates = _rest_proj(xp, g_mix, w_rest, hg_lb_logits, tm=512)
    outs, lses = [], []
    for g, (_, dil) in enumerate(ATT_GROUPS):
        o, lse = _att_group(q_g[g], k_g[g], v_g[g], dil)
        outs.append(o.reshape(b * t, WIDTH))
        lses.append(lse.reshape(b * t, HEADS * LSE_REP))
    hg_p, p_hgrn = _hgrn_prompt(hq, hk, lf, hv, g_hg_out, b, t)
    p_mem_k, p_mem_v, mk_b, mv_b = _mem_kv(mem_prompt, g_mem, bf(w_mem_k), bf(w_mem_v))
    att_p, mem_p = _combine(outs, lses, mq, mk_b, mv_b, b, t, tm=512)
    xp = _merge(xp, att_p, hg_p, mem_p, gates, *branch_w, tm=512)
    y_prompt = _ffn(xp, *ffn2, tm=512, final=True).reshape(b, t, D_MODEL)

    return (y_prompt, y_sample,
            p_k[0], p_v[0], p_k[1], p_v[1], p_k[2], p_v[2],
            p_mem_k, p_mem_v, p_hgrn,
            s_wins[0], s_wins[1], s_wins[2], s_wins[3], s_wins[4], s_wins[5],
            s_hgrn)
```

```python
import functools

import numpy as np
import jax
import jax.numpy as jnp
from jax import lax
from jax.experimental import pallas as pl
from jax.experimental.pallas import tpu as pltpu

F32 = jnp.float32
BF16 = jnp.bfloat16

D_MODEL = 1024
D_FF = 2816
EPS = 1e-6
HEADS = 4
HEAD_DIM = 128
WIDTH = HEADS * HEAD_DIM
ATT_GROUPS = ((128, 1), (512, 4), (2048, 16))
N_GROUPS = 3
BLOCK = 128
MEM_LEN = 256
N_BRANCH = 3
LSE_REP = 32
QK_SCALE = HEAD_DIM ** -0.5
VMEM_LIMIT_BYTES = 56 * 1024 * 1024

COL_Q, COL_K, COL_V, COL_HQ, COL_HF, COL_HI, COL_MQ, COL_GATE = 0, 3, 6, 9, 10, 11, 12, 13
IN_WIDTH = (COL_GATE + N_BRANCH * D_MODEL // WIDTH) * WIDTH


def _params(*sem):
    return pltpu.CompilerParams(dimension_semantics=sem, vmem_limit_bytes=VMEM_LIMIT_BYTES)


def _const_spec(shape):
    nd = len(shape)
    return pl.BlockSpec(shape, lambda *_: (0,) * nd, pipeline_mode=pl.Buffered(1))


def _rms(x, g):
    return x * lax.rsqrt(jnp.mean(x * x, axis=-1, keepdims=True) + EPS) * g


def _dot(a, b):
    return jnp.dot(a, b, preferred_element_type=F32)


def _dot_nt(a, b):
    return lax.dot_general(a, b, (((1,), (1,)), ((), ())), preferred_element_type=F32)


def _ffn_body(x_ref, g_ref, wg_ref, wu_ref, wd_ref, gf_ref, o_ref, *, final):
    x = x_ref[...]
    h = _rms(x, g_ref[...]).astype(BF16)
    a = _dot(h, wg_ref[...])
    u = _dot(h, wu_ref[...])
    act = (a * jax.nn.sigmoid(a) * u).astype(BF16)
    y = x + 0.5 * _dot(act, wd_ref[...])
    if final:
        y = _rms(y, gf_ref[...])
    o_ref[...] = y


def _ffn(x, g, wg, wu, wd, g_final, *, tm, final):
    m = x.shape[0]
    return pl.pallas_call(
        functools.partial(_ffn_body, final=final),
        grid=(m // tm,),
        in_specs=[pl.BlockSpec((tm, D_MODEL), lambda i: (i, 0)),
                  _const_spec((1, D_MODEL)),
                  _const_spec((D_MODEL, D_FF)), _const_spec((D_MODEL, D_FF)), _const_spec((D_FF, D_MODEL)),
                  _const_spec((1, D_MODEL))],
        out_specs=pl.BlockSpec((tm, D_MODEL), lambda i: (i, 0)),
        out_shape=jax.ShapeDtypeStruct((m, D_MODEL), F32),
        compiler_params=_params("parallel"),
        name="ffn_final" if final else "ffn",
    )(x, g, wg, wu, wd, g_final)


def _store_heads(ref, z):
    for h in range(HEADS):
        ref[:, h, :] = z[:, h * HEAD_DIM:(h + 1) * HEAD_DIM]


def _to_strided_view(dst_ref, z, stage_ref, dil):
    if dil == 1:
        dst_ref[...] = z.astype(dst_ref.dtype)
        return
    n = z.shape[0] // dil
    for c in range(stage_ref.shape[0]):
        stage_ref[c] = z[:, c * 128:(c + 1) * 128]
    for r in range(dil):
        for c in range(stage_ref.shape[0]):
            dst_ref[:, r * WIDTH + c * 128:r * WIDTH + (c + 1) * 128] = (
                stage_ref[c, pl.ds(r, n, stride=dil), :].astype(dst_ref.dtype))


def _qkv_body(x_ref, g_ref, w_ref, *refs, tm, n_tiles):
    q_refs, k_refs, v_refs = refs[0:3], refs[3:6], refs[6:9]
    pk_refs, pv_refs, stage_ref = refs[9:12], refs[12:15], refs[15]
    i = pl.program_id(1)
    h = _rms(x_ref[...], g_ref[...]).astype(BF16)

    def window_out(ref, z, g):
        keep = min(ATT_GROUPS[g][0], tm * n_tiles)
        if keep == tm * n_tiles:
            _store_heads(ref, z)
        else:
            @pl.when(i == n_tiles - 1)
            def _():
                _store_heads(ref, z[tm - keep:, :])

    for g, (_, dil) in enumerate(ATT_GROUPS):
        zq = _dot(h, w_ref[:, (COL_Q + g) * WIDTH:(COL_Q + g + 1) * WIDTH])
        _to_strided_view(q_refs[g], zq * QK_SCALE, stage_ref, dil)
        zk = _dot(h, w_ref[:, (COL_K + g) * WIDTH:(COL_K + g + 1) * WIDTH])
        _to_strided_view(k_refs[g], zk, stage_ref, dil)
        window_out(pk_refs[g], zk, g)
        zv = _dot(h, w_ref[:, (COL_V + g) * WIDTH:(COL_V + g + 1) * WIDTH])
        _to_strided_view(v_refs[g], zv, stage_ref, dil)
        window_out(pv_refs[g], zv, g)


def _qkv_proj(x, g_mix, w_qkv, *, tm):
    b, t, _ = x.shape
    n_tiles = t // tm
    view_specs = [pl.BlockSpec((None, tm // d, d * WIDTH), lambda bi, i: (bi, i, 0)) for _, d in ATT_GROUPS]
    view_shapes = [jax.ShapeDtypeStruct((b, t // d, d * WIDTH), BF16) for _, d in ATT_GROUPS]
    win_specs, win_shapes = [], []
    for win, _ in ATT_GROUPS:
        keep = min(win, t)
        assert keep == t or keep <= tm
        if keep == t:
            win_specs.append(pl.BlockSpec((None, None, tm, HEADS, HEAD_DIM), lambda bi, i: (0, bi, i, 0, 0)))
        else:
            win_specs.append(pl.BlockSpec((None, None, keep, HEADS, HEAD_DIM), lambda bi, i: (0, bi, 0, 0, 0)))
        win_shapes.append(jax.ShapeDtypeStruct((1, b, keep, HEADS, HEAD_DIM), F32))
    return pl.pallas_call(
        functools.partial(_qkv_body, tm=tm, n_tiles=n_tiles),
        grid=(b, n_tiles),
        in_specs=[pl.BlockSpec((None, tm, D_MODEL), lambda bi, i: (bi, i, 0)),
                  _const_spec((1, D_MODEL)),
                  _const_spec((D_MODEL, 3 * N_GROUPS * WIDTH))],
        out_specs=view_specs * 3 + win_specs + win_specs,
        out_shape=view_shapes * 3 + win_shapes + win_shapes,
        scratch_shapes=[pltpu.VMEM((WIDTH // 128, tm, 128), F32)],
        compiler_params=_params("parallel", "arbitrary"),
        name="qkv_proj",
    )(x, g_mix, w_qkv)


def _lower_bound(logits_ref):
    lg = logits_ref[...]
    e = jnp.exp(lg - jnp.max(lg, axis=0, keepdims=True))
    return e[0:1, :] / jnp.sum(e, axis=0, keepdims=True)


def _rest_body(x_ref, g_ref, w_ref, lb_ref, hq_ref, hk_ref, lf_ref, hv_ref, mq_ref, gate_ref):
    h = _rms(x_ref[...], g_ref[...]).astype(BF16)
    lb = _lower_bound(lb_ref)
    hq_ref[...] = jax.nn.sigmoid(_dot(h, w_ref[:, 0:WIDTH])).astype(BF16)
    kk = (1.0 - lb) * jax.nn.sigmoid(_dot(h, w_ref[:, WIDTH:2 * WIDTH]))
    hk_ref[...] = kk.astype(BF16)
    lf_ref[...] = jnp.log(lb + kk)
    hv_ref[...] = _dot(h, w_ref[:, 2 * WIDTH:3 * WIDTH]).astype(BF16)
    mq_ref[...] = (_dot(h, w_ref[:, 3 * WIDTH:4 * WIDTH]) * QK_SCALE).astype(BF16)
    for c in range(N_BRANCH * D_MODEL // WIDTH):
        z = _dot(h, w_ref[:, (4 + c) * WIDTH:(5 + c) * WIDTH])
        gate_ref[:, c * WIDTH:(c + 1) * WIDTH] = jax.nn.sigmoid(z).astype(BF16)


def _rest_proj(x, g_mix, w_rest, lb_logits, *, tm):
    m = x.shape[0]
    spec = pl.BlockSpec((tm, WIDTH), lambda i: (i, 0))
    bf = jax.ShapeDtypeStruct((m, WIDTH), BF16)
    return pl.pallas_call(
        _rest_body,
        grid=(m // tm,),
        in_specs=[pl.BlockSpec((tm, D_MODEL), lambda i: (i, 0)),
                  _const_spec((1, D_MODEL)),
                  _const_spec(w_rest.shape),
                  _const_spec(lb_logits.shape)],
        out_specs=[spec, spec, spec, spec, spec, pl.BlockSpec((tm, N_BRANCH * D_MODEL), lambda i: (i, 0))],
        out_shape=[bf, bf, jax.ShapeDtypeStruct((m, WIDTH), F32), bf, bf,
                   jax.ShapeDtypeStruct((m, N_BRANCH * D_MODEL), BF16)],
        compiler_params=_params("parallel"),
        name="rest_proj",
    )(x, g_mix, w_rest, lb_logits)


def _att_body(q_ref, ko_ref, kp_ref, vo_ref, vp_ref, o_ref, lse_ref):
    i = pl.program_id(2)
    qi = lax.broadcasted_iota(jnp.int32, (BLOCK, BLOCK), 0)
    ki = lax.broadcasted_iota(jnp.int32, (BLOCK, BLOCK), 1)
    own_ok = ki <= qi
    prev_ok = jnp.logical_and(ki >= qi, i > 0)
    for h in range(HEADS):
        sl = slice(h * HEAD_DIM, (h + 1) * HEAD_DIM)
        q = q_ref[:, sl]
        s_o = jnp.where(own_ok, _dot_nt(q, ko_ref[:, sl]), -jnp.inf)
        s_p = jnp.where(prev_ok, _dot_nt(q, kp_ref[:, sl]), -jnp.inf)
        m = jnp.maximum(jnp.max(s_o, axis=-1, keepdims=True), jnp.max(s_p, axis=-1, keepdims=True))
        p_o = jnp.exp(s_o - m)
        p_p = jnp.exp(s_p - m)
        l = jnp.sum(p_o, axis=-1, keepdims=True) + jnp.sum(p_p, axis=-1, keepdims=True)
        acc = _dot(p_o.astype(BF16), vo_ref[:, sl]) + _dot(p_p.astype(BF16), vp_ref[:, sl])
        o_ref[:, sl] = (acc / l).astype(BF16)
        lse_ref[:, h * LSE_REP:(h + 1) * LSE_REP] = jnp.broadcast_to(m + jnp.log(l), (BLOCK, LSE_REP))


def _att_group(q, k, v, dil):
    b, sub_len, _ = q.shape
    nb = sub_len // BLOCK
    own = pl.BlockSpec((None, BLOCK, WIDTH), lambda bi, r, i: (bi, i, r))
    prev = pl.BlockSpec((None, BLOCK, WIDTH), lambda bi, r, i: (bi, jnp.maximum(i - 1, 0), r))
    o, lse = pl.pallas_call(
        _att_body,
        grid=(b, dil, nb),
        in_specs=[own, own, prev, own, prev],
        out_specs=[own, pl.BlockSpec((None, BLOCK, HEADS * LSE_REP), lambda bi, r, i: (bi, i, r))],
        out_shape=[jax.ShapeDtypeStruct((b, sub_len, dil * WIDTH), BF16),
                   jax.ShapeDtypeStruct((b, sub_len, dil * HEADS * LSE_REP), F32)],
        compiler_params=_params("parallel", "parallel", "arbitrary"),
        name=f"att_dil{dil}",
    )(q, k, k, v, v)
    return o, lse


N_LEVELS = 7
ROW_CUM, ROW_REM = N_LEVELS, N_LEVELS + 1


def _hgrn_constants():
    r = np.arange(BLOCK)[:, None]
    s = np.arange(BLOCK)[None, :]
    mats = []
    for lvl in range(N_LEVELS):
        half = 1 << lvl
        mid = (r // (2 * half)) * 2 * half + half
        upper = (r % (2 * half)) >= half
        mats.append(np.where(upper, (s >= mid) & (s <= r), (s > r) & (s <= mid - 1)))
    mats.append(s <= r)
    mats.append(s > r)
    range_mat = np.concatenate(mats, axis=0).astype(np.float32)
    x = r ^ s
    level = np.where(s < r, np.floor(np.log2(np.maximum(x, 1))).astype(np.int32), np.where(s == r, N_LEVELS, -1))
    return jnp.asarray(range_mat, BF16), jnp.asarray(level, jnp.int32)


def _hgrn_body(q_ref, k_ref, lf_ref, v_ref, rng_ref, lvl_ref, gain_ref, o_ref, s_ref, st_ref, *, n_chunks):
    c = pl.program_id(1)

    @pl.when(c == 0)
    def _():
        st_ref[...] = jnp.zeros_like(st_ref)

    lf = lf_ref[...]
    lf_hi = lf.astype(BF16)
    lf_lo = (lf - lf_hi.astype(F32)).astype(BF16)
    rng = rng_ref[...]
    z = _dot(rng, lf_hi) + _dot(rng, lf_lo)
    q = q_ref[...].astype(F32)
    kk = k_ref[...].astype(F32)
    lhs, rhs = [], []
    for lvl in range(N_LEVELS):
        x = jnp.exp(z[lvl * BLOCK:(lvl + 1) * BLOCK])
        lhs.append((q * x).astype(BF16))
        rhs.append((kk * x).astype(BF16))
    lhs.append(q_ref[...])
    rhs.append(k_ref[...])
    b_cum = z[ROW_CUM * BLOCK:(ROW_CUM + 1) * BLOCK]
    q_in = (q * jnp.exp(b_cum)).astype(BF16)
    k_out = (kk * jnp.exp(z[ROW_REM * BLOCK:(ROW_REM + 1) * BLOCK])).astype(BF16)
    decay = jnp.exp(b_cum[BLOCK - 1:BLOCK, :])
    level = lvl_ref[...]
    gain = gain_ref[...]
    for h in range(HEADS):
        sl = slice(h * HEAD_DIM, (h + 1) * HEAD_DIM)
        a = jnp.zeros((BLOCK, BLOCK), F32)
        for lvl in range(N_LEVELS + 1):
            a = jnp.where(level == lvl, _dot_nt(lhs[lvl][:, sl], rhs[lvl][:, sl]), a)
        v = v_ref[:, sl]
        st = st_ref[h]
        o = _dot(a.astype(BF16), v) + _dot_nt(q_in[:, sl], st.astype(BF16))
        o_ref[:, sl] = _rms(o, gain[:, sl]).astype(BF16)
        v_t = v.astype(F32).T.astype(BF16)
        st_new = st * decay[:, sl] + _dot(v_t, k_out[:, sl])
        st_ref[h] = st_new

        @pl.when(c == n_chunks - 1)
        def _():
            s_ref[h] = st_new.T


def _hgrn_prompt(hq, hk, lf, hv, gain, b, t):
    n_chunks = t // BLOCK
    range_mat, level = _hgrn_constants()
    spec = pl.BlockSpec((BLOCK, WIDTH), lambda bi, c: (bi * n_chunks + c, 0))
    return pl.pallas_call(
        functools.partial(_hgrn_body, n_chunks=n_chunks),
        grid=(b, n_chunks),
        in_specs=[spec, spec, spec, spec, _const_spec(range_mat.shape), _const_spec(level.shape),
                  _const_spec((1, WIDTH))],
        out_specs=[spec, pl.BlockSpec((None, None, HEADS, HEAD_DIM, HEAD_DIM), lambda bi, c: (0, bi, 0, 0, 0))],
        out_shape=[jax.ShapeDtypeStruct((b * t, WIDTH), BF16),
                   jax.ShapeDtypeStruct((1, b, HEADS, HEAD_DIM, HEAD_DIM), F32)],
        scratch_shapes=[pltpu.VMEM((HEADS, HEAD_DIM, HEAD_DIM), F32)],
        compiler_params=_params("parallel", "arbitrary"),
        name="hgrn_prompt",
    )(hq, hk, lf, hv, range_mat, level, gain)


def _memkv_body(m_ref, g_ref, wk_ref, wv_ref, pk_ref, pv_ref, kb_ref, vb_ref):
    h = _rms(m_ref[...], g_ref[...]).astype(BF16)
    k = _dot(h, wk_ref[...])
    v = _dot(h, wv_ref[...])
    _store_heads(pk_ref, k)
    _store_heads(pv_ref, v)
    kb_ref[...] = k.astype(BF16)
    vb_ref[...] = v.astype(BF16)


def _mem_kv(mem, g_mem, w_mk, w_mv):
    b = mem.shape[0]
    head_spec = pl.BlockSpec((None, None, MEM_LEN, HEADS, HEAD_DIM), lambda bi: (0, bi, 0, 0, 0))
    flat_spec = pl.BlockSpec((None, MEM_LEN, WIDTH), lambda bi: (bi, 0, 0))
    head_shape = jax.ShapeDtypeStruct((1, b, MEM_LEN, HEADS, HEAD_DIM), F32)
    flat_shape = jax.ShapeDtypeStruct((b, MEM_LEN, WIDTH), BF16)
    return pl.pallas_call(
        _memkv_body,
        grid=(b,),
        in_specs=[pl.BlockSpec((None, MEM_LEN, D_MODEL), lambda bi: (bi, 0, 0)),
                  _const_spec((1, D_MODEL)), _const_spec((D_MODEL, WIDTH)), _const_spec((D_MODEL, WIDTH))],
        out_specs=[head_spec, head_spec, flat_spec, flat_spec],
        out_shape=[head_shape, head_shape, flat_shape, flat_shape],
        compiler_params=_params("parallel"),
        name="mem_kv",
    )(mem, g_mem, w_mk, w_mv)


def _from_strided_view(src_ref, stage_ref, dil, width):
    if dil == 1:
        return src_ref[...].astype(F32)
    n = src_ref.shape[0]
    n_col = width // 128
    for r in range(dil):
        for c in range(n_col):
            stage_ref[c, pl.ds(r, n, stride=dil), :] = (
                src_ref[:, r * width + c * 128:r * width + (c + 1) * 128].astype(F32))
    return jnp.concatenate([stage_ref[c] for c in range(n_col)], axis=-1)


def _combine_body(o0_ref, o1_ref, o2_ref, l0_ref, l1_ref, l2_ref, mq_ref, mk_ref, mv_ref, att_ref, mem_ref,
                  *stage_refs):
    outs = [_from_strided_view(r, stage_refs[2 * g], ATT_GROUPS[g][1], WIDTH)
            for g, r in enumerate((o0_ref, o1_ref, o2_ref))]
    lses = [_from_strided_view(r, stage_refs[2 * g + 1], ATT_GROUPS[g][1], HEADS * LSE_REP)
            for g, r in enumerate((l0_ref, l1_ref, l2_ref))]
    for h in range(HEADS):
        sl = slice(h * HEAD_DIM, (h + 1) * HEAD_DIM)
        lse = [x[:, h * LSE_REP:h * LSE_REP + 1] for x in lses]
        m = jnp.maximum(jnp.maximum(lse[0], lse[1]), lse[2])
        e = [jnp.exp(x - m) for x in lse]
        inv = 1.0 / (e[0] + e[1] + e[2])
        att = sum((e[g] * inv) * outs[g][:, sl] for g in range(N_GROUPS))
        att_ref[:, sl] = att.astype(BF16)
        s = _dot_nt(mq_ref[:, sl], mk_ref[:, sl])
        p = jnp.exp(s - jnp.max(s, axis=-1, keepdims=True))
        o = _dot(p.astype(BF16), mv_ref[:, sl]) / jnp.sum(p, axis=-1, keepdims=True)
        mem_ref[:, sl] = o.astype(BF16)


def _combine(outs, lses, mq, mk, mv, b, t, *, tm):
    n_tiles = t // tm
    lw = HEADS * LSE_REP
    row = pl.BlockSpec((tm, WIDTH), lambda bi, i: (bi * n_tiles + i, 0))
    o_specs = [pl.BlockSpec((None, tm // d, d * WIDTH), lambda bi, i: (bi, i, 0)) for _, d in ATT_GROUPS]
    l_specs = [pl.BlockSpec((None, tm // d, d * lw), lambda bi, i: (bi, i, 0)) for _, d in ATT_GROUPS]
    memspec = pl.BlockSpec((None, MEM_LEN, WIDTH), lambda bi, i: (bi, 0, 0))
    shape = jax.ShapeDtypeStruct((b * t, WIDTH), BF16)
    stages = []
    for _ in ATT_GROUPS:
        stages += [pltpu.VMEM((WIDTH // 128, tm, 128), F32), pltpu.VMEM((lw // 128, tm, 128), F32)]
    return pl.pallas_call(
        _combine_body,
        grid=(b, n_tiles),
        in_specs=o_specs + l_specs + [row, memspec, memspec],
        out_specs=[row, row],
        out_shape=[shape, shape],
        scratch_shapes=stages,
        compiler_params=_params("parallel", "parallel"),
        name="combine",
    )(*outs, *lses, mq, mk, mv)


def _merge_body(x_ref, att_ref, hg_ref, mem_ref, gate_ref, wa_ref, wb_ref, wc_ref, wo_ref, o_ref):
    def gate(j):
        return gate_ref[:, j * D_MODEL:(j + 1) * D_MODEL].astype(F32)

    m = (gate(0) * _dot(att_ref[...].astype(BF16), wa_ref[...])
         + gate(1) * _dot(hg_ref[...].astype(BF16), wb_ref[...])
         + gate(2) * _dot(mem_ref[...].astype(BF16), wc_ref[...]))
    o_ref[...] = x_ref[...] + _dot(m.astype(BF16), wo_ref[...])


def _merge(x, att, hg, mem, gates, wa, wb, wc, wo, *, tm):
    m = x.shape[0]
    row = pl.BlockSpec((tm, WIDTH), lambda i: (i, 0))
    wide = pl.BlockSpec((tm, D_MODEL), lambda i: (i, 0))
    return pl.pallas_call(
        _merge_body,
        grid=(m // tm,),
        in_specs=[wide, row, row, row, pl.BlockSpec((tm, N_BRANCH * D_MODEL), lambda i: (i, 0)),
                  _const_spec((WIDTH, D_MODEL)), _const_spec((WIDTH, D_MODEL)), _const_spec((WIDTH, D_MODEL)),
                  _const_spec((D_MODEL, D_MODEL))],
        out_specs=wide,
        out_shape=jax.ShapeDtypeStruct((m, D_MODEL), F32),
        compiler_params=_params("parallel"),
        name="merge",
    )(x, att, hg, mem, gates, wa, wb, wc, wo)


def _norm_matmul_body(x_ref, g_ref, w_ref, o_ref):
    o_ref[...] = _dot(_rms(x_ref[...], g_ref[...]).astype(BF16), w_ref[...])


def _sample_in_proj(x, g_mix, w_in):
    m = x.shape[0]
    n = w_in.shape[1]
    return pl.pallas_call(
        _norm_matmul_body,
        grid=(n // WIDTH,),
        in_specs=[_const_spec((m, D_MODEL)), _const_spec((1, D_MODEL)),
                  pl.BlockSpec((D_MODEL, WIDTH), lambda j: (0, j))],
        out_specs=pl.BlockSpec((m, WIDTH), lambda j: (0, j)),
        out_shape=jax.ShapeDtypeStruct((m, n), F32),
        compiler_params=_params("parallel"),
        name="sample_in_proj",
    )(x, g_mix, w_in)


def _one_query_attend(q, k_rows, v_rows, k_new=None, v_new=None):
    s = jnp.sum(k_rows * q, axis=-1, keepdims=True)
    m = jnp.max(s, axis=0, keepdims=True)
    if k_new is not None:
        s_new = jnp.sum(k_new * q, axis=-1, keepdims=True)
        m = jnp.maximum(m, s_new)
    p = jnp.exp(s - m)
    l = jnp.sum(p, axis=0, keepdims=True)
    acc = jnp.sum(p * v_rows, axis=0, keepdims=True)
    if k_new is not None:
        p_new = jnp.exp(s_new - m)
        l = l + p_new
        acc = acc + p_new * v_new
    return acc / l, m + jnp.log(l)


def _sample_att_body(q_ref, kn_ref, vn_ref, mq_ref, k1_ref, v1_ref, k4_ref, v4_ref, k16_ref, v16_ref,
                     mk_ref, mv_ref, att_ref, mem_ref):
    k_refs = (k1_ref, k4_ref, k16_ref)
    v_refs = (v1_ref, v4_ref, v16_ref)
    for h in range(HEADS):
        sl = slice(h * HEAD_DIM, (h + 1) * HEAD_DIM)
        outs, lses = [], []
        for g in range(N_GROUPS):
            gs = slice(g * WIDTH + h * HEAD_DIM, g * WIDTH + (h + 1) * HEAD_DIM)
            o, lse = _one_query_attend(q_ref[:, gs] * QK_SCALE, k_refs[g][:, h, :], v_refs[g][:, h, :],
                                       kn_ref[:, gs], vn_ref[:, gs])
            outs.append(o)
            lses.append(lse)
        m = jnp.maximum(jnp.maximum(lses[0], lses[1]), lses[2])
        e = [jnp.exp(x - m) for x in lses]
        inv = 1.0 / (e[0] + e[1] + e[2])
        att_ref[:, sl] = sum((e[g] * inv) * outs[g] for g in range(N_GROUPS))
        o, _ = _one_query_attend(mq_ref[:, sl] * QK_SCALE, mk_ref[:, h, :], mv_ref[:, h, :])
        mem_ref[:, sl] = o


def _sample_attend(z3, caches_k, caches_v, cache_mk, cache_mv):
    bd = z3.shape[0]
    zspec = lambda col, n: pl.BlockSpec((None, 1, n * WIDTH), lambda b: (b, 0, col // n))
    cache_specs, cache_args = [], []
    for (win, dil), ck, cv in zip(ATT_GROUPS, caches_k, caches_v):
        rows = ck.shape[2]
        assert rows == win and rows // dil == BLOCK
        spec = pl.BlockSpec((None, None, BLOCK, None, HEADS, HEAD_DIM), lambda b: (0, b, 0, 0, 0, 0))
        for c in (ck, cv):
            cache_specs.append(spec)
            cache_args.append(c.reshape(1, bd, BLOCK, dil, HEADS, HEAD_DIM))
    mem_spec = pl.BlockSpec((None, None, MEM_LEN, HEADS, HEAD_DIM), lambda b: (0, b, 0, 0, 0))
    out_spec = pl.BlockSpec((None, 1, WIDTH), lambda b: (b, 0, 0))
    out_shape = jax.ShapeDtypeStruct((bd, 1, WIDTH), F32)
    return pl.pallas_call(
        _sample_att_body,
        grid=(bd,),
        in_specs=[zspec(COL_Q, 3), zspec(COL_K, 3), zspec(COL_V, 3), zspec(COL_MQ, 1)] + cache_specs
                 + [mem_spec, mem_spec],
        out_specs=[out_spec, out_spec],
        out_shape=[out_shape, out_shape],
        compiler_params=_params("parallel"),
        name="sample_attend",
    )(z3, z3, z3, z3, *cache_args, cache_mk, cache_mv)


SHIFT_ROWS = 64


def _shift_rows(cache_ref, out_ref, new_row):
    rows = cache_ref.shape[0]
    n_full, rem = divmod(rows - 1, SHIFT_ROWS)

    def body(c, carry):
        start = pl.multiple_of(c * SHIFT_ROWS, SHIFT_ROWS)
        out_ref[pl.ds(start, SHIFT_ROWS)] = cache_ref[pl.ds(start + 1, SHIFT_ROWS)]
        return carry

    lax.fori_loop(0, n_full, body, 0)
    if rem:
        out_ref[pl.ds(n_full * SHIFT_ROWS, rem)] = cache_ref[pl.ds(n_full * SHIFT_ROWS + 1, rem)]
    for h in range(HEADS):
        out_ref[pl.ds(rows - 1, 1), h, :] = new_row[:, h * HEAD_DIM:(h + 1) * HEAD_DIM]


def _cache_shift_body(kn_ref, vn_ref, *refs):
    n = len(refs) // 2
    for j in range(n):
        g, is_v = divmod(j, 2)
        new = (vn_ref if is_v else kn_ref)[:, g * WIDTH:(g + 1) * WIDTH]
        _shift_rows(refs[j], refs[n + j], new)


def _cache_shift(z3, caches):
    bd = z3.shape[0]
    zspec = lambda col: pl.BlockSpec((None, 1, N_GROUPS * WIDTH), lambda b: (b, 0, col // N_GROUPS))
    specs = [pl.BlockSpec((None, None, c.shape[2], HEADS, HEAD_DIM), lambda b: (0, b, 0, 0, 0)) for c in caches]
    return pl.pallas_call(
        _cache_shift_body,
        grid=(bd,),
        in_specs=[zspec(COL_K), zspec(COL_V)] + specs,
        out_specs=specs,
        out_shape=[jax.ShapeDtypeStruct(c.shape, c.dtype) for c in caches],
        compiler_params=_params("parallel"),
        name="cache_shift",
    )(z3, z3, *caches)


SAMPLE_GROUP = 8


def _sample_hgrn_body(zq_ref, zf_ref, zi_ref, lb_ref, gain_ref, s_ref, o_ref, so_ref):
    lb = _lower_bound(lb_ref)
    gain = gain_ref[...]
    q = jax.nn.sigmoid(zq_ref[...])
    kk = (1.0 - lb) * jax.nn.sigmoid(zf_ref[...])
    f = lb + kk
    v = zi_ref[...]
    pad = jnp.zeros((HEAD_DIM - SAMPLE_GROUP, HEAD_DIM), F32)

    def columns(a, h):
        return jnp.concatenate([a[:, h * HEAD_DIM:(h + 1) * HEAD_DIM], pad], axis=0).T

    for h in range(HEADS):
        sl = slice(h * HEAD_DIM, (h + 1) * HEAD_DIM)
        q_t, k_t, f_t = columns(q, h), columns(kk, h), columns(f, h)
        for j in range(SAMPLE_GROUP):
            s1 = f_t[:, j:j + 1] * s_ref[j, h] + k_t[:, j:j + 1] * v[j:j + 1, sl]
            so_ref[j, h] = s1
            o = jnp.sum(q_t[:, j:j + 1] * s1, axis=0, keepdims=True)
            o_ref[j:j + 1, sl] = _rms(o, gain[:, sl])


def _sample_hgrn(z, lb_logits, gain, state):
    bd = z.shape[0]
    zspec = lambda col: pl.BlockSpec((SAMPLE_GROUP, WIDTH), lambda i: (i, col))
    sspec = pl.BlockSpec((None, SAMPLE_GROUP, HEADS, HEAD_DIM, HEAD_DIM), lambda i: (0, i, 0, 0, 0))
    return pl.pallas_call(
        _sample_hgrn_body,
        grid=(bd // SAMPLE_GROUP,),
        in_specs=[zspec(COL_HQ), zspec(COL_HF), zspec(COL_HI), _const_spec(lb_logits.shape),
                  _const_spec((1, WIDTH)), sspec],
        out_specs=[pl.BlockSpec((SAMPLE_GROUP, WIDTH), lambda i: (i, 0)), sspec],
        out_shape=[jax.ShapeDtypeStruct((bd, WIDTH), F32), jax.ShapeDtypeStruct(state.shape, F32)],
        compiler_params=_params("parallel"),
        name="sample_hgrn",
    )(z, z, z, lb_logits, gain, state)


def _gate_body(z_ref, o_ref):
    o_ref[...] = jax.nn.sigmoid(z_ref[...]).astype(BF16)


def _sample_gates(z):
    bd = z.shape[0]
    n = N_BRANCH * D_MODEL
    return pl.pallas_call(
        _gate_body,
        grid=(n // WIDTH,),
        in_specs=[pl.BlockSpec((bd, WIDTH), lambda j: (0, COL_GATE + j))],
        out_specs=pl.BlockSpec((bd, WIDTH), lambda j: (0, j)),
        out_shape=jax.ShapeDtypeStruct((bd, n), BF16),
        name="sample_gates",
    )(z)


def kernel(x_prompt, x_sample, mem_prompt, cache_win1_k, cache_win1_v, cache_win4_k, cache_win4_v, cache_win16_k, cache_win16_v, cache_mem_k, cache_mem_v, state_hgrn, g_ff1, w_ff1_gate, w_ff1_up, w_ff1_down, g_mix, w_in, hg_lb_logits, g_hg_out, g_mem, w_mem_k, w_mem_v, w_branch_att, w_branch_hg, w_branch_mem, w_out, g_ff2, w_ff2_gate, w_ff2_up, w_ff2_down, g_final):
    b, t, _ = x_prompt.shape
    bd = x_sample.shape[0]
    assert g_ff1.shape[0] == 1 and x_sample.shape[1] == 1 and w_in.shape[2] == IN_WIDTH
    bf = lambda w: w[0].astype(BF16)
    ffn1 = (g_ff1, bf(w_ff1_gate), bf(w_ff1_up), bf(w_ff1_down), g_final.reshape(1, D_MODEL))
    ffn2 = (g_ff2, bf(w_ff2_gate), bf(w_ff2_up), bf(w_ff2_down), g_final.reshape(1, D_MODEL))
    w_in_b = bf(w_in)
    w_qkv, w_rest = w_in_b[:, :COL_HQ * WIDTH], w_in_b[:, COL_HQ * WIDTH:]
    branch_w = (bf(w_branch_att), bf(w_branch_hg), bf(w_branch_mem), bf(w_out))

    xs = _ffn(x_sample.reshape(bd, D_MODEL), *ffn1, tm=bd, final=False)
    zs = _sample_in_proj(xs, g_mix, w_in_b)
    caches_k = (cache_win1_k, cache_win4_k, cache_win16_k)
    caches_v = (cache_win1_v, cache_win4_v, cache_win16_v)
    z3 = zs.reshape(bd, 1, IN_WIDTH)
    att_s, mem_s = _sample_attend(z3, caches_k, caches_v, cache_mem_k, cache_mem_v)
    s_wins = _cache_shift(z3, [c for pair in zip(caches_k, caches_v) for c in pair])
    hg_s, s_hgrn = _sample_hgrn(zs, hg_lb_logits, g_hg_out, state_hgrn)
    xs = _merge(xs, att_s.reshape(bd, WIDTH), hg_s, mem_s.reshape(bd, WIDTH), _sample_gates(zs), *branch_w, tm=bd)
    y_sample = _ffn(xs, *ffn2, tm=bd, final=True).reshape(bd, 1, D_MODEL)

    xp = _ffn(x_prompt.reshape(b * t, D_MODEL), *ffn1, tm=512, final=False)
    qkv = _qkv_proj(xp.reshape(b, t, D_MODEL), g_mix, w_qkv, tm=512)
    q_g, k_g, v_g, p_k, p_v = qkv[0:3], qkv[3:6], qkv[6:9], qkv[9:12], qkv[12:15]
    hq, hk, lf, hv, mq, gates = _rest_proj(xp, g_mix, w_rest, hg_lb_logits, tm=512)
    outs, lses = [], []
    for g, (_, dil) in enumerate(ATT_GROUPS):
        o, lse = _att_group(q_g[g], k_g[g], v_g[g], dil)
        outs.append(o)
        lses.append(lse)
    hg_p, p_hgrn = _hgrn_prompt(hq, hk, lf, hv, g_hg_out, b, t)
    p_mem_k, p_mem_v, mk_b, mv_b = _mem_kv(mem_prompt, g_mem, bf(w_mem_k), bf(w_mem_v))
    att_p, mem_p = _combine(outs, lses, mq, mk_b, mv_b, b, t, tm=512)
    xp = _merge(xp, att_p, hg_p, mem_p, gates, *branch_w, tm=512)
    y_prompt = _ffn(xp, *ffn2, tm=512, final=True).reshape(b, t, D_MODEL)

    return (y_prompt, y_sample,
            p_k[0], p_v[0], p_k[1], p_v[1], p_k[2], p_v[2],
            p_mem_k, p_mem_v, p_hgrn,
            s_wins[0], s_wins[1], s_wins[2], s_wins[3], s_wins[4], s_wins[5],
            s_hgrn)
```

```python
import functools

import numpy as np
import jax
import jax.numpy as jnp
from jax import lax
from jax.experimental import pallas as pl
from jax.experimental.pallas import tpu as pltpu

F32 = jnp.float32
BF16 = jnp.bfloat16

D_MODEL = 1024
D_FF = 2816
EPS = 1e-6
HEADS = 4
HEAD_DIM = 128
WIDTH = HEADS * HEAD_DIM
ATT_GROUPS = ((128, 1), (512, 4), (2048, 16))
N_GROUPS = 3
BLOCK = 128
MEM_LEN = 256
N_BRANCH = 3
LSE_REP = 32
QK_SCALE = HEAD_DIM ** -0.5
VMEM_LIMIT_BYTES = 56 * 1024 * 1024

COL_Q, COL_K, COL_V, COL_HQ, COL_HF, COL_HI, COL_MQ, COL_GATE = 0, 3, 6, 9, 10, 11, 12, 13
IN_WIDTH = (COL_GATE + N_BRANCH * D_MODEL // WIDTH) * WIDTH


def _params(*sem):
    return pltpu.CompilerParams(dimension_semantics=sem, vmem_limit_bytes=VMEM_LIMIT_BYTES)


def _const_spec(shape):
    nd = len(shape)
    return pl.BlockSpec(shape, lambda *_: (0,) * nd, pipeline_mode=pl.Buffered(1))


def _rms(x, g):
    return x * lax.rsqrt(jnp.mean(x * x, axis=-1, keepdims=True) + EPS) * g


def _dot(a, b):
    return jnp.dot(a, b, preferred_element_type=F32)


def _dot_nt(a, b):
    return lax.dot_general(a, b, (((1,), (1,)), ((), ())), preferred_element_type=F32)


def _ffn_body(x_ref, g_ref, wg_ref, wu_ref, wd_ref, gf_ref, o_ref, *, final):
    x = x_ref[...]
    h = _rms(x, g_ref[...]).astype(BF16)
    a = _dot(h, wg_ref[...])
    u = _dot(h, wu_ref[...])
    act = (a * jax.nn.sigmoid(a) * u).astype(BF16)
    y = x + 0.5 * _dot(act, wd_ref[...])
    if final:
        y = _rms(y, gf_ref[...])
    o_ref[...] = y


def _ffn(x, g, wg, wu, wd, g_final, *, tm, final):
    m = x.shape[0]
    return pl.pallas_call(
        functools.partial(_ffn_body, final=final),
        grid=(m // tm,),
        in_specs=[pl.BlockSpec((tm, D_MODEL), lambda i: (i, 0)),
                  _const_spec((1, D_MODEL)),
                  _const_spec((D_MODEL, D_FF)), _const_spec((D_MODEL, D_FF)), _const_spec((D_FF, D_MODEL)),
                  _const_spec((1, D_MODEL))],
        out_specs=pl.BlockSpec((tm, D_MODEL), lambda i: (i, 0)),
        out_shape=jax.ShapeDtypeStruct((m, D_MODEL), F32),
        compiler_params=_params("parallel"),
        name="ffn_final" if final else "ffn",
    )(x, g, wg, wu, wd, g_final)


def _store_heads(ref, z):
    for h in range(HEADS):
        ref[:, h, :] = z[:, h * HEAD_DIM:(h + 1) * HEAD_DIM]


def _to_strided_view(dst_ref, z, stage_ref, dil):
    if dil == 1:
        dst_ref[...] = z.astype(dst_ref.dtype)
        return
    n = z.shape[0] // dil
    for c in range(stage_ref.shape[0]):
        stage_ref[c] = z[:, c * 128:(c + 1) * 128]
    for r in range(dil):
        for c in range(stage_ref.shape[0]):
            dst_ref[:, r * WIDTH + c * 128:r * WIDTH + (c + 1) * 128] = (
                stage_ref[c, pl.ds(r, n, stride=dil), :].astype(dst_ref.dtype))


def _qkv_body(x_ref, g_ref, w_ref, *refs, tm, n_tiles):
    q_refs, k_refs, v_refs = refs[0:3], refs[3:6], refs[6:9]
    pk_refs, pv_refs, stage_ref = refs[9:12], refs[12:15], refs[15]
    i = pl.program_id(1)
    h = _rms(x_ref[...], g_ref[...]).astype(BF16)

    def window_out(ref, z, g):
        keep = min(ATT_GROUPS[g][0], tm * n_tiles)
        if keep == tm * n_tiles:
            _store_heads(ref, z)
        else:
            @pl.when(i == n_tiles - 1)
            def _():
                _store_heads(ref, z[tm - keep:, :])

    for g, (_, dil) in enumerate(ATT_GROUPS):
        zq = _dot(h, w_ref[:, (COL_Q + g) * WIDTH:(COL_Q + g + 1) * WIDTH])
        _to_strided_view(q_refs[g], zq * QK_SCALE, stage_ref, dil)
        zk = _dot(h, w_ref[:, (COL_K + g) * WIDTH:(COL_K + g + 1) * WIDTH])
        _to_strided_view(k_refs[g], zk, stage_ref, dil)
        window_out(pk_refs[g], zk, g)
        zv = _dot(h, w_ref[:, (COL_V + g) * WIDTH:(COL_V + g + 1) * WIDTH])
        _to_strided_view(v_refs[g], zv, stage_ref, dil)
        window_out(pv_refs[g], zv, g)


def _qkv_proj(x, g_mix, w_qkv, *, tm):
    b, t, _ = x.shape
    n_tiles = t // tm
    view_specs = [pl.BlockSpec((None, tm // d, d * WIDTH), lambda bi, i: (bi, i, 0)) for _, d in ATT_GROUPS]
    view_shapes = [jax.ShapeDtypeStruct((b, t // d, d * WIDTH), BF16) for _, d in ATT_GROUPS]
    win_specs, win_shapes = [], []
    for win, _ in ATT_GROUPS:
        keep = min(win, t)
        assert keep == t or keep <= tm
        if keep == t:
            win_specs.append(pl.BlockSpec((None, None, tm, HEADS, HEAD_DIM), lambda bi, i: (0, bi, i, 0, 0)))
        else:
            win_specs.append(pl.BlockSpec((None, None, keep, HEADS, HEAD_DIM), lambda bi, i: (0, bi, 0, 0, 0)))
        win_shapes.append(jax.ShapeDtypeStruct((1, b, keep, HEADS, HEAD_DIM), F32))
    return pl.pallas_call(
        functools.partial(_qkv_body, tm=tm, n_tiles=n_tiles),
        grid=(b, n_tiles),
        in_specs=[pl.BlockSpec((None, tm, D_MODEL), lambda bi, i: (bi, i, 0)),
                  _const_spec((1, D_MODEL)),
                  _const_spec((D_MODEL, 3 * N_GROUPS * WIDTH))],
        out_specs=view_specs * 3 + win_specs + win_specs,
        out_shape=view_shapes * 3 + win_shapes + win_shapes,
        scratch_shapes=[pltpu.VMEM((WIDTH // 128, tm, 128), F32)],
        compiler_params=_params("parallel", "arbitrary"),
        name="qkv_proj",
    )(x, g_mix, w_qkv)


def _lower_bound(logits_ref):
    lg = logits_ref[...]
    e = jnp.exp(lg - jnp.max(lg, axis=0, keepdims=True))
    return e[0:1, :] / jnp.sum(e, axis=0, keepdims=True)


def _rest_body(x_ref, g_ref, w_ref, lb_ref, hq_ref, hk_ref, lf_ref, hv_ref, mq_ref, gate_ref):
    h = _rms(x_ref[...], g_ref[...]).astype(BF16)
    lb = _lower_bound(lb_ref)
    hq_ref[...] = jax.nn.sigmoid(_dot(h, w_ref[:, 0:WIDTH])).astype(BF16)
    kk = (1.0 - lb) * jax.nn.sigmoid(_dot(h, w_ref[:, WIDTH:2 * WIDTH]))
    hk_ref[...] = kk.astype(BF16)
    lf_ref[...] = jnp.log(lb + kk)
    hv_ref[...] = _dot(h, w_ref[:, 2 * WIDTH:3 * WIDTH]).astype(BF16)
    mq_ref[...] = (_dot(h, w_ref[:, 3 * WIDTH:4 * WIDTH]) * QK_SCALE).astype(BF16)
    for c in range(N_BRANCH * D_MODEL // WIDTH):
        z = _dot(h, w_ref[:, (4 + c) * WIDTH:(5 + c) * WIDTH])
        gate_ref[:, c * WIDTH:(c + 1) * WIDTH] = jax.nn.sigmoid(z).astype(BF16)


def _rest_proj(x, g_mix, w_rest, lb_logits, *, tm):
    m = x.shape[0]
    spec = pl.BlockSpec((tm, WIDTH), lambda i: (i, 0))
    bf = jax.ShapeDtypeStruct((m, WIDTH), BF16)
    return pl.pallas_call(
        _rest_body,
        grid=(m // tm,),
        in_specs=[pl.BlockSpec((tm, D_MODEL), lambda i: (i, 0)),
                  _const_spec((1, D_MODEL)),
                  _const_spec(w_rest.shape),
                  _const_spec(lb_logits.shape)],
        out_specs=[spec, spec, spec, spec, spec, pl.BlockSpec((tm, N_BRANCH * D_MODEL), lambda i: (i, 0))],
        out_shape=[bf, bf, jax.ShapeDtypeStruct((m, WIDTH), F32), bf, bf,
                   jax.ShapeDtypeStruct((m, N_BRANCH * D_MODEL), BF16)],
        compiler_params=_params("parallel"),
        name="rest_proj",
    )(x, g_mix, w_rest, lb_logits)


ATT_UNITS = 4


def _att_body(q_ref, k_ref, v_ref, *rest, by_rows, has_halo):
    if has_halo:
        kh_ref, vh_ref, o_ref, lse_ref = rest
        halo_ok = pl.program_id(1) > 0
    else:
        o_ref, lse_ref = rest
    qi = lax.broadcasted_iota(jnp.int32, (BLOCK, BLOCK), 0)
    ki = lax.broadcasted_iota(jnp.int32, (BLOCK, BLOCK), 1)
    own_ok = ki <= qi
    prev_ok = ki >= qi

    units = []
    for u in range(ATT_UNITS):
        for h in range(HEADS):
            if by_rows:
                rows = slice(u * BLOCK, (u + 1) * BLOCK)
                cols = slice(h * HEAD_DIM, (h + 1) * HEAD_DIM)
                lcols = slice(h * LSE_REP, (h + 1) * LSE_REP)
                if u > 0:
                    before = slice((u - 1) * BLOCK, u * BLOCK)
                    prev = (k_ref, v_ref, before, prev_ok)
                elif has_halo:
                    prev = (kh_ref, vh_ref, slice(0, BLOCK), jnp.logical_and(prev_ok, halo_ok))
                else:
                    prev = None
            else:
                rows = slice(0, BLOCK)
                cols = slice(u * WIDTH + h * HEAD_DIM, u * WIDTH + (h + 1) * HEAD_DIM)
                lcols = slice((u * HEADS + h) * LSE_REP, (u * HEADS + h + 1) * LSE_REP)
                prev = None
            units.append((rows, cols, lcols, prev))

    scores = []
    for rows, cols, _, prev in units:
        q = q_ref[rows, cols]
        s_o = jnp.where(own_ok, _dot_nt(q, k_ref[rows, cols]), -jnp.inf)
        s_p = None
        if prev is not None:
            kp_ref, _, prows, ok = prev
            s_p = jnp.where(ok, _dot_nt(q, kp_ref[prows, cols]), -jnp.inf)
        scores.append((s_o, s_p))

    probs = []
    for s_o, s_p in scores:
        m = jnp.max(s_o, axis=-1, keepdims=True)
        if s_p is not None:
            m = jnp.maximum(m, jnp.max(s_p, axis=-1, keepdims=True))
        p_o = jnp.exp(s_o - m)
        l = jnp.sum(p_o, axis=-1, keepdims=True)
        p_p = None
        if s_p is not None:
            p_p = jnp.exp(s_p - m)
            l = l + jnp.sum(p_p, axis=-1, keepdims=True)
            p_p = p_p.astype(BF16)
        probs.append((p_o.astype(BF16), p_p, 1.0 / l, m + jnp.log(l)))

    for (rows, cols, lcols, prev), (p_o, p_p, inv_l, lse) in zip(units, probs):
        acc = _dot(p_o, v_ref[rows, cols])
        if prev is not None:
            _, vp_ref, prows, _ = prev
            acc = acc + _dot(p_p, vp_ref[prows, cols])
        o_ref[rows, cols] = (acc * inv_l).astype(BF16)
        lse_ref[rows, lcols] = jnp.broadcast_to(lse, (BLOCK, LSE_REP))


def _att_group(q, k, v, dil):
    b, sub_len, _ = q.shape
    lw = HEADS * LSE_REP
    span = ATT_UNITS * BLOCK
    by_rows = sub_len >= span
    has_halo = sub_len > span
    if by_rows:
        assert sub_len % span == 0
        n = (sub_len // span) * dil
        per_seq = sub_len // span
        tile = lambda w: pl.BlockSpec((None, span, w), lambda bi, j: (bi, j % per_seq, j // per_seq))
        halo = pl.BlockSpec((None, BLOCK, WIDTH),
                            lambda bi, j: (bi, jnp.maximum((j % per_seq) * ATT_UNITS - 1, 0), j // per_seq))
        assert not has_halo or dil == 1
    else:
        assert sub_len == BLOCK and dil % ATT_UNITS == 0
        n = dil // ATT_UNITS
        tile = lambda w: pl.BlockSpec((None, BLOCK, ATT_UNITS * w), lambda bi, j: (bi, 0, j))
    in_specs = [tile(WIDTH)] * 3 + ([halo, halo] if has_halo else [])
    args = (q, k, v) + ((k, v) if has_halo else ())
    return pl.pallas_call(
        functools.partial(_att_body, by_rows=by_rows, has_halo=has_halo),
        grid=(b, n),
        in_specs=in_specs,
        out_specs=[tile(WIDTH), tile(lw)],
        out_shape=[jax.ShapeDtypeStruct((b, sub_len, dil * WIDTH), BF16),
                   jax.ShapeDtypeStruct((b, sub_len, dil * lw), F32)],
        compiler_params=_params("parallel", "arbitrary"),
        name=f"att_dil{dil}",
    )(*args)


N_LEVELS = 7
ROW_CUM, ROW_REM = N_LEVELS, N_LEVELS + 1


def _hgrn_constants():
    r = np.arange(BLOCK)[:, None]
    s = np.arange(BLOCK)[None, :]
    mats = []
    for lvl in range(N_LEVELS):
        half = 1 << lvl
        mid = (r // (2 * half)) * 2 * half + half
        upper = (r % (2 * half)) >= half
        mats.append(np.where(upper, (s >= mid) & (s <= r), (s > r) & (s <= mid - 1)))
    mats.append(s <= r)
    mats.append(s > r)
    range_mat = np.concatenate(mats, axis=0).astype(np.float32)
    range_mat = np.concatenate([range_mat, range_mat], axis=1)
    x = r ^ s
    level = np.where(s < r, np.floor(np.log2(np.maximum(x, 1))).astype(np.int32), np.where(s == r, N_LEVELS, -1))
    return jnp.asarray(range_mat, BF16), jnp.asarray(level, jnp.int32)


HGRN_SEQS = 2


def _hgrn_intra(q_b, k_b, lf, rng, level):
    lf_hi = lf.astype(BF16)
    lf_lo = (lf - lf_hi.astype(F32)).astype(BF16)
    z = _dot(rng, jnp.concatenate([lf_hi, lf_lo], axis=0))
    q = q_b.astype(F32)
    kk = k_b.astype(F32)
    lhs, rhs = [], []
    for lvl in range(N_LEVELS):
        x = jnp.exp(z[lvl * BLOCK:(lvl + 1) * BLOCK])
        lhs.append((q * x).astype(BF16))
        rhs.append((kk * x).astype(BF16))
    lhs.append(q_b)
    rhs.append(k_b)
    b_cum = z[ROW_CUM * BLOCK:(ROW_CUM + 1) * BLOCK]
    q_in = (q * jnp.exp(b_cum)).astype(BF16)
    k_out = (kk * jnp.exp(z[ROW_REM * BLOCK:(ROW_REM + 1) * BLOCK])).astype(BF16)
    decay = jnp.exp(b_cum[BLOCK - 1:BLOCK, :])
    mats = []
    for h in range(HEADS):
        sl = slice(h * HEAD_DIM, (h + 1) * HEAD_DIM)
        a = jnp.zeros((BLOCK, BLOCK), F32)
        for lvl in range(N_LEVELS + 1):
            a = jnp.where(level == lvl, _dot_nt(lhs[lvl][:, sl], rhs[lvl][:, sl]), a)
        mats.append(a.astype(BF16))
    return mats, q_in, k_out, decay


def _hgrn_state(mats, q_in, k_out, decay, v_b, gain, st_ref, j):
    outs = []
    for h in range(HEADS):
        sl = slice(h * HEAD_DIM, (h + 1) * HEAD_DIM)
        v = v_b[:, sl]
        st = st_ref[j, h]
        o = _dot(mats[h], v) + _dot_nt(q_in[:, sl], st.astype(BF16))
        outs.append(_rms(o, gain[:, sl]).astype(BF16))
        v_t = v.astype(F32).T.astype(BF16)
        st_ref[j, h] = st * decay[:, sl] + _dot(v_t, k_out[:, sl])
    return jnp.concatenate(outs, axis=-1)


def _hgrn_body(q_ref, k_ref, lf_ref, v_ref, rng_ref, lvl_ref, gain_ref, o_ref, s_ref, st_ref, *, n_chunks):
    c = pl.program_id(1)

    @pl.when(c == 0)
    def _():
        st_ref[...] = jnp.zeros_like(st_ref)

    rng, level, gain = rng_ref[...], lvl_ref[...], gain_ref[...]
    intra = [_hgrn_intra(q_ref[j], k_ref[j], lf_ref[j], rng, level) for j in range(HGRN_SEQS)]
    for j in range(HGRN_SEQS):
        o_ref[j] = _hgrn_state(*intra[j], v_ref[j], gain, st_ref, j)

    @pl.when(c == n_chunks - 1)
    def _():
        for j in range(HGRN_SEQS):
            for h in range(HEADS):
                s_ref[j, h] = st_ref[j, h].T


def _hgrn_prompt(hq, hk, lf, hv, gain, b, t):
    n_chunks = t // BLOCK
    range_mat, level = _hgrn_constants()
    as3d = lambda a: a.reshape(b, t, WIDTH)
    spec = pl.BlockSpec((HGRN_SEQS, BLOCK, WIDTH), lambda bi, c: (bi, c, 0))
    state_spec = pl.BlockSpec((None, HGRN_SEQS, HEADS, HEAD_DIM, HEAD_DIM), lambda bi, c: (0, bi, 0, 0, 0))
    o, state = pl.pallas_call(
        functools.partial(_hgrn_body, n_chunks=n_chunks),
        grid=(b // HGRN_SEQS, n_chunks),
        in_specs=[spec, spec, spec, spec, _const_spec(range_mat.shape), _const_spec(level.shape),
                  _const_spec((1, WIDTH))],
        out_specs=[spec, state_spec],
        out_shape=[jax.ShapeDtypeStruct((b, t, WIDTH), BF16),
                   jax.ShapeDtypeStruct((1, b, HEADS, HEAD_DIM, HEAD_DIM), F32)],
        scratch_shapes=[pltpu.VMEM((HGRN_SEQS, HEADS, HEAD_DIM, HEAD_DIM), F32)],
        compiler_params=_params("parallel", "arbitrary"),
        name="hgrn_prompt",
    )(as3d(hq), as3d(hk), as3d(lf), as3d(hv), range_mat, level, gain)
    return o.reshape(b * t, WIDTH), state


def _memkv_body(m_ref, g_ref, wk_ref, wv_ref, pk_ref, pv_ref, kb_ref, vb_ref):
    h = _rms(m_ref[...], g_ref[...]).astype(BF16)
    k = _dot(h, wk_ref[...])
    v = _dot(h, wv_ref[...])
    _store_heads(pk_ref, k)
    _store_heads(pv_ref, v)
    kb_ref[...] = k.astype(BF16)
    vb_ref[...] = v.astype(BF16)


def _mem_kv(mem, g_mem, w_mk, w_mv):
    b = mem.shape[0]
    head_spec = pl.BlockSpec((None, None, MEM_LEN, HEADS, HEAD_DIM), lambda bi: (0, bi, 0, 0, 0))
    flat_spec = pl.BlockSpec((None, MEM_LEN, WIDTH), lambda bi: (bi, 0, 0))
    head_shape = jax.ShapeDtypeStruct((1, b, MEM_LEN, HEADS, HEAD_DIM), F32)
    flat_shape = jax.ShapeDtypeStruct((b, MEM_LEN, WIDTH), BF16)
    return pl.pallas_call(
        _memkv_body,
        grid=(b,),
        in_specs=[pl.BlockSpec((None, MEM_LEN, D_MODEL), lambda bi: (bi, 0, 0)),
                  _const_spec((1, D_MODEL)), _const_spec((D_MODEL, WIDTH)), _const_spec((D_MODEL, WIDTH))],
        out_specs=[head_spec, head_spec, flat_spec, flat_spec],
        out_shape=[head_shape, head_shape, flat_shape, flat_shape],
        compiler_params=_params("parallel"),
        name="mem_kv",
    )(mem, g_mem, w_mk, w_mv)


def _from_strided_view(src_ref, stage_ref, dil, width):
    if dil == 1:
        return src_ref[...].astype(F32)
    n = src_ref.shape[0]
    n_col = width // 128
    for r in range(dil):
        for c in range(n_col):
            stage_ref[c, pl.ds(r, n, stride=dil), :] = (
                src_ref[:, r * width + c * 128:r * width + (c + 1) * 128].astype(F32))
    return jnp.concatenate([stage_ref[c] for c in range(n_col)], axis=-1)


def _combine_body(o0_ref, o1_ref, o2_ref, l0_ref, l1_ref, l2_ref, mq_ref, mk_ref, mv_ref, att_ref, mem_ref,
                  *stage_refs):
    outs = [_from_strided_view(r, stage_refs[2 * g], ATT_GROUPS[g][1], WIDTH)
            for g, r in enumerate((o0_ref, o1_ref, o2_ref))]
    lses = [_from_strided_view(r, stage_refs[2 * g + 1], ATT_GROUPS[g][1], HEADS * LSE_REP)
            for g, r in enumerate((l0_ref, l1_ref, l2_ref))]
    for h in range(HEADS):
        sl = slice(h * HEAD_DIM, (h + 1) * HEAD_DIM)
        lse = [x[:, h * LSE_REP:h * LSE_REP + 1] for x in lses]
        m = jnp.maximum(jnp.maximum(lse[0], lse[1]), lse[2])
        e = [jnp.exp(x - m) for x in lse]
        inv = 1.0 / (e[0] + e[1] + e[2])
        att = sum((e[g] * inv) * outs[g][:, sl] for g in range(N_GROUPS))
        att_ref[:, sl] = att.astype(BF16)
        s = _dot_nt(mq_ref[:, sl], mk_ref[:, sl])
        p = jnp.exp(s - jnp.max(s, axis=-1, keepdims=True))
        o = _dot(p.astype(BF16), mv_ref[:, sl]) / jnp.sum(p, axis=-1, keepdims=True)
        mem_ref[:, sl] = o.astype(BF16)


def _combine(outs, lses, mq, mk, mv, b, t, *, tm):
    n_tiles = t // tm
    lw = HEADS * LSE_REP
    row = pl.BlockSpec((tm, WIDTH), lambda bi, i: (bi * n_tiles + i, 0))
    o_specs = [pl.BlockSpec((None, tm // d, d * WIDTH), lambda bi, i: (bi, i, 0)) for _, d in ATT_GROUPS]
    l_specs = [pl.BlockSpec((None, tm // d, d * lw), lambda bi, i: (bi, i, 0)) for _, d in ATT_GROUPS]
    memspec = pl.BlockSpec((None, MEM_LEN, WIDTH), lambda bi, i: (bi, 0, 0))
    shape = jax.ShapeDtypeStruct((b * t, WIDTH), BF16)
    stages = []
    for _ in ATT_GROUPS:
        stages += [pltpu.VMEM((WIDTH // 128, tm, 128), F32), pltpu.VMEM((lw // 128, tm, 128), F32)]
    return pl.pallas_call(
        _combine_body,
        grid=(b, n_tiles),
        in_specs=o_specs + l_specs + [row, memspec, memspec],
        out_specs=[row, row],
        out_shape=[shape, shape],
        scratch_shapes=stages,
        compiler_params=_params("parallel", "parallel"),
        name="combine",
    )(*outs, *lses, mq, mk, mv)


def _merge_body(x_ref, att_ref, hg_ref, mem_ref, gate_ref, wa_ref, wb_ref, wc_ref, wo_ref, o_ref):
    def gate(j):
        return gate_ref[:, j * D_MODEL:(j + 1) * D_MODEL].astype(F32)

    m = (gate(0) * _dot(att_ref[...].astype(BF16), wa_ref[...])
         + gate(1) * _dot(hg_ref[...].astype(BF16), wb_ref[...])
         + gate(2) * _dot(mem_ref[...].astype(BF16), wc_ref[...]))
    o_ref[...] = x_ref[...] + _dot(m.astype(BF16), wo_ref[...])


def _merge(x, att, hg, mem, gates, wa, wb, wc, wo, *, tm):
    m = x.shape[0]
    row = pl.BlockSpec((tm, WIDTH), lambda i: (i, 0))
    wide = pl.BlockSpec((tm, D_MODEL), lambda i: (i, 0))
    return pl.pallas_call(
        _merge_body,
        grid=(m // tm,),
        in_specs=[wide, row, row, row, pl.BlockSpec((tm, N_BRANCH * D_MODEL), lambda i: (i, 0)),
                  _const_spec((WIDTH, D_MODEL)), _const_spec((WIDTH, D_MODEL)), _const_spec((WIDTH, D_MODEL)),
                  _const_spec((D_MODEL, D_MODEL))],
        out_specs=wide,
        out_shape=jax.ShapeDtypeStruct((m, D_MODEL), F32),
        compiler_params=_params("parallel"),
        name="merge",
    )(x, att, hg, mem, gates, wa, wb, wc, wo)


def _norm_matmul_body(x_ref, g_ref, w_ref, o_ref):
    o_ref[...] = _dot(_rms(x_ref[...], g_ref[...]).astype(BF16), w_ref[...])


def _sample_in_proj(x, g_mix, w_in):
    m = x.shape[0]
    n = w_in.shape[1]
    return pl.pallas_call(
        _norm_matmul_body,
        grid=(n // WIDTH,),
        in_specs=[_const_spec((m, D_MODEL)), _const_spec((1, D_MODEL)),
                  pl.BlockSpec((D_MODEL, WIDTH), lambda j: (0, j))],
        out_specs=pl.BlockSpec((m, WIDTH), lambda j: (0, j)),
        out_shape=jax.ShapeDtypeStruct((m, n), F32),
        compiler_params=_params("parallel"),
        name="sample_in_proj",
    )(x, g_mix, w_in)


def _one_query_attend(q, k_rows, v_rows, k_new=None, v_new=None):
    s = jnp.sum(k_rows * q, axis=-1, keepdims=True)
    m = jnp.max(s, axis=0, keepdims=True)
    if k_new is not None:
        s_new = jnp.sum(k_new * q, axis=-1, keepdims=True)
        m = jnp.maximum(m, s_new)
    p = jnp.exp(s - m)
    l = jnp.sum(p, axis=0, keepdims=True)
    acc = jnp.sum(p * v_rows, axis=0, keepdims=True)
    if k_new is not None:
        p_new = jnp.exp(s_new - m)
        l = l + p_new
        acc = acc + p_new * v_new
    return acc / l, m + jnp.log(l)


def _sample_att_body(q_ref, kn_ref, vn_ref, mq_ref, k1_ref, v1_ref, k4_ref, v4_ref, k16_ref, v16_ref,
                     mk_ref, mv_ref, att_ref, mem_ref):
    k_refs = (k1_ref, k4_ref, k16_ref)
    v_refs = (v1_ref, v4_ref, v16_ref)
    for h in range(HEADS):
        sl = slice(h * HEAD_DIM, (h + 1) * HEAD_DIM)
        outs, lses = [], []
        for g in range(N_GROUPS):
            gs = slice(g * WIDTH + h * HEAD_DIM, g * WIDTH + (h + 1) * HEAD_DIM)
            o, lse = _one_query_attend(q_ref[:, gs] * QK_SCALE, k_refs[g][:, h, :], v_refs[g][:, h, :],
                                       kn_ref[:, gs], vn_ref[:, gs])
            outs.append(o)
            lses.append(lse)
        m = jnp.maximum(jnp.maximum(lses[0], lses[1]), lses[2])
        e = [jnp.exp(x - m) for x in lses]
        inv = 1.0 / (e[0] + e[1] + e[2])
        att_ref[:, sl] = sum((e[g] * inv) * outs[g] for g in range(N_GROUPS))
        o, _ = _one_query_attend(mq_ref[:, sl] * QK_SCALE, mk_ref[:, h, :], mv_ref[:, h, :])
        mem_ref[:, sl] = o


def _sample_attend(z3, caches_k, caches_v, cache_mk, cache_mv):
    bd = z3.shape[0]
    zspec = lambda col, n: pl.BlockSpec((None, 1, n * WIDTH), lambda b: (b, 0, col // n))
    cache_specs, cache_args = [], []
    for (win, dil), ck, cv in zip(ATT_GROUPS, caches_k, caches_v):
        rows = ck.shape[2]
        assert rows == win and rows // dil == BLOCK
        spec = pl.BlockSpec((None, None, BLOCK, None, HEADS, HEAD_DIM), lambda b: (0, b, 0, 0, 0, 0))
        for c in (ck, cv):
            cache_specs.append(spec)
            cache_args.append(c.reshape(1, bd, BLOCK, dil, HEADS, HEAD_DIM))
    mem_spec = pl.BlockSpec((None, None, MEM_LEN, HEADS, HEAD_DIM), lambda b: (0, b, 0, 0, 0))
    out_spec = pl.BlockSpec((None, 1, WIDTH), lambda b: (b, 0, 0))
    out_shape = jax.ShapeDtypeStruct((bd, 1, WIDTH), F32)
    return pl.pallas_call(
        _sample_att_body,
        grid=(bd,),
        in_specs=[zspec(COL_Q, 3), zspec(COL_K, 3), zspec(COL_V, 3), zspec(COL_MQ, 1)] + cache_specs
                 + [mem_spec, mem_spec],
        out_specs=[out_spec, out_spec],
        out_shape=[out_shape, out_shape],
        compiler_params=_params("parallel"),
        name="sample_attend",
    )(z3, z3, z3, z3, *cache_args, cache_mk, cache_mv)


SHIFT_ROWS = 64


def _shift_rows(cache_ref, out_ref, new_row):
    rows = cache_ref.shape[0]
    n_full, rem = divmod(rows - 1, SHIFT_ROWS)

    def body(c, carry):
        start = pl.multiple_of(c * SHIFT_ROWS, SHIFT_ROWS)
        out_ref[pl.ds(start, SHIFT_ROWS)] = cache_ref[pl.ds(start + 1, SHIFT_ROWS)]
        return carry

    lax.fori_loop(0, n_full, body, 0)
    if rem:
        out_ref[pl.ds(n_full * SHIFT_ROWS, rem)] = cache_ref[pl.ds(n_full * SHIFT_ROWS + 1, rem)]
    for h in range(HEADS):
        out_ref[pl.ds(rows - 1, 1), h, :] = new_row[:, h * HEAD_DIM:(h + 1) * HEAD_DIM]


def _cache_shift_body(kn_ref, vn_ref, *refs):
    n = len(refs) // 2
    for j in range(n):
        g, is_v = divmod(j, 2)
        new = (vn_ref if is_v else kn_ref)[:, g * WIDTH:(g + 1) * WIDTH]
        _shift_rows(refs[j], refs[n + j], new)


def _cache_shift(z3, caches):
    bd = z3.shape[0]
    zspec = lambda col: pl.BlockSpec((None, 1, N_GROUPS * WIDTH), lambda b: (b, 0, col // N_GROUPS))
    specs = [pl.BlockSpec((None, None, c.shape[2], HEADS, HEAD_DIM), lambda b: (0, b, 0, 0, 0)) for c in caches]
    return pl.pallas_call(
        _cache_shift_body,
        grid=(bd,),
        in_specs=[zspec(COL_K), zspec(COL_V)] + specs,
        out_specs=specs,
        out_shape=[jax.ShapeDtypeStruct(c.shape, c.dtype) for c in caches],
        compiler_params=_params("parallel"),
        name="cache_shift",
    )(z3, z3, *caches)


SAMPLE_GROUP = 8


def _sample_hgrn_body(zq_ref, zf_ref, zi_ref, lb_ref, gain_ref, s_ref, o_ref, so_ref):
    lb = _lower_bound(lb_ref)
    gain = gain_ref[...]
    q = jax.nn.sigmoid(zq_ref[...])
    kk = (1.0 - lb) * jax.nn.sigmoid(zf_ref[...])
    f = lb + kk
    v = zi_ref[...]
    pad = jnp.zeros((HEAD_DIM - SAMPLE_GROUP, HEAD_DIM), F32)

    def columns(a, h):
        return jnp.concatenate([a[:, h * HEAD_DIM:(h + 1) * HEAD_DIM], pad], axis=0).T

    for h in range(HEADS):
        sl = slice(h * HEAD_DIM, (h + 1) * HEAD_DIM)
        q_t, k_t, f_t = columns(q, h), columns(kk, h), columns(f, h)
        for j in range(SAMPLE_GROUP):
            s1 = f_t[:, j:j + 1] * s_ref[j, h] + k_t[:, j:j + 1] * v[j:j + 1, sl]
            so_ref[j, h] = s1
            o = jnp.sum(q_t[:, j:j + 1] * s1, axis=0, keepdims=True)
            o_ref[j:j + 1, sl] = _rms(o, gain[:, sl])


def _sample_hgrn(z, lb_logits, gain, state):
    bd = z.shape[0]
    zspec = lambda col: pl.BlockSpec((SAMPLE_GROUP, WIDTH), lambda i: (i, col))
    sspec = pl.BlockSpec((None, SAMPLE_GROUP, HEADS, HEAD_DIM, HEAD_DIM), lambda i: (0, i, 0, 0, 0))
    return pl.pallas_call(
        _sample_hgrn_body,
        grid=(bd // SAMPLE_GROUP,),
        in_specs=[zspec(COL_HQ), zspec(COL_HF), zspec(COL_HI), _const_spec(lb_logits.shape),
                  _const_spec((1, WIDTH)), sspec],
        out_specs=[pl.BlockSpec((SAMPLE_GROUP, WIDTH), lambda i: (i, 0)), sspec],
        out_shape=[jax.ShapeDtypeStruct((bd, WIDTH), F32), jax.ShapeDtypeStruct(state.shape, F32)],
        compiler_params=_params("parallel"),
        name="sample_hgrn",
    )(z, z, z, lb_logits, gain, state)


def _gate_body(z_ref, o_ref):
    o_ref[...] = jax.nn.sigmoid(z_ref[...]).astype(BF16)


def _sample_gates(z):
    bd = z.shape[0]
    n = N_BRANCH * D_MODEL
    return pl.pallas_call(
        _gate_body,
        grid=(n // WIDTH,),
        in_specs=[pl.BlockSpec((bd, WIDTH), lambda j: (0, COL_GATE + j))],
        out_specs=pl.BlockSpec((bd, WIDTH), lambda j: (0, j)),
        out_shape=jax.ShapeDtypeStruct((bd, n), BF16),
        name="sample_gates",
    )(z)


def kernel(x_prompt, x_sample, mem_prompt, cache_win1_k, cache_win1_v, cache_win4_k, cache_win4_v, cache_win16_k, cache_win16_v, cache_mem_k, cache_mem_v, state_hgrn, g_ff1, w_ff1_gate, w_ff1_up, w_ff1_down, g_mix, w_in, hg_lb_logits, g_hg_out, g_mem, w_mem_k, w_mem_v, w_branch_att, w_branch_hg, w_branch_mem, w_out, g_ff2, w_ff2_gate, w_ff2_up, w_ff2_down, g_final):
    b, t, _ = x_prompt.shape
    bd = x_sample.shape[0]
    assert g_ff1.shape[0] == 1 and x_sample.shape[1] == 1 and w_in.shape[2] == IN_WIDTH
    bf = lambda w: w[0].astype(BF16)
    ffn1 = (g_ff1, bf(w_ff1_gate), bf(w_ff1_up), bf(w_ff1_down), g_final.reshape(1, D_MODEL))
    ffn2 = (g_ff2, bf(w_ff2_gate), bf(w_ff2_up), bf(w_ff2_down), g_final.reshape(1, D_MODEL))
    w_in_b = bf(w_in)
    w_qkv, w_rest = w_in_b[:, :COL_HQ * WIDTH], w_in_b[:, COL_HQ * WIDTH:]
    branch_w = (bf(w_branch_att), bf(w_branch_hg), bf(w_branch_mem), bf(w_out))

    xs = _ffn(x_sample.reshape(bd, D_MODEL), *ffn1, tm=bd, final=False)
    zs = _sample_in_proj(xs, g_mix, w_in_b)
    caches_k = (cache_win1_k, cache_win4_k, cache_win16_k)
    caches_v = (cache_win1_v, cache_win4_v, cache_win16_v)
    z3 = zs.reshape(bd, 1, IN_WIDTH)
    att_s, mem_s = _sample_attend(z3, caches_k, caches_v, cache_mem_k, cache_mem_v)
    s_wins = _cache_shift(z3, [c for pair in zip(caches_k, caches_v) for c in pair])
    hg_s, s_hgrn = _sample_hgrn(zs, hg_lb_logits, g_hg_out, state_hgrn)
    xs = _merge(xs, att_s.reshape(bd, WIDTH), hg_s, mem_s.reshape(bd, WIDTH), _sample_gates(zs), *branch_w, tm=bd)
    y_sample = _ffn(xs, *ffn2, tm=bd, final=True).reshape(bd, 1, D_MODEL)

    xp = _ffn(x_prompt.reshape(b * t, D_MODEL), *ffn1, tm=512, final=False)
    qkv = _qkv_proj(xp.reshape(b, t, D_MODEL), g_mix, w_qkv, tm=512)
    q_g, k_g, v_g, p_k, p_v = qkv[0:3], qkv[3:6], qkv[6:9], qkv[9:12], qkv[12:15]
    hq, hk, lf, hv, mq, gates = _rest_proj(xp, g_mix, w_rest, hg_lb_logits, tm=512)
    outs, lses = [], []
    for g, (_, dil) in enumerate(ATT_GROUPS):
        o, lse = _att_group(q_g[g], k_g[g], v_g[g], dil)
        outs.append(o)
        lses.append(lse)
    hg_p, p_hgrn = _hgrn_prompt(hq, hk, lf, hv, g_hg_out, b, t)
    p_mem_k, p_mem_v, mk_b, mv_b = _mem_kv(mem_prompt, g_mem, bf(w_mem_k), bf(w_mem_v))
    att_p, mem_p = _combine(outs, lses, mq, mk_b, mv_b, b, t, tm=512)
    xp = _merge(xp, att_p, hg_p, mem_p, gates, *branch_w, tm=512)
    y_prompt = _ffn(xp, *ffn2, tm=512, final=True).reshape(b, t, D_MODEL)

    return (y_prompt, y_sample,
            p_k[0], p_v[0], p_k[1], p_v[1], p_k[2], p_v[2],
            p_mem_k, p_mem_v, p_hgrn,
            s_wins[0], s_wins[1], s_wins[2], s_wins[3], s_wins[4], s_wins[5],
            s_hgrn)
```

```python
import functools

import numpy as np
import jax
import jax.numpy as jnp
from jax import lax
from jax.experimental import pallas as pl
from jax.experimental.pallas import tpu as pltpu

F32 = jnp.float32
BF16 = jnp.bfloat16

D_MODEL = 1024
D_FF = 2816
EPS = 1e-6
HEADS = 4
HEAD_DIM = 128
WIDTH = HEADS * HEAD_DIM
ATT_GROUPS = ((128, 1), (512, 4), (2048, 16))
N_GROUPS = 3
BLOCK = 128
MEM_LEN = 256
N_BRANCH = 3
LSE_REP = 32
QK_SCALE = HEAD_DIM ** -0.5
VMEM_LIMIT_BYTES = 56 * 1024 * 1024

COL_Q, COL_K, COL_V, COL_HQ, COL_HF, COL_HI, COL_MQ, COL_GATE = 0, 3, 6, 9, 10, 11, 12, 13
IN_WIDTH = (COL_GATE + N_BRANCH * D_MODEL // WIDTH) * WIDTH


def _params(*sem):
    return pltpu.CompilerParams(dimension_semantics=sem, vmem_limit_bytes=VMEM_LIMIT_BYTES)


def _const_spec(shape):
    nd = len(shape)
    return pl.BlockSpec(shape, lambda *_: (0,) * nd, pipeline_mode=pl.Buffered(1))


def _rms(x, g):
    return x * lax.rsqrt(jnp.mean(x * x, axis=-1, keepdims=True) + EPS) * g


def _dot(a, b):
    return jnp.dot(a, b, preferred_element_type=F32)


def _dot_nt(a, b):
    return lax.dot_general(a, b, (((1,), (1,)), ((), ())), preferred_element_type=F32)


SHIFT_ROWS = 64


def _shift_rows(cache_ref, out_ref, new_row):
    rows = cache_ref.shape[0]
    n_full, rem = divmod(rows - 1, SHIFT_ROWS)

    def body(c, carry):
        start = pl.multiple_of(c * SHIFT_ROWS, SHIFT_ROWS)
        out_ref[pl.ds(start, SHIFT_ROWS)] = cache_ref[pl.ds(start + 1, SHIFT_ROWS)]
        return carry

    lax.fori_loop(0, n_full, body, 0)
    if rem:
        out_ref[pl.ds(n_full * SHIFT_ROWS, rem)] = cache_ref[pl.ds(n_full * SHIFT_ROWS + 1, rem)]
    for h in range(HEADS):
        out_ref[pl.ds(rows - 1, 1), h, :] = new_row[:, h * HEAD_DIM:(h + 1) * HEAD_DIM]


def _ride_specs(ride, n_steps):
    if not ride:
        return [], [], [], [], ()
    z3, riders = ride
    assert z3.shape[0] == n_steps
    zspec = lambda col: pl.BlockSpec((None, 1, N_GROUPS * WIDTH), lambda i: (i, 0, col // N_GROUPS))
    specs = [pl.BlockSpec((None, None, c.shape[2], HEADS, HEAD_DIM), lambda i: (0, i, 0, 0, 0)) for c, _, _ in riders]
    shapes = [jax.ShapeDtypeStruct(c.shape, c.dtype) for c, _, _ in riders]
    return ([zspec(COL_K), zspec(COL_V)] + specs, [z3, z3] + [c for c, _, _ in riders], specs, shapes,
            tuple((g, is_v) for _, g, is_v in riders))


def _ride_body(in_refs, out_refs, meta):
    kn_ref, vn_ref = in_refs[:2]
    for (g, is_v), cache_ref, out_ref in zip(meta, in_refs[2:], out_refs):
        new = (vn_ref if is_v else kn_ref)[:, g * WIDTH:(g + 1) * WIDTH]
        _shift_rows(cache_ref, out_ref, new)


FF_CHUNKS = 2


def _ffn_body(x_ref, g_ref, wg_ref, wu_ref, wd_ref, gf_ref, *rest, final, ride):
    o_ref = rest[2 + len(ride)] if ride else rest[0]
    x = x_ref[...]
    h = _rms(x, g_ref[...]).astype(BF16)
    y = x
    width = D_FF // FF_CHUNKS
    for c in range(FF_CHUNKS):
        cols = slice(c * width, (c + 1) * width)
        a = _dot(h, wg_ref[:, cols])
        u = _dot(h, wu_ref[:, cols])
        act = (a * jax.nn.sigmoid(a) * u).astype(BF16)
        y = y + 0.5 * _dot(act, wd_ref[cols, :])
    if final:
        y = _rms(y, gf_ref[...])
    o_ref[...] = y
    if ride:
        _ride_body(rest[:2 + len(ride)], rest[3 + len(ride):], ride)


def _ffn(x, g, wg, wu, wd, g_final, *, tm, final, ride=None):
    m = x.shape[0]
    r_in, r_args, r_out, r_shapes, meta = _ride_specs(ride, m // tm)
    res = pl.pallas_call(
        functools.partial(_ffn_body, final=final, ride=meta),
        grid=(m // tm,),
        in_specs=[pl.BlockSpec((tm, D_MODEL), lambda i: (i, 0)),
                  _const_spec((1, D_MODEL)),
                  _const_spec((D_MODEL, D_FF)), _const_spec((D_MODEL, D_FF)), _const_spec((D_FF, D_MODEL)),
                  _const_spec((1, D_MODEL))] + r_in,
        out_specs=[pl.BlockSpec((tm, D_MODEL), lambda i: (i, 0))] + r_out,
        out_shape=[jax.ShapeDtypeStruct((m, D_MODEL), F32)] + r_shapes,
        compiler_params=_params("parallel"),
        name="ffn_final" if final else "ffn",
    )(x, g, wg, wu, wd, g_final, *r_args)
    return res[0], res[1:]


def _store_heads(ref, z):
    rows = ref.shape[0]
    flat = ref.reshape(rows * HEADS, HEAD_DIM)
    for h in range(HEADS):
        flat[pl.ds(h, rows, stride=HEADS), :] = z[:, h * HEAD_DIM:(h + 1) * HEAD_DIM]


def _load_head(ref, h):
    rows = ref.shape[0]
    return ref.reshape(rows * HEADS, HEAD_DIM)[pl.ds(h, rows, stride=HEADS), :]


def _to_strided_view(dst_ref, z, stage_ref, dil):
    if dil == 1:
        dst_ref[...] = z.astype(dst_ref.dtype)
        return
    n = z.shape[0] // dil
    for c in range(stage_ref.shape[0]):
        stage_ref[c] = z[:, c * 128:(c + 1) * 128]
    for r in range(dil):
        for c in range(stage_ref.shape[0]):
            dst_ref[:, r * WIDTH + c * 128:r * WIDTH + (c + 1) * 128] = (
                stage_ref[c, pl.ds(r, n, stride=dil), :].astype(dst_ref.dtype))


def _qkv_body(x_ref, g_ref, w_ref, *refs, tm, n_tiles):
    q_refs, k_refs, v_refs = refs[0:3], refs[3:6], refs[6:9]
    pk_refs, pv_refs, stage_refs = refs[9:12], refs[12:15], refs[15:]
    h = _rms(x_ref[...], g_ref[...]).astype(BF16)
    late = []
    for g, (win, dil) in enumerate(ATT_GROUPS):
        keep = min(win, tm * n_tiles)
        for kind, (col, dst_refs, win_refs) in enumerate(((COL_Q, q_refs, None), (COL_K, k_refs, pk_refs),
                                                          (COL_V, v_refs, pv_refs))):
            z = _dot(h, w_ref[:, (col + g) * WIDTH:(col + g + 1) * WIDTH])
            if win_refs is None:
                z = z * QK_SCALE
            stage = stage_refs[3 * (g - 1) + kind] if dil > 1 else None
            _to_strided_view(dst_refs[g], z, stage, dil)
            if win_refs is None:
                continue
            if keep == tm * n_tiles:
                _store_heads(win_refs[g], z)
            elif dil == 1:
                late.append((win_refs[g], z[tm - keep:, :], None))
            else:
                assert keep == tm
                late.append((win_refs[g], None, stage))

    @pl.when(pl.program_id(1) == n_tiles - 1)
    def _():
        for ref, rows, stage in late:
            if stage is None:
                _store_heads(ref, rows)
            else:
                _store_heads(ref, jnp.concatenate([stage[hd] for hd in range(HEADS)], axis=-1))


def _qkv_proj(x, g_mix, w_qkv, *, tm):
    b, t, _ = x.shape
    n_tiles = t // tm
    view_specs = [pl.BlockSpec((None, tm // d, d * WIDTH), lambda bi, i: (bi, i, 0)) for _, d in ATT_GROUPS]
    view_shapes = [jax.ShapeDtypeStruct((b, t // d, d * WIDTH), BF16) for _, d in ATT_GROUPS]
    win_specs, win_shapes = [], []
    for win, _ in ATT_GROUPS:
        keep = min(win, t)
        assert keep == t or keep <= tm
        if keep == t:
            win_specs.append(pl.BlockSpec((None, None, tm, HEADS, HEAD_DIM), lambda bi, i: (0, bi, i, 0, 0)))
        else:
            win_specs.append(pl.BlockSpec((None, None, keep, HEADS, HEAD_DIM), lambda bi, i: (0, bi, 0, 0, 0)))
        win_shapes.append(jax.ShapeDtypeStruct((1, b, keep, HEADS, HEAD_DIM), F32))
    return pl.pallas_call(
        functools.partial(_qkv_body, tm=tm, n_tiles=n_tiles),
        grid=(b, n_tiles),
        in_specs=[pl.BlockSpec((None, tm, D_MODEL), lambda bi, i: (bi, i, 0)),
                  _const_spec((1, D_MODEL)),
                  _const_spec((D_MODEL, 3 * N_GROUPS * WIDTH))],
        out_specs=view_specs * 3 + win_specs + win_specs,
        out_shape=view_shapes * 3 + win_shapes + win_shapes,
        scratch_shapes=[pltpu.VMEM((WIDTH // 128, tm, 128), F32)] * (3 * sum(d > 1 for _, d in ATT_GROUPS)),
        compiler_params=_params("parallel", "arbitrary"),
        name="qkv_proj",
    )(x, g_mix, w_qkv)


def _lower_bound(logits_ref):
    lg = logits_ref[...]
    e = jnp.exp(lg - jnp.max(lg, axis=0, keepdims=True))
    return e[0:1, :] / jnp.sum(e, axis=0, keepdims=True)


def _rest_body(x_ref, g_ref, w_ref, lb_ref, *rest, ride):
    n_in = 2 + len(ride) if ride else 0
    hq_ref, hk_ref, lf_ref, hv_ref, mq_ref, gate_ref = rest[n_in:n_in + 6]
    if ride:
        _ride_body(rest[:n_in], rest[n_in + 6:], ride)
    h = _rms(x_ref[...], g_ref[...]).astype(BF16)
    lb = _lower_bound(lb_ref)
    hq_ref[...] = jax.nn.sigmoid(_dot(h, w_ref[:, 0:WIDTH])).astype(BF16)
    kk = (1.0 - lb) * jax.nn.sigmoid(_dot(h, w_ref[:, WIDTH:2 * WIDTH]))
    hk_ref[...] = kk.astype(BF16)
    lf_ref[...] = jnp.log(lb + kk)
    hv_ref[...] = _dot(h, w_ref[:, 2 * WIDTH:3 * WIDTH]).astype(BF16)
    mq_ref[...] = (_dot(h, w_ref[:, 3 * WIDTH:4 * WIDTH]) * QK_SCALE).astype(BF16)
    for c in range(N_BRANCH * D_MODEL // WIDTH):
        z = _dot(h, w_ref[:, (4 + c) * WIDTH:(5 + c) * WIDTH])
        gate_ref[:, c * WIDTH:(c + 1) * WIDTH] = jax.nn.sigmoid(z).astype(BF16)


def _rest_proj(x, g_mix, w_rest, lb_logits, *, tm, ride=None):
    m = x.shape[0]
    spec = pl.BlockSpec((tm, WIDTH), lambda i: (i, 0))
    bf = jax.ShapeDtypeStruct((m, WIDTH), BF16)
    r_in, r_args, r_out, r_shapes, meta = _ride_specs(ride, m // tm)
    res = pl.pallas_call(
        functools.partial(_rest_body, ride=meta),
        grid=(m // tm,),
        in_specs=[pl.BlockSpec((tm, D_MODEL), lambda i: (i, 0)),
                  _const_spec((1, D_MODEL)),
                  _const_spec(w_rest.shape),
                  _const_spec(lb_logits.shape)] + r_in,
        out_specs=[spec, spec, spec, spec, spec, pl.BlockSpec((tm, N_BRANCH * D_MODEL), lambda i: (i, 0))] + r_out,
        out_shape=[bf, bf, jax.ShapeDtypeStruct((m, WIDTH), F32), bf, bf,
                   jax.ShapeDtypeStruct((m, N_BRANCH * D_MODEL), BF16)] + r_shapes,
        compiler_params=_params("parallel"),
        name="rest_proj",
    )(x, g_mix, w_rest, lb_logits, *r_args)
    return res[:6], res[6:]


ATT_UNITS = 4


def _att_body(q_ref, k_ref, v_ref, *rest, by_rows, has_halo):
    if has_halo:
        kh_ref, vh_ref, o_ref, lse_ref = rest
        halo_ok = pl.program_id(1) > 0
    else:
        o_ref, lse_ref = rest
    qi = lax.broadcasted_iota(jnp.int32, (BLOCK, BLOCK), 0)
    ki = lax.broadcasted_iota(jnp.int32, (BLOCK, BLOCK), 1)
    own_ok = ki <= qi
    prev_ok = ki >= qi

    units = []
    for u in range(ATT_UNITS):
        for h in range(HEADS):
            if by_rows:
                rows = slice(u * BLOCK, (u + 1) * BLOCK)
                cols = slice(h * HEAD_DIM, (h + 1) * HEAD_DIM)
                lcols = slice(h * LSE_REP, (h + 1) * LSE_REP)
                if u > 0:
                    before = slice((u - 1) * BLOCK, u * BLOCK)
                    prev = (k_ref, v_ref, before, prev_ok)
                elif has_halo:
                    prev = (kh_ref, vh_ref, slice(0, BLOCK), jnp.logical_and(prev_ok, halo_ok))
                else:
                    prev = None
            else:
                rows = slice(0, BLOCK)
                cols = slice(u * WIDTH + h * HEAD_DIM, u * WIDTH + (h + 1) * HEAD_DIM)
                lcols = slice((u * HEADS + h) * LSE_REP, (u * HEADS + h + 1) * LSE_REP)
                prev = None
            units.append((rows, cols, lcols, prev))

    scores = []
    for rows, cols, _, prev in units:
        q = q_ref[rows, cols]
        s_o = jnp.where(own_ok, _dot_nt(q, k_ref[rows, cols]), -jnp.inf)
        s_p = None
        if prev is not None:
            kp_ref, _, prows, ok = prev
            s_p = jnp.where(ok, _dot_nt(q, kp_ref[prows, cols]), -jnp.inf)
        scores.append((s_o, s_p))

    probs = []
    for s_o, s_p in scores:
        m = jnp.max(s_o, axis=-1, keepdims=True)
        if s_p is not None:
            m = jnp.maximum(m, jnp.max(s_p, axis=-1, keepdims=True))
        p_o = jnp.exp(s_o - m)
        l = jnp.sum(p_o, axis=-1, keepdims=True)
        p_p = None
        if s_p is not None:
            p_p = jnp.exp(s_p - m)
            l = l + jnp.sum(p_p, axis=-1, keepdims=True)
            p_p = p_p.astype(BF16)
        probs.append((p_o.astype(BF16), p_p, 1.0 / l, m + jnp.log(l)))

    for (rows, cols, lcols, prev), (p_o, p_p, inv_l, lse) in zip(units, probs):
        acc = _dot(p_o, v_ref[rows, cols])
        if prev is not None:
            _, vp_ref, prows, _ = prev
            acc = acc + _dot(p_p, vp_ref[prows, cols])
        o_ref[rows, cols] = (acc * inv_l).astype(BF16)
        lse_ref[rows, lcols] = jnp.broadcast_to(lse, (BLOCK, LSE_REP))


def _att_group(q, k, v, dil):
    b, sub_len, _ = q.shape
    lw = HEADS * LSE_REP
    span = ATT_UNITS * BLOCK
    by_rows = sub_len >= span
    has_halo = sub_len > span
    if by_rows:
        assert sub_len % span == 0
        n = (sub_len // span) * dil
        per_seq = sub_len // span
        tile = lambda w: pl.BlockSpec((None, span, w), lambda bi, j: (bi, j % per_seq, j // per_seq))
        halo = pl.BlockSpec((None, BLOCK, WIDTH),
                            lambda bi, j: (bi, jnp.maximum((j % per_seq) * ATT_UNITS - 1, 0), j // per_seq))
        assert not has_halo or dil == 1
    else:
        assert sub_len == BLOCK and dil % ATT_UNITS == 0
        n = dil // ATT_UNITS
        tile = lambda w: pl.BlockSpec((None, BLOCK, ATT_UNITS * w), lambda bi, j: (bi, 0, j))
    in_specs = [tile(WIDTH)] * 3 + ([halo, halo] if has_halo else [])
    args = (q, k, v) + ((k, v) if has_halo else ())
    return pl.pallas_call(
        functools.partial(_att_body, by_rows=by_rows, has_halo=has_halo),
        grid=(b, n),
        in_specs=in_specs,
        out_specs=[tile(WIDTH), tile(lw)],
        out_shape=[jax.ShapeDtypeStruct((b, sub_len, dil * WIDTH), BF16),
                   jax.ShapeDtypeStruct((b, sub_len, dil * lw), F32)],
        compiler_params=_params("parallel", "arbitrary"),
        name=f"att_dil{dil}",
    )(*args)


N_LEVELS = 7
ROW_CUM, ROW_REM = N_LEVELS, N_LEVELS + 1


def _hgrn_constants():
    r = np.arange(BLOCK)[:, None]
    s = np.arange(BLOCK)[None, :]
    mats = []
    for lvl in range(N_LEVELS):
        half = 1 << lvl
        mid = (r // (2 * half)) * 2 * half + half
        upper = (r % (2 * half)) >= half
        mats.append(np.where(upper, (s >= mid) & (s <= r), (s > r) & (s <= mid - 1)))
    mats.append(s <= r)
    mats.append(s > r)
    range_mat = np.concatenate(mats, axis=0).astype(np.float32)
    range_mat = np.concatenate([range_mat, range_mat], axis=1)
    x = r ^ s
    level = np.where(s < r, np.floor(np.log2(np.maximum(x, 1))).astype(np.int32), np.where(s == r, N_LEVELS, -1))
    return jnp.asarray(range_mat, BF16), jnp.asarray(level, jnp.int32)


HGRN_SEQS = 2


def _hgrn_intra(q_b, k_b, lf, rng, level):
    lf_hi = lf.astype(BF16)
    lf_lo = (lf - lf_hi.astype(F32)).astype(BF16)
    z = _dot(rng, jnp.concatenate([lf_hi, lf_lo], axis=0))
    q = q_b.astype(F32)
    kk = k_b.astype(F32)
    lhs, rhs = [], []
    for lvl in range(N_LEVELS):
        x = jnp.exp(z[lvl * BLOCK:(lvl + 1) * BLOCK])
        lhs.append((q * x).astype(BF16))
        rhs.append((kk * x).astype(BF16))
    lhs.append(q_b)
    rhs.append(k_b)
    b_cum = z[ROW_CUM * BLOCK:(ROW_CUM + 1) * BLOCK]
    q_in = (q * jnp.exp(b_cum)).astype(BF16)
    k_out = (kk * jnp.exp(z[ROW_REM * BLOCK:(ROW_REM + 1) * BLOCK])).astype(BF16)
    decay = jnp.exp(b_cum[BLOCK - 1:BLOCK, :])
    mats = []
    for h in range(HEADS):
        sl = slice(h * HEAD_DIM, (h + 1) * HEAD_DIM)
        a = jnp.zeros((BLOCK, BLOCK), F32)
        for lvl in range(N_LEVELS + 1):
            a = jnp.where(level == lvl, _dot_nt(lhs[lvl][:, sl], rhs[lvl][:, sl]), a)
        mats.append(a.astype(BF16))
    return mats, q_in, k_out, decay


def _hgrn_state(mats, q_in, k_out, decay, v_b, gain, st_ref, j):
    outs = []
    for h in range(HEADS):
        sl = slice(h * HEAD_DIM, (h + 1) * HEAD_DIM)
        v = v_b[:, sl]
        st = st_ref[j, h]
        o = _dot(mats[h], v) + _dot_nt(q_in[:, sl], st.astype(BF16))
        outs.append(_rms(o, gain[:, sl]).astype(BF16))
        v_t = v.astype(F32).T.astype(BF16)
        st_ref[j, h] = st * decay[:, sl] + _dot(v_t, k_out[:, sl])
    return jnp.concatenate(outs, axis=-1)


def _hgrn_body(q_ref, k_ref, lf_ref, v_ref, rng_ref, lvl_ref, gain_ref, o_ref, s_ref, st_ref, *, n_chunks):
    c = pl.program_id(1)

    @pl.when(c == 0)
    def _():
        st_ref[...] = jnp.zeros_like(st_ref)

    rng, level, gain = rng_ref[...], lvl_ref[...], gain_ref[...]
    intra = [_hgrn_intra(q_ref[j], k_ref[j], lf_ref[j], rng, level) for j in range(HGRN_SEQS)]
    for j in range(HGRN_SEQS):
        o_ref[j] = _hgrn_state(*intra[j], v_ref[j], gain, st_ref, j)

    @pl.when(c == n_chunks - 1)
    def _():
        for j in range(HGRN_SEQS):
            for h in range(HEADS):
                s_ref[j, h] = st_ref[j, h].T


def _hgrn_prompt(hq, hk, lf, hv, gain, b, t):
    n_chunks = t // BLOCK
    range_mat, level = _hgrn_constants()
    as3d = lambda a: a.reshape(b, t, WIDTH)
    spec = pl.BlockSpec((HGRN_SEQS, BLOCK, WIDTH), lambda bi, c: (bi, c, 0))
    state_spec = pl.BlockSpec((None, HGRN_SEQS, HEADS, HEAD_DIM, HEAD_DIM), lambda bi, c: (0, bi, 0, 0, 0))
    o, state = pl.pallas_call(
        functools.partial(_hgrn_body, n_chunks=n_chunks),
        grid=(b // HGRN_SEQS, n_chunks),
        in_specs=[spec, spec, spec, spec, _const_spec(range_mat.shape), _const_spec(level.shape),
                  _const_spec((1, WIDTH))],
        out_specs=[spec, state_spec],
        out_shape=[jax.ShapeDtypeStruct((b, t, WIDTH), BF16),
                   jax.ShapeDtypeStruct((1, b, HEADS, HEAD_DIM, HEAD_DIM), F32)],
        scratch_shapes=[pltpu.VMEM((HGRN_SEQS, HEADS, HEAD_DIM, HEAD_DIM), F32)],
        compiler_params=_params("parallel", "arbitrary"),
        name="hgrn_prompt",
    )(as3d(hq), as3d(hk), as3d(lf), as3d(hv), range_mat, level, gain)
    return o.reshape(b * t, WIDTH), state


def _memkv_body(m_ref, g_ref, wk_ref, wv_ref, pk_ref, pv_ref, kb_ref, vb_ref):
    h = _rms(m_ref[...], g_ref[...]).astype(BF16)
    k = _dot(h, wk_ref[...])
    v = _dot(h, wv_ref[...])
    _store_heads(pk_ref, k)
    _store_heads(pv_ref, v)
    kb_ref[...] = k.astype(BF16)
    vb_ref[...] = v.astype(BF16)


def _mem_kv(mem, g_mem, w_mk, w_mv):
    b = mem.shape[0]
    head_spec = pl.BlockSpec((None, None, MEM_LEN, HEADS, HEAD_DIM), lambda bi: (0, bi, 0, 0, 0))
    flat_spec = pl.BlockSpec((None, MEM_LEN, WIDTH), lambda bi: (bi, 0, 0))
    head_shape = jax.ShapeDtypeStruct((1, b, MEM_LEN, HEADS, HEAD_DIM), F32)
    flat_shape = jax.ShapeDtypeStruct((b, MEM_LEN, WIDTH), BF16)
    return pl.pallas_call(
        _memkv_body,
        grid=(b,),
        in_specs=[pl.BlockSpec((None, MEM_LEN, D_MODEL), lambda bi: (bi, 0, 0)),
                  _const_spec((1, D_MODEL)), _const_spec((D_MODEL, WIDTH)), _const_spec((D_MODEL, WIDTH))],
        out_specs=[head_spec, head_spec, flat_spec, flat_spec],
        out_shape=[head_shape, head_shape, flat_shape, flat_shape],
        compiler_params=_params("parallel"),
        name="mem_kv",
    )(mem, g_mem, w_mk, w_mv)


def _from_strided_view(src_ref, stage_ref, dil, width):
    if dil == 1:
        return src_ref[...].astype(F32)
    n = src_ref.shape[0]
    n_col = width // 128
    for r in range(dil):
        for c in range(n_col):
            stage_ref[c, pl.ds(r, n, stride=dil), :] = (
                src_ref[:, r * width + c * 128:r * width + (c + 1) * 128].astype(F32))
    return jnp.concatenate([stage_ref[c] for c in range(n_col)], axis=-1)


def _combine_body(o0_ref, o1_ref, o2_ref, l0_ref, l1_ref, l2_ref, mq_ref, mk_ref, mv_ref, att_ref, mem_ref,
                  *stage_refs):
    outs = [_from_strided_view(r, stage_refs[2 * g], ATT_GROUPS[g][1], WIDTH)
            for g, r in enumerate((o0_ref, o1_ref, o2_ref))]
    lses = [_from_strided_view(r, stage_refs[2 * g + 1], ATT_GROUPS[g][1], HEADS * LSE_REP)
            for g, r in enumerate((l0_ref, l1_ref, l2_ref))]
    for h in range(HEADS):
        sl = slice(h * HEAD_DIM, (h + 1) * HEAD_DIM)
        lse = [x[:, h * LSE_REP:h * LSE_REP + 1] for x in lses]
        m = jnp.maximum(jnp.maximum(lse[0], lse[1]), lse[2])
        e = [jnp.exp(x - m) for x in lse]
        inv = 1.0 / (e[0] + e[1] + e[2])
        att = sum((e[g] * inv) * outs[g][:, sl] for g in range(N_GROUPS))
        att_ref[:, sl] = att.astype(BF16)
        s = _dot_nt(mq_ref[:, sl], mk_ref[:, sl])
        p = jnp.exp(s - jnp.max(s, axis=-1, keepdims=True))
        o = _dot(p.astype(BF16), mv_ref[:, sl]) / jnp.sum(p, axis=-1, keepdims=True)
        mem_ref[:, sl] = o.astype(BF16)


def _combine(outs, lses, mq, mk, mv, b, t, *, tm):
    n_tiles = t // tm
    lw = HEADS * LSE_REP
    row = pl.BlockSpec((tm, WIDTH), lambda bi, i: (bi * n_tiles + i, 0))
    o_specs = [pl.BlockSpec((None, tm // d, d * WIDTH), lambda bi, i: (bi, i, 0)) for _, d in ATT_GROUPS]
    l_specs = [pl.BlockSpec((None, tm // d, d * lw), lambda bi, i: (bi, i, 0)) for _, d in ATT_GROUPS]
    memspec = pl.BlockSpec((None, MEM_LEN, WIDTH), lambda bi, i: (bi, 0, 0))
    shape = jax.ShapeDtypeStruct((b * t, WIDTH), BF16)
    stages = []
    for _ in ATT_GROUPS:
        stages += [pltpu.VMEM((WIDTH // 128, tm, 128), F32), pltpu.VMEM((lw // 128, tm, 128), F32)]
    return pl.pallas_call(
        _combine_body,
        grid=(b, n_tiles),
        in_specs=o_specs + l_specs + [row, memspec, memspec],
        out_specs=[row, row],
        out_shape=[shape, shape],
        scratch_shapes=stages,
        compiler_params=_params("parallel", "parallel"),
        name="combine",
    )(*outs, *lses, mq, mk, mv)


def _merge_body(x_ref, att_ref, hg_ref, mem_ref, gate_ref, wa_ref, wb_ref, wc_ref, wo_ref, o_ref):
    def gate(j):
        return gate_ref[:, j * D_MODEL:(j + 1) * D_MODEL].astype(F32)

    m = (gate(0) * _dot(att_ref[...].astype(BF16), wa_ref[...])
         + gate(1) * _dot(hg_ref[...].astype(BF16), wb_ref[...])
         + gate(2) * _dot(mem_ref[...].astype(BF16), wc_ref[...]))
    o_ref[...] = x_ref[...] + _dot(m.astype(BF16), wo_ref[...])


def _merge(x, att, hg, mem, gates, wa, wb, wc, wo, *, tm):
    m = x.shape[0]
    row = pl.BlockSpec((tm, WIDTH), lambda i: (i, 0))
    wide = pl.BlockSpec((tm, D_MODEL), lambda i: (i, 0))
    return pl.pallas_call(
        _merge_body,
        grid=(m // tm,),
        in_specs=[wide, row, row, row, pl.BlockSpec((tm, N_BRANCH * D_MODEL), lambda i: (i, 0)),
                  _const_spec((WIDTH, D_MODEL)), _const_spec((WIDTH, D_MODEL)), _const_spec((WIDTH, D_MODEL)),
                  _const_spec((D_MODEL, D_MODEL))],
        out_specs=wide,
        out_shape=jax.ShapeDtypeStruct((m, D_MODEL), F32),
        compiler_params=_params("parallel"),
        name="merge",
    )(x, att, hg, mem, gates, wa, wb, wc, wo)


def _norm_matmul_body(x_ref, g_ref, w_ref, o_ref):
    o_ref[...] = _dot(_rms(x_ref[...], g_ref[...]).astype(BF16), w_ref[...])


def _sample_in_proj(x, g_mix, w_in):
    m = x.shape[0]
    n = w_in.shape[1]
    return pl.pallas_call(
        _norm_matmul_body,
        grid=(n // WIDTH,),
        in_specs=[_const_spec((m, D_MODEL)), _const_spec((1, D_MODEL)),
                  pl.BlockSpec((D_MODEL, WIDTH), lambda j: (0, j))],
        out_specs=pl.BlockSpec((m, WIDTH), lambda j: (0, j)),
        out_shape=jax.ShapeDtypeStruct((m, n), F32),
        compiler_params=_params("parallel"),
        name="sample_in_proj",
    )(x, g_mix, w_in)


def _one_query_attend(q, k_rows, v_rows, k_new=None, v_new=None):
    s = jnp.sum(k_rows * q, axis=-1, keepdims=True)
    m = jnp.max(s, axis=0, keepdims=True)
    if k_new is not None:
        s_new = jnp.sum(k_new * q, axis=-1, keepdims=True)
        m = jnp.maximum(m, s_new)
    p = jnp.exp(s - m)
    l = jnp.sum(p, axis=0, keepdims=True)
    acc = jnp.sum(p * v_rows, axis=0, keepdims=True)
    if k_new is not None:
        p_new = jnp.exp(s_new - m)
        l = l + p_new
        acc = acc + p_new * v_new
    return acc / l, m + jnp.log(l)


def _sample_att_body(q_ref, kn_ref, vn_ref, mq_ref, k1_ref, v1_ref, k4_ref, v4_ref, k16_ref, v16_ref,
                     mk_ref, mv_ref, att_ref, mem_ref):
    k_refs = (k1_ref, k4_ref, k16_ref)
    v_refs = (v1_ref, v4_ref, v16_ref)
    for h in range(HEADS):
        sl = slice(h * HEAD_DIM, (h + 1) * HEAD_DIM)
        outs, lses = [], []
        for g in range(N_GROUPS):
            gs = slice(g * WIDTH + h * HEAD_DIM, g * WIDTH + (h + 1) * HEAD_DIM)
            o, lse = _one_query_attend(q_ref[:, gs] * QK_SCALE, _load_head(k_refs[g], h), _load_head(v_refs[g], h),
                                       kn_ref[:, gs], vn_ref[:, gs])
            outs.append(o)
            lses.append(lse)
        m = jnp.maximum(jnp.maximum(lses[0], lses[1]), lses[2])
        e = [jnp.exp(x - m) for x in lses]
        inv = 1.0 / (e[0] + e[1] + e[2])
        att_ref[:, sl] = sum((e[g] * inv) * outs[g] for g in range(N_GROUPS))
        o, _ = _one_query_attend(mq_ref[:, sl] * QK_SCALE, _load_head(mk_ref, h), _load_head(mv_ref, h))
        mem_ref[:, sl] = o


def _sample_attend(z3, caches_k, caches_v, cache_mk, cache_mv):
    bd = z3.shape[0]
    zspec = lambda col, n: pl.BlockSpec((None, 1, n * WIDTH), lambda b: (b, 0, col // n))
    cache_specs, cache_args = [], []
    for (win, dil), ck, cv in zip(ATT_GROUPS, caches_k, caches_v):
        rows = ck.shape[2]
        assert rows == win and rows // dil == BLOCK
        spec = pl.BlockSpec((None, None, BLOCK, None, HEADS, HEAD_DIM), lambda b: (0, b, 0, 0, 0, 0))
        for c in (ck, cv):
            cache_specs.append(spec)
            cache_args.append(c.reshape(1, bd, BLOCK, dil, HEADS, HEAD_DIM))
    mem_spec = pl.BlockSpec((None, None, MEM_LEN, HEADS, HEAD_DIM), lambda b: (0, b, 0, 0, 0))
    out_spec = pl.BlockSpec((None, 1, WIDTH), lambda b: (b, 0, 0))
    out_shape = jax.ShapeDtypeStruct((bd, 1, WIDTH), F32)
    return pl.pallas_call(
        _sample_att_body,
        grid=(bd,),
        in_specs=[zspec(COL_Q, 3), zspec(COL_K, 3), zspec(COL_V, 3), zspec(COL_MQ, 1)] + cache_specs
                 + [mem_spec, mem_spec],
        out_specs=[out_spec, out_spec],
        out_shape=[out_shape, out_shape],
        compiler_params=_params("parallel"),
        name="sample_attend",
    )(z3, z3, z3, z3, *cache_args, cache_mk, cache_mv)


SAMPLE_GROUP = 8


def _sample_hgrn_body(zq_ref, zf_ref, zi_ref, lb_ref, gain_ref, s_ref, o_ref, so_ref):
    lb = _lower_bound(lb_ref)
    gain = gain_ref[...]
    q = jax.nn.sigmoid(zq_ref[...])
    kk = (1.0 - lb) * jax.nn.sigmoid(zf_ref[...])
    f = lb + kk
    v = zi_ref[...]
    pad = jnp.zeros((HEAD_DIM - SAMPLE_GROUP, HEAD_DIM), F32)

    def columns(a, h):
        return jnp.concatenate([a[:, h * HEAD_DIM:(h + 1) * HEAD_DIM], pad], axis=0).T

    for h in range(HEADS):
        sl = slice(h * HEAD_DIM, (h + 1) * HEAD_DIM)
        q_t, k_t, f_t = columns(q, h), columns(kk, h), columns(f, h)
        for j in range(SAMPLE_GROUP):
            s1 = f_t[:, j:j + 1] * s_ref[j, h] + k_t[:, j:j + 1] * v[j:j + 1, sl]
            so_ref[j, h] = s1
            o = jnp.sum(q_t[:, j:j + 1] * s1, axis=0, keepdims=True)
            o_ref[j:j + 1, sl] = _rms(o, gain[:, sl])


def _sample_hgrn(z, lb_logits, gain, state):
    bd = z.shape[0]
    zspec = lambda col: pl.BlockSpec((SAMPLE_GROUP, WIDTH), lambda i: (i, col))
    sspec = pl.BlockSpec((None, SAMPLE_GROUP, HEADS, HEAD_DIM, HEAD_DIM), lambda i: (0, i, 0, 0, 0))
    return pl.pallas_call(
        _sample_hgrn_body,
        grid=(bd // SAMPLE_GROUP,),
        in_specs=[zspec(COL_HQ), zspec(COL_HF), zspec(COL_HI), _const_spec(lb_logits.shape),
                  _const_spec((1, WIDTH)), sspec],
        out_specs=[pl.BlockSpec((SAMPLE_GROUP, WIDTH), lambda i: (i, 0)), sspec],
        out_shape=[jax.ShapeDtypeStruct((bd, WIDTH), F32), jax.ShapeDtypeStruct(state.shape, F32)],
        compiler_params=_params("parallel"),
        name="sample_hgrn",
    )(z, z, z, lb_logits, gain, state)


def _gate_body(z_ref, o_ref):
    o_ref[...] = jax.nn.sigmoid(z_ref[...]).astype(BF16)


def _sample_gates(z):
    bd = z.shape[0]
    n = N_BRANCH * D_MODEL
    return pl.pallas_call(
        _gate_body,
        grid=(n // WIDTH,),
        in_specs=[pl.BlockSpec((bd, WIDTH), lambda j: (0, COL_GATE + j))],
        out_specs=pl.BlockSpec((bd, WIDTH), lambda j: (0, j)),
        out_shape=jax.ShapeDtypeStruct((bd, n), BF16),
        name="sample_gates",
    )(z)


def kernel(x_prompt, x_sample, mem_prompt, cache_win1_k, cache_win1_v, cache_win4_k, cache_win4_v, cache_win16_k, cache_win16_v, cache_mem_k, cache_mem_v, state_hgrn, g_ff1, w_ff1_gate, w_ff1_up, w_ff1_down, g_mix, w_in, hg_lb_logits, g_hg_out, g_mem, w_mem_k, w_mem_v, w_branch_att, w_branch_hg, w_branch_mem, w_out, g_ff2, w_ff2_gate, w_ff2_up, w_ff2_down, g_final):
    b, t, _ = x_prompt.shape
    bd = x_sample.shape[0]
    assert g_ff1.shape[0] == 1 and x_sample.shape[1] == 1 and w_in.shape[2] == IN_WIDTH
    bf = lambda w: w[0].astype(BF16)
    ffn1 = (g_ff1, bf(w_ff1_gate), bf(w_ff1_up), bf(w_ff1_down), g_final.reshape(1, D_MODEL))
    ffn2 = (g_ff2, bf(w_ff2_gate), bf(w_ff2_up), bf(w_ff2_down), g_final.reshape(1, D_MODEL))
    w_in_b = bf(w_in)
    w_qkv, w_rest = w_in_b[:, :COL_HQ * WIDTH], w_in_b[:, COL_HQ * WIDTH:]
    branch_w = (bf(w_branch_att), bf(w_branch_hg), bf(w_branch_mem), bf(w_out))

    xs, _ = _ffn(x_sample.reshape(bd, D_MODEL), *ffn1, tm=bd, final=False)
    zs = _sample_in_proj(xs, g_mix, w_in_b)
    caches_k = (cache_win1_k, cache_win4_k, cache_win16_k)
    caches_v = (cache_win1_v, cache_win4_v, cache_win16_v)
    z3 = zs.reshape(bd, 1, IN_WIDTH)
    att_s, mem_s = _sample_attend(z3, caches_k, caches_v, cache_mem_k, cache_mem_v)
    hg_s, s_hgrn = _sample_hgrn(zs, hg_lb_logits, g_hg_out, state_hgrn)
    xs = _merge(xs, att_s.reshape(bd, WIDTH), hg_s, mem_s.reshape(bd, WIDTH), _sample_gates(zs), *branch_w, tm=bd)
    y_sample = _ffn(xs, *ffn2, tm=bd, final=True)[0].reshape(bd, 1, D_MODEL)

    tm = b * t // bd
    xp, (s_win16_k,) = _ffn(x_prompt.reshape(b * t, D_MODEL), *ffn1, tm=tm, final=False,
                            ride=(z3, [(cache_win16_k, 2, False)]))
    qkv = _qkv_proj(xp.reshape(b, t, D_MODEL), g_mix, w_qkv, tm=512)
    q_g, k_g, v_g, p_k, p_v = qkv[0:3], qkv[3:6], qkv[6:9], qkv[9:12], qkv[12:15]
    (hq, hk, lf, hv, mq, gates), (s_win1_k, s_win1_v, s_win4_k, s_win4_v) = _rest_proj(
        xp, g_mix, w_rest, hg_lb_logits, tm=tm,
        ride=(z3, [(cache_win1_k, 0, False), (cache_win1_v, 0, True), (cache_win4_k, 1, False), (cache_win4_v, 1, True)]))
    outs, lses = [], []
    for g, (_, dil) in enumerate(ATT_GROUPS):
        o, lse = _att_group(q_g[g], k_g[g], v_g[g], dil)
        outs.append(o)
        lses.append(lse)
    hg_p, p_hgrn = _hgrn_prompt(hq, hk, lf, hv, g_hg_out, b, t)
    p_mem_k, p_mem_v, mk_b, mv_b = _mem_kv(mem_prompt, g_mem, bf(w_mem_k), bf(w_mem_v))
    att_p, mem_p = _combine(outs, lses, mq, mk_b, mv_b, b, t, tm=512)
    xp = _merge(xp, att_p, hg_p, mem_p, gates, *branch_w, tm=512)
    y_prompt, (s_win16_v,) = _ffn(xp, *ffn2, tm=tm, final=True, ride=(z3, [(cache_win16_v, 2, True)]))

    return (y_prompt.reshape(b, t, D_MODEL), y_sample,
            p_k[0], p_v[0], p_k[1], p_v[1], p_k[2], p_v[2],
            p_mem_k, p_mem_v, p_hgrn,
            s_win1_k, s_win1_v, s_win4_k, s_win4_v, s_win16_k, s_win16_v,
            s_hgrn)
```

```python
import functools

import numpy as np
import jax
import jax.numpy as jnp
from jax import lax
from jax.experimental import pallas as pl
from jax.experimental.pallas import tpu as pltpu

F32 = jnp.float32
BF16 = jnp.bfloat16

D_MODEL = 1024
D_FF = 2816
EPS = 1e-6
HEADS = 4
HEAD_DIM = 128
WIDTH = HEADS * HEAD_DIM
ATT_GROUPS = ((128, 1), (512, 4), (2048, 16))
N_GROUPS = 3
BLOCK = 128
MEM_LEN = 256
N_BRANCH = 3
LSE_REP = 32
QK_SCALE = HEAD_DIM ** -0.5
VMEM_LIMIT_BYTES = 56 * 1024 * 1024

COL_Q, COL_K, COL_V, COL_HQ, COL_HF, COL_HI, COL_MQ, COL_GATE = 0, 3, 6, 9, 10, 11, 12, 13
IN_WIDTH = (COL_GATE + N_BRANCH * D_MODEL // WIDTH) * WIDTH


def _params(*sem):
    return pltpu.CompilerParams(dimension_semantics=sem, vmem_limit_bytes=VMEM_LIMIT_BYTES)


def _const_spec(shape):
    nd = len(shape)
    return pl.BlockSpec(shape, lambda *_: (0,) * nd, pipeline_mode=pl.Buffered(1))


def _rms(x, g):
    return x * lax.rsqrt(jnp.mean(x * x, axis=-1, keepdims=True) + EPS) * g


def _dot(a, b):
    return jnp.dot(a, b, preferred_element_type=F32)


def _dot_nt(a, b):
    return lax.dot_general(a, b, (((1,), (1,)), ((), ())), preferred_element_type=F32)


def _ride_specs(ride, n_steps):
    if not ride:
        return [], [], [], [], [], ()
    z3, riders = ride
    assert z3.shape[0] == n_steps
    zspec = lambda col: pl.BlockSpec((None, 1, N_GROUPS * WIDTH), lambda i: (i, 0, col // N_GROUPS))
    in_specs = [pl.BlockSpec((1, 1, c.shape[2], HEADS, HEAD_DIM), lambda i: (0, i, 0, 0, 0)) for c, _, _ in riders]
    out_specs = [pl.BlockSpec(memory_space=pl.ANY)] * len(riders)
    shapes = [jax.ShapeDtypeStruct(c.shape, c.dtype) for c, _, _ in riders]
    scratch = ([pltpu.VMEM((1, 1, 1, HEADS, HEAD_DIM), F32)] * len(riders)
               + [pltpu.SemaphoreType.DMA((2 * len(riders),))])
    return ([zspec(COL_K), zspec(COL_V)] + in_specs, [z3, z3] + [c for c, _, _ in riders], out_specs, shapes, scratch,
            tuple((g, is_v) for _, g, is_v in riders))


def _ride_copies(in_refs, out_refs, scratch_refs, meta):
    kn_ref, vn_ref = in_refs[:2]
    row_refs, sem = scratch_refs[:-1], scratch_refs[-1]
    seq = pl.program_id(0)
    copies = []
    for j, ((g, is_v), cache_ref, out_ref, row_ref) in enumerate(zip(meta, in_refs[2:], out_refs, row_refs)):
        rows = cache_ref.shape[2]
        new = (vn_ref if is_v else kn_ref)[:, g * WIDTH:(g + 1) * WIDTH]
        for h in range(HEADS):
            row_ref[0, 0, :, h, :] = new[:, h * HEAD_DIM:(h + 1) * HEAD_DIM]
        here = (slice(None), pl.ds(seq, 1))
        copies.append(pltpu.make_async_copy(cache_ref.at[:, :, pl.ds(1, rows - 1)],
                                            out_ref.at[here + (pl.ds(0, rows - 1),)], sem.at[2 * j]))
        copies.append(pltpu.make_async_copy(row_ref, out_ref.at[here + (pl.ds(rows - 1, 1),)], sem.at[2 * j + 1]))
    return copies


FF_CHUNKS = 1


def _ffn_body(x_ref, g_ref, wg_ref, wu_ref, wd_ref, gf_ref, *rest, final, ride):
    n_in = 2 + len(ride) if ride else 0
    o_ref = rest[n_in]
    copies = []
    if ride:
        copies = _ride_copies(rest[:n_in], rest[n_in + 1:n_in + 1 + len(ride)], rest[n_in + 1 + len(ride):], ride)
        for cp in copies:
            cp.start()
    x = x_ref[...]
    h = _rms(x, g_ref[...]).astype(BF16)
    y = x
    width = D_FF // FF_CHUNKS
    for c in range(FF_CHUNKS):
        cols = slice(c * width, (c + 1) * width)
        a = _dot(h, wg_ref[:, cols])
        u = _dot(h, wu_ref[:, cols])
        act = (a * jax.nn.sigmoid(a) * u).astype(BF16)
        y = y + 0.5 * _dot(act, wd_ref[cols, :])
    if final:
        y = _rms(y, gf_ref[...])
    o_ref[...] = y
    for cp in copies:
        cp.wait()


def _ffn(x, g, wg, wu, wd, g_final, *, tm, final, ride=None):
    m = x.shape[0]
    r_in, r_args, r_out, r_shapes, r_scratch, meta = _ride_specs(ride, m // tm)
    res = pl.pallas_call(
        functools.partial(_ffn_body, final=final, ride=meta),
        grid=(m // tm,),
        in_specs=[pl.BlockSpec((tm, D_MODEL), lambda i: (i, 0)),
                  _const_spec((1, D_MODEL)),
                  _const_spec((D_MODEL, D_FF)), _const_spec((D_MODEL, D_FF)), _const_spec((D_FF, D_MODEL)),
                  _const_spec((1, D_MODEL))] + r_in,
        out_specs=[pl.BlockSpec((tm, D_MODEL), lambda i: (i, 0))] + r_out,
        out_shape=[jax.ShapeDtypeStruct((m, D_MODEL), F32)] + r_shapes,
        scratch_shapes=r_scratch,
        compiler_params=_params("parallel"),
        name="ffn_final" if final else "ffn",
    )(x, g, wg, wu, wd, g_final, *r_args)
    return res[0], res[1:]


def _store_heads(ref, z):
    rows = ref.shape[0]
    flat = ref.reshape(rows * HEADS, HEAD_DIM)
    for h in range(HEADS):
        flat[pl.ds(h, rows, stride=HEADS), :] = z[:, h * HEAD_DIM:(h + 1) * HEAD_DIM]


def _load_head(ref, h):
    rows = ref.shape[0]
    return ref.reshape(rows * HEADS, HEAD_DIM)[pl.ds(h, rows, stride=HEADS), :]


def _to_strided_view(dst_ref, z, stage_ref, dil):
    if dil == 1:
        dst_ref[...] = z.astype(dst_ref.dtype)
        return
    n = z.shape[0] // dil
    for c in range(stage_ref.shape[0]):
        stage_ref[c] = z[:, c * 128:(c + 1) * 128]
    for r in range(dil):
        for c in range(stage_ref.shape[0]):
            dst_ref[:, r * WIDTH + c * 128:r * WIDTH + (c + 1) * 128] = (
                stage_ref[c, pl.ds(r, n, stride=dil), :].astype(dst_ref.dtype))


def _qkv_body(x_ref, g_ref, w_ref, *refs, tm, n_tiles):
    q_refs, k_refs, v_refs = refs[0:3], refs[3:6], refs[6:9]
    pk_refs, pv_refs, stage_refs = refs[9:12], refs[12:15], refs[15:]
    h = _rms(x_ref[...], g_ref[...]).astype(BF16)
    late = []
    for g, (win, dil) in enumerate(ATT_GROUPS):
        keep = min(win, tm * n_tiles)
        for kind, (col, dst_refs, win_refs) in enumerate(((COL_Q, q_refs, None), (COL_K, k_refs, pk_refs),
                                                          (COL_V, v_refs, pv_refs))):
            z = _dot(h, w_ref[:, (col + g) * WIDTH:(col + g + 1) * WIDTH])
            if win_refs is None:
                z = z * QK_SCALE
            stage = stage_refs[3 * (g - 1) + kind] if dil > 1 else None
            _to_strided_view(dst_refs[g], z, stage, dil)
            if win_refs is None:
                continue
            if keep == tm * n_tiles:
                _store_heads(win_refs[g], z)
            elif dil == 1:
                late.append((win_refs[g], z[tm - keep:, :], None))
            else:
                assert keep == tm
                late.append((win_refs[g], None, stage))

    @pl.when(pl.program_id(1) == n_tiles - 1)
    def _():
        for ref, rows, stage in late:
            if stage is None:
                _store_heads(ref, rows)
            else:
                _store_heads(ref, jnp.concatenate([stage[hd] for hd in range(HEADS)], axis=-1))


def _qkv_proj(x, g_mix, w_qkv, *, tm):
    b, t, _ = x.shape
    n_tiles = t // tm
    view_specs = [pl.BlockSpec((None, tm // d, d * WIDTH), lambda bi, i: (bi, i, 0)) for _, d in ATT_GROUPS]
    view_shapes = [jax.ShapeDtypeStruct((b, t // d, d * WIDTH), BF16) for _, d in ATT_GROUPS]
    win_specs, win_shapes = [], []
    for win, _ in ATT_GROUPS:
        keep = min(win, t)
        assert keep == t or keep <= tm
        if keep == t:
            win_specs.append(pl.BlockSpec((None, None, tm, HEADS, HEAD_DIM), lambda bi, i: (0, bi, i, 0, 0)))
        else:
            win_specs.append(pl.BlockSpec((None, None, keep, HEADS, HEAD_DIM), lambda bi, i: (0, bi, 0, 0, 0)))
        win_shapes.append(jax.ShapeDtypeStruct((1, b, keep, HEADS, HEAD_DIM), F32))
    return pl.pallas_call(
        functools.partial(_qkv_body, tm=tm, n_tiles=n_tiles),
        grid=(b, n_tiles),
        in_specs=[pl.BlockSpec((None, tm, D_MODEL), lambda bi, i: (bi, i, 0)),
                  _const_spec((1, D_MODEL)),
                  _const_spec((D_MODEL, 3 * N_GROUPS * WIDTH))],
        out_specs=view_specs * 3 + win_specs + win_specs,
        out_shape=view_shapes * 3 + win_shapes + win_shapes,
        scratch_shapes=[pltpu.VMEM((WIDTH // 128, tm, 128), F32)] * (3 * sum(d > 1 for _, d in ATT_GROUPS)),
        compiler_params=_params("parallel", "arbitrary"),
        name="qkv_proj",
    )(x, g_mix, w_qkv)


def _lower_bound(logits_ref):
    lg = logits_ref[...]
    e = jnp.exp(lg - jnp.max(lg, axis=0, keepdims=True))
    return e[0:1, :] / jnp.sum(e, axis=0, keepdims=True)


def _rest_body(x_ref, g_ref, w_ref, lb_ref, *rest, ride):
    n_in = 2 + len(ride) if ride else 0
    hq_ref, hk_ref, lf_ref, hv_ref, mq_ref, gate_ref = rest[n_in:n_in + 6]
    copies = []
    if ride:
        copies = _ride_copies(rest[:n_in], rest[n_in + 6:n_in + 6 + len(ride)], rest[n_in + 6 + len(ride):], ride)
        for cp in copies:
            cp.start()
    h = _rms(x_ref[...], g_ref[...]).astype(BF16)
    lb = _lower_bound(lb_ref)
    hq_ref[...] = jax.nn.sigmoid(_dot(h, w_ref[:, 0:WIDTH])).astype(BF16)
    kk = (1.0 - lb) * jax.nn.sigmoid(_dot(h, w_ref[:, WIDTH:2 * WIDTH]))
    hk_ref[...] = kk.astype(BF16)
    lf_ref[...] = jnp.log(lb + kk)
    hv_ref[...] = _dot(h, w_ref[:, 2 * WIDTH:3 * WIDTH]).astype(BF16)
    mq_ref[...] = (_dot(h, w_ref[:, 3 * WIDTH:4 * WIDTH]) * QK_SCALE).astype(BF16)
    for c in range(N_BRANCH * D_MODEL // WIDTH):
        z = _dot(h, w_ref[:, (4 + c) * WIDTH:(5 + c) * WIDTH])
        gate_ref[:, c * WIDTH:(c + 1) * WIDTH] = jax.nn.sigmoid(z).astype(BF16)
    for cp in copies:
        cp.wait()


def _rest_proj(x, g_mix, w_rest, lb_logits, *, tm, ride=None):
    m = x.shape[0]
    spec = pl.BlockSpec((tm, WIDTH), lambda i: (i, 0))
    bf = jax.ShapeDtypeStruct((m, WIDTH), BF16)
    r_in, r_args, r_out, r_shapes, r_scratch, meta = _ride_specs(ride, m // tm)
    res = pl.pallas_call(
        functools.partial(_rest_body, ride=meta),
        grid=(m // tm,),
        in_specs=[pl.BlockSpec((tm, D_MODEL), lambda i: (i, 0)),
                  _const_spec((1, D_MODEL)),
                  _const_spec(w_rest.shape),
                  _const_spec(lb_logits.shape)] + r_in,
        out_specs=[spec, spec, spec, spec, spec, pl.BlockSpec((tm, N_BRANCH * D_MODEL), lambda i: (i, 0))] + r_out,
        out_shape=[bf, bf, jax.ShapeDtypeStruct((m, WIDTH), F32), bf, bf,
                   jax.ShapeDtypeStruct((m, N_BRANCH * D_MODEL), BF16)] + r_shapes,
        scratch_shapes=r_scratch,
        compiler_params=_params("parallel"),
        name="rest_proj",
    )(x, g_mix, w_rest, lb_logits, *r_args)
    return res[:6], res[6:]


ATT_UNITS = 4


def _att_body(q_ref, k_ref, v_ref, *rest, by_rows, has_halo):
    if has_halo:
        kh_ref, vh_ref, o_ref, lse_ref = rest
        halo_ok = pl.program_id(1) > 0
    else:
        o_ref, lse_ref = rest
    qi = lax.broadcasted_iota(jnp.int32, (BLOCK, BLOCK), 0)
    ki = lax.broadcasted_iota(jnp.int32, (BLOCK, BLOCK), 1)
    own_ok = ki <= qi
    prev_ok = ki >= qi

    units = []
    for u in range(ATT_UNITS):
        for h in range(HEADS):
            if by_rows:
                rows = slice(u * BLOCK, (u + 1) * BLOCK)
                cols = slice(h * HEAD_DIM, (h + 1) * HEAD_DIM)
                lcols = slice(h * LSE_REP, (h + 1) * LSE_REP)
                if u > 0:
                    before = slice((u - 1) * BLOCK, u * BLOCK)
                    prev = (k_ref, v_ref, before, prev_ok)
                elif has_halo:
                    prev = (kh_ref, vh_ref, slice(0, BLOCK), jnp.logical_and(prev_ok, halo_ok))
                else:
                    prev = None
            else:
                rows = slice(0, BLOCK)
                cols = slice(u * WIDTH + h * HEAD_DIM, u * WIDTH + (h + 1) * HEAD_DIM)
                lcols = slice((u * HEADS + h) * LSE_REP, (u * HEADS + h + 1) * LSE_REP)
                prev = None
            units.append((rows, cols, lcols, prev))

    scores = []
    for rows, cols, _, prev in units:
        q = q_ref[rows, cols]
        s_o = jnp.where(own_ok, _dot_nt(q, k_ref[rows, cols]), -jnp.inf)
        s_p = None
        if prev is not None:
            kp_ref, _, prows, ok = prev
            s_p = jnp.where(ok, _dot_nt(q, kp_ref[prows, cols]), -jnp.inf)
        scores.append((s_o, s_p))

    probs = []
    for s_o, s_p in scores:
        m = jnp.max(s_o, axis=-1, keepdims=True)
        if s_p is not None:
            m = jnp.maximum(m, jnp.max(s_p, axis=-1, keepdims=True))
        p_o = jnp.exp(s_o - m)
        l = jnp.sum(p_o, axis=-1, keepdims=True)
        p_p = None
        if s_p is not None:
            p_p = jnp.exp(s_p - m)
            l = l + jnp.sum(p_p, axis=-1, keepdims=True)
            p_p = p_p.astype(BF16)
        probs.append((p_o.astype(BF16), p_p, 1.0 / l, m + jnp.log(l)))

    for (rows, cols, lcols, prev), (p_o, p_p, inv_l, lse) in zip(units, probs):
        acc = _dot(p_o, v_ref[rows, cols])
        if prev is not None:
            _, vp_ref, prows, _ = prev
            acc = acc + _dot(p_p, vp_ref[prows, cols])
        o_ref[rows, cols] = (acc * inv_l).astype(BF16)
        lse_ref[rows, lcols] = jnp.broadcast_to(lse, (BLOCK, LSE_REP))


def _att_group(q, k, v, dil):
    b, sub_len, _ = q.shape
    lw = HEADS * LSE_REP
    span = ATT_UNITS * BLOCK
    by_rows = sub_len >= span
    has_halo = sub_len > span
    if by_rows:
        assert sub_len % span == 0
        n = (sub_len // span) * dil
        per_seq = sub_len // span
        tile = lambda w: pl.BlockSpec((None, span, w), lambda bi, j: (bi, j % per_seq, j // per_seq))
        halo = pl.BlockSpec((None, BLOCK, WIDTH),
                            lambda bi, j: (bi, jnp.maximum((j % per_seq) * ATT_UNITS - 1, 0), j // per_seq))
        assert not has_halo or dil == 1
    else:
        assert sub_len == BLOCK and dil % ATT_UNITS == 0
        n = dil // ATT_UNITS
        tile = lambda w: pl.BlockSpec((None, BLOCK, ATT_UNITS * w), lambda bi, j: (bi, 0, j))
    in_specs = [tile(WIDTH)] * 3 + ([halo, halo] if has_halo else [])
    args = (q, k, v) + ((k, v) if has_halo else ())
    return pl.pallas_call(
        functools.partial(_att_body, by_rows=by_rows, has_halo=has_halo),
        grid=(b, n),
        in_specs=in_specs,
        out_specs=[tile(WIDTH), tile(lw)],
        out_shape=[jax.ShapeDtypeStruct((b, sub_len, dil * WIDTH), BF16),
                   jax.ShapeDtypeStruct((b, sub_len, dil * lw), F32)],
        compiler_params=_params("parallel", "arbitrary"),
        name=f"att_dil{dil}",
    )(*args)


N_LEVELS = 7
ROW_CUM, ROW_REM = N_LEVELS, N_LEVELS + 1


def _hgrn_constants():
    r = np.arange(BLOCK)[:, None]
    s = np.arange(BLOCK)[None, :]
    mats = []
    for lvl in range(N_LEVELS):
        half = 1 << lvl
        mid = (r // (2 * half)) * 2 * half + half
        upper = (r % (2 * half)) >= half
        mats.append(np.where(upper, (s >= mid) & (s <= r), (s > r) & (s <= mid - 1)))
    mats.append(s <= r)
    mats.append(s > r)
    range_mat = np.concatenate(mats, axis=0).astype(np.float32)
    range_mat = np.concatenate([range_mat, range_mat], axis=1)
    x = r ^ s
    level = np.where(s < r, np.floor(np.log2(np.maximum(x, 1))).astype(np.int32), np.where(s == r, N_LEVELS, -1))
    return jnp.asarray(range_mat, BF16), jnp.asarray(level, jnp.int32)


HGRN_SEQS = 2


def _hgrn_intra(q_b, k_b, lf, rng, level):
    lf_hi = lf.astype(BF16)
    lf_lo = (lf - lf_hi.astype(F32)).astype(BF16)
    z = _dot(rng, jnp.concatenate([lf_hi, lf_lo], axis=0))
    q = q_b.astype(F32)
    kk = k_b.astype(F32)
    lhs, rhs = [], []
    for lvl in range(N_LEVELS):
        x = jnp.exp(z[lvl * BLOCK:(lvl + 1) * BLOCK])
        lhs.append((q * x).astype(BF16))
        rhs.append((kk * x).astype(BF16))
    lhs.append(q_b)
    rhs.append(k_b)
    b_cum = z[ROW_CUM * BLOCK:(ROW_CUM + 1) * BLOCK]
    q_in = (q * jnp.exp(b_cum)).astype(BF16)
    k_out = (kk * jnp.exp(z[ROW_REM * BLOCK:(ROW_REM + 1) * BLOCK])).astype(BF16)
    decay = jnp.exp(b_cum[BLOCK - 1:BLOCK, :])
    mats = []
    for h in range(HEADS):
        sl = slice(h * HEAD_DIM, (h + 1) * HEAD_DIM)
        a = jnp.zeros((BLOCK, BLOCK), F32)
        for lvl in range(N_LEVELS + 1):
            a = jnp.where(level == lvl, _dot_nt(lhs[lvl][:, sl], rhs[lvl][:, sl]), a)
        mats.append(a.astype(BF16))
    return mats, q_in, k_out, decay


def _hgrn_state(mats, q_in, k_out, decay, v_b, gain, st_ref, j):
    outs = []
    for h in range(HEADS):
        sl = slice(h * HEAD_DIM, (h + 1) * HEAD_DIM)
        v = v_b[:, sl]
        st = st_ref[j, h]
        o = _dot(mats[h], v) + _dot_nt(q_in[:, sl], st.astype(BF16))
        outs.append(_rms(o, gain[:, sl]).astype(BF16))
        v_t = v.astype(F32).T.astype(BF16)
        st_ref[j, h] = st * decay[:, sl] + _dot(v_t, k_out[:, sl])
    return jnp.concatenate(outs, axis=-1)


def _hgrn_body(q_ref, k_ref, lf_ref, v_ref, rng_ref, lvl_ref, gain_ref, o_ref, s_ref, st_ref, *, n_chunks):
    c = pl.program_id(1)

    @pl.when(c == 0)
    def _():
        st_ref[...] = jnp.zeros_like(st_ref)

    rng, level, gain = rng_ref[...], lvl_ref[...], gain_ref[...]
    intra = [_hgrn_intra(q_ref[j], k_ref[j], lf_ref[j], rng, level) for j in range(HGRN_SEQS)]
    for j in range(HGRN_SEQS):
        o_ref[j] = _hgrn_state(*intra[j], v_ref[j], gain, st_ref, j)

    @pl.when(c == n_chunks - 1)
    def _():
        for j in range(HGRN_SEQS):
            for h in range(HEADS):
                s_ref[j, h] = st_ref[j, h].T


def _hgrn_prompt(hq, hk, lf, hv, gain, b, t):
    n_chunks = t // BLOCK
    range_mat, level = _hgrn_constants()
    as3d = lambda a: a.reshape(b, t, WIDTH)
    spec = pl.BlockSpec((HGRN_SEQS, BLOCK, WIDTH), lambda bi, c: (bi, c, 0))
    state_spec = pl.BlockSpec((None, HGRN_SEQS, HEADS, HEAD_DIM, HEAD_DIM), lambda bi, c: (0, bi, 0, 0, 0))
    o, state = pl.pallas_call(
        functools.partial(_hgrn_body, n_chunks=n_chunks),
        grid=(b // HGRN_SEQS, n_chunks),
        in_specs=[spec, spec, spec, spec, _const_spec(range_mat.shape), _const_spec(level.shape),
                  _const_spec((1, WIDTH))],
        out_specs=[spec, state_spec],
        out_shape=[jax.ShapeDtypeStruct((b, t, WIDTH), BF16),
                   jax.ShapeDtypeStruct((1, b, HEADS, HEAD_DIM, HEAD_DIM), F32)],
        scratch_shapes=[pltpu.VMEM((HGRN_SEQS, HEADS, HEAD_DIM, HEAD_DIM), F32)],
        compiler_params=_params("parallel", "arbitrary"),
        name="hgrn_prompt",
    )(as3d(hq), as3d(hk), as3d(lf), as3d(hv), range_mat, level, gain)
    return o.reshape(b * t, WIDTH), state


def _memkv_body(m_ref, g_ref, wk_ref, wv_ref, pk_ref, pv_ref, kb_ref, vb_ref):
    h = _rms(m_ref[...], g_ref[...]).astype(BF16)
    k = _dot(h, wk_ref[...])
    v = _dot(h, wv_ref[...])
    _store_heads(pk_ref, k)
    _store_heads(pv_ref, v)
    kb_ref[...] = k.astype(BF16)
    vb_ref[...] = v.astype(BF16)


def _mem_kv(mem, g_mem, w_mk, w_mv):
    b = mem.shape[0]
    head_spec = pl.BlockSpec((None, None, MEM_LEN, HEADS, HEAD_DIM), lambda bi: (0, bi, 0, 0, 0))
    flat_spec = pl.BlockSpec((None, MEM_LEN, WIDTH), lambda bi: (bi, 0, 0))
    head_shape = jax.ShapeDtypeStruct((1, b, MEM_LEN, HEADS, HEAD_DIM), F32)
    flat_shape = jax.ShapeDtypeStruct((b, MEM_LEN, WIDTH), BF16)
    return pl.pallas_call(
        _memkv_body,
        grid=(b,),
        in_specs=[pl.BlockSpec((None, MEM_LEN, D_MODEL), lambda bi: (bi, 0, 0)),
                  _const_spec((1, D_MODEL)), _const_spec((D_MODEL, WIDTH)), _const_spec((D_MODEL, WIDTH))],
        out_specs=[head_spec, head_spec, flat_spec, flat_spec],
        out_shape=[head_shape, head_shape, flat_shape, flat_shape],
        compiler_params=_params("parallel"),
        name="mem_kv",
    )(mem, g_mem, w_mk, w_mv)


def _from_strided_view(src_ref, stage_ref, dil, width):
    if dil == 1:
        return src_ref[...].astype(F32)
    n = src_ref.shape[0]
    n_col = width // 128
    for r in range(dil):
        for c in range(n_col):
            stage_ref[c, pl.ds(r, n, stride=dil), :] = (
                src_ref[:, r * width + c * 128:r * width + (c + 1) * 128].astype(F32))
    return jnp.concatenate([stage_ref[c] for c in range(n_col)], axis=-1)


def _combine_body(o0_ref, o1_ref, o2_ref, l0_ref, l1_ref, l2_ref, mq_ref, mk_ref, mv_ref, att_ref, mem_ref,
                  *stage_refs):
    outs = [_from_strided_view(r, stage_refs[2 * g], ATT_GROUPS[g][1], WIDTH)
            for g, r in enumerate((o0_ref, o1_ref, o2_ref))]
    lses = [_from_strided_view(r, stage_refs[2 * g + 1], ATT_GROUPS[g][1], HEADS * LSE_REP)
            for g, r in enumerate((l0_ref, l1_ref, l2_ref))]
    for h in range(HEADS):
        sl = slice(h * HEAD_DIM, (h + 1) * HEAD_DIM)
        lse = [x[:, h * LSE_REP:h * LSE_REP + 1] for x in lses]
        m = jnp.maximum(jnp.maximum(lse[0], lse[1]), lse[2])
        e = [jnp.exp(x - m) for x in lse]
        inv = 1.0 / (e[0] + e[1] + e[2])
        att = sum((e[g] * inv) * outs[g][:, sl] for g in range(N_GROUPS))
        att_ref[:, sl] = att.astype(BF16)
        s = _dot_nt(mq_ref[:, sl], mk_ref[:, sl])
        p = jnp.exp(s - jnp.max(s, axis=-1, keepdims=True))
        o = _dot(p.astype(BF16), mv_ref[:, sl]) / jnp.sum(p, axis=-1, keepdims=True)
        mem_ref[:, sl] = o.astype(BF16)


def _combine(outs, lses, mq, mk, mv, b, t, *, tm):
    n_tiles = t // tm
    lw = HEADS * LSE_REP
    row = pl.BlockSpec((tm, WIDTH), lambda bi, i: (bi * n_tiles + i, 0))
    o_specs = [pl.BlockSpec((None, tm // d, d * WIDTH), lambda bi, i: (bi, i, 0)) for _, d in ATT_GROUPS]
    l_specs = [pl.BlockSpec((None, tm // d, d * lw), lambda bi, i: (bi, i, 0)) for _, d in ATT_GROUPS]
    memspec = pl.BlockSpec((None, MEM_LEN, WIDTH), lambda bi, i: (bi, 0, 0))
    shape = jax.ShapeDtypeStruct((b * t, WIDTH), BF16)
    stages = []
    for _ in ATT_GROUPS:
        stages += [pltpu.VMEM((WIDTH // 128, tm, 128), F32), pltpu.VMEM((lw // 128, tm, 128), F32)]
    return pl.pallas_call(
        _combine_body,
        grid=(b, n_tiles),
        in_specs=o_specs + l_specs + [row, memspec, memspec],
        out_specs=[row, row],
        out_shape=[shape, shape],
        scratch_shapes=stages,
        compiler_params=_params("parallel", "parallel"),
        name="combine",
    )(*outs, *lses, mq, mk, mv)


def _merge_body(x_ref, att_ref, hg_ref, mem_ref, gate_ref, wa_ref, wb_ref, wc_ref, wo_ref, o_ref):
    def gate(j):
        return gate_ref[:, j * D_MODEL:(j + 1) * D_MODEL].astype(F32)

    m = (gate(0) * _dot(att_ref[...].astype(BF16), wa_ref[...])
         + gate(1) * _dot(hg_ref[...].astype(BF16), wb_ref[...])
         + gate(2) * _dot(mem_ref[...].astype(BF16), wc_ref[...]))
    o_ref[...] = x_ref[...] + _dot(m.astype(BF16), wo_ref[...])


def _merge(x, att, hg, mem, gates, wa, wb, wc, wo, *, tm):
    m = x.shape[0]
    row = pl.BlockSpec((tm, WIDTH), lambda i: (i, 0))
    wide = pl.BlockSpec((tm, D_MODEL), lambda i: (i, 0))
    return pl.pallas_call(
        _merge_body,
        grid=(m // tm,),
        in_specs=[wide, row, row, row, pl.BlockSpec((tm, N_BRANCH * D_MODEL), lambda i: (i, 0)),
                  _const_spec((WIDTH, D_MODEL)), _const_spec((WIDTH, D_MODEL)), _const_spec((WIDTH, D_MODEL)),
                  _const_spec((D_MODEL, D_MODEL))],
        out_specs=wide,
        out_shape=jax.ShapeDtypeStruct((m, D_MODEL), F32),
        compiler_params=_params("parallel"),
        name="merge",
    )(x, att, hg, mem, gates, wa, wb, wc, wo)


def _norm_matmul_body(x_ref, g_ref, w_ref, o_ref):
    o_ref[...] = _dot(_rms(x_ref[...], g_ref[...]).astype(BF16), w_ref[...])


def _sample_in_proj(x, g_mix, w_in):
    m = x.shape[0]
    n = w_in.shape[1]
    return pl.pallas_call(
        _norm_matmul_body,
        grid=(n // WIDTH,),
        in_specs=[_const_spec((m, D_MODEL)), _const_spec((1, D_MODEL)),
                  pl.BlockSpec((D_MODEL, WIDTH), lambda j: (0, j))],
        out_specs=pl.BlockSpec((m, WIDTH), lambda j: (0, j)),
        out_shape=jax.ShapeDtypeStruct((m, n), F32),
        compiler_params=_params("parallel"),
        name="sample_in_proj",
    )(x, g_mix, w_in)


def _one_query_attend(q, k_rows, v_rows, k_new=None, v_new=None):
    s = jnp.sum(k_rows * q, axis=-1, keepdims=True)
    m = jnp.max(s, axis=0, keepdims=True)
    if k_new is not None:
        s_new = jnp.sum(k_new * q, axis=-1, keepdims=True)
        m = jnp.maximum(m, s_new)
    p = jnp.exp(s - m)
    l = jnp.sum(p, axis=0, keepdims=True)
    acc = jnp.sum(p * v_rows, axis=0, keepdims=True)
    if k_new is not None:
        p_new = jnp.exp(s_new - m)
        l = l + p_new
        acc = acc + p_new * v_new
    return acc / l, m + jnp.log(l)


def _sample_att_body(q_ref, kn_ref, vn_ref, mq_ref, k1_ref, v1_ref, k4_ref, v4_ref, k16_ref, v16_ref,
                     mk_ref, mv_ref, att_ref, mem_ref):
    k_refs = (k1_ref, k4_ref, k16_ref)
    v_refs = (v1_ref, v4_ref, v16_ref)
    for h in range(HEADS):
        sl = slice(h * HEAD_DIM, (h + 1) * HEAD_DIM)
        outs, lses = [], []
        for g in range(N_GROUPS):
            gs = slice(g * WIDTH + h * HEAD_DIM, g * WIDTH + (h + 1) * HEAD_DIM)
            o, lse = _one_query_attend(q_ref[:, gs] * QK_SCALE, _load_head(k_refs[g], h), _load_head(v_refs[g], h),
                                       kn_ref[:, gs], vn_ref[:, gs])
            outs.append(o)
            lses.append(lse)
        m = jnp.maximum(jnp.maximum(lses[0], lses[1]), lses[2])
        e = [jnp.exp(x - m) for x in lses]
        inv = 1.0 / (e[0] + e[1] + e[2])
        att_ref[:, sl] = sum((e[g] * inv) * outs[g] for g in range(N_GROUPS))
        o, _ = _one_query_attend(mq_ref[:, sl] * QK_SCALE, _load_head(mk_ref, h), _load_head(mv_ref, h))
        mem_ref[:, sl] = o


def _sample_attend(z3, caches_k, caches_v, cache_mk, cache_mv):
    bd = z3.shape[0]
    zspec = lambda col, n: pl.BlockSpec((None, 1, n * WIDTH), lambda b: (b, 0, col // n))
    cache_specs, cache_args = [], []
    for (win, dil), ck, cv in zip(ATT_GROUPS, caches_k, caches_v):
        rows = ck.shape[2]
        assert rows == win and rows // dil == BLOCK
        spec = pl.BlockSpec((None, None, BLOCK, None, HEADS, HEAD_DIM), lambda b: (0, b, 0, 0, 0, 0))
        for c in (ck, cv):
            cache_specs.append(spec)
            cache_args.append(c.reshape(1, bd, BLOCK, dil, HEADS, HEAD_DIM))
    mem_spec = pl.BlockSpec((None, None, MEM_LEN, HEADS, HEAD_DIM), lambda b: (0, b, 0, 0, 0))
    out_spec = pl.BlockSpec((None, 1, WIDTH), lambda b: (b, 0, 0))
    out_shape = jax.ShapeDtypeStruct((bd, 1, WIDTH), F32)
    return pl.pallas_call(
        _sample_att_body,
        grid=(bd,),
        in_specs=[zspec(COL_Q, 3), zspec(COL_K, 3), zspec(COL_V, 3), zspec(COL_MQ, 1)] + cache_specs
                 + [mem_spec, mem_spec],
        out_specs=[out_spec, out_spec],
        out_shape=[out_shape, out_shape],
        compiler_params=_params("parallel"),
        name="sample_attend",
    )(z3, z3, z3, z3, *cache_args, cache_mk, cache_mv)


SAMPLE_GROUP = 8


def _sample_hgrn_body(zq_ref, zf_ref, zi_ref, lb_ref, gain_ref, s_ref, o_ref, so_ref):
    lb = _lower_bound(lb_ref)
    gain = gain_ref[...]
    q = jax.nn.sigmoid(zq_ref[...])
    kk = (1.0 - lb) * jax.nn.sigmoid(zf_ref[...])
    f = lb + kk
    v = zi_ref[...]
    pad = jnp.zeros((HEAD_DIM - SAMPLE_GROUP, HEAD_DIM), F32)

    def columns(a, h):
        return jnp.concatenate([a[:, h * HEAD_DIM:(h + 1) * HEAD_DIM], pad], axis=0).T

    for h in range(HEADS):
        sl = slice(h * HEAD_DIM, (h + 1) * HEAD_DIM)
        q_t, k_t, f_t = columns(q, h), columns(kk, h), columns(f, h)
        for j in range(SAMPLE_GROUP):
            s1 = f_t[:, j:j + 1] * s_ref[j, h] + k_t[:, j:j + 1] * v[j:j + 1, sl]
            so_ref[j, h] = s1
            o = jnp.sum(q_t[:, j:j + 1] * s1, axis=0, keepdims=True)
            o_ref[j:j + 1, sl] = _rms(o, gain[:, sl])


def _sample_hgrn(z, lb_logits, gain, state):
    bd = z.shape[0]
    zspec = lambda col: pl.BlockSpec((SAMPLE_GROUP, WIDTH), lambda i: (i, col))
    sspec = pl.BlockSpec((None, SAMPLE_GROUP, HEADS, HEAD_DIM, HEAD_DIM), lambda i: (0, i, 0, 0, 0))
    return pl.pallas_call(
        _sample_hgrn_body,
        grid=(bd // SAMPLE_GROUP,),
        in_specs=[zspec(COL_HQ), zspec(COL_HF), zspec(COL_HI), _const_spec(lb_logits.shape),
                  _const_spec((1, WIDTH)), sspec],
        out_specs=[pl.BlockSpec((SAMPLE_GROUP, WIDTH), lambda i: (i, 0)), sspec],
        out_shape=[jax.ShapeDtypeStruct((bd, WIDTH), F32), jax.ShapeDtypeStruct(state.shape, F32)],
        compiler_params=_params("parallel"),
        name="sample_hgrn",
    )(z, z, z, lb_logits, gain, state)


def _gate_body(z_ref, o_ref):
    o_ref[...] = jax.nn.sigmoid(z_ref[...]).astype(BF16)


def _sample_gates(z):
    bd = z.shape[0]
    n = N_BRANCH * D_MODEL
    return pl.pallas_call(
        _gate_body,
        grid=(n // WIDTH,),
        in_specs=[pl.BlockSpec((bd, WIDTH), lambda j: (0, COL_GATE + j))],
        out_specs=pl.BlockSpec((bd, WIDTH), lambda j: (0, j)),
        out_shape=jax.ShapeDtypeStruct((bd, n), BF16),
        name="sample_gates",
    )(z)


def kernel(x_prompt, x_sample, mem_prompt, cache_win1_k, cache_win1_v, cache_win4_k, cache_win4_v, cache_win16_k, cache_win16_v, cache_mem_k, cache_mem_v, state_hgrn, g_ff1, w_ff1_gate, w_ff1_up, w_ff1_down, g_mix, w_in, hg_lb_logits, g_hg_out, g_mem, w_mem_k, w_mem_v, w_branch_att, w_branch_hg, w_branch_mem, w_out, g_ff2, w_ff2_gate, w_ff2_up, w_ff2_down, g_final):
    b, t, _ = x_prompt.shape
    bd = x_sample.shape[0]
    assert g_ff1.shape[0] == 1 and x_sample.shape[1] == 1 and w_in.shape[2] == IN_WIDTH
    bf = lambda w: w[0].astype(BF16)
    ffn1 = (g_ff1, bf(w_ff1_gate), bf(w_ff1_up), bf(w_ff1_down), g_final.reshape(1, D_MODEL))
    ffn2 = (g_ff2, bf(w_ff2_gate), bf(w_ff2_up), bf(w_ff2_down), g_final.reshape(1, D_MODEL))
    w_in_b = bf(w_in)
    w_qkv, w_rest = w_in_b[:, :COL_HQ * WIDTH], w_in_b[:, COL_HQ * WIDTH:]
    branch_w = (bf(w_branch_att), bf(w_branch_hg), bf(w_branch_mem), bf(w_out))

    xs, _ = _ffn(x_sample.reshape(bd, D_MODEL), *ffn1, tm=bd, final=False)
    zs = _sample_in_proj(xs, g_mix, w_in_b)
    caches_k = (cache_win1_k, cache_win4_k, cache_win16_k)
    caches_v = (cache_win1_v, cache_win4_v, cache_win16_v)
    z3 = zs.reshape(bd, 1, IN_WIDTH)
    att_s, mem_s = _sample_attend(z3, caches_k, caches_v, cache_mem_k, cache_mem_v)
    hg_s, s_hgrn = _sample_hgrn(zs, hg_lb_logits, g_hg_out, state_hgrn)
    xs = _merge(xs, att_s.reshape(bd, WIDTH), hg_s, mem_s.reshape(bd, WIDTH), _sample_gates(zs), *branch_w, tm=bd)
    y_sample = _ffn(xs, *ffn2, tm=bd, final=True)[0].reshape(bd, 1, D_MODEL)

    tm = b * t // bd
    xp, (s_win16_k,) = _ffn(x_prompt.reshape(b * t, D_MODEL), *ffn1, tm=tm, final=False,
                            ride=(z3, [(cache_win16_k, 2, False)]))
    qkv = _qkv_proj(xp.reshape(b, t, D_MODEL), g_mix, w_qkv, tm=512)
    q_g, k_g, v_g, p_k, p_v = qkv[0:3], qkv[3:6], qkv[6:9], qkv[9:12], qkv[12:15]
    (hq, hk, lf, hv, mq, gates), (s_win1_k, s_win1_v, s_win4_k, s_win4_v) = _rest_proj(
        xp, g_mix, w_rest, hg_lb_logits, tm=tm,
        ride=(z3, [(cache_win1_k, 0, False), (cache_win1_v, 0, True), (cache_win4_k, 1, False), (cache_win4_v, 1, True)]))
    outs, lses = [], []
    for g, (_, dil) in enumerate(ATT_GROUPS):
        o, lse = _att_group(q_g[g], k_g[g], v_g[g], dil)
        outs.append(o)
        lses.append(lse)
    hg_p, p_hgrn = _hgrn_prompt(hq, hk, lf, hv, g_hg_out, b, t)
    p_mem_k, p_mem_v, mk_b, mv_b = _mem_kv(mem_prompt, g_mem, bf(w_mem_k), bf(w_mem_v))
    att_p, mem_p = _combine(outs, lses, mq, mk_b, mv_b, b, t, tm=512)
    xp = _merge(xp, att_p, hg_p, mem_p, gates, *branch_w, tm=512)
    y_prompt, (s_win16_v,) = _ffn(xp, *ffn2, tm=tm, final=True, ride=(z3, [(cache_win16_v, 2, True)]))

    return (y_prompt.reshape(b, t, D_MODEL), y_sample,
            p_k[0], p_v[0], p_k[1], p_v[1], p_k[2], p_v[2],
            p_mem_k, p_mem_v, p_hgrn,
            s_win1_k, s_win1_v, s_win4_k, s_win4_v, s_win16_k, s_win16_v,
            s_hgrn)
```

```python
import functools

import numpy as np
import jax
import jax.numpy as jnp
from jax import lax
from jax.experimental import pallas as pl
from jax.experimental.pallas import tpu as pltpu

F32 = jnp.float32
BF16 = jnp.bfloat16

D_MODEL = 1024
D_FF = 2816
EPS = 1e-6
HEADS = 4
HEAD_DIM = 128
WIDTH = HEADS * HEAD_DIM
ATT_GROUPS = ((128, 1), (512, 4), (2048, 16))
N_GROUPS = 3
BLOCK = 128
MEM_LEN = 256
N_BRANCH = 3
LSE_REP = 32
QK_SCALE = HEAD_DIM ** -0.5
VMEM_LIMIT_BYTES = 56 * 1024 * 1024

COL_Q, COL_K, COL_V, COL_HQ, COL_HF, COL_HI, COL_MQ, COL_GATE = 0, 3, 6, 9, 10, 11, 12, 13
IN_WIDTH = (COL_GATE + N_BRANCH * D_MODEL // WIDTH) * WIDTH


def _params(*sem):
    return pltpu.CompilerParams(dimension_semantics=sem, vmem_limit_bytes=VMEM_LIMIT_BYTES)


def _const_spec(shape):
    nd = len(shape)
    return pl.BlockSpec(shape, lambda *_: (0,) * nd, pipeline_mode=pl.Buffered(1))


def _rms(x, g):
    return x * lax.rsqrt(jnp.mean(x * x, axis=-1, keepdims=True) + EPS) * g


def _dot(a, b):
    return jnp.dot(a, b, preferred_element_type=F32)


def _dot_nt(a, b):
    return lax.dot_general(a, b, (((1,), (1,)), ((), ())), preferred_element_type=F32)


def _ride_specs(caches, n_steps):
    if not caches:
        return [], [], [], []
    assert all(c.shape[1] == n_steps for c in caches)
    in_specs = [pl.BlockSpec((1, 1, c.shape[2], HEADS, HEAD_DIM), lambda i: (0, i, 0, 0, 0)) for c in caches]
    out_specs = [pl.BlockSpec(memory_space=pl.ANY)] * len(caches)
    shapes = [jax.ShapeDtypeStruct(c.shape, c.dtype) for c in caches]
    return in_specs, out_specs, shapes, [pltpu.SemaphoreType.DMA((2 * len(caches),))]


def _ride_copies(in_refs, out_refs, sem):
    seq = pl.program_id(0)
    copies = []
    for j, (cache_ref, out_ref) in enumerate(zip(in_refs, out_refs)):
        rows = cache_ref.shape[2]
        copies.append(pltpu.make_async_copy(cache_ref.at[:, :, pl.ds(1, rows - 1)],
                                            out_ref.at[:, pl.ds(seq, 1), pl.ds(0, rows - 1)], sem.at[2 * j]))
        copies.append(pltpu.make_async_copy(cache_ref.at[:, :, pl.ds(rows - 1, 1)],
                                            out_ref.at[:, pl.ds(seq, 1), pl.ds(rows - 1, 1)], sem.at[2 * j + 1]))
    for cp in copies:
        cp.start()
    return copies


def _new_rows_body(kn_ref, vn_ref, *refs, meta):
    n = len(meta)
    out_refs, row_refs, sem = refs[n:2 * n], refs[2 * n:3 * n], refs[3 * n]
    copies = []
    for j, ((g, is_v), out_ref, row_ref) in enumerate(zip(meta, out_refs, row_refs)):
        bd, rows = out_ref.shape[1], out_ref.shape[2]
        new = (vn_ref if is_v else kn_ref)[:, 0, g * WIDTH:(g + 1) * WIDTH]
        flat = row_ref.reshape(bd * HEADS, HEAD_DIM)
        for h in range(HEADS):
            flat[pl.ds(h, bd, stride=HEADS), :] = new[:, h * HEAD_DIM:(h + 1) * HEAD_DIM]
        copies.append(pltpu.make_async_copy(row_ref, out_ref.at[:, :, pl.ds(rows - 1, 1)], sem.at[j]))
        copies[-1].start()
    for cp in copies:
        cp.wait()


def _write_new_rows(z3, shifted, meta):
    bd = z3.shape[0]
    n = len(shifted)
    zspec = lambda col: pl.BlockSpec((bd, 1, N_GROUPS * WIDTH), lambda i: (0, 0, col // N_GROUPS))
    any_spec = pl.BlockSpec(memory_space=pl.ANY)
    return pl.pallas_call(
        functools.partial(_new_rows_body, meta=meta),
        grid=(1,),
        in_specs=[zspec(COL_K), zspec(COL_V)] + [any_spec] * n,
        out_specs=[any_spec] * n,
        out_shape=[jax.ShapeDtypeStruct(c.shape, c.dtype) for c in shifted],
        input_output_aliases={2 + j: j for j in range(n)},
        scratch_shapes=[pltpu.VMEM((1, bd, 1, HEADS, HEAD_DIM), F32)] * n + [pltpu.SemaphoreType.DMA((n,))],
        name="write_new_rows",
    )(z3, z3, *shifted)


FF_CHUNKS = 1


def _ffn_body(x_ref, g_ref, wg_ref, wu_ref, wd_ref, gf_ref, *rest, final, n_ride):
    o_ref = rest[n_ride]
    copies = _ride_copies(rest[:n_ride], rest[n_ride + 1:2 * n_ride + 1], rest[-1]) if n_ride else []
    x = x_ref[...]
    h = _rms(x, g_ref[...]).astype(BF16)
    y = x
    width = D_FF // FF_CHUNKS
    for c in range(FF_CHUNKS):
        cols = slice(c * width, (c + 1) * width)
        a = _dot(h, wg_ref[:, cols])
        u = _dot(h, wu_ref[:, cols])
        act = (a * jax.nn.sigmoid(a) * u).astype(BF16)
        y = y + 0.5 * _dot(act, wd_ref[cols, :])
    if final:
        y = _rms(y, gf_ref[...])
    o_ref[...] = y
    for cp in copies:
        cp.wait()


def _ffn(x, g, wg, wu, wd, g_final, *, tm, final, ride=()):
    m = x.shape[0]
    r_in, r_out, r_shapes, r_scratch = _ride_specs(ride, m // tm)
    res = pl.pallas_call(
        functools.partial(_ffn_body, final=final, n_ride=len(ride)),
        grid=(m // tm,),
        in_specs=[pl.BlockSpec((tm, D_MODEL), lambda i: (i, 0)),
                  _const_spec((1, D_MODEL)),
                  _const_spec((D_MODEL, D_FF)), _const_spec((D_MODEL, D_FF)), _const_spec((D_FF, D_MODEL)),
                  _const_spec((1, D_MODEL))] + r_in,
        out_specs=[pl.BlockSpec((tm, D_MODEL), lambda i: (i, 0))] + r_out,
        out_shape=[jax.ShapeDtypeStruct((m, D_MODEL), F32)] + r_shapes,
        scratch_shapes=r_scratch,
        compiler_params=_params("parallel"),
        name="ffn_final" if final else "ffn",
    )(x, g, wg, wu, wd, g_final, *ride)
    return res[0], res[1:]


def _store_heads(ref, z):
    rows = ref.shape[0]
    flat = ref.reshape(rows * HEADS, HEAD_DIM)
    for h in range(HEADS):
        flat[pl.ds(h, rows, stride=HEADS), :] = z[:, h * HEAD_DIM:(h + 1) * HEAD_DIM]


def _load_head(ref, h):
    rows = ref.shape[0]
    return ref.reshape(rows * HEADS, HEAD_DIM)[pl.ds(h, rows, stride=HEADS), :]


def _to_strided_view(dst_ref, z, stage_ref, dil):
    if dil == 1:
        dst_ref[...] = z.astype(dst_ref.dtype)
        return
    n = z.shape[0] // dil
    for c in range(stage_ref.shape[0]):
        stage_ref[c] = z[:, c * 128:(c + 1) * 128]
    for r in range(dil):
        for c in range(stage_ref.shape[0]):
            dst_ref[:, r * WIDTH + c * 128:r * WIDTH + (c + 1) * 128] = (
                stage_ref[c, pl.ds(r, n, stride=dil), :].astype(dst_ref.dtype))


def _qkv_body(x_ref, g_ref, w_ref, *refs, tm, n_tiles):
    q_refs, k_refs, v_refs = refs[0:3], refs[3:6], refs[6:9]
    pk_refs, pv_refs, stage_refs = refs[9:12], refs[12:15], refs[15:]
    h = _rms(x_ref[...], g_ref[...]).astype(BF16)
    late = []
    for g, (win, dil) in enumerate(ATT_GROUPS):
        keep = min(win, tm * n_tiles)
        for kind, (col, dst_refs, win_refs) in enumerate(((COL_Q, q_refs, None), (COL_K, k_refs, pk_refs),
                                                          (COL_V, v_refs, pv_refs))):
            z = _dot(h, w_ref[:, (col + g) * WIDTH:(col + g + 1) * WIDTH])
            if win_refs is None:
                z = z * QK_SCALE
            stage = stage_refs[3 * (g - 1) + kind] if dil > 1 else None
            _to_strided_view(dst_refs[g], z, stage, dil)
            if win_refs is None:
                continue
            if keep == tm * n_tiles:
                _store_heads(win_refs[g], z)
            elif dil == 1:
                late.append((win_refs[g], z[tm - keep:, :], None))
            else:
                assert keep == tm
                late.append((win_refs[g], None, stage))

    @pl.when(pl.program_id(1) == n_tiles - 1)
    def _():
        for ref, rows, stage in late:
            if stage is None:
                _store_heads(ref, rows)
            else:
                _store_heads(ref, jnp.concatenate([stage[hd] for hd in range(HEADS)], axis=-1))


def _qkv_proj(x, g_mix, w_qkv, *, tm):
    b, t, _ = x.shape
    n_tiles = t // tm
    view_specs = [pl.BlockSpec((None, tm // d, d * WIDTH), lambda bi, i: (bi, i, 0)) for _, d in ATT_GROUPS]
    view_shapes = [jax.ShapeDtypeStruct((b, t // d, d * WIDTH), BF16) for _, d in ATT_GROUPS]
    win_specs, win_shapes = [], []
    for win, _ in ATT_GROUPS:
        keep = min(win, t)
        assert keep == t or keep <= tm
        if keep == t:
            win_specs.append(pl.BlockSpec((None, None, tm, HEADS, HEAD_DIM), lambda bi, i: (0, bi, i, 0, 0)))
        else:
            win_specs.append(pl.BlockSpec((None, None, keep, HEADS, HEAD_DIM), lambda bi, i: (0, bi, 0, 0, 0)))
        win_shapes.append(jax.ShapeDtypeStruct((1, b, keep, HEADS, HEAD_DIM), F32))
    return pl.pallas_call(
        functools.partial(_qkv_body, tm=tm, n_tiles=n_tiles),
        grid=(b, n_tiles),
        in_specs=[pl.BlockSpec((None, tm, D_MODEL), lambda bi, i: (bi, i, 0)),
                  _const_spec((1, D_MODEL)),
                  _const_spec((D_MODEL, 3 * N_GROUPS * WIDTH))],
        out_specs=view_specs * 3 + win_specs + win_specs,
        out_shape=view_shapes * 3 + win_shapes + win_shapes,
        scratch_shapes=[pltpu.VMEM((WIDTH // 128, tm, 128), F32)] * (3 * sum(d > 1 for _, d in ATT_GROUPS)),
        compiler_params=_params("parallel", "arbitrary"),
        name="qkv_proj",
    )(x, g_mix, w_qkv)


def _lower_bound(logits_ref):
    lg = logits_ref[...]
    e = jnp.exp(lg - jnp.max(lg, axis=0, keepdims=True))
    return e[0:1, :] / jnp.sum(e, axis=0, keepdims=True)


def _rest_body(x_ref, g_ref, w_ref, lb_ref, *rest, n_ride):
    hq_ref, hk_ref, lf_ref, hv_ref, mq_ref, gate_ref = rest[n_ride:n_ride + 6]
    copies = _ride_copies(rest[:n_ride], rest[n_ride + 6:2 * n_ride + 6], rest[-1]) if n_ride else []
    h = _rms(x_ref[...], g_ref[...]).astype(BF16)
    lb = _lower_bound(lb_ref)
    hq_ref[...] = jax.nn.sigmoid(_dot(h, w_ref[:, 0:WIDTH])).astype(BF16)
    kk = (1.0 - lb) * jax.nn.sigmoid(_dot(h, w_ref[:, WIDTH:2 * WIDTH]))
    hk_ref[...] = kk.astype(BF16)
    lf_ref[...] = jnp.log(lb + kk)
    hv_ref[...] = _dot(h, w_ref[:, 2 * WIDTH:3 * WIDTH]).astype(BF16)
    mq_ref[...] = (_dot(h, w_ref[:, 3 * WIDTH:4 * WIDTH]) * QK_SCALE).astype(BF16)
    for c in range(N_BRANCH * D_MODEL // WIDTH):
        z = _dot(h, w_ref[:, (4 + c) * WIDTH:(5 + c) * WIDTH])
        gate_ref[:, c * WIDTH:(c + 1) * WIDTH] = jax.nn.sigmoid(z).astype(BF16)
    for cp in copies:
        cp.wait()


def _rest_proj(x, g_mix, w_rest, lb_logits, *, tm, ride=()):
    m = x.shape[0]
    spec = pl.BlockSpec((tm, WIDTH), lambda i: (i, 0))
    bf = jax.ShapeDtypeStruct((m, WIDTH), BF16)
    r_in, r_out, r_shapes, r_scratch = _ride_specs(ride, m // tm)
    res = pl.pallas_call(
        functools.partial(_rest_body, n_ride=len(ride)),
        grid=(m // tm,),
        in_specs=[pl.BlockSpec((tm, D_MODEL), lambda i: (i, 0)),
                  _const_spec((1, D_MODEL)),
                  _const_spec(w_rest.shape),
                  _const_spec(lb_logits.shape)] + r_in,
        out_specs=[spec, spec, spec, spec, spec, pl.BlockSpec((tm, N_BRANCH * D_MODEL), lambda i: (i, 0))] + r_out,
        out_shape=[bf, bf, jax.ShapeDtypeStruct((m, WIDTH), F32), bf, bf,
                   jax.ShapeDtypeStruct((m, N_BRANCH * D_MODEL), BF16)] + r_shapes,
        scratch_shapes=r_scratch,
        compiler_params=_params("parallel"),
        name="rest_proj",
    )(x, g_mix, w_rest, lb_logits, *ride)
    return res[:6], res[6:]


ATT_UNITS = 4


def _att_body(q_ref, k_ref, v_ref, *rest, by_rows, has_halo):
    if has_halo:
        kh_ref, vh_ref, o_ref, lse_ref = rest
        halo_ok = pl.program_id(1) > 0
    else:
        o_ref, lse_ref = rest
    qi = lax.broadcasted_iota(jnp.int32, (BLOCK, BLOCK), 0)
    ki = lax.broadcasted_iota(jnp.int32, (BLOCK, BLOCK), 1)
    own_ok = ki <= qi
    prev_ok = ki >= qi

    units = []
    for u in range(ATT_UNITS):
        for h in range(HEADS):
            if by_rows:
                rows = slice(u * BLOCK, (u + 1) * BLOCK)
                cols = slice(h * HEAD_DIM, (h + 1) * HEAD_DIM)
                lcols = slice(h * LSE_REP, (h + 1) * LSE_REP)
                if u > 0:
                    before = slice((u - 1) * BLOCK, u * BLOCK)
                    prev = (k_ref, v_ref, before, prev_ok)
                elif has_halo:
                    prev = (kh_ref, vh_ref, slice(0, BLOCK), jnp.logical_and(prev_ok, halo_ok))
                else:
                    prev = None
            else:
                rows = slice(0, BLOCK)
                cols = slice(u * WIDTH + h * HEAD_DIM, u * WIDTH + (h + 1) * HEAD_DIM)
                lcols = slice((u * HEADS + h) * LSE_REP, (u * HEADS + h + 1) * LSE_REP)
                prev = None
            units.append((rows, cols, lcols, prev))

    scores = []
    for rows, cols, _, prev in units:
        q = q_ref[rows, cols]
        s_o = jnp.where(own_ok, _dot_nt(q, k_ref[rows, cols]), -jnp.inf)
        s_p = None
        if prev is not None:
            kp_ref, _, prows, ok = prev
            s_p = jnp.where(ok, _dot_nt(q, kp_ref[prows, cols]), -jnp.inf)
        scores.append((s_o, s_p))

    probs = []
    for s_o, s_p in scores:
        m = jnp.max(s_o, axis=-1, keepdims=True)
        if s_p is not None:
            m = jnp.maximum(m, jnp.max(s_p, axis=-1, keepdims=True))
        p_o = jnp.exp(s_o - m)
        l = jnp.sum(p_o, axis=-1, keepdims=True)
        p_p = None
        if s_p is not None:
            p_p = jnp.exp(s_p - m)
            l = l + jnp.sum(p_p, axis=-1, keepdims=True)
            p_p = p_p.astype(BF16)
        probs.append((p_o.astype(BF16), p_p, 1.0 / l, m + jnp.log(l)))

    for (rows, cols, lcols, prev), (p_o, p_p, inv_l, lse) in zip(units, probs):
        acc = _dot(p_o, v_ref[rows, cols])
        if prev is not None:
            _, vp_ref, prows, _ = prev
            acc = acc + _dot(p_p, vp_ref[prows, cols])
        o_ref[rows, cols] = (acc * inv_l).astype(BF16)
        lse_ref[rows, lcols] = jnp.broadcast_to(lse, (BLOCK, LSE_REP))


def _att_group(q, k, v, dil):
    b, sub_len, _ = q.shape
    lw = HEADS * LSE_REP
    span = ATT_UNITS * BLOCK
    by_rows = sub_len >= span
    has_halo = sub_len > span
    if by_rows:
        assert sub_len % span == 0
        n = (sub_len // span) * dil
        per_seq = sub_len // span
        tile = lambda w: pl.BlockSpec((None, span, w), lambda bi, j: (bi, j % per_seq, j // per_seq))
        halo = pl.BlockSpec((None, BLOCK, WIDTH),
                            lambda bi, j: (bi, jnp.maximum((j % per_seq) * ATT_UNITS - 1, 0), j // per_seq))
        assert not has_halo or dil == 1
    else:
        assert sub_len == BLOCK and dil % ATT_UNITS == 0
        n = dil // ATT_UNITS
        tile = lambda w: pl.BlockSpec((None, BLOCK, ATT_UNITS * w), lambda bi, j: (bi, 0, j))
    in_specs = [tile(WIDTH)] * 3 + ([halo, halo] if has_halo else [])
    args = (q, k, v) + ((k, v) if has_halo else ())
    return pl.pallas_call(
        functools.partial(_att_body, by_rows=by_rows, has_halo=has_halo),
        grid=(b, n),
        in_specs=in_specs,
        out_specs=[tile(WIDTH), tile(lw)],
        out_shape=[jax.ShapeDtypeStruct((b, sub_len, dil * WIDTH), BF16),
                   jax.ShapeDtypeStruct((b, sub_len, dil * lw), F32)],
        compiler_params=_params("parallel", "arbitrary"),
        name=f"att_dil{dil}",
    )(*args)


N_LEVELS = 7
ROW_CUM, ROW_REM = N_LEVELS, N_LEVELS + 1


def _hgrn_constants():
    r = np.arange(BLOCK)[:, None]
    s = np.arange(BLOCK)[None, :]
    mats = []
    for lvl in range(N_LEVELS):
        half = 1 << lvl
        mid = (r // (2 * half)) * 2 * half + half
        upper = (r % (2 * half)) >= half
        mats.append(np.where(upper, (s >= mid) & (s <= r), (s > r) & (s <= mid - 1)))
    mats.append(s <= r)
    mats.append(s > r)
    range_mat = np.concatenate(mats, axis=0).astype(np.float32)
    range_mat = np.concatenate([range_mat, range_mat], axis=1)
    x = r ^ s
    level = np.where(s < r, np.floor(np.log2(np.maximum(x, 1))).astype(np.int32), np.where(s == r, N_LEVELS, -1))
    return jnp.asarray(range_mat, BF16), jnp.asarray(level, jnp.int32)


HGRN_SEQS = 2


def _hgrn_intra(q_b, k_b, lf, rng, level):
    lf_hi = lf.astype(BF16)
    lf_lo = (lf - lf_hi.astype(F32)).astype(BF16)
    z = _dot(rng, jnp.concatenate([lf_hi, lf_lo], axis=0))
    q = q_b.astype(F32)
    kk = k_b.astype(F32)
    lhs, rhs = [], []
    for lvl in range(N_LEVELS):
        x = jnp.exp(z[lvl * BLOCK:(lvl + 1) * BLOCK])
        lhs.append((q * x).astype(BF16))
        rhs.append((kk * x).astype(BF16))
    lhs.append(q_b)
    rhs.append(k_b)
    b_cum = z[ROW_CUM * BLOCK:(ROW_CUM + 1) * BLOCK]
    q_in = (q * jnp.exp(b_cum)).astype(BF16)
    k_out = (kk * jnp.exp(z[ROW_REM * BLOCK:(ROW_REM + 1) * BLOCK])).astype(BF16)
    decay = jnp.exp(b_cum[BLOCK - 1:BLOCK, :])
    mats = []
    for h in range(HEADS):
        sl = slice(h * HEAD_DIM, (h + 1) * HEAD_DIM)
        a = jnp.zeros((BLOCK, BLOCK), F32)
        for lvl in range(N_LEVELS + 1):
            a = jnp.where(level == lvl, _dot_nt(lhs[lvl][:, sl], rhs[lvl][:, sl]), a)
        mats.append(a.astype(BF16))
    return mats, q_in, k_out, decay


def _hgrn_state(mats, q_in, k_out, decay, v_b, gain, st_ref, j):
    outs = []
    for h in range(HEADS):
        sl = slice(h * HEAD_DIM, (h + 1) * HEAD_DIM)
        v = v_b[:, sl]
        st = st_ref[j, h]
        o = _dot(mats[h], v) + _dot_nt(q_in[:, sl], st.astype(BF16))
        outs.append(_rms(o, gain[:, sl]).astype(BF16))
        v_t = v.astype(F32).T.astype(BF16)
        st_ref[j, h] = st * decay[:, sl] + _dot(v_t, k_out[:, sl])
    return jnp.concatenate(outs, axis=-1)


def _hgrn_body(q_ref, k_ref, lf_ref, v_ref, rng_ref, lvl_ref, gain_ref, o_ref, s_ref, st_ref, *, n_chunks):
    c = pl.program_id(1)

    @pl.when(c == 0)
    def _():
        st_ref[...] = jnp.zeros_like(st_ref)

    rng, level, gain = rng_ref[...], lvl_ref[...], gain_ref[...]
    intra = [_hgrn_intra(q_ref[j], k_ref[j], lf_ref[j], rng, level) for j in range(HGRN_SEQS)]
    for j in range(HGRN_SEQS):
        o_ref[j] = _hgrn_state(*intra[j], v_ref[j], gain, st_ref, j)

    @pl.when(c == n_chunks - 1)
    def _():
        for j in range(HGRN_SEQS):
            for h in range(HEADS):
                s_ref[j, h] = st_ref[j, h].T


def _hgrn_prompt(hq, hk, lf, hv, gain, b, t):
    n_chunks = t // BLOCK
    range_mat, level = _hgrn_constants()
    as3d = lambda a: a.reshape(b, t, WIDTH)
    spec = pl.BlockSpec((HGRN_SEQS, BLOCK, WIDTH), lambda bi, c: (bi, c, 0))
    state_spec = pl.BlockSpec((None, HGRN_SEQS, HEADS, HEAD_DIM, HEAD_DIM), lambda bi, c: (0, bi, 0, 0, 0))
    o, state = pl.pallas_call(
        functools.partial(_hgrn_body, n_chunks=n_chunks),
        grid=(b // HGRN_SEQS, n_chunks),
        in_specs=[spec, spec, spec, spec, _const_spec(range_mat.shape), _const_spec(level.shape),
                  _const_spec((1, WIDTH))],
        out_specs=[spec, state_spec],
        out_shape=[jax.ShapeDtypeStruct((b, t, WIDTH), BF16),
                   jax.ShapeDtypeStruct((1, b, HEADS, HEAD_DIM, HEAD_DIM), F32)],
        scratch_shapes=[pltpu.VMEM((HGRN_SEQS, HEADS, HEAD_DIM, HEAD_DIM), F32)],
        compiler_params=_params("parallel", "arbitrary"),
        name="hgrn_prompt",
    )(as3d(hq), as3d(hk), as3d(lf), as3d(hv), range_mat, level, gain)
    return o.reshape(b * t, WIDTH), state


def _memkv_body(m_ref, g_ref, wk_ref, wv_ref, pk_ref, pv_ref, kb_ref, vb_ref):
    h = _rms(m_ref[...], g_ref[...]).astype(BF16)
    k = _dot(h, wk_ref[...])
    v = _dot(h, wv_ref[...])
    _store_heads(pk_ref, k)
    _store_heads(pv_ref, v)
    kb_ref[...] = k.astype(BF16)
    vb_ref[...] = v.astype(BF16)


def _mem_kv(mem, g_mem, w_mk, w_mv):
    b = mem.shape[0]
    head_spec = pl.BlockSpec((None, None, MEM_LEN, HEADS, HEAD_DIM), lambda bi: (0, bi, 0, 0, 0))
    flat_spec = pl.BlockSpec((None, MEM_LEN, WIDTH), lambda bi: (bi, 0, 0))
    head_shape = jax.ShapeDtypeStruct((1, b, MEM_LEN, HEADS, HEAD_DIM), F32)
    flat_shape = jax.ShapeDtypeStruct((b, MEM_LEN, WIDTH), BF16)
    return pl.pallas_call(
        _memkv_body,
        grid=(b,),
        in_specs=[pl.BlockSpec((None, MEM_LEN, D_MODEL), lambda bi: (bi, 0, 0)),
                  _const_spec((1, D_MODEL)), _const_spec((D_MODEL, WIDTH)), _const_spec((D_MODEL, WIDTH))],
        out_specs=[head_spec, head_spec, flat_spec, flat_spec],
        out_shape=[head_shape, head_shape, flat_shape, flat_shape],
        compiler_params=_params("parallel"),
        name="mem_kv",
    )(mem, g_mem, w_mk, w_mv)


def _from_strided_view(src_ref, stage_ref, dil, width):
    if dil == 1:
        return src_ref[...].astype(F32)
    n = src_ref.shape[0]
    n_col = width // 128
    for r in range(dil):
        for c in range(n_col):
            stage_ref[c, pl.ds(r, n, stride=dil), :] = (
                src_ref[:, r * width + c * 128:r * width + (c + 1) * 128].astype(F32))
    return jnp.concatenate([stage_ref[c] for c in range(n_col)], axis=-1)


def _combine_body(o0_ref, o1_ref, o2_ref, l0_ref, l1_ref, l2_ref, mq_ref, mk_ref, mv_ref, att_ref, mem_ref,
                  *stage_refs):
    outs = [_from_strided_view(r, stage_refs[2 * g], ATT_GROUPS[g][1], WIDTH)
            for g, r in enumerate((o0_ref, o1_ref, o2_ref))]
    lses = [_from_strided_view(r, stage_refs[2 * g + 1], ATT_GROUPS[g][1], HEADS * LSE_REP)
            for g, r in enumerate((l0_ref, l1_ref, l2_ref))]
    for h in range(HEADS):
        sl = slice(h * HEAD_DIM, (h + 1) * HEAD_DIM)
        lse = [x[:, h * LSE_REP:h * LSE_REP + 1] for x in lses]
        m = jnp.maximum(jnp.maximum(lse[0], lse[1]), lse[2])
        e = [jnp.exp(x - m) for x in lse]
        inv = 1.0 / (e[0] + e[1] + e[2])
        att = sum((e[g] * inv) * outs[g][:, sl] for g in range(N_GROUPS))
        att_ref[:, sl] = att.astype(BF16)
        s = _dot_nt(mq_ref[:, sl], mk_ref[:, sl])
        p = jnp.exp(s - jnp.max(s, axis=-1, keepdims=True))
        o = _dot(p.astype(BF16), mv_ref[:, sl]) / jnp.sum(p, axis=-1, keepdims=True)
        mem_ref[:, sl] = o.astype(BF16)


def _combine(outs, lses, mq, mk, mv, b, t, *, tm):
    n_tiles = t // tm
    lw = HEADS * LSE_REP
    row = pl.BlockSpec((tm, WIDTH), lambda bi, i: (bi * n_tiles + i, 0))
    o_specs = [pl.BlockSpec((None, tm // d, d * WIDTH), lambda bi, i: (bi, i, 0)) for _, d in ATT_GROUPS]
    l_specs = [pl.BlockSpec((None, tm // d, d * lw), lambda bi, i: (bi, i, 0)) for _, d in ATT_GROUPS]
    memspec = pl.BlockSpec((None, MEM_LEN, WIDTH), lambda bi, i: (bi, 0, 0))
    shape = jax.ShapeDtypeStruct((b * t, WIDTH), BF16)
    stages = []
    for _ in ATT_GROUPS:
        stages += [pltpu.VMEM((WIDTH // 128, tm, 128), F32), pltpu.VMEM((lw // 128, tm, 128), F32)]
    return pl.pallas_call(
        _combine_body,
        grid=(b, n_tiles),
        in_specs=o_specs + l_specs + [row, memspec, memspec],
        out_specs=[row, row],
        out_shape=[shape, shape],
        scratch_shapes=stages,
        compiler_params=_params("parallel", "parallel"),
        name="combine",
    )(*outs, *lses, mq, mk, mv)


def _merge_body(x_ref, att_ref, hg_ref, mem_ref, gate_ref, wa_ref, wb_ref, wc_ref, wo_ref, o_ref):
    def gate(j):
        return gate_ref[:, j * D_MODEL:(j + 1) * D_MODEL].astype(F32)

    m = (gate(0) * _dot(att_ref[...].astype(BF16), wa_ref[...])
         + gate(1) * _dot(hg_ref[...].astype(BF16), wb_ref[...])
         + gate(2) * _dot(mem_ref[...].astype(BF16), wc_ref[...]))
    o_ref[...] = x_ref[...] + _dot(m.astype(BF16), wo_ref[...])


def _merge(x, att, hg, mem, gates, wa, wb, wc, wo, *, tm):
    m = x.shape[0]
    row = pl.BlockSpec((tm, WIDTH), lambda i: (i, 0))
    wide = pl.BlockSpec((tm, D_MODEL), lambda i: (i, 0))
    return pl.pallas_call(
        _merge_body,
        grid=(m // tm,),
        in_specs=[wide, row, row, row, pl.BlockSpec((tm, N_BRANCH * D_MODEL), lambda i: (i, 0)),
                  _const_spec((WIDTH, D_MODEL)), _const_spec((WIDTH, D_MODEL)), _const_spec((WIDTH, D_MODEL)),
                  _const_spec((D_MODEL, D_MODEL))],
        out_specs=wide,
        out_shape=jax.ShapeDtypeStruct((m, D_MODEL), F32),
        compiler_params=_params("parallel"),
        name="merge",
    )(x, att, hg, mem, gates, wa, wb, wc, wo)


def _norm_matmul_body(x_ref, g_ref, w_ref, o_ref):
    o_ref[...] = _dot(_rms(x_ref[...], g_ref[...]).astype(BF16), w_ref[...])


def _sample_in_proj(x, g_mix, w_in):
    m = x.shape[0]
    n = w_in.shape[1]
    return pl.pallas_call(
        _norm_matmul_body,
        grid=(n // WIDTH,),
        in_specs=[_const_spec((m, D_MODEL)), _const_spec((1, D_MODEL)),
                  pl.BlockSpec((D_MODEL, WIDTH), lambda j: (0, j))],
        out_specs=pl.BlockSpec((m, WIDTH), lambda j: (0, j)),
        out_shape=jax.ShapeDtypeStruct((m, n), F32),
        compiler_params=_params("parallel"),
        name="sample_in_proj",
    )(x, g_mix, w_in)


def _one_query_attend(q, k_rows, v_rows, k_new=None, v_new=None):
    s = jnp.sum(k_rows * q, axis=-1, keepdims=True)
    m = jnp.max(s, axis=0, keepdims=True)
    if k_new is not None:
        s_new = jnp.sum(k_new * q, axis=-1, keepdims=True)
        m = jnp.maximum(m, s_new)
    p = jnp.exp(s - m)
    l = jnp.sum(p, axis=0, keepdims=True)
    acc = jnp.sum(p * v_rows, axis=0, keepdims=True)
    if k_new is not None:
        p_new = jnp.exp(s_new - m)
        l = l + p_new
        acc = acc + p_new * v_new
    return acc / l, m + jnp.log(l)


def _sample_att_body(q_ref, kn_ref, vn_ref, mq_ref, k1_ref, v1_ref, k4_ref, v4_ref, k16_ref, v16_ref,
                     mk_ref, mv_ref, att_ref, mem_ref):
    k_refs = (k1_ref, k4_ref, k16_ref)
    v_refs = (v1_ref, v4_ref, v16_ref)
    for h in range(HEADS):
        sl = slice(h * HEAD_DIM, (h + 1) * HEAD_DIM)
        outs, lses = [], []
        for g in range(N_GROUPS):
            gs = slice(g * WIDTH + h * HEAD_DIM, g * WIDTH + (h + 1) * HEAD_DIM)
            o, lse = _one_query_attend(q_ref[:, gs] * QK_SCALE, _load_head(k_refs[g], h), _load_head(v_refs[g], h),
                                       kn_ref[:, gs], vn_ref[:, gs])
            outs.append(o)
            lses.append(lse)
        m = jnp.maximum(jnp.maximum(lses[0], lses[1]), lses[2])
        e = [jnp.exp(x - m) for x in lses]
        inv = 1.0 / (e[0] + e[1] + e[2])
        att_ref[:, sl] = sum((e[g] * inv) * outs[g] for g in range(N_GROUPS))
        o, _ = _one_query_attend(mq_ref[:, sl] * QK_SCALE, _load_head(mk_ref, h), _load_head(mv_ref, h))
        mem_ref[:, sl] = o


def _sample_attend(z3, caches_k, caches_v, cache_mk, cache_mv):
    bd = z3.shape[0]
    zspec = lambda col, n: pl.BlockSpec((None, 1, n * WIDTH), lambda b: (b, 0, col // n))
    cache_specs, cache_args = [], []
    for (win, dil), ck, cv in zip(ATT_GROUPS, caches_k, caches_v):
        rows = ck.shape[2]
        assert rows == win and rows // dil == BLOCK
        spec = pl.BlockSpec((None, None, BLOCK, None, HEADS, HEAD_DIM), lambda b: (0, b, 0, 0, 0, 0))
        for c in (ck, cv):
            cache_specs.append(spec)
            cache_args.append(c.reshape(1, bd, BLOCK, dil, HEADS, HEAD_DIM))
    mem_spec = pl.BlockSpec((None, None, MEM_LEN, HEADS, HEAD_DIM), lambda b: (0, b, 0, 0, 0))
    out_spec = pl.BlockSpec((None, 1, WIDTH), lambda b: (b, 0, 0))
    out_shape = jax.ShapeDtypeStruct((bd, 1, WIDTH), F32)
    return pl.pallas_call(
        _sample_att_body,
        grid=(bd,),
        in_specs=[zspec(COL_Q, 3), zspec(COL_K, 3), zspec(COL_V, 3), zspec(COL_MQ, 1)] + cache_specs
                 + [mem_spec, mem_spec],
        out_specs=[out_spec, out_spec],
        out_shape=[out_shape, out_shape],
        compiler_params=_params("parallel"),
        name="sample_attend",
    )(z3, z3, z3, z3, *cache_args, cache_mk, cache_mv)


SAMPLE_GROUP = 8


def _sample_hgrn_body(zq_ref, zf_ref, zi_ref, lb_ref, gain_ref, s_ref, o_ref, so_ref):
    lb = _lower_bound(lb_ref)
    gain = gain_ref[...]
    q = jax.nn.sigmoid(zq_ref[...])
    kk = (1.0 - lb) * jax.nn.sigmoid(zf_ref[...])
    f = lb + kk
    v = zi_ref[...]
    pad = jnp.zeros((HEAD_DIM - SAMPLE_GROUP, HEAD_DIM), F32)

    def columns(a, h):
        return jnp.concatenate([a[:, h * HEAD_DIM:(h + 1) * HEAD_DIM], pad], axis=0).T

    for h in range(HEADS):
        sl = slice(h * HEAD_DIM, (h + 1) * HEAD_DIM)
        q_t, k_t, f_t = columns(q, h), columns(kk, h), columns(f, h)
        for j in range(SAMPLE_GROUP):
            s1 = f_t[:, j:j + 1] * s_ref[j, h] + k_t[:, j:j + 1] * v[j:j + 1, sl]
            so_ref[j, h] = s1
            o = jnp.sum(q_t[:, j:j + 1] * s1, axis=0, keepdims=True)
            o_ref[j:j + 1, sl] = _rms(o, gain[:, sl])


def _sample_hgrn(z, lb_logits, gain, state):
    bd = z.shape[0]
    zspec = lambda col: pl.BlockSpec((SAMPLE_GROUP, WIDTH), lambda i: (i, col))
    sspec = pl.BlockSpec((None, SAMPLE_GROUP, HEADS, HEAD_DIM, HEAD_DIM), lambda i: (0, i, 0, 0, 0))
    return pl.pallas_call(
        _sample_hgrn_body,
        grid=(bd // SAMPLE_GROUP,),
        in_specs=[zspec(COL_HQ), zspec(COL_HF), zspec(COL_HI), _const_spec(lb_logits.shape),
                  _const_spec((1, WIDTH)), sspec],
        out_specs=[pl.BlockSpec((SAMPLE_GROUP, WIDTH), lambda i: (i, 0)), sspec],
        out_shape=[jax.ShapeDtypeStruct((bd, WIDTH), F32), jax.ShapeDtypeStruct(state.shape, F32)],
        compiler_params=_params("parallel"),
        name="sample_hgrn",
    )(z, z, z, lb_logits, gain, state)


def _gate_body(z_ref, o_ref):
    o_ref[...] = jax.nn.sigmoid(z_ref[...]).astype(BF16)


def _sample_gates(z):
    bd = z.shape[0]
    n = N_BRANCH * D_MODEL
    return pl.pallas_call(
        _gate_body,
        grid=(n // WIDTH,),
        in_specs=[pl.BlockSpec((bd, WIDTH), lambda j: (0, COL_GATE + j))],
        out_specs=pl.BlockSpec((bd, WIDTH), lambda j: (0, j)),
        out_shape=jax.ShapeDtypeStruct((bd, n), BF16),
        name="sample_gates",
    )(z)


def kernel(x_prompt, x_sample, mem_prompt, cache_win1_k, cache_win1_v, cache_win4_k, cache_win4_v, cache_win16_k, cache_win16_v, cache_mem_k, cache_mem_v, state_hgrn, g_ff1, w_ff1_gate, w_ff1_up, w_ff1_down, g_mix, w_in, hg_lb_logits, g_hg_out, g_mem, w_mem_k, w_mem_v, w_branch_att, w_branch_hg, w_branch_mem, w_out, g_ff2, w_ff2_gate, w_ff2_up, w_ff2_down, g_final):
    b, t, _ = x_prompt.shape
    bd = x_sample.shape[0]
    assert g_ff1.shape[0] == 1 and x_sample.shape[1] == 1 and w_in.shape[2] == IN_WIDTH
    bf = lambda w: w[0].astype(BF16)
    ffn1 = (g_ff1, bf(w_ff1_gate), bf(w_ff1_up), bf(w_ff1_down), g_final.reshape(1, D_MODEL))
    ffn2 = (g_ff2, bf(w_ff2_gate), bf(w_ff2_up), bf(w_ff2_down), g_final.reshape(1, D_MODEL))
    w_in_b = bf(w_in)
    w_qkv, w_rest = w_in_b[:, :COL_HQ * WIDTH], w_in_b[:, COL_HQ * WIDTH:]
    branch_w = (bf(w_branch_att), bf(w_branch_hg), bf(w_branch_mem), bf(w_out))

    xs, _ = _ffn(x_sample.reshape(bd, D_MODEL), *ffn1, tm=bd, final=False)
    zs = _sample_in_proj(xs, g_mix, w_in_b)
    caches_k = (cache_win1_k, cache_win4_k, cache_win16_k)
    caches_v = (cache_win1_v, cache_win4_v, cache_win16_v)
    z3 = zs.reshape(bd, 1, IN_WIDTH)
    att_s, mem_s = _sample_attend(z3, caches_k, caches_v, cache_mem_k, cache_mem_v)
    hg_s, s_hgrn = _sample_hgrn(zs, hg_lb_logits, g_hg_out, state_hgrn)
    xs = _merge(xs, att_s.reshape(bd, WIDTH), hg_s, mem_s.reshape(bd, WIDTH), _sample_gates(zs), *branch_w, tm=bd)
    y_sample = _ffn(xs, *ffn2, tm=bd, final=True)[0].reshape(bd, 1, D_MODEL)

    tm = b * t // bd
    xp, (s_win16_k,) = _ffn(x_prompt.reshape(b * t, D_MODEL), *ffn1, tm=tm, final=False, ride=(cache_win16_k,))
    qkv = _qkv_proj(xp.reshape(b, t, D_MODEL), g_mix, w_qkv, tm=512)
    q_g, k_g, v_g, p_k, p_v = qkv[0:3], qkv[3:6], qkv[6:9], qkv[9:12], qkv[12:15]
    (hq, hk, lf, hv, mq, gates), (s_win1_k, s_win1_v, s_win4_k, s_win4_v) = _rest_proj(
        xp, g_mix, w_rest, hg_lb_logits, tm=tm, ride=(cache_win1_k, cache_win1_v, cache_win4_k, cache_win4_v))
    outs, lses = [], []
    for g, (_, dil) in enumerate(ATT_GROUPS):
        o, lse = _att_group(q_g[g], k_g[g], v_g[g], dil)
        outs.append(o)
        lses.append(lse)
    hg_p, p_hgrn = _hgrn_prompt(hq, hk, lf, hv, g_hg_out, b, t)
    p_mem_k, p_mem_v, mk_b, mv_b = _mem_kv(mem_prompt, g_mem, bf(w_mem_k), bf(w_mem_v))
    att_p, mem_p = _combine(outs, lses, mq, mk_b, mv_b, b, t, tm=512)
    xp = _merge(xp, att_p, hg_p, mem_p, gates, *branch_w, tm=512)
    y_prompt, (s_win16_v,) = _ffn(xp, *ffn2, tm=tm, final=True, ride=(cache_win16_v,))
    s_win1_k, s_win1_v, s_win4_k, s_win4_v, s_win16_k, s_win16_v = _write_new_rows(
        z3, [s_win1_k, s_win1_v, s_win4_k, s_win4_v, s_win16_k, s_win16_v],
        ((0, False), (0, True), (1, False), (1, True), (2, False), (2, True)))

    return (y_prompt.reshape(b, t, D_MODEL), y_sample,
            p_k[0], p_v[0], p_k[1], p_v[1], p_k[2], p_v[2],
            p_mem_k, p_mem_v, p_hgrn,
            s_win1_k, s_win1_v, s_win4_k, s_win4_v, s_win16_k, s_win16_v,
            s_hgrn)
```

```python
import functools

import numpy as np
import jax
import jax.numpy as jnp
from jax import lax
from jax.experimental import pallas as pl
from jax.experimental.pallas import tpu as pltpu

F32 = jnp.float32
BF16 = jnp.bfloat16

D_MODEL = 1024
D_FF = 2816
EPS = 1e-6
HEADS = 4
HEAD_DIM = 128
WIDTH = HEADS * HEAD_DIM
ATT_GROUPS = ((128, 1), (512, 4), (2048, 16))
N_GROUPS = 3
BLOCK = 128
MEM_LEN = 256
N_BRANCH = 3
LSE_REP = 32
QK_SCALE = HEAD_DIM ** -0.5
VMEM_LIMIT_BYTES = 56 * 1024 * 1024

COL_Q, COL_K, COL_V, COL_HQ, COL_HF, COL_HI, COL_MQ, COL_GATE = 0, 3, 6, 9, 10, 11, 12, 13
IN_WIDTH = (COL_GATE + N_BRANCH * D_MODEL // WIDTH) * WIDTH


def _params(*sem):
    return pltpu.CompilerParams(dimension_semantics=sem, vmem_limit_bytes=VMEM_LIMIT_BYTES)


def _const_spec(shape):
    nd = len(shape)
    return pl.BlockSpec(shape, lambda *_: (0,) * nd, pipeline_mode=pl.Buffered(1))


def _rms(x, g):
    return x * lax.rsqrt(jnp.mean(x * x, axis=-1, keepdims=True) + EPS) * g


def _dot(a, b):
    return jnp.dot(a, b, preferred_element_type=F32)


def _dot_nt(a, b):
    return lax.dot_general(a, b, (((1,), (1,)), ((), ())), preferred_element_type=F32)


def _ride_specs(caches, n_steps):
    if not caches:
        return [], [], [], []
    assert all(c.shape[1] == n_steps for c in caches)
    in_specs = [pl.BlockSpec((1, 1, c.shape[2], HEADS, HEAD_DIM), lambda i: (0, i, 0, 0, 0)) for c in caches]
    out_specs = [pl.BlockSpec(memory_space=pl.ANY)] * len(caches)
    shapes = [jax.ShapeDtypeStruct(c.shape, c.dtype) for c in caches]
    return in_specs, out_specs, shapes, [pltpu.SemaphoreType.DMA((2 * len(caches),))]


def _ride_copies(in_refs, out_refs, sem):
    seq = pl.program_id(0)
    copies = []
    for j, (cache_ref, out_ref) in enumerate(zip(in_refs, out_refs)):
        rows = cache_ref.shape[2]
        copies.append(pltpu.make_async_copy(cache_ref.at[:, :, pl.ds(1, rows - 1)],
                                            out_ref.at[:, pl.ds(seq, 1), pl.ds(0, rows - 1)], sem.at[2 * j]))
        copies.append(pltpu.make_async_copy(cache_ref.at[:, :, pl.ds(rows - 1, 1)],
                                            out_ref.at[:, pl.ds(seq, 1), pl.ds(rows - 1, 1)], sem.at[2 * j + 1]))
    for cp in copies:
        cp.start()
    return copies


def _new_rows_body(kn_ref, vn_ref, *refs, meta):
    n = len(meta)
    out_refs, row_refs, sem = refs[n:2 * n], refs[2 * n:3 * n], refs[3 * n]
    copies = []
    for j, ((g, is_v), out_ref, row_ref) in enumerate(zip(meta, out_refs, row_refs)):
        bd, rows = out_ref.shape[1], out_ref.shape[2]
        new = (vn_ref if is_v else kn_ref)[:, 0, g * WIDTH:(g + 1) * WIDTH]
        flat = row_ref.reshape(bd * HEADS, HEAD_DIM)
        for h in range(HEADS):
            flat[pl.ds(h, bd, stride=HEADS), :] = new[:, h * HEAD_DIM:(h + 1) * HEAD_DIM]
        copies.append(pltpu.make_async_copy(row_ref, out_ref.at[:, :, pl.ds(rows - 1, 1)], sem.at[j]))
        copies[-1].start()
    for cp in copies:
        cp.wait()


def _write_new_rows(z3, shifted, meta):
    bd = z3.shape[0]
    n = len(shifted)
    zspec = lambda col: pl.BlockSpec((bd, 1, N_GROUPS * WIDTH), lambda i: (0, 0, col // N_GROUPS))
    any_spec = pl.BlockSpec(memory_space=pl.ANY)
    return pl.pallas_call(
        functools.partial(_new_rows_body, meta=meta),
        grid=(1,),
        in_specs=[zspec(COL_K), zspec(COL_V)] + [any_spec] * n,
        out_specs=[any_spec] * n,
        out_shape=[jax.ShapeDtypeStruct(c.shape, c.dtype) for c in shifted],
        input_output_aliases={2 + j: j for j in range(n)},
        scratch_shapes=[pltpu.VMEM((1, bd, 1, HEADS, HEAD_DIM), F32)] * n + [pltpu.SemaphoreType.DMA((n,))],
        name="write_new_rows",
    )(z3, z3, *shifted)


FF_CHUNKS = 1


def _ffn_body(x_ref, g_ref, wg_ref, wu_ref, wd_ref, gf_ref, *rest, final, n_ride):
    o_ref = rest[n_ride]
    copies = _ride_copies(rest[:n_ride], rest[n_ride + 1:2 * n_ride + 1], rest[-1]) if n_ride else []
    x = x_ref[...]
    h = _rms(x, g_ref[...]).astype(BF16)
    y = x
    width = D_FF // FF_CHUNKS
    for c in range(FF_CHUNKS):
        cols = slice(c * width, (c + 1) * width)
        a = _dot(h, wg_ref[:, cols])
        u = _dot(h, wu_ref[:, cols])
        act = (a * jax.nn.sigmoid(a) * u).astype(BF16)
        y = y + 0.5 * _dot(act, wd_ref[cols, :])
    if final:
        y = _rms(y, gf_ref[...])
    o_ref[...] = y
    for cp in copies:
        cp.wait()


def _ffn(x, g, wg, wu, wd, g_final, *, tm, final, ride=()):
    m = x.shape[0]
    r_in, r_out, r_shapes, r_scratch = _ride_specs(ride, m // tm)
    res = pl.pallas_call(
        functools.partial(_ffn_body, final=final, n_ride=len(ride)),
        grid=(m // tm,),
        in_specs=[pl.BlockSpec((tm, D_MODEL), lambda i: (i, 0)),
                  _const_spec((1, D_MODEL)),
                  _const_spec((D_MODEL, D_FF)), _const_spec((D_MODEL, D_FF)), _const_spec((D_FF, D_MODEL)),
                  _const_spec((1, D_MODEL))] + r_in,
        out_specs=[pl.BlockSpec((tm, D_MODEL), lambda i: (i, 0))] + r_out,
        out_shape=[jax.ShapeDtypeStruct((m, D_MODEL), F32)] + r_shapes,
        scratch_shapes=r_scratch,
        compiler_params=_params("parallel"),
        name="ffn_final" if final else "ffn",
    )(x, g, wg, wu, wd, g_final, *ride)
    return res[0], res[1:]


def _store_heads(ref, z):
    rows = ref.shape[0]
    flat = ref.reshape(rows * HEADS, HEAD_DIM)
    for h in range(HEADS):
        flat[pl.ds(h, rows, stride=HEADS), :] = z[:, h * HEAD_DIM:(h + 1) * HEAD_DIM]


def _load_head(ref, h):
    rows = ref.shape[0]
    return ref.reshape(rows * HEADS, HEAD_DIM)[pl.ds(h, rows, stride=HEADS), :]


def _to_strided_view(dst_ref, z, stage_ref, dil):
    if dil == 1:
        dst_ref[...] = z.astype(dst_ref.dtype)
        return
    n = z.shape[0] // dil
    for c in range(stage_ref.shape[0]):
        stage_ref[c] = z[:, c * 128:(c + 1) * 128]
    for r in range(dil):
        for c in range(stage_ref.shape[0]):
            dst_ref[:, r * WIDTH + c * 128:r * WIDTH + (c + 1) * 128] = (
                stage_ref[c, pl.ds(r, n, stride=dil), :].astype(dst_ref.dtype))


def _qkv_body(x_ref, g_ref, w_ref, *refs, tm, n_tiles):
    q_refs, k_refs, v_refs = refs[0:3], refs[3:6], refs[6:9]
    pk_refs, pv_refs, stage_refs = refs[9:12], refs[12:15], refs[15:]
    h = _rms(x_ref[...], g_ref[...]).astype(BF16)
    late = []
    for g, (win, dil) in enumerate(ATT_GROUPS):
        keep = min(win, tm * n_tiles)
        for kind, (col, dst_refs, win_refs) in enumerate(((COL_Q, q_refs, None), (COL_K, k_refs, pk_refs),
                                                          (COL_V, v_refs, pv_refs))):
            z = _dot(h, w_ref[:, (col + g) * WIDTH:(col + g + 1) * WIDTH])
            if win_refs is None:
                z = z * QK_SCALE
            stage = stage_refs[3 * (g - 1) + kind] if dil > 1 else None
            _to_strided_view(dst_refs[g], z, stage, dil)
            if win_refs is None:
                continue
            if keep == tm * n_tiles:
                _store_heads(win_refs[g], z)
            elif dil == 1:
                late.append((win_refs[g], z[tm - keep:, :], None))
            else:
                assert keep == tm
                late.append((win_refs[g], None, stage))

    @pl.when(pl.program_id(1) == n_tiles - 1)
    def _():
        for ref, rows, stage in late:
            if stage is None:
                _store_heads(ref, rows)
            else:
                _store_heads(ref, jnp.concatenate([stage[hd] for hd in range(HEADS)], axis=-1))


def _qkv_proj(x, g_mix, w_qkv, *, tm):
    b, t, _ = x.shape
    n_tiles = t // tm
    view_specs = [pl.BlockSpec((None, tm // d, d * WIDTH), lambda bi, i: (bi, i, 0)) for _, d in ATT_GROUPS]
    view_shapes = [jax.ShapeDtypeStruct((b, t // d, d * WIDTH), BF16) for _, d in ATT_GROUPS]
    win_specs, win_shapes = [], []
    for win, _ in ATT_GROUPS:
        keep = min(win, t)
        assert keep == t or keep <= tm
        if keep == t:
            win_specs.append(pl.BlockSpec((None, None, tm, HEADS, HEAD_DIM), lambda bi, i: (0, bi, i, 0, 0)))
        else:
            win_specs.append(pl.BlockSpec((None, None, keep, HEADS, HEAD_DIM), lambda bi, i: (0, bi, 0, 0, 0)))
        win_shapes.append(jax.ShapeDtypeStruct((1, b, keep, HEADS, HEAD_DIM), F32))
    return pl.pallas_call(
        functools.partial(_qkv_body, tm=tm, n_tiles=n_tiles),
        grid=(b, n_tiles),
        in_specs=[pl.BlockSpec((None, tm, D_MODEL), lambda bi, i: (bi, i, 0)),
                  _const_spec((1, D_MODEL)),
                  _const_spec((D_MODEL, 3 * N_GROUPS * WIDTH))],
        out_specs=view_specs * 3 + win_specs + win_specs,
        out_shape=view_shapes * 3 + win_shapes + win_shapes,
        scratch_shapes=[pltpu.VMEM((WIDTH // 128, tm, 128), F32)] * (3 * sum(d > 1 for _, d in ATT_GROUPS)),
        compiler_params=_params("parallel", "arbitrary"),
        name="qkv_proj",
    )(x, g_mix, w_qkv)


def _lower_bound(logits_ref):
    lg = logits_ref[...]
    e = jnp.exp(lg - jnp.max(lg, axis=0, keepdims=True))
    return e[0:1, :] / jnp.sum(e, axis=0, keepdims=True)


def _rest_body(x_ref, g_ref, w_ref, lb_ref, *rest, n_ride):
    hq_ref, hk_ref, lf_ref, hv_ref, mq_ref, gate_ref = rest[n_ride:n_ride + 6]
    copies = _ride_copies(rest[:n_ride], rest[n_ride + 6:2 * n_ride + 6], rest[-1]) if n_ride else []
    h = _rms(x_ref[...], g_ref[...]).astype(BF16)
    lb = _lower_bound(lb_ref)
    hq_ref[...] = jax.nn.sigmoid(_dot(h, w_ref[:, 0:WIDTH])).astype(BF16)
    kk = (1.0 - lb) * jax.nn.sigmoid(_dot(h, w_ref[:, WIDTH:2 * WIDTH]))
    hk_ref[...] = kk.astype(BF16)
    lf_ref[...] = jnp.log(lb + kk)
    hv_ref[...] = _dot(h, w_ref[:, 2 * WIDTH:3 * WIDTH]).astype(BF16)
    mq_ref[...] = (_dot(h, w_ref[:, 3 * WIDTH:4 * WIDTH]) * QK_SCALE).astype(BF16)
    for c in range(N_BRANCH * D_MODEL // WIDTH):
        z = _dot(h, w_ref[:, (4 + c) * WIDTH:(5 + c) * WIDTH])
        gate_ref[:, c * WIDTH:(c + 1) * WIDTH] = jax.nn.sigmoid(z).astype(BF16)
    for cp in copies:
        cp.wait()


def _rest_proj(x, g_mix, w_rest, lb_logits, *, tm, ride=()):
    m = x.shape[0]
    spec = pl.BlockSpec((tm, WIDTH), lambda i: (i, 0))
    bf = jax.ShapeDtypeStruct((m, WIDTH), BF16)
    r_in, r_out, r_shapes, r_scratch = _ride_specs(ride, m // tm)
    res = pl.pallas_call(
        functools.partial(_rest_body, n_ride=len(ride)),
        grid=(m // tm,),
        in_specs=[pl.BlockSpec((tm, D_MODEL), lambda i: (i, 0)),
                  _const_spec((1, D_MODEL)),
                  _const_spec(w_rest.shape),
                  _const_spec(lb_logits.shape)] + r_in,
        out_specs=[spec, spec, spec, spec, spec, pl.BlockSpec((tm, N_BRANCH * D_MODEL), lambda i: (i, 0))] + r_out,
        out_shape=[bf, bf, jax.ShapeDtypeStruct((m, WIDTH), F32), bf, bf,
                   jax.ShapeDtypeStruct((m, N_BRANCH * D_MODEL), BF16)] + r_shapes,
        scratch_shapes=r_scratch,
        compiler_params=_params("parallel"),
        name="rest_proj",
    )(x, g_mix, w_rest, lb_logits, *ride)
    return res[:6], res[6:]


ATT_UNITS = 4
STAT_REP = LSE_REP // 2


def _att_body(q_ref, k_ref, v_ref, *rest, n_row_blocks, n_col_blocks, has_halo):
    if has_halo:
        kh_ref, vh_ref, o_ref, st_ref = rest
        halo_ok = pl.program_id(1) > 0
    else:
        o_ref, st_ref = rest
    qi = lax.broadcasted_iota(jnp.int32, (BLOCK, 2 * BLOCK), 0)
    ki = lax.broadcasted_iota(jnp.int32, (BLOCK, 2 * BLOCK), 1)
    own_ok = (lax.broadcasted_iota(jnp.int32, (BLOCK, BLOCK), 1)
              <= lax.broadcasted_iota(jnp.int32, (BLOCK, BLOCK), 0))
    both_ok = jnp.where(ki < BLOCK, ki, qi + BLOCK) >= jnp.where(ki < BLOCK, qi, ki)
    if has_halo:
        first_valid = jnp.where(halo_ok, 0, BLOCK)
        halo_both_ok = jnp.logical_and(both_ok, ki >= first_valid)

    units = []
    for c in range(n_col_blocks):
        for u in range(n_row_blocks):
            for h in range(HEADS):
                rows = slice(u * BLOCK, (u + 1) * BLOCK)
                cols = slice(c * WIDTH + h * HEAD_DIM, c * WIDTH + (h + 1) * HEAD_DIM)
                lcols = (c * HEADS + h) * LSE_REP
                if u > 0:
                    with_prev = slice((u - 1) * BLOCK, (u + 1) * BLOCK)
                    keys = lambda ref, rr=with_prev, cc=cols: ref[rr, cc]
                    ok = both_ok
                elif has_halo:
                    hcols = slice(h * HEAD_DIM, (h + 1) * HEAD_DIM)
                    keys = lambda ref, rr=rows, cc=cols, hc=hcols: jnp.concatenate(
                        [(kh_ref if ref is k_ref else vh_ref)[:, hc], ref[rr, cc]], axis=0)
                    ok = halo_both_ok
                else:
                    keys = lambda ref, rr=rows, cc=cols: ref[rr, cc]
                    ok = own_ok
                units.append((rows, cols, lcols, keys, ok))

    scores = [jnp.where(ok, _dot_nt(q_ref[rows, cols], keys(k_ref)), -jnp.inf) for rows, cols, _, keys, ok in units]

    probs = []
    for s in scores:
        m = jnp.max(s, axis=-1, keepdims=True)
        p = jnp.exp(s - m)
        probs.append((p.astype(BF16), m, jnp.sum(p, axis=-1, keepdims=True)))

    for (rows, cols, lcols, keys, _), (p, m, l) in zip(units, probs):
        o_ref[rows, cols] = _dot(p, keys(v_ref)).astype(BF16)
        st_ref[rows, lcols:lcols + STAT_REP] = jnp.broadcast_to(m, (BLOCK, STAT_REP))
        st_ref[rows, lcols + STAT_REP:lcols + LSE_REP] = jnp.broadcast_to(l, (BLOCK, STAT_REP))


def _att_group(q, k, v, dil):
    b, sub_len, _ = q.shape
    lw = HEADS * LSE_REP
    n_row_blocks = min(ATT_UNITS, sub_len // BLOCK)
    n_col_blocks = ATT_UNITS // n_row_blocks
    span = n_row_blocks * BLOCK
    assert sub_len % span == 0 and dil % n_col_blocks == 0
    has_halo = sub_len > span
    assert not has_halo or (dil == 1 and n_col_blocks == 1)
    n = (sub_len // span) * (dil // n_col_blocks)
    tile = lambda w: pl.BlockSpec((None, span, n_col_blocks * w), lambda bi, j: (bi, j if has_halo else 0,
                                                                                  0 if has_halo else j))
    halo = pl.BlockSpec((None, BLOCK, WIDTH), lambda bi, j: (bi, jnp.maximum(j * n_row_blocks - 1, 0), 0))
    in_specs = [tile(WIDTH)] * 3 + ([halo, halo] if has_halo else [])
    args = (q, k, v) + ((k, v) if has_halo else ())
    return pl.pallas_call(
        functools.partial(_att_body, n_row_blocks=n_row_blocks, n_col_blocks=n_col_blocks, has_halo=has_halo),
        grid=(b, n),
        in_specs=in_specs,
        out_specs=[tile(WIDTH), tile(lw)],
        out_shape=[jax.ShapeDtypeStruct((b, sub_len, dil * WIDTH), BF16),
                   jax.ShapeDtypeStruct((b, sub_len, dil * lw), F32)],
        compiler_params=_params("parallel", "arbitrary"),
        name=f"att_dil{dil}",
    )(*args)


N_LEVELS = 7
ROW_CUM, ROW_REM = N_LEVELS, N_LEVELS + 1


def _hgrn_constants():
    r = np.arange(BLOCK)[:, None]
    s = np.arange(BLOCK)[None, :]
    mats = []
    for lvl in range(N_LEVELS):
        half = 1 << lvl
        mid = (r // (2 * half)) * 2 * half + half
        upper = (r % (2 * half)) >= half
        mats.append(np.where(upper, (s >= mid) & (s <= r), (s > r) & (s <= mid - 1)))
    mats.append(s <= r)
    mats.append(s > r)
    range_mat = np.concatenate(mats, axis=0).astype(np.float32)
    range_mat = np.concatenate([range_mat, range_mat], axis=1)
    x = r ^ s
    level = np.where(s < r, np.floor(np.log2(np.maximum(x, 1))).astype(np.int32), np.where(s == r, N_LEVELS, -1))
    return jnp.asarray(range_mat, BF16), jnp.asarray(level, jnp.int32)


HGRN_SEQS = 2


def _hgrn_intra(q_b, k_b, lf, rng, level):
    lf_hi = lf.astype(BF16)
    lf_lo = (lf - lf_hi.astype(F32)).astype(BF16)
    z = _dot(rng, jnp.concatenate([lf_hi, lf_lo], axis=0))
    q = q_b.astype(F32)
    kk = k_b.astype(F32)
    lhs, rhs = [], []
    for lvl in range(N_LEVELS):
        x = jnp.exp(z[lvl * BLOCK:(lvl + 1) * BLOCK])
        lhs.append((q * x).astype(BF16))
        rhs.append((kk * x).astype(BF16))
    lhs.append(q_b)
    rhs.append(k_b)
    b_cum = z[ROW_CUM * BLOCK:(ROW_CUM + 1) * BLOCK]
    q_in = (q * jnp.exp(b_cum)).astype(BF16)
    k_out = (kk * jnp.exp(z[ROW_REM * BLOCK:(ROW_REM + 1) * BLOCK])).astype(BF16)
    decay = jnp.exp(b_cum[BLOCK - 1:BLOCK, :])
    mats = []
    for h in range(HEADS):
        sl = slice(h * HEAD_DIM, (h + 1) * HEAD_DIM)
        a = jnp.zeros((BLOCK, BLOCK), F32)
        for lvl in range(N_LEVELS + 1):
            a = jnp.where(level == lvl, _dot_nt(lhs[lvl][:, sl], rhs[lvl][:, sl]), a)
        mats.append(a.astype(BF16))
    return mats, q_in, k_out, decay


def _hgrn_state(mats, q_in, k_out, decay, v_b, gain, st_ref, j):
    outs = []
    for h in range(HEADS):
        sl = slice(h * HEAD_DIM, (h + 1) * HEAD_DIM)
        v = v_b[:, sl]
        st = st_ref[j, h]
        o = _dot(mats[h], v) + _dot_nt(q_in[:, sl], st.astype(BF16))
        outs.append(_rms(o, gain[:, sl]).astype(BF16))
        v_t = v.astype(F32).T.astype(BF16)
        st_ref[j, h] = st * decay[:, sl] + _dot(v_t, k_out[:, sl])
    return jnp.concatenate(outs, axis=-1)


def _hgrn_body(q_ref, k_ref, lf_ref, v_ref, rng_ref, lvl_ref, gain_ref, o_ref, s_ref, st_ref, *, n_chunks):
    c = pl.program_id(1)

    @pl.when(c == 0)
    def _():
        st_ref[...] = jnp.zeros_like(st_ref)

    rng, level, gain = rng_ref[...], lvl_ref[...], gain_ref[...]
    intra = [_hgrn_intra(q_ref[j], k_ref[j], lf_ref[j], rng, level) for j in range(HGRN_SEQS)]
    for j in range(HGRN_SEQS):
        o_ref[j] = _hgrn_state(*intra[j], v_ref[j], gain, st_ref, j)

    @pl.when(c == n_chunks - 1)
    def _():
        for j in range(HGRN_SEQS):
            for h in range(HEADS):
                s_ref[j, h] = st_ref[j, h].T


def _hgrn_prompt(hq, hk, lf, hv, gain, b, t):
    n_chunks = t // BLOCK
    range_mat, level = _hgrn_constants()
    as3d = lambda a: a.reshape(b, t, WIDTH)
    spec = pl.BlockSpec((HGRN_SEQS, BLOCK, WIDTH), lambda bi, c: (bi, c, 0))
    state_spec = pl.BlockSpec((None, HGRN_SEQS, HEADS, HEAD_DIM, HEAD_DIM), lambda bi, c: (0, bi, 0, 0, 0))
    o, state = pl.pallas_call(
        functools.partial(_hgrn_body, n_chunks=n_chunks),
        grid=(b // HGRN_SEQS, n_chunks),
        in_specs=[spec, spec, spec, spec, _const_spec(range_mat.shape), _const_spec(level.shape),
                  _const_spec((1, WIDTH))],
        out_specs=[spec, state_spec],
        out_shape=[jax.ShapeDtypeStruct((b, t, WIDTH), BF16),
                   jax.ShapeDtypeStruct((1, b, HEADS, HEAD_DIM, HEAD_DIM), F32)],
        scratch_shapes=[pltpu.VMEM((HGRN_SEQS, HEADS, HEAD_DIM, HEAD_DIM), F32)],
        compiler_params=_params("parallel", "arbitrary"),
        name="hgrn_prompt",
    )(as3d(hq), as3d(hk), as3d(lf), as3d(hv), range_mat, level, gain)
    return o.reshape(b * t, WIDTH), state


def _memkv_body(m_ref, g_ref, wk_ref, wv_ref, pk_ref, pv_ref, kb_ref, vb_ref):
    h = _rms(m_ref[...], g_ref[...]).astype(BF16)
    k = _dot(h, wk_ref[...])
    v = _dot(h, wv_ref[...])
    _store_heads(pk_ref, k)
    _store_heads(pv_ref, v)
    kb_ref[...] = k.astype(BF16)
    vb_ref[...] = v.astype(BF16)


def _mem_kv(mem, g_mem, w_mk, w_mv):
    b = mem.shape[0]
    head_spec = pl.BlockSpec((None, None, MEM_LEN, HEADS, HEAD_DIM), lambda bi: (0, bi, 0, 0, 0))
    flat_spec = pl.BlockSpec((None, MEM_LEN, WIDTH), lambda bi: (bi, 0, 0))
    head_shape = jax.ShapeDtypeStruct((1, b, MEM_LEN, HEADS, HEAD_DIM), F32)
    flat_shape = jax.ShapeDtypeStruct((b, MEM_LEN, WIDTH), BF16)
    return pl.pallas_call(
        _memkv_body,
        grid=(b,),
        in_specs=[pl.BlockSpec((None, MEM_LEN, D_MODEL), lambda bi: (bi, 0, 0)),
                  _const_spec((1, D_MODEL)), _const_spec((D_MODEL, WIDTH)), _const_spec((D_MODEL, WIDTH))],
        out_specs=[head_spec, head_spec, flat_spec, flat_spec],
        out_shape=[head_shape, head_shape, flat_shape, flat_shape],
        compiler_params=_params("parallel"),
        name="mem_kv",
    )(mem, g_mem, w_mk, w_mv)


def _from_strided_view(src_ref, stage_ref, dil, width):
    if dil == 1:
        return src_ref[...].astype(F32)
    n = src_ref.shape[0]
    n_col = width // 128
    for r in range(dil):
        for c in range(n_col):
            stage_ref[c, pl.ds(r, n, stride=dil), :] = (
                src_ref[:, r * width + c * 128:r * width + (c + 1) * 128].astype(F32))
    return jnp.concatenate([stage_ref[c] for c in range(n_col)], axis=-1)


def _gated_merge(x, att, hg, mem, gate_ref, wa_ref, wb_ref, wc_ref, wo_ref):
    def gate(j):
        return gate_ref[:, j * D_MODEL:(j + 1) * D_MODEL].astype(F32)

    m = gate(0) * _dot(att, wa_ref[...]) + gate(1) * _dot(hg, wb_ref[...]) + gate(2) * _dot(mem, wc_ref[...])
    return x + _dot(m.astype(BF16), wo_ref[...])


def _mix_body(x_ref, o0_ref, o1_ref, o2_ref, s0_ref, s1_ref, s2_ref, mq_ref, mk_ref, mv_ref, hg_ref, gate_ref,
              wa_ref, wb_ref, wc_ref, wo_ref, out_ref, *stage_refs):
    def gate(j):
        return gate_ref[:, j * D_MODEL:(j + 1) * D_MODEL].astype(F32)

    heads = [slice(h * HEAD_DIM, (h + 1) * HEAD_DIM) for h in range(HEADS)]
    merged = gate(1) * _dot(hg_ref[...], wb_ref[...])
    scores = [_dot_nt(mq_ref[:, sl], mk_ref[:, sl]) for sl in heads]
    probs = []
    for s in scores:
        p = jnp.exp(s - jnp.max(s, axis=-1, keepdims=True))
        probs.append((p.astype(BF16), 1.0 / jnp.sum(p, axis=-1, keepdims=True)))
    mem = [(_dot(p, mv_ref[:, sl]) * inv).astype(BF16) for (p, inv), sl in zip(probs, heads)]
    merged = merged + gate(2) * _dot(jnp.concatenate(mem, axis=-1), wc_ref[...])

    accs = [_from_strided_view(r, stage_refs[2 * g], ATT_GROUPS[g][1], WIDTH)
            for g, r in enumerate((o0_ref, o1_ref, o2_ref))]
    stats = [_from_strided_view(r, stage_refs[2 * g + 1], ATT_GROUPS[g][1], HEADS * LSE_REP)
             for g, r in enumerate((s0_ref, s1_ref, s2_ref))]
    att = []
    for h, sl in enumerate(heads):
        ms = [x[:, h * LSE_REP:h * LSE_REP + 1] for x in stats]
        ls = [x[:, h * LSE_REP + STAT_REP:h * LSE_REP + STAT_REP + 1] for x in stats]
        m = jnp.maximum(jnp.maximum(ms[0], ms[1]), ms[2])
        e = [jnp.exp(x - m) for x in ms]
        inv = 1.0 / (e[0] * ls[0] + e[1] * ls[1] + e[2] * ls[2])
        att.append(sum((e[g] * inv) * accs[g][:, sl] for g in range(N_GROUPS)).astype(BF16))
    merged = merged + gate(0) * _dot(jnp.concatenate(att, axis=-1), wa_ref[...])
    out_ref[...] = x_ref[...] + _dot(merged.astype(BF16), wo_ref[...])


def _mix(x, accs, stats, mq, mk, mv, hg, gates, wa, wb, wc, wo, b, t, *, tm):
    n_tiles = t // tm
    lw = HEADS * LSE_REP
    rows = lambda w: pl.BlockSpec((tm, w), lambda bi, i: (bi * n_tiles + i, 0))
    o_specs = [pl.BlockSpec((None, tm // d, d * WIDTH), lambda bi, i: (bi, i, 0)) for _, d in ATT_GROUPS]
    s_specs = [pl.BlockSpec((None, tm // d, d * lw), lambda bi, i: (bi, i, 0)) for _, d in ATT_GROUPS]
    memspec = pl.BlockSpec((None, MEM_LEN, WIDTH), lambda bi, i: (bi, 0, 0))
    stages = []
    for _ in ATT_GROUPS:
        stages += [pltpu.VMEM((WIDTH // 128, tm, 128), F32), pltpu.VMEM((lw // 128, tm, 128), F32)]
    return pl.pallas_call(
        _mix_body,
        grid=(b, n_tiles),
        in_specs=[rows(D_MODEL)] + o_specs + s_specs + [rows(WIDTH), memspec, memspec, rows(WIDTH),
                                                         rows(N_BRANCH * D_MODEL),
                                                         _const_spec((WIDTH, D_MODEL)), _const_spec((WIDTH, D_MODEL)),
                                                         _const_spec((WIDTH, D_MODEL)), _const_spec((D_MODEL, D_MODEL))],
        out_specs=rows(D_MODEL),
        out_shape=jax.ShapeDtypeStruct((b * t, D_MODEL), F32),
        scratch_shapes=stages,
        compiler_params=_params("parallel", "parallel"),
        name="mix",
    )(x, *accs, *stats, mq, mk, mv, hg, gates, wa, wb, wc, wo)


def _merge_body(x_ref, att_ref, hg_ref, mem_ref, gate_ref, wa_ref, wb_ref, wc_ref, wo_ref, o_ref):
    o_ref[...] = _gated_merge(x_ref[...], att_ref[...].astype(BF16), hg_ref[...].astype(BF16),
                              mem_ref[...].astype(BF16), gate_ref, wa_ref, wb_ref, wc_ref, wo_ref)


def _merge(x, att, hg, mem, gates, wa, wb, wc, wo, *, tm):
    m = x.shape[0]
    row = pl.BlockSpec((tm, WIDTH), lambda i: (i, 0))
    wide = pl.BlockSpec((tm, D_MODEL), lambda i: (i, 0))
    return pl.pallas_call(
        _merge_body,
        grid=(m // tm,),
        in_specs=[wide, row, row, row, pl.BlockSpec((tm, N_BRANCH * D_MODEL), lambda i: (i, 0)),
                  _const_spec((WIDTH, D_MODEL)), _const_spec((WIDTH, D_MODEL)), _const_spec((WIDTH, D_MODEL)),
                  _const_spec((D_MODEL, D_MODEL))],
        out_specs=wide,
        out_shape=jax.ShapeDtypeStruct((m, D_MODEL), F32),
        compiler_params=_params("parallel"),
        name="merge",
    )(x, att, hg, mem, gates, wa, wb, wc, wo)


def _norm_matmul_body(x_ref, g_ref, w_ref, o_ref):
    o_ref[...] = _dot(_rms(x_ref[...], g_ref[...]).astype(BF16), w_ref[...])


def _sample_in_proj(x, g_mix, w_in):
    m = x.shape[0]
    n = w_in.shape[1]
    return pl.pallas_call(
        _norm_matmul_body,
        grid=(n // WIDTH,),
        in_specs=[_const_spec((m, D_MODEL)), _const_spec((1, D_MODEL)),
                  pl.BlockSpec((D_MODEL, WIDTH), lambda j: (0, j))],
        out_specs=pl.BlockSpec((m, WIDTH), lambda j: (0, j)),
        out_shape=jax.ShapeDtypeStruct((m, n), F32),
        compiler_params=_params("parallel"),
        name="sample_in_proj",
    )(x, g_mix, w_in)


def _one_query_attend(q, k_rows, v_rows, k_new=None, v_new=None):
    s = jnp.sum(k_rows * q, axis=-1, keepdims=True)
    m = jnp.max(s, axis=0, keepdims=True)
    if k_new is not None:
        s_new = jnp.sum(k_new * q, axis=-1, keepdims=True)
        m = jnp.maximum(m, s_new)
    p = jnp.exp(s - m)
    l = jnp.sum(p, axis=0, keepdims=True)
    acc = jnp.sum(p * v_rows, axis=0, keepdims=True)
    if k_new is not None:
        p_new = jnp.exp(s_new - m)
        l = l + p_new
        acc = acc + p_new * v_new
    return acc / l, m + jnp.log(l)


def _sample_att_body(q_ref, kn_ref, vn_ref, mq_ref, k1_ref, v1_ref, k4_ref, v4_ref, k16_ref, v16_ref,
                     mk_ref, mv_ref, att_ref, mem_ref):
    k_refs = (k1_ref, k4_ref, k16_ref)
    v_refs = (v1_ref, v4_ref, v16_ref)
    for h in range(HEADS):
        sl = slice(h * HEAD_DIM, (h + 1) * HEAD_DIM)
        outs, lses = [], []
        for g in range(N_GROUPS):
            gs = slice(g * WIDTH + h * HEAD_DIM, g * WIDTH + (h + 1) * HEAD_DIM)
            o, lse = _one_query_attend(q_ref[:, gs] * QK_SCALE, _load_head(k_refs[g], h), _load_head(v_refs[g], h),
                                       kn_ref[:, gs], vn_ref[:, gs])
            outs.append(o)
            lses.append(lse)
        m = jnp.maximum(jnp.maximum(lses[0], lses[1]), lses[2])
        e = [jnp.exp(x - m) for x in lses]
        inv = 1.0 / (e[0] + e[1] + e[2])
        att_ref[:, sl] = sum((e[g] * inv) * outs[g] for g in range(N_GROUPS))
        o, _ = _one_query_attend(mq_ref[:, sl] * QK_SCALE, _load_head(mk_ref, h), _load_head(mv_ref, h))
        mem_ref[:, sl] = o


def _sample_attend(z3, caches_k, caches_v, cache_mk, cache_mv):
    bd = z3.shape[0]
    zspec = lambda col, n: pl.BlockSpec((None, 1, n * WIDTH), lambda b: (b, 0, col // n))
    cache_specs, cache_args = [], []
    for (win, dil), ck, cv in zip(ATT_GROUPS, caches_k, caches_v):
        rows = ck.shape[2]
        assert rows == win and rows // dil == BLOCK
        spec = pl.BlockSpec((None, None, BLOCK, None, HEADS, HEAD_DIM), lambda b: (0, b, 0, 0, 0, 0))
        for c in (ck, cv):
            cache_specs.append(spec)
            cache_args.append(c.reshape(1, bd, BLOCK, dil, HEADS, HEAD_DIM))
    mem_spec = pl.BlockSpec((None, None, MEM_LEN, HEADS, HEAD_DIM), lambda b: (0, b, 0, 0, 0))
    out_spec = pl.BlockSpec((None, 1, WIDTH), lambda b: (b, 0, 0))
    out_shape = jax.ShapeDtypeStruct((bd, 1, WIDTH), F32)
    return pl.pallas_call(
        _sample_att_body,
        grid=(bd,),
        in_specs=[zspec(COL_Q, 3), zspec(COL_K, 3), zspec(COL_V, 3), zspec(COL_MQ, 1)] + cache_specs
                 + [mem_spec, mem_spec],
        out_specs=[out_spec, out_spec],
        out_shape=[out_shape, out_shape],
        compiler_params=_params("parallel"),
        name="sample_attend",
    )(z3, z3, z3, z3, *cache_args, cache_mk, cache_mv)


SAMPLE_GROUP = 8


def _sample_hgrn_body(zq_ref, zf_ref, zi_ref, lb_ref, gain_ref, s_ref, o_ref, so_ref):
    lb = _lower_bound(lb_ref)
    gain = gain_ref[...]
    q = jax.nn.sigmoid(zq_ref[...])
    kk = (1.0 - lb) * jax.nn.sigmoid(zf_ref[...])
    f = lb + kk
    v = zi_ref[...]
    pad = jnp.zeros((HEAD_DIM - SAMPLE_GROUP, HEAD_DIM), F32)

    def columns(a, h):
        return jnp.concatenate([a[:, h * HEAD_DIM:(h + 1) * HEAD_DIM], pad], axis=0).T

    for h in range(HEADS):
        sl = slice(h * HEAD_DIM, (h + 1) * HEAD_DIM)
        q_t, k_t, f_t = columns(q, h), columns(kk, h), columns(f, h)
        for j in range(SAMPLE_GROUP):
            s1 = f_t[:, j:j + 1] * s_ref[j, h] + k_t[:, j:j + 1] * v[j:j + 1, sl]
            so_ref[j, h] = s1
            o = jnp.sum(q_t[:, j:j + 1] * s1, axis=0, keepdims=True)
            o_ref[j:j + 1, sl] = _rms(o, gain[:, sl])


def _sample_hgrn(z, lb_logits, gain, state):
    bd = z.shape[0]
    zspec = lambda col: pl.BlockSpec((SAMPLE_GROUP, WIDTH), lambda i: (i, col))
    sspec = pl.BlockSpec((None, SAMPLE_GROUP, HEADS, HEAD_DIM, HEAD_DIM), lambda i: (0, i, 0, 0, 0))
    return pl.pallas_call(
        _sample_hgrn_body,
        grid=(bd // SAMPLE_GROUP,),
        in_specs=[zspec(COL_HQ), zspec(COL_HF), zspec(COL_HI), _const_spec(lb_logits.shape),
                  _const_spec((1, WIDTH)), sspec],
        out_specs=[pl.BlockSpec((SAMPLE_GROUP, WIDTH), lambda i: (i, 0)), sspec],
        out_shape=[jax.ShapeDtypeStruct((bd, WIDTH), F32), jax.ShapeDtypeStruct(state.shape, F32)],
        compiler_params=_params("parallel"),
        name="sample_hgrn",
    )(z, z, z, lb_logits, gain, state)


def _gate_body(z_ref, o_ref):
    o_ref[...] = jax.nn.sigmoid(z_ref[...]).astype(BF16)


def _sample_gates(z):
    bd = z.shape[0]
    n = N_BRANCH * D_MODEL
    return pl.pallas_call(
        _gate_body,
        grid=(n // WIDTH,),
        in_specs=[pl.BlockSpec((bd, WIDTH), lambda j: (0, COL_GATE + j))],
        out_specs=pl.BlockSpec((bd, WIDTH), lambda j: (0, j)),
        out_shape=jax.ShapeDtypeStruct((bd, n), BF16),
        name="sample_gates",
    )(z)


def kernel(x_prompt, x_sample, mem_prompt, cache_win1_k, cache_win1_v, cache_win4_k, cache_win4_v, cache_win16_k, cache_win16_v, cache_mem_k, cache_mem_v, state_hgrn, g_ff1, w_ff1_gate, w_ff1_up, w_ff1_down, g_mix, w_in, hg_lb_logits, g_hg_out, g_mem, w_mem_k, w_mem_v, w_branch_att, w_branch_hg, w_branch_mem, w_out, g_ff2, w_ff2_gate, w_ff2_up, w_ff2_down, g_final):
    b, t, _ = x_prompt.shape
    bd = x_sample.shape[0]
    assert g_ff1.shape[0] == 1 and x_sample.shape[1] == 1 and w_in.shape[2] == IN_WIDTH
    bf = lambda w: w[0].astype(BF16)
    ffn1 = (g_ff1, bf(w_ff1_gate), bf(w_ff1_up), bf(w_ff1_down), g_final.reshape(1, D_MODEL))
    ffn2 = (g_ff2, bf(w_ff2_gate), bf(w_ff2_up), bf(w_ff2_down), g_final.reshape(1, D_MODEL))
    w_in_b = bf(w_in)
    w_qkv, w_rest = w_in_b[:, :COL_HQ * WIDTH], w_in_b[:, COL_HQ * WIDTH:]
    branch_w = (bf(w_branch_att), bf(w_branch_hg), bf(w_branch_mem), bf(w_out))

    xs, _ = _ffn(x_sample.reshape(bd, D_MODEL), *ffn1, tm=bd, final=False)
    zs = _sample_in_proj(xs, g_mix, w_in_b)
    caches_k = (cache_win1_k, cache_win4_k, cache_win16_k)
    caches_v = (cache_win1_v, cache_win4_v, cache_win16_v)
    z3 = zs.reshape(bd, 1, IN_WIDTH)
    att_s, mem_s = _sample_attend(z3, caches_k, caches_v, cache_mem_k, cache_mem_v)
    hg_s, s_hgrn = _sample_hgrn(zs, hg_lb_logits, g_hg_out, state_hgrn)
    xs = _merge(xs, att_s.reshape(bd, WIDTH), hg_s, mem_s.reshape(bd, WIDTH), _sample_gates(zs), *branch_w, tm=bd)
    y_sample = _ffn(xs, *ffn2, tm=bd, final=True)[0].reshape(bd, 1, D_MODEL)

    tm = b * t // bd
    xp, (s_win16_k,) = _ffn(x_prompt.reshape(b * t, D_MODEL), *ffn1, tm=tm, final=False, ride=(cache_win16_k,))
    qkv = _qkv_proj(xp.reshape(b, t, D_MODEL), g_mix, w_qkv, tm=512)
    q_g, k_g, v_g, p_k, p_v = qkv[0:3], qkv[3:6], qkv[6:9], qkv[9:12], qkv[12:15]
    (hq, hk, lf, hv, mq, gates), (s_win1_k, s_win1_v, s_win4_k, s_win4_v) = _rest_proj(
        xp, g_mix, w_rest, hg_lb_logits, tm=tm, ride=(cache_win1_k, cache_win1_v, cache_win4_k, cache_win4_v))
    outs, lses = [], []
    for g, (_, dil) in enumerate(ATT_GROUPS):
        o, lse = _att_group(q_g[g], k_g[g], v_g[g], dil)
        outs.append(o)
        lses.append(lse)
    hg_p, p_hgrn = _hgrn_prompt(hq, hk, lf, hv, g_hg_out, b, t)
    p_mem_k, p_mem_v, mk_b, mv_b = _mem_kv(mem_prompt, g_mem, bf(w_mem_k), bf(w_mem_v))
    xp = _mix(xp, outs, lses, mq, mk_b, mv_b, hg_p, gates, *branch_w, b, t, tm=512)
    y_prompt, (s_win16_v,) = _ffn(xp, *ffn2, tm=tm, final=True, ride=(cache_win16_v,))
    s_win1_k, s_win1_v, s_win4_k, s_win4_v, s_win16_k, s_win16_v = _write_new_rows(
        z3, [s_win1_k, s_win1_v, s_win4_k, s_win4_v, s_win16_k, s_win16_v],
        ((0, False), (0, True), (1, False), (1, True), (2, False), (2, True)))

    return (y_prompt.reshape(b, t, D_MODEL), y_sample,
            p_k[0], p_v[0], p_k[1], p_v[1], p_k[2], p_v[2],
            p_mem_k, p_mem_v, p_hgrn,
            s_win1_k, s_win1_v, s_win4_k, s_win4_v, s_win16_k, s_win16_v,
            s_hgrn)
```

```python
import functools

import numpy as np
import jax
import jax.numpy as jnp
from jax import lax
from jax.experimental import pallas as pl
from jax.experimental.pallas import tpu as pltpu

F32 = jnp.float32
BF16 = jnp.bfloat16

D_MODEL = 1024
D_FF = 2816
EPS = 1e-6
HEADS = 4
HEAD_DIM = 128
WIDTH = HEADS * HEAD_DIM
ATT_GROUPS = ((128, 1), (512, 4), (2048, 16))
N_GROUPS = 3
BLOCK = 128
MEM_LEN = 256
N_BRANCH = 3
LSE_REP = 32
QK_SCALE = HEAD_DIM ** -0.5
VMEM_LIMIT_BYTES = 56 * 1024 * 1024

COL_Q, COL_K, COL_V, COL_HQ, COL_HF, COL_HI, COL_MQ, COL_GATE = 0, 3, 6, 9, 10, 11, 12, 13
IN_WIDTH = (COL_GATE + N_BRANCH * D_MODEL // WIDTH) * WIDTH


def _params(*sem):
    return pltpu.CompilerParams(dimension_semantics=sem, vmem_limit_bytes=VMEM_LIMIT_BYTES)


def _const_spec(shape):
    nd = len(shape)
    return pl.BlockSpec(shape, lambda *_: (0,) * nd, pipeline_mode=pl.Buffered(1))


def _rms(x, g):
    return x * lax.rsqrt(jnp.mean(x * x, axis=-1, keepdims=True) + EPS) * g


def _dot(a, b):
    return jnp.dot(a, b, preferred_element_type=F32)


def _dot_nt(a, b):
    return lax.dot_general(a, b, (((1,), (1,)), ((), ())), preferred_element_type=F32)


def _ride_specs(caches, n_steps):
    if not caches:
        return [], [], [], []
    assert all(c.shape[1] == n_steps for c in caches)
    in_specs = [pl.BlockSpec((1, 1, c.shape[2], HEADS, HEAD_DIM), lambda i: (0, i, 0, 0, 0)) for c in caches]
    out_specs = [pl.BlockSpec(memory_space=pl.ANY)] * len(caches)
    shapes = [jax.ShapeDtypeStruct(c.shape, c.dtype) for c in caches]
    return in_specs, out_specs, shapes, [pltpu.SemaphoreType.DMA((2 * len(caches),))]


def _ride_copies(in_refs, out_refs, sem):
    seq = pl.program_id(0)
    copies = []
    for j, (cache_ref, out_ref) in enumerate(zip(in_refs, out_refs)):
        rows = cache_ref.shape[2]
        copies.append(pltpu.make_async_copy(cache_ref.at[:, :, pl.ds(1, rows - 1)],
                                            out_ref.at[:, pl.ds(seq, 1), pl.ds(0, rows - 1)], sem.at[2 * j]))
        copies.append(pltpu.make_async_copy(cache_ref.at[:, :, pl.ds(rows - 1, 1)],
                                            out_ref.at[:, pl.ds(seq, 1), pl.ds(rows - 1, 1)], sem.at[2 * j + 1]))
    for cp in copies:
        cp.start()
    return copies


def _new_rows_body(kn_ref, vn_ref, *refs, meta):
    n = len(meta)
    out_refs, row_refs, sem = refs[n:2 * n], refs[2 * n:3 * n], refs[3 * n]
    copies = []
    for j, ((g, is_v), out_ref, row_ref) in enumerate(zip(meta, out_refs, row_refs)):
        bd, rows = out_ref.shape[1], out_ref.shape[2]
        new = (vn_ref if is_v else kn_ref)[:, 0, g * WIDTH:(g + 1) * WIDTH]
        flat = row_ref.reshape(bd * HEADS, HEAD_DIM)
        for h in range(HEADS):
            flat[pl.ds(h, bd, stride=HEADS), :] = new[:, h * HEAD_DIM:(h + 1) * HEAD_DIM]
        copies.append(pltpu.make_async_copy(row_ref, out_ref.at[:, :, pl.ds(rows - 1, 1)], sem.at[j]))
        copies[-1].start()
    for cp in copies:
        cp.wait()


def _write_new_rows(z3, shifted, meta):
    bd = z3.shape[0]
    n = len(shifted)
    zspec = lambda col: pl.BlockSpec((bd, 1, N_GROUPS * WIDTH), lambda i: (0, 0, col // N_GROUPS))
    any_spec = pl.BlockSpec(memory_space=pl.ANY)
    return pl.pallas_call(
        functools.partial(_new_rows_body, meta=meta),
        grid=(1,),
        in_specs=[zspec(COL_K), zspec(COL_V)] + [any_spec] * n,
        out_specs=[any_spec] * n,
        out_shape=[jax.ShapeDtypeStruct(c.shape, c.dtype) for c in shifted],
        input_output_aliases={2 + j: j for j in range(n)},
        scratch_shapes=[pltpu.VMEM((1, bd, 1, HEADS, HEAD_DIM), F32)] * n + [pltpu.SemaphoreType.DMA((n,))],
        name="write_new_rows",
    )(z3, z3, *shifted)


FF_CHUNKS = 1


def _ffn_body(x_ref, g_ref, wg_ref, wu_ref, wd_ref, gf_ref, *rest, final, n_ride):
    o_ref = rest[n_ride]
    copies = _ride_copies(rest[:n_ride], rest[n_ride + 1:2 * n_ride + 1], rest[-1]) if n_ride else []
    x = x_ref[...]
    h = _rms(x, g_ref[...]).astype(BF16)
    y = x
    width = D_FF // FF_CHUNKS
    for c in range(FF_CHUNKS):
        cols = slice(c * width, (c + 1) * width)
        a = _dot(h, wg_ref[:, cols])
        u = _dot(h, wu_ref[:, cols])
        act = (a * jax.nn.sigmoid(a) * u).astype(BF16)
        y = y + 0.5 * _dot(act, wd_ref[cols, :])
    if final:
        y = _rms(y, gf_ref[...])
    o_ref[...] = y
    for cp in copies:
        cp.wait()


def _ffn(x, g, wg, wu, wd, g_final, *, tm, final, ride=()):
    m = x.shape[0]
    r_in, r_out, r_shapes, r_scratch = _ride_specs(ride, m // tm)
    res = pl.pallas_call(
        functools.partial(_ffn_body, final=final, n_ride=len(ride)),
        grid=(m // tm,),
        in_specs=[pl.BlockSpec((tm, D_MODEL), lambda i: (i, 0)),
                  _const_spec((1, D_MODEL)),
                  _const_spec((D_MODEL, D_FF)), _const_spec((D_MODEL, D_FF)), _const_spec((D_FF, D_MODEL)),
                  _const_spec((1, D_MODEL))] + r_in,
        out_specs=[pl.BlockSpec((tm, D_MODEL), lambda i: (i, 0))] + r_out,
        out_shape=[jax.ShapeDtypeStruct((m, D_MODEL), F32)] + r_shapes,
        scratch_shapes=r_scratch,
        compiler_params=_params("parallel"),
        name="ffn_final" if final else "ffn",
    )(x, g, wg, wu, wd, g_final, *ride)
    return res[0], res[1:]


def _store_heads(ref, z):
    rows = ref.shape[0]
    flat = ref.reshape(rows * HEADS, HEAD_DIM)
    for h in range(HEADS):
        flat[pl.ds(h, rows, stride=HEADS), :] = z[:, h * HEAD_DIM:(h + 1) * HEAD_DIM]


def _load_head(ref, h):
    rows = ref.shape[0]
    return ref.reshape(rows * HEADS, HEAD_DIM)[pl.ds(h, rows, stride=HEADS), :]


def _to_strided_view(dst_ref, z, stage_ref, dil):
    if dil == 1:
        dst_ref[...] = z.astype(dst_ref.dtype)
        return
    n = z.shape[0] // dil
    for c in range(stage_ref.shape[0]):
        stage_ref[c] = z[:, c * 128:(c + 1) * 128]
    for r in range(dil):
        for c in range(stage_ref.shape[0]):
            dst_ref[:, r * WIDTH + c * 128:r * WIDTH + (c + 1) * 128] = (
                stage_ref[c, pl.ds(r, n, stride=dil), :].astype(dst_ref.dtype))


def _qkv_body(x_ref, g_ref, w_ref, *refs, tm, n_tiles):
    q_refs, k_refs, v_refs = refs[0:3], refs[3:6], refs[6:9]
    pk_refs, pv_refs, stage_refs = refs[9:12], refs[12:15], refs[15:]
    h = _rms(x_ref[...], g_ref[...]).astype(BF16)
    late = []
    for g, (win, dil) in enumerate(ATT_GROUPS):
        keep = min(win, tm * n_tiles)
        for kind, (col, dst_refs, win_refs) in enumerate(((COL_Q, q_refs, None), (COL_K, k_refs, pk_refs),
                                                          (COL_V, v_refs, pv_refs))):
            z = _dot(h, w_ref[:, (col + g) * WIDTH:(col + g + 1) * WIDTH])
            if win_refs is None:
                z = z * QK_SCALE
            stage = stage_refs[3 * (g - 1) + kind] if dil > 1 else None
            _to_strided_view(dst_refs[g], z, stage, dil)
            if win_refs is None:
                continue
            if keep == tm * n_tiles:
                _store_heads(win_refs[g], z)
            elif dil == 1:
                late.append((win_refs[g], z[tm - keep:, :], None))
            else:
                assert keep == tm
                late.append((win_refs[g], None, stage))

    @pl.when(pl.program_id(1) == n_tiles - 1)
    def _():
        for ref, rows, stage in late:
            if stage is None:
                _store_heads(ref, rows)
            else:
                _store_heads(ref, jnp.concatenate([stage[hd] for hd in range(HEADS)], axis=-1))


def _qkv_proj(x, g_mix, w_qkv, *, tm):
    b, t, _ = x.shape
    n_tiles = t // tm
    view_specs = [pl.BlockSpec((None, tm // d, d * WIDTH), lambda bi, i: (bi, i, 0)) for _, d in ATT_GROUPS]
    view_shapes = [jax.ShapeDtypeStruct((b, t // d, d * WIDTH), BF16) for _, d in ATT_GROUPS]
    win_specs, win_shapes = [], []
    for win, _ in ATT_GROUPS:
        keep = min(win, t)
        assert keep == t or keep <= tm
        if keep == t:
            win_specs.append(pl.BlockSpec((None, None, tm, HEADS, HEAD_DIM), lambda bi, i: (0, bi, i, 0, 0)))
        else:
            win_specs.append(pl.BlockSpec((None, None, keep, HEADS, HEAD_DIM), lambda bi, i: (0, bi, 0, 0, 0)))
        win_shapes.append(jax.ShapeDtypeStruct((1, b, keep, HEADS, HEAD_DIM), F32))
    return pl.pallas_call(
        functools.partial(_qkv_body, tm=tm, n_tiles=n_tiles),
        grid=(b, n_tiles),
        in_specs=[pl.BlockSpec((None, tm, D_MODEL), lambda bi, i: (bi, i, 0)),
                  _const_spec((1, D_MODEL)),
                  _const_spec((D_MODEL, 3 * N_GROUPS * WIDTH))],
        out_specs=view_specs * 3 + win_specs + win_specs,
        out_shape=view_shapes * 3 + win_shapes + win_shapes,
        scratch_shapes=[pltpu.VMEM((WIDTH // 128, tm, 128), F32)] * (3 * sum(d > 1 for _, d in ATT_GROUPS)),
        compiler_params=_params("parallel", "arbitrary"),
        name="qkv_proj",
    )(x, g_mix, w_qkv)


def _lower_bound(logits_ref):
    lg = logits_ref[...]
    e = jnp.exp(lg - jnp.max(lg, axis=0, keepdims=True))
    return e[0:1, :] / jnp.sum(e, axis=0, keepdims=True)


def _rest_body(x_ref, g_ref, w_ref, lb_ref, *rest, n_ride):
    hq_ref, hk_ref, lf_ref, hv_ref, mq_ref, gate_ref = rest[n_ride:n_ride + 6]
    copies = _ride_copies(rest[:n_ride], rest[n_ride + 6:2 * n_ride + 6], rest[-1]) if n_ride else []
    h = _rms(x_ref[...], g_ref[...]).astype(BF16)
    lb = _lower_bound(lb_ref)
    hq_ref[...] = jax.nn.sigmoid(_dot(h, w_ref[:, 0:WIDTH])).astype(BF16)
    kk = (1.0 - lb) * jax.nn.sigmoid(_dot(h, w_ref[:, WIDTH:2 * WIDTH]))
    hk_ref[...] = kk.astype(BF16)
    lf_ref[...] = jnp.log(lb + kk)
    hv_ref[...] = _dot(h, w_ref[:, 2 * WIDTH:3 * WIDTH]).astype(BF16)
    mq_ref[...] = (_dot(h, w_ref[:, 3 * WIDTH:4 * WIDTH]) * QK_SCALE).astype(BF16)
    for c in range(N_BRANCH * D_MODEL // WIDTH):
        z = _dot(h, w_ref[:, (4 + c) * WIDTH:(5 + c) * WIDTH])
        gate_ref[:, c * WIDTH:(c + 1) * WIDTH] = jax.nn.sigmoid(z).astype(BF16)
    for cp in copies:
        cp.wait()


def _rest_proj(x, g_mix, w_rest, lb_logits, *, tm, ride=()):
    m = x.shape[0]
    spec = pl.BlockSpec((tm, WIDTH), lambda i: (i, 0))
    bf = jax.ShapeDtypeStruct((m, WIDTH), BF16)
    r_in, r_out, r_shapes, r_scratch = _ride_specs(ride, m // tm)
    res = pl.pallas_call(
        functools.partial(_rest_body, n_ride=len(ride)),
        grid=(m // tm,),
        in_specs=[pl.BlockSpec((tm, D_MODEL), lambda i: (i, 0)),
                  _const_spec((1, D_MODEL)),
                  _const_spec(w_rest.shape),
                  _const_spec(lb_logits.shape)] + r_in,
        out_specs=[spec, spec, spec, spec, spec, pl.BlockSpec((tm, N_BRANCH * D_MODEL), lambda i: (i, 0))] + r_out,
        out_shape=[bf, bf, jax.ShapeDtypeStruct((m, WIDTH), F32), bf, bf,
                   jax.ShapeDtypeStruct((m, N_BRANCH * D_MODEL), BF16)] + r_shapes,
        scratch_shapes=r_scratch,
        compiler_params=_params("parallel"),
        name="rest_proj",
    )(x, g_mix, w_rest, lb_logits, *ride)
    return res[:6], res[6:]


ATT_UNITS = 4
STAT_REP = LSE_REP // 2


def _att_body(*refs, tiles):
    n_in = [5 if has_halo else 3 for _, _, has_halo in tiles]
    outs = refs[sum(n_in):]
    pos = 0
    for g, (n_row_blocks, n_col_blocks, has_halo) in enumerate(tiles):
        _att_tile(*refs[pos:pos + n_in[g]], outs[2 * g], outs[2 * g + 1],
                  n_row_blocks=n_row_blocks, n_col_blocks=n_col_blocks, has_halo=has_halo)
        pos += n_in[g]


def _att_tile(q_ref, k_ref, v_ref, *rest, n_row_blocks, n_col_blocks, has_halo):
    if has_halo:
        kh_ref, vh_ref, o_ref, st_ref = rest
        halo_ok = pl.program_id(1) > 0
    else:
        o_ref, st_ref = rest
    qi = lax.broadcasted_iota(jnp.int32, (BLOCK, 2 * BLOCK), 0)
    ki = lax.broadcasted_iota(jnp.int32, (BLOCK, 2 * BLOCK), 1)
    own_ok = (lax.broadcasted_iota(jnp.int32, (BLOCK, BLOCK), 1)
              <= lax.broadcasted_iota(jnp.int32, (BLOCK, BLOCK), 0))
    both_ok = jnp.where(ki < BLOCK, ki, qi + BLOCK) >= jnp.where(ki < BLOCK, qi, ki)
    if has_halo:
        first_valid = jnp.where(halo_ok, 0, BLOCK)
        halo_both_ok = jnp.logical_and(both_ok, ki >= first_valid)

    units = []
    for c in range(n_col_blocks):
        for u in range(n_row_blocks):
            for h in range(HEADS):
                rows = slice(u * BLOCK, (u + 1) * BLOCK)
                cols = slice(c * WIDTH + h * HEAD_DIM, c * WIDTH + (h + 1) * HEAD_DIM)
                lcols = (c * HEADS + h) * LSE_REP
                if u > 0:
                    with_prev = slice((u - 1) * BLOCK, (u + 1) * BLOCK)
                    keys = lambda ref, rr=with_prev, cc=cols: ref[rr, cc]
                    ok = both_ok
                elif has_halo:
                    hcols = slice(h * HEAD_DIM, (h + 1) * HEAD_DIM)
                    keys = lambda ref, rr=rows, cc=cols, hc=hcols: jnp.concatenate(
                        [(kh_ref if ref is k_ref else vh_ref)[:, hc], ref[rr, cc]], axis=0)
                    ok = halo_both_ok
                else:
                    keys = lambda ref, rr=rows, cc=cols: ref[rr, cc]
                    ok = own_ok
                units.append((rows, cols, lcols, keys, ok))

    scores = [jnp.where(ok, _dot_nt(q_ref[rows, cols], keys(k_ref)), -jnp.inf) for rows, cols, _, keys, ok in units]

    probs = []
    for s in scores:
        m = jnp.max(s, axis=-1, keepdims=True)
        p = jnp.exp(s - m)
        probs.append((p.astype(BF16), m, jnp.sum(p, axis=-1, keepdims=True)))

    for (rows, cols, lcols, keys, _), (p, m, l) in zip(units, probs):
        o_ref[rows, cols] = _dot(p, keys(v_ref)).astype(BF16)
        st_ref[rows, lcols:lcols + STAT_REP] = jnp.broadcast_to(m, (BLOCK, STAT_REP))
        st_ref[rows, lcols + STAT_REP:lcols + LSE_REP] = jnp.broadcast_to(l, (BLOCK, STAT_REP))


def _attention(q_g, k_g, v_g):
    b = q_g[0].shape[0]
    lw = HEADS * LSE_REP
    n_steps = None
    in_specs, args, out_specs, out_shapes, tiles = [], [], [], [], []
    for (_, dil), q, k, v in zip(ATT_GROUPS, q_g, k_g, v_g):
        sub_len = q.shape[1]
        n_row_blocks = min(ATT_UNITS, sub_len // BLOCK)
        n_col_blocks = ATT_UNITS // n_row_blocks
        span = n_row_blocks * BLOCK
        assert sub_len % span == 0 and dil % n_col_blocks == 0
        has_halo = sub_len > span
        assert not has_halo or (dil == 1 and n_col_blocks == 1)
        n = (sub_len // span) * (dil // n_col_blocks)
        assert n_steps in (None, n)
        n_steps = n
        if has_halo:
            tile = lambda w, span=span: pl.BlockSpec((None, span, w), lambda bi, j: (bi, j, 0))
        else:
            tile = lambda w, span=span, nc=n_col_blocks: pl.BlockSpec((None, span, nc * w), lambda bi, j: (bi, 0, j))
        halo = pl.BlockSpec((None, BLOCK, WIDTH),
                            lambda bi, j, nr=n_row_blocks: (bi, jnp.maximum(j * nr - 1, 0), 0))
        in_specs += [tile(WIDTH)] * 3 + ([halo, halo] if has_halo else [])
        args += [q, k, v] + ([k, v] if has_halo else [])
        out_specs += [tile(WIDTH), tile(lw)]
        out_shapes += [jax.ShapeDtypeStruct((b, sub_len, dil * WIDTH), BF16),
                       jax.ShapeDtypeStruct((b, sub_len, dil * lw), F32)]
        tiles.append((n_row_blocks, n_col_blocks, has_halo))
    res = pl.pallas_call(
        functools.partial(_att_body, tiles=tuple(tiles)),
        grid=(b, n_steps),
        in_specs=in_specs,
        out_specs=out_specs,
        out_shape=out_shapes,
        compiler_params=_params("parallel", "arbitrary"),
        name="attention",
    )(*args)
    return res[0::2], res[1::2]


N_LEVELS = 7
ROW_CUM, ROW_REM = N_LEVELS, N_LEVELS + 1


def _hgrn_constants():
    r = np.arange(BLOCK)[:, None]
    s = np.arange(BLOCK)[None, :]
    mats = []
    for lvl in range(N_LEVELS):
        half = 1 << lvl
        mid = (r // (2 * half)) * 2 * half + half
        upper = (r % (2 * half)) >= half
        mats.append(np.where(upper, (s >= mid) & (s <= r), (s > r) & (s <= mid - 1)))
    mats.append(s <= r)
    mats.append(s > r)
    range_mat = np.concatenate(mats, axis=0).astype(np.float32)
    range_mat = np.concatenate([range_mat, range_mat], axis=1)
    x = r ^ s
    level = np.where(s < r, np.floor(np.log2(np.maximum(x, 1))).astype(np.int32), np.where(s == r, N_LEVELS, -1))
    return jnp.asarray(range_mat, BF16), jnp.asarray(level, jnp.int32)


HGRN_SEQS = 2


def _hgrn_intra(q_b, k_b, lf, rng, level):
    lf_hi = lf.astype(BF16)
    lf_lo = (lf - lf_hi.astype(F32)).astype(BF16)
    z = _dot(rng, jnp.concatenate([lf_hi, lf_lo], axis=0))
    q = q_b.astype(F32)
    kk = k_b.astype(F32)
    lhs, rhs = [], []
    for lvl in range(N_LEVELS):
        x = jnp.exp(z[lvl * BLOCK:(lvl + 1) * BLOCK])
        lhs.append((q * x).astype(BF16))
        rhs.append((kk * x).astype(BF16))
    lhs.append(q_b)
    rhs.append(k_b)
    b_cum = z[ROW_CUM * BLOCK:(ROW_CUM + 1) * BLOCK]
    q_in = (q * jnp.exp(b_cum)).astype(BF16)
    k_out = (kk * jnp.exp(z[ROW_REM * BLOCK:(ROW_REM + 1) * BLOCK])).astype(BF16)
    decay = jnp.exp(b_cum[BLOCK - 1:BLOCK, :])
    mats = []
    for h in range(HEADS):
        sl = slice(h * HEAD_DIM, (h + 1) * HEAD_DIM)
        a = jnp.zeros((BLOCK, BLOCK), F32)
        for lvl in range(N_LEVELS + 1):
            a = jnp.where(level == lvl, _dot_nt(lhs[lvl][:, sl], rhs[lvl][:, sl]), a)
        mats.append(a.astype(BF16))
    return mats, q_in, k_out, decay


def _hgrn_state(mats, q_in, k_out, decay, v_b, gain, st_ref, j):
    outs = []
    for h in range(HEADS):
        sl = slice(h * HEAD_DIM, (h + 1) * HEAD_DIM)
        v = v_b[:, sl]
        st = st_ref[j, h]
        o = _dot(mats[h], v) + _dot_nt(q_in[:, sl], st.astype(BF16))
        outs.append(_rms(o, gain[:, sl]).astype(BF16))
        v_t = v.astype(F32).T.astype(BF16)
        st_ref[j, h] = st * decay[:, sl] + _dot(v_t, k_out[:, sl])
    return jnp.concatenate(outs, axis=-1)


def _hgrn_body(q_ref, k_ref, lf_ref, v_ref, rng_ref, lvl_ref, gain_ref, o_ref, s_ref, st_ref, *, n_chunks):
    c = pl.program_id(1)

    @pl.when(c == 0)
    def _():
        st_ref[...] = jnp.zeros_like(st_ref)

    rng, level, gain = rng_ref[...], lvl_ref[...], gain_ref[...]
    intra = [_hgrn_intra(q_ref[j], k_ref[j], lf_ref[j], rng, level) for j in range(HGRN_SEQS)]
    for j in range(HGRN_SEQS):
        o_ref[j] = _hgrn_state(*intra[j], v_ref[j], gain, st_ref, j)

    @pl.when(c == n_chunks - 1)
    def _():
        for j in range(HGRN_SEQS):
            for h in range(HEADS):
                s_ref[j, h] = st_ref[j, h].T


def _hgrn_prompt(hq, hk, lf, hv, gain, b, t):
    n_chunks = t // BLOCK
    range_mat, level = _hgrn_constants()
    as3d = lambda a: a.reshape(b, t, WIDTH)
    spec = pl.BlockSpec((HGRN_SEQS, BLOCK, WIDTH), lambda bi, c: (bi, c, 0))
    state_spec = pl.BlockSpec((None, HGRN_SEQS, HEADS, HEAD_DIM, HEAD_DIM), lambda bi, c: (0, bi, 0, 0, 0))
    o, state = pl.pallas_call(
        functools.partial(_hgrn_body, n_chunks=n_chunks),
        grid=(b // HGRN_SEQS, n_chunks),
        in_specs=[spec, spec, spec, spec, _const_spec(range_mat.shape), _const_spec(level.shape),
                  _const_spec((1, WIDTH))],
        out_specs=[spec, state_spec],
        out_shape=[jax.ShapeDtypeStruct((b, t, WIDTH), BF16),
                   jax.ShapeDtypeStruct((1, b, HEADS, HEAD_DIM, HEAD_DIM), F32)],
        scratch_shapes=[pltpu.VMEM((HGRN_SEQS, HEADS, HEAD_DIM, HEAD_DIM), F32)],
        compiler_params=_params("parallel", "arbitrary"),
        name="hgrn_prompt",
    )(as3d(hq), as3d(hk), as3d(lf), as3d(hv), range_mat, level, gain)
    return o.reshape(b * t, WIDTH), state


def _memkv_body(m_ref, g_ref, wk_ref, wv_ref, pk_ref, pv_ref, kb_ref, vb_ref):
    h = _rms(m_ref[...], g_ref[...]).astype(BF16)
    k = _dot(h, wk_ref[...])
    v = _dot(h, wv_ref[...])
    _store_heads(pk_ref, k)
    _store_heads(pv_ref, v)
    kb_ref[...] = k.astype(BF16)
    vb_ref[...] = v.astype(BF16)


def _mem_kv(mem, g_mem, w_mk, w_mv):
    b = mem.shape[0]
    head_spec = pl.BlockSpec((None, None, MEM_LEN, HEADS, HEAD_DIM), lambda bi: (0, bi, 0, 0, 0))
    flat_spec = pl.BlockSpec((None, MEM_LEN, WIDTH), lambda bi: (bi, 0, 0))
    head_shape = jax.ShapeDtypeStruct((1, b, MEM_LEN, HEADS, HEAD_DIM), F32)
    flat_shape = jax.ShapeDtypeStruct((b, MEM_LEN, WIDTH), BF16)
    return pl.pallas_call(
        _memkv_body,
        grid=(b,),
        in_specs=[pl.BlockSpec((None, MEM_LEN, D_MODEL), lambda bi: (bi, 0, 0)),
                  _const_spec((1, D_MODEL)), _const_spec((D_MODEL, WIDTH)), _const_spec((D_MODEL, WIDTH))],
        out_specs=[head_spec, head_spec, flat_spec, flat_spec],
        out_shape=[head_shape, head_shape, flat_shape, flat_shape],
        compiler_params=_params("parallel"),
        name="mem_kv",
    )(mem, g_mem, w_mk, w_mv)


def _from_strided_view(src_ref, stage_ref, dil, width):
    if dil == 1:
        return src_ref[...].astype(F32)
    n = src_ref.shape[0]
    n_col = width // 128
    for r in range(dil):
        for c in range(n_col):
            stage_ref[c, pl.ds(r, n, stride=dil), :] = (
                src_ref[:, r * width + c * 128:r * width + (c + 1) * 128].astype(F32))
    return jnp.concatenate([stage_ref[c] for c in range(n_col)], axis=-1)


def _gated_merge(x, att, hg, mem, gate_ref, wa_ref, wb_ref, wc_ref, wo_ref):
    def gate(j):
        return gate_ref[:, j * D_MODEL:(j + 1) * D_MODEL].astype(F32)

    m = gate(0) * _dot(att, wa_ref[...]) + gate(1) * _dot(hg, wb_ref[...]) + gate(2) * _dot(mem, wc_ref[...])
    return x + _dot(m.astype(BF16), wo_ref[...])


def _mix_body(x_ref, o0_ref, o1_ref, o2_ref, s0_ref, s1_ref, s2_ref, mq_ref, mk_ref, mv_ref, hg_ref, gate_ref,
              wa_ref, wb_ref, wc_ref, wo_ref, out_ref, *stage_refs):
    def gate(j):
        return gate_ref[:, j * D_MODEL:(j + 1) * D_MODEL].astype(F32)

    heads = [slice(h * HEAD_DIM, (h + 1) * HEAD_DIM) for h in range(HEADS)]
    merged = gate(1) * _dot(hg_ref[...], wb_ref[...])
    scores = [_dot_nt(mq_ref[:, sl], mk_ref[:, sl]) for sl in heads]
    probs = []
    for s in scores:
        p = jnp.exp(s - jnp.max(s, axis=-1, keepdims=True))
        probs.append((p.astype(BF16), 1.0 / jnp.sum(p, axis=-1, keepdims=True)))
    mem = [(_dot(p, mv_ref[:, sl]) * inv).astype(BF16) for (p, inv), sl in zip(probs, heads)]
    merged = merged + gate(2) * _dot(jnp.concatenate(mem, axis=-1), wc_ref[...])

    accs = [_from_strided_view(r, stage_refs[2 * g], ATT_GROUPS[g][1], WIDTH)
            for g, r in enumerate((o0_ref, o1_ref, o2_ref))]
    stats = [_from_strided_view(r, stage_refs[2 * g + 1], ATT_GROUPS[g][1], HEADS * LSE_REP)
             for g, r in enumerate((s0_ref, s1_ref, s2_ref))]
    att = []
    for h, sl in enumerate(heads):
        ms = [x[:, h * LSE_REP:h * LSE_REP + 1] for x in stats]
        ls = [x[:, h * LSE_REP + STAT_REP:h * LSE_REP + STAT_REP + 1] for x in stats]
        m = jnp.maximum(jnp.maximum(ms[0], ms[1]), ms[2])
        e = [jnp.exp(x - m) for x in ms]
        inv = 1.0 / (e[0] * ls[0] + e[1] * ls[1] + e[2] * ls[2])
        att.append(sum((e[g] * inv) * accs[g][:, sl] for g in range(N_GROUPS)).astype(BF16))
    merged = merged + gate(0) * _dot(jnp.concatenate(att, axis=-1), wa_ref[...])
    out_ref[...] = x_ref[...] + _dot(merged.astype(BF16), wo_ref[...])


def _mix(x, accs, stats, mq, mk, mv, hg, gates, wa, wb, wc, wo, b, t, *, tm):
    n_tiles = t // tm
    lw = HEADS * LSE_REP
    rows = lambda w: pl.BlockSpec((tm, w), lambda bi, i: (bi * n_tiles + i, 0))
    o_specs = [pl.BlockSpec((None, tm // d, d * WIDTH), lambda bi, i: (bi, i, 0)) for _, d in ATT_GROUPS]
    s_specs = [pl.BlockSpec((None, tm // d, d * lw), lambda bi, i: (bi, i, 0)) for _, d in ATT_GROUPS]
    memspec = pl.BlockSpec((None, MEM_LEN, WIDTH), lambda bi, i: (bi, 0, 0))
    stages = []
    for _ in ATT_GROUPS:
        stages += [pltpu.VMEM((WIDTH // 128, tm, 128), F32), pltpu.VMEM((lw // 128, tm, 128), F32)]
    return pl.pallas_call(
        _mix_body,
        grid=(b, n_tiles),
        in_specs=[rows(D_MODEL)] + o_specs + s_specs + [rows(WIDTH), memspec, memspec, rows(WIDTH),
                                                         rows(N_BRANCH * D_MODEL),
                                                         _const_spec((WIDTH, D_MODEL)), _const_spec((WIDTH, D_MODEL)),
                                                         _const_spec((WIDTH, D_MODEL)), _const_spec((D_MODEL, D_MODEL))],
        out_specs=rows(D_MODEL),
        out_shape=jax.ShapeDtypeStruct((b * t, D_MODEL), F32),
        scratch_shapes=stages,
        compiler_params=_params("parallel", "parallel"),
        name="mix",
    )(x, *accs, *stats, mq, mk, mv, hg, gates, wa, wb, wc, wo)


def _merge_body(x_ref, att_ref, hg_ref, mem_ref, gate_ref, wa_ref, wb_ref, wc_ref, wo_ref, o_ref):
    o_ref[...] = _gated_merge(x_ref[...], att_ref[...].astype(BF16), hg_ref[...].astype(BF16),
                              mem_ref[...].astype(BF16), gate_ref, wa_ref, wb_ref, wc_ref, wo_ref)


def _merge(x, att, hg, mem, gates, wa, wb, wc, wo, *, tm):
    m = x.shape[0]
    row = pl.BlockSpec((tm, WIDTH), lambda i: (i, 0))
    wide = pl.BlockSpec((tm, D_MODEL), lambda i: (i, 0))
    return pl.pallas_call(
        _merge_body,
        grid=(m // tm,),
        in_specs=[wide, row, row, row, pl.BlockSpec((tm, N_BRANCH * D_MODEL), lambda i: (i, 0)),
                  _const_spec((WIDTH, D_MODEL)), _const_spec((WIDTH, D_MODEL)), _const_spec((WIDTH, D_MODEL)),
                  _const_spec((D_MODEL, D_MODEL))],
        out_specs=wide,
        out_shape=jax.ShapeDtypeStruct((m, D_MODEL), F32),
        compiler_params=_params("parallel"),
        name="merge",
    )(x, att, hg, mem, gates, wa, wb, wc, wo)


def _norm_matmul_body(x_ref, g_ref, w_ref, o_ref):
    o_ref[...] = _dot(_rms(x_ref[...], g_ref[...]).astype(BF16), w_ref[...])


def _sample_in_proj(x, g_mix, w_in):
    m = x.shape[0]
    n = w_in.shape[1]
    return pl.pallas_call(
        _norm_matmul_body,
        grid=(n // WIDTH,),
        in_specs=[_const_spec((m, D_MODEL)), _const_spec((1, D_MODEL)),
                  pl.BlockSpec((D_MODEL, WIDTH), lambda j: (0, j))],
        out_specs=pl.BlockSpec((m, WIDTH), lambda j: (0, j)),
        out_shape=jax.ShapeDtypeStruct((m, n), F32),
        compiler_params=_params("parallel"),
        name="sample_in_proj",
    )(x, g_mix, w_in)


def _one_query_attend(q, k_rows, v_rows, k_new=None, v_new=None):
    s = jnp.sum(k_rows * q, axis=-1, keepdims=True)
    m = jnp.max(s, axis=0, keepdims=True)
    if k_new is not None:
        s_new = jnp.sum(k_new * q, axis=-1, keepdims=True)
        m = jnp.maximum(m, s_new)
    p = jnp.exp(s - m)
    l = jnp.sum(p, axis=0, keepdims=True)
    acc = jnp.sum(p * v_rows, axis=0, keepdims=True)
    if k_new is not None:
        p_new = jnp.exp(s_new - m)
        l = l + p_new
        acc = acc + p_new * v_new
    return acc / l, m + jnp.log(l)


def _sample_att_body(q_ref, kn_ref, vn_ref, mq_ref, k1_ref, v1_ref, k4_ref, v4_ref, k16_ref, v16_ref,
                     mk_ref, mv_ref, att_ref, mem_ref):
    k_refs = (k1_ref, k4_ref, k16_ref)
    v_refs = (v1_ref, v4_ref, v16_ref)
    for h in range(HEADS):
        sl = slice(h * HEAD_DIM, (h + 1) * HEAD_DIM)
        outs, lses = [], []
        for g in range(N_GROUPS):
            gs = slice(g * WIDTH + h * HEAD_DIM, g * WIDTH + (h + 1) * HEAD_DIM)
            o, lse = _one_query_attend(q_ref[:, gs] * QK_SCALE, _load_head(k_refs[g], h), _load_head(v_refs[g], h),
                                       kn_ref[:, gs], vn_ref[:, gs])
            outs.append(o)
            lses.append(lse)
        m = jnp.maximum(jnp.maximum(lses[0], lses[1]), lses[2])
        e = [jnp.exp(x - m) for x in lses]
        inv = 1.0 / (e[0] + e[1] + e[2])
        att_ref[:, sl] = sum((e[g] * inv) * outs[g] for g in range(N_GROUPS))
        o, _ = _one_query_attend(mq_ref[:, sl] * QK_SCALE, _load_head(mk_ref, h), _load_head(mv_ref, h))
        mem_ref[:, sl] = o


def _sample_attend(z3, caches_k, caches_v, cache_mk, cache_mv):
    bd = z3.shape[0]
    zspec = lambda col, n: pl.BlockSpec((None, 1, n * WIDTH), lambda b: (b, 0, col // n))
    cache_specs, cache_args = [], []
    for (win, dil), ck, cv in zip(ATT_GROUPS, caches_k, caches_v):
        rows = ck.shape[2]
        assert rows == win and rows // dil == BLOCK
        spec = pl.BlockSpec((None, None, BLOCK, None, HEADS, HEAD_DIM), lambda b: (0, b, 0, 0, 0, 0))
        for c in (ck, cv):
            cache_specs.append(spec)
            cache_args.append(c.reshape(1, bd, BLOCK, dil, HEADS, HEAD_DIM))
    mem_spec = pl.BlockSpec((None, None, MEM_LEN, HEADS, HEAD_DIM), lambda b: (0, b, 0, 0, 0))
    out_spec = pl.BlockSpec((None, 1, WIDTH), lambda b: (b, 0, 0))
    out_shape = jax.ShapeDtypeStruct((bd, 1, WIDTH), F32)
    return pl.pallas_call(
        _sample_att_body,
        grid=(bd,),
        in_specs=[zspec(COL_Q, 3), zspec(COL_K, 3), zspec(COL_V, 3), zspec(COL_MQ, 1)] + cache_specs
                 + [mem_spec, mem_spec],
        out_specs=[out_spec, out_spec],
        out_shape=[out_shape, out_shape],
        compiler_params=_params("parallel"),
        name="sample_attend",
    )(z3, z3, z3, z3, *cache_args, cache_mk, cache_mv)


SAMPLE_GROUP = 8


def _sample_hgrn_body(zq_ref, zf_ref, zi_ref, lb_ref, gain_ref, s_ref, o_ref, so_ref):
    lb = _lower_bound(lb_ref)
    gain = gain_ref[...]
    q = jax.nn.sigmoid(zq_ref[...])
    kk = (1.0 - lb) * jax.nn.sigmoid(zf_ref[...])
    f = lb + kk
    v = zi_ref[...]
    pad = jnp.zeros((HEAD_DIM - SAMPLE_GROUP, HEAD_DIM), F32)

    def columns(a, h):
        return jnp.concatenate([a[:, h * HEAD_DIM:(h + 1) * HEAD_DIM], pad], axis=0).T

    for h in range(HEADS):
        sl = slice(h * HEAD_DIM, (h + 1) * HEAD_DIM)
        q_t, k_t, f_t = columns(q, h), columns(kk, h), columns(f, h)
        for j in range(SAMPLE_GROUP):
            s1 = f_t[:, j:j + 1] * s_ref[j, h] + k_t[:, j:j + 1] * v[j:j + 1, sl]
            so_ref[j, h] = s1
            o = jnp.sum(q_t[:, j:j + 1] * s1, axis=0, keepdims=True)
            o_ref[j:j + 1, sl] = _rms(o, gain[:, sl])


def _sample_hgrn(z, lb_logits, gain, state):
    bd = z.shape[0]
    zspec = lambda col: pl.BlockSpec((SAMPLE_GROUP, WIDTH), lambda i: (i, col))
    sspec = pl.BlockSpec((None, SAMPLE_GROUP, HEADS, HEAD_DIM, HEAD_DIM), lambda i: (0, i, 0, 0, 0))
    return pl.pallas_call(
        _sample_hgrn_body,
        grid=(bd // SAMPLE_GROUP,),
        in_specs=[zspec(COL_HQ), zspec(COL_HF), zspec(COL_HI), _const_spec(lb_logits.shape),
                  _const_spec((1, WIDTH)), sspec],
        out_specs=[pl.BlockSpec((SAMPLE_GROUP, WIDTH), lambda i: (i, 0)), sspec],
        out_shape=[jax.ShapeDtypeStruct((bd, WIDTH), F32), jax.ShapeDtypeStruct(state.shape, F32)],
        compiler_params=_params("parallel"),
        name="sample_hgrn",
    )(z, z, z, lb_logits, gain, state)


def _gate_body(z_ref, o_ref):
    o_ref[...] = jax.nn.sigmoid(z_ref[...]).astype(BF16)


def _sample_gates(z):
    bd = z.shape[0]
    n = N_BRANCH * D_MODEL
    return pl.pallas_call(
        _gate_body,
        grid=(n // WIDTH,),
        in_specs=[pl.BlockSpec((bd, WIDTH), lambda j: (0, COL_GATE + j))],
        out_specs=pl.BlockSpec((bd, WIDTH), lambda j: (0, j)),
        out_shape=jax.ShapeDtypeStruct((bd, n), BF16),
        name="sample_gates",
    )(z)


def kernel(x_prompt, x_sample, mem_prompt, cache_win1_k, cache_win1_v, cache_win4_k, cache_win4_v, cache_win16_k, cache_win16_v, cache_mem_k, cache_mem_v, state_hgrn, g_ff1, w_ff1_gate, w_ff1_up, w_ff1_down, g_mix, w_in, hg_lb_logits, g_hg_out, g_mem, w_mem_k, w_mem_v, w_branch_att, w_branch_hg, w_branch_mem, w_out, g_ff2, w_ff2_gate, w_ff2_up, w_ff2_down, g_final):
    b, t, _ = x_prompt.shape
    bd = x_sample.shape[0]
    assert g_ff1.shape[0] == 1 and x_sample.shape[1] == 1 and w_in.shape[2] == IN_WIDTH
    bf = lambda w: w[0].astype(BF16)
    ffn1 = (g_ff1, bf(w_ff1_gate), bf(w_ff1_up), bf(w_ff1_down), g_final.reshape(1, D_MODEL))
    ffn2 = (g_ff2, bf(w_ff2_gate), bf(w_ff2_up), bf(w_ff2_down), g_final.reshape(1, D_MODEL))
    w_in_b = bf(w_in)
    w_qkv, w_rest = w_in_b[:, :COL_HQ * WIDTH], w_in_b[:, COL_HQ * WIDTH:]
    branch_w = (bf(w_branch_att), bf(w_branch_hg), bf(w_branch_mem), bf(w_out))

    xs, _ = _ffn(x_sample.reshape(bd, D_MODEL), *ffn1, tm=bd, final=False)
    zs = _sample_in_proj(xs, g_mix, w_in_b)
    caches_k = (cache_win1_k, cache_win4_k, cache_win16_k)
    caches_v = (cache_win1_v, cache_win4_v, cache_win16_v)
    z3 = zs.reshape(bd, 1, IN_WIDTH)
    att_s, mem_s = _sample_attend(z3, caches_k, caches_v, cache_mem_k, cache_mem_v)
    hg_s, s_hgrn = _sample_hgrn(zs, hg_lb_logits, g_hg_out, state_hgrn)
    xs = _merge(xs, att_s.reshape(bd, WIDTH), hg_s, mem_s.reshape(bd, WIDTH), _sample_gates(zs), *branch_w, tm=bd)
    y_sample = _ffn(xs, *ffn2, tm=bd, final=True)[0].reshape(bd, 1, D_MODEL)

    tm = b * t // bd
    xp, (s_win16_k,) = _ffn(x_prompt.reshape(b * t, D_MODEL), *ffn1, tm=tm, final=False, ride=(cache_win16_k,))
    qkv = _qkv_proj(xp.reshape(b, t, D_MODEL), g_mix, w_qkv, tm=512)
    q_g, k_g, v_g, p_k, p_v = qkv[0:3], qkv[3:6], qkv[6:9], qkv[9:12], qkv[12:15]
    (hq, hk, lf, hv, mq, gates), (s_win1_k, s_win1_v, s_win4_k, s_win4_v) = _rest_proj(
        xp, g_mix, w_rest, hg_lb_logits, tm=tm, ride=(cache_win1_k, cache_win1_v, cache_win4_k, cache_win4_v))
    outs, lses = _attention(q_g, k_g, v_g)
    hg_p, p_hgrn = _hgrn_prompt(hq, hk, lf, hv, g_hg_out, b, t)
    p_mem_k, p_mem_v, mk_b, mv_b = _mem_kv(mem_prompt, g_mem, bf(w_mem_k), bf(w_mem_v))
    xp = _mix(xp, outs, lses, mq, mk_b, mv_b, hg_p, gates, *branch_w, b, t, tm=512)
    y_prompt, (s_win16_v,) = _ffn(xp, *ffn2, tm=tm, final=True, ride=(cache_win16_v,))
    s_win1_k, s_win1_v, s_win4_k, s_win4_v, s_win16_k, s_win16_v = _write_new_rows(
        z3, [s_win1_k, s_win1_v, s_win4_k, s_win4_v, s_win16_k, s_win16_v],
        ((0, False), (0, True), (1, False), (1, True), (2, False), (2, True)))

    return (y_prompt.reshape(b, t, D_MODEL), y_sample,
            p_k[0], p_v[0], p_k[1], p_v[1], p_k[2], p_v[2],
            p_mem_k, p_mem_v, p_hgrn,
            s_win1_k, s_win1_v, s_win4_k, s_win4_v, s_win16_k, s_win16_v,
            s_hgrn)
```

```python
import functools

import numpy as np
import jax
import jax.numpy as jnp
from jax import lax
from jax.experimental import pallas as pl
from jax.experimental.pallas import tpu as pltpu

F32 = jnp.float32
BF16 = jnp.bfloat16

D_MODEL = 1024
D_FF = 2816
EPS = 1e-6
HEADS = 4
HEAD_DIM = 128
WIDTH = HEADS * HEAD_DIM
ATT_GROUPS = ((128, 1), (512, 4), (2048, 16))
N_GROUPS = 3
BLOCK = 128
MEM_LEN = 256
N_BRANCH = 3
LSE_REP = 32
QK_SCALE = HEAD_DIM ** -0.5
VMEM_LIMIT_BYTES = 56 * 1024 * 1024

COL_Q, COL_K, COL_V, COL_HQ, COL_HF, COL_HI, COL_MQ, COL_GATE = 0, 3, 6, 9, 10, 11, 12, 13
IN_WIDTH = (COL_GATE + N_BRANCH * D_MODEL // WIDTH) * WIDTH


def _params(*sem):
    return pltpu.CompilerParams(dimension_semantics=sem, vmem_limit_bytes=VMEM_LIMIT_BYTES)


def _const_spec(shape):
    nd = len(shape)
    return pl.BlockSpec(shape, lambda *_: (0,) * nd, pipeline_mode=pl.Buffered(1))


def _rms(x, g):
    return x * lax.rsqrt(jnp.mean(x * x, axis=-1, keepdims=True) + EPS) * g


def _dot(a, b):
    return jnp.dot(a, b, preferred_element_type=F32)


def _dot_nt(a, b):
    return lax.dot_general(a, b, (((1,), (1,)), ((), ())), preferred_element_type=F32)


def _ride_specs(caches, n_steps):
    if not caches:
        return [], [], [], []
    assert all(c.shape[1] == n_steps for c in caches)
    in_specs = [pl.BlockSpec((1, 1, c.shape[2], HEADS, HEAD_DIM), lambda i: (0, i, 0, 0, 0)) for c in caches]
    out_specs = [pl.BlockSpec(memory_space=pl.ANY)] * len(caches)
    shapes = [jax.ShapeDtypeStruct(c.shape, c.dtype) for c in caches]
    return in_specs, out_specs, shapes, [pltpu.SemaphoreType.DMA((2 * len(caches),))]


def _ride_copies(in_refs, out_refs, sem):
    seq = pl.program_id(0)
    copies = []
    for j, (cache_ref, out_ref) in enumerate(zip(in_refs, out_refs)):
        rows = cache_ref.shape[2]
        copies.append(pltpu.make_async_copy(cache_ref.at[:, :, pl.ds(1, rows - 1)],
                                            out_ref.at[:, pl.ds(seq, 1), pl.ds(0, rows - 1)], sem.at[2 * j]))
        copies.append(pltpu.make_async_copy(cache_ref.at[:, :, pl.ds(rows - 1, 1)],
                                            out_ref.at[:, pl.ds(seq, 1), pl.ds(rows - 1, 1)], sem.at[2 * j + 1]))
    for cp in copies:
        cp.start()
    return copies


def _new_rows_body(kn_ref, vn_ref, *refs, meta):
    n = len(meta)
    out_refs, row_refs, sem = refs[n:2 * n], refs[2 * n:3 * n], refs[3 * n]
    copies = []
    for j, ((g, is_v), out_ref, row_ref) in enumerate(zip(meta, out_refs, row_refs)):
        bd, rows = out_ref.shape[1], out_ref.shape[2]
        new = (vn_ref if is_v else kn_ref)[:, 0, g * WIDTH:(g + 1) * WIDTH]
        flat = row_ref.reshape(bd * HEADS, HEAD_DIM)
        for h in range(HEADS):
            flat[pl.ds(h, bd, stride=HEADS), :] = new[:, h * HEAD_DIM:(h + 1) * HEAD_DIM]
        copies.append(pltpu.make_async_copy(row_ref, out_ref.at[:, :, pl.ds(rows - 1, 1)], sem.at[j]))
        copies[-1].start()
    for cp in copies:
        cp.wait()


def _write_new_rows(z3, shifted, meta):
    bd = z3.shape[0]
    n = len(shifted)
    zspec = lambda col: pl.BlockSpec((bd, 1, N_GROUPS * WIDTH), lambda i: (0, 0, col // N_GROUPS))
    any_spec = pl.BlockSpec(memory_space=pl.ANY)
    return pl.pallas_call(
        functools.partial(_new_rows_body, meta=meta),
        grid=(1,),
        in_specs=[zspec(COL_K), zspec(COL_V)] + [any_spec] * n,
        out_specs=[any_spec] * n,
        out_shape=[jax.ShapeDtypeStruct(c.shape, c.dtype) for c in shifted],
        input_output_aliases={2 + j: j for j in range(n)},
        scratch_shapes=[pltpu.VMEM((1, bd, 1, HEADS, HEAD_DIM), F32)] * n + [pltpu.SemaphoreType.DMA((n,))],
        name="write_new_rows",
    )(z3, z3, *shifted)


FF_CHUNKS = 1


def _ffn_body(x_ref, g_ref, wg_ref, wu_ref, wd_ref, gf_ref, *rest, final, n_ride):
    o_ref = rest[n_ride]
    copies = _ride_copies(rest[:n_ride], rest[n_ride + 1:2 * n_ride + 1], rest[-1]) if n_ride else []
    x = x_ref[...]
    h = _rms(x, g_ref[...]).astype(BF16)
    y = x
    width = D_FF // FF_CHUNKS
    for c in range(FF_CHUNKS):
        cols = slice(c * width, (c + 1) * width)
        a = _dot(h, wg_ref[:, cols])
        u = _dot(h, wu_ref[:, cols])
        act = (a * jax.nn.sigmoid(a) * u).astype(BF16)
        y = y + 0.5 * _dot(act, wd_ref[cols, :])
    if final:
        y = _rms(y, gf_ref[...])
    o_ref[...] = y
    for cp in copies:
        cp.wait()


def _ffn(x, g, wg, wu, wd, g_final, *, tm, final, ride=()):
    m = x.shape[0]
    r_in, r_out, r_shapes, r_scratch = _ride_specs(ride, m // tm)
    res = pl.pallas_call(
        functools.partial(_ffn_body, final=final, n_ride=len(ride)),
        grid=(m // tm,),
        in_specs=[pl.BlockSpec((tm, D_MODEL), lambda i: (i, 0)),
                  _const_spec((1, D_MODEL)),
                  _const_spec((D_MODEL, D_FF)), _const_spec((D_MODEL, D_FF)), _const_spec((D_FF, D_MODEL)),
                  _const_spec((1, D_MODEL))] + r_in,
        out_specs=[pl.BlockSpec((tm, D_MODEL), lambda i: (i, 0))] + r_out,
        out_shape=[jax.ShapeDtypeStruct((m, D_MODEL), F32)] + r_shapes,
        scratch_shapes=r_scratch,
        compiler_params=_params("parallel"),
        name="ffn_final" if final else "ffn",
    )(x, g, wg, wu, wd, g_final, *ride)
    return res[0], res[1:]


def _store_heads(ref, z):
    rows = ref.shape[0]
    flat = ref.reshape(rows * HEADS, HEAD_DIM)
    for h in range(HEADS):
        flat[pl.ds(h, rows, stride=HEADS), :] = z[:, h * HEAD_DIM:(h + 1) * HEAD_DIM]


def _load_head(ref, h):
    rows = ref.shape[0]
    return ref.reshape(rows * HEADS, HEAD_DIM)[pl.ds(h, rows, stride=HEADS), :]


def _to_strided_view(dst_ref, z, stage_ref, dil):
    if dil == 1:
        dst_ref[...] = z.astype(dst_ref.dtype)
        return
    n = z.shape[0] // dil
    for c in range(stage_ref.shape[0]):
        stage_ref[c] = z[:, c * 128:(c + 1) * 128]
    for r in range(dil):
        for c in range(stage_ref.shape[0]):
            dst_ref[:, r * WIDTH + c * 128:r * WIDTH + (c + 1) * 128] = (
                stage_ref[c, pl.ds(r, n, stride=dil), :].astype(dst_ref.dtype))


def _qkv_body(x_ref, g_ref, w_ref, *refs, tm, n_tiles):
    q_refs, k_refs, v_refs = refs[0:3], refs[3:6], refs[6:9]
    pk_refs, pv_refs, stage_refs = refs[9:12], refs[12:15], refs[15:]
    h = _rms(x_ref[...], g_ref[...]).astype(BF16)
    late = []
    for g, (win, dil) in enumerate(ATT_GROUPS):
        keep = min(win, tm * n_tiles)
        for kind, (col, dst_refs, win_refs) in enumerate(((COL_Q, q_refs, None), (COL_K, k_refs, pk_refs),
                                                          (COL_V, v_refs, pv_refs))):
            z = _dot(h, w_ref[:, (col + g) * WIDTH:(col + g + 1) * WIDTH])
            if win_refs is None:
                z = z * QK_SCALE
            stage = stage_refs[3 * (g - 1) + kind] if dil > 1 else None
            _to_strided_view(dst_refs[g], z, stage, dil)
            if win_refs is None:
                continue
            if keep == tm * n_tiles:
                _store_heads(win_refs[g], z)
            elif dil == 1:
                late.append((win_refs[g], z[tm - keep:, :], None))
            else:
                assert keep == tm
                late.append((win_refs[g], None, stage))

    @pl.when(pl.program_id(1) == n_tiles - 1)
    def _():
        for ref, rows, stage in late:
            if stage is None:
                _store_heads(ref, rows)
            else:
                _store_heads(ref, jnp.concatenate([stage[hd] for hd in range(HEADS)], axis=-1))


def _qkv_proj(x, g_mix, w_qkv, *, tm):
    b, t, _ = x.shape
    n_tiles = t // tm
    view_specs = [pl.BlockSpec((None, tm // d, d * WIDTH), lambda bi, i: (bi, i, 0)) for _, d in ATT_GROUPS]
    view_shapes = [jax.ShapeDtypeStruct((b, t // d, d * WIDTH), BF16) for _, d in ATT_GROUPS]
    win_specs, win_shapes = [], []
    for win, _ in ATT_GROUPS:
        keep = min(win, t)
        assert keep == t or keep <= tm
        if keep == t:
            win_specs.append(pl.BlockSpec((None, None, tm, HEADS, HEAD_DIM), lambda bi, i: (0, bi, i, 0, 0)))
        else:
            win_specs.append(pl.BlockSpec((None, None, keep, HEADS, HEAD_DIM), lambda bi, i: (0, bi, 0, 0, 0)))
        win_shapes.append(jax.ShapeDtypeStruct((1, b, keep, HEADS, HEAD_DIM), F32))
    return pl.pallas_call(
        functools.partial(_qkv_body, tm=tm, n_tiles=n_tiles),
        grid=(b, n_tiles),
        in_specs=[pl.BlockSpec((None, tm, D_MODEL), lambda bi, i: (bi, i, 0)),
                  _const_spec((1, D_MODEL)),
                  _const_spec((D_MODEL, 3 * N_GROUPS * WIDTH))],
        out_specs=view_specs * 3 + win_specs + win_specs,
        out_shape=view_shapes * 3 + win_shapes + win_shapes,
        scratch_shapes=[pltpu.VMEM((WIDTH // 128, tm, 128), F32)] * (3 * sum(d > 1 for _, d in ATT_GROUPS)),
        compiler_params=_params("parallel", "arbitrary"),
        name="qkv_proj",
    )(x, g_mix, w_qkv)


def _lower_bound(logits_ref):
    lg = logits_ref[...]
    e = jnp.exp(lg - jnp.max(lg, axis=0, keepdims=True))
    return e[0:1, :] / jnp.sum(e, axis=0, keepdims=True)


def _rest_steps(x_ref, g_ref, w_ref, lb_ref, hq_ref, hk_ref, lf_ref, hv_ref, mq_ref, gate_ref):
    h = _rms(x_ref[...], g_ref[...]).astype(BF16)
    lb = _lower_bound(lb_ref)
    proj = lambda c: _dot(h, w_ref[:, c * WIDTH:(c + 1) * WIDTH])

    def hgrn_query():
        hq_ref[...] = jax.nn.sigmoid(proj(0)).astype(BF16)

    def hgrn_key():
        kk = (1.0 - lb) * jax.nn.sigmoid(proj(1))
        hk_ref[...] = kk.astype(BF16)
        lf_ref[...] = jnp.log(lb + kk)

    def hgrn_value():
        hv_ref[...] = proj(2).astype(BF16)

    def mem_query():
        mq_ref[...] = (proj(3) * QK_SCALE).astype(BF16)

    def gate(c):
        gate_ref[:, c * WIDTH:(c + 1) * WIDTH] = jax.nn.sigmoid(proj(4 + c)).astype(BF16)

    return [hgrn_query, hgrn_key, hgrn_value, mem_query] + [functools.partial(gate, c)
                                                            for c in range(N_BRANCH * D_MODEL // WIDTH)]


def _rest_att_body(x_ref, g_ref, w_ref, lb_ref, *rest, n_ride, tiles, tiles_per_seq):
    n_att_in = sum(5 if has_halo else 3 for _, _, has_halo in tiles)
    caches, rest = rest[:n_ride], rest[n_ride:]
    att_in, rest = rest[:n_att_in], rest[n_att_in:]
    proj_out, rest = rest[:6], rest[6:]
    ride_out, rest = rest[:n_ride], rest[n_ride:]
    att_out, scratch = rest[:2 * len(tiles)], rest[2 * len(tiles):]
    copies = _ride_copies(caches, ride_out, scratch[-1]) if n_ride else []
    steps = _rest_steps(x_ref, g_ref, w_ref, lb_ref, *proj_out)
    gens = _att_tiles(att_in + att_out, tiles, pl.program_id(0) % tiles_per_seq)
    steps.pop(0)()
    for gen in gens:
        next(gen)
        steps.pop(0)()
        next(gen)
        steps.pop(0)()
        next(gen, None)
    for step in steps:
        step()
    for cp in copies:
        cp.wait()


def _rest_body(x_ref, g_ref, w_ref, lb_ref, *rest, n_ride):
    copies = _ride_copies(rest[:n_ride], rest[n_ride + 6:2 * n_ride + 6], rest[-1]) if n_ride else []
    for step in _rest_steps(x_ref, g_ref, w_ref, lb_ref, *rest[n_ride:n_ride + 6]):
        step()
    for cp in copies:
        cp.wait()


def _rest_att(x, g_mix, w_rest, lb_logits, q_g, k_g, v_g, *, tm, ride=()):
    m = x.shape[0]
    spec = pl.BlockSpec((tm, WIDTH), lambda i: (i, 0))
    bf = jax.ShapeDtypeStruct((m, WIDTH), BF16)
    r_in, r_out, r_shapes, r_scratch = _ride_specs(ride, m // tm)
    tiles_per_seq = q_g[0].shape[1] // (ATT_UNITS * BLOCK)
    a_in, a_args, a_out, a_shapes, tiles, n_tiles = _attention_specs(
        q_g, k_g, v_g, lambda i: (i // tiles_per_seq, i % tiles_per_seq))
    assert n_tiles == tiles_per_seq and q_g[0].shape[0] * n_tiles == m // tm
    res = pl.pallas_call(
        functools.partial(_rest_att_body, n_ride=len(ride), tiles=tiles, tiles_per_seq=tiles_per_seq),
        grid=(m // tm,),
        in_specs=[pl.BlockSpec((tm, D_MODEL), lambda i: (i, 0)),
                  _const_spec((1, D_MODEL)),
                  _const_spec(w_rest.shape),
                  _const_spec(lb_logits.shape)] + r_in + a_in,
        out_specs=[spec, spec, spec, spec, spec, pl.BlockSpec((tm, N_BRANCH * D_MODEL), lambda i: (i, 0))]
                  + r_out + a_out,
        out_shape=[bf, bf, jax.ShapeDtypeStruct((m, WIDTH), F32), bf, bf,
                   jax.ShapeDtypeStruct((m, N_BRANCH * D_MODEL), BF16)] + r_shapes + a_shapes,
        scratch_shapes=r_scratch,
        compiler_params=_params("arbitrary"),
        name="rest_att",
    )(x, g_mix, w_rest, lb_logits, *ride, *a_args)
    n = 6 + len(ride)
    return res[:6], res[6:n], res[n::2], res[n + 1::2]


ATT_UNITS = 4
STAT_REP = LSE_REP // 2


def _att_body(*refs, tiles):
    for tile in _att_tiles(refs, tiles, pl.program_id(1)):
        for _ in tile:
            pass


def _att_tiles(refs, tiles, tile_index):
    n_in = [5 if has_halo else 3 for _, _, has_halo in tiles]
    outs = refs[sum(n_in):]
    gens, pos = [], 0
    for g, (n_row_blocks, n_col_blocks, has_halo) in enumerate(tiles):
        gens.append(_att_tile(*refs[pos:pos + n_in[g]], outs[2 * g], outs[2 * g + 1], n_row_blocks=n_row_blocks,
                              n_col_blocks=n_col_blocks, has_halo=has_halo, tile_index=tile_index))
        pos += n_in[g]
    return gens


def _att_tile(q_ref, k_ref, v_ref, *rest, n_row_blocks, n_col_blocks, has_halo, tile_index):
    if has_halo:
        kh_ref, vh_ref, o_ref, st_ref = rest
        halo_ok = tile_index > 0
    else:
        o_ref, st_ref = rest
    qi = lax.broadcasted_iota(jnp.int32, (BLOCK, 2 * BLOCK), 0)
    ki = lax.broadcasted_iota(jnp.int32, (BLOCK, 2 * BLOCK), 1)
    own_ok = (lax.broadcasted_iota(jnp.int32, (BLOCK, BLOCK), 1)
              <= lax.broadcasted_iota(jnp.int32, (BLOCK, BLOCK), 0))
    both_ok = jnp.where(ki < BLOCK, ki, qi + BLOCK) >= jnp.where(ki < BLOCK, qi, ki)
    if has_halo:
        first_valid = jnp.where(halo_ok, 0, BLOCK)
        halo_both_ok = jnp.logical_and(both_ok, ki >= first_valid)

    units = []
    for c in range(n_col_blocks):
        for u in range(n_row_blocks):
            for h in range(HEADS):
                rows = slice(u * BLOCK, (u + 1) * BLOCK)
                cols = slice(c * WIDTH + h * HEAD_DIM, c * WIDTH + (h + 1) * HEAD_DIM)
                lcols = (c * HEADS + h) * LSE_REP
                if u > 0:
                    with_prev = slice((u - 1) * BLOCK, (u + 1) * BLOCK)
                    keys = lambda ref, rr=with_prev, cc=cols: ref[rr, cc]
                    ok = both_ok
                elif has_halo:
                    hcols = slice(h * HEAD_DIM, (h + 1) * HEAD_DIM)
                    keys = lambda ref, rr=rows, cc=cols, hc=hcols: jnp.concatenate(
                        [(kh_ref if ref is k_ref else vh_ref)[:, hc], ref[rr, cc]], axis=0)
                    ok = halo_both_ok
                else:
                    keys = lambda ref, rr=rows, cc=cols: ref[rr, cc]
                    ok = own_ok
                units.append((rows, cols, lcols, keys, ok))

    scores = [jnp.where(ok, _dot_nt(q_ref[rows, cols], keys(k_ref)), -jnp.inf) for rows, cols, _, keys, ok in units]
    yield

    probs = []
    for s in scores:
        m = jnp.max(s, axis=-1, keepdims=True)
        p = jnp.exp(s - m)
        probs.append((p.astype(BF16), m, jnp.sum(p, axis=-1, keepdims=True)))
    yield

    for (rows, cols, lcols, keys, _), (p, m, l) in zip(units, probs):
        o_ref[rows, cols] = _dot(p, keys(v_ref)).astype(BF16)
        st_ref[rows, lcols:lcols + STAT_REP] = jnp.broadcast_to(m, (BLOCK, STAT_REP))
        st_ref[rows, lcols + STAT_REP:lcols + LSE_REP] = jnp.broadcast_to(l, (BLOCK, STAT_REP))


def _attention_specs(q_g, k_g, v_g, to_bj):
    b = q_g[0].shape[0]
    lw = HEADS * LSE_REP
    n_steps = None
    in_specs, args, out_specs, out_shapes, tiles = [], [], [], [], []

    def spec(shape, index):
        return pl.BlockSpec(shape, lambda *grid_idx: index(*to_bj(*grid_idx)))

    for (_, dil), q, k, v in zip(ATT_GROUPS, q_g, k_g, v_g):
        sub_len = q.shape[1]
        n_row_blocks = min(ATT_UNITS, sub_len // BLOCK)
        n_col_blocks = ATT_UNITS // n_row_blocks
        span = n_row_blocks * BLOCK
        assert sub_len % span == 0 and dil % n_col_blocks == 0
        has_halo = sub_len > span
        assert not has_halo or (dil == 1 and n_col_blocks == 1)
        n = (sub_len // span) * (dil // n_col_blocks)
        assert n_steps in (None, n)
        n_steps = n
        if has_halo:
            tile = lambda w, span=span: spec((None, span, w), lambda bi, j: (bi, j, 0))
        else:
            tile = lambda w, span=span, nc=n_col_blocks: spec((None, span, nc * w), lambda bi, j: (bi, 0, j))
        halo = spec((None, BLOCK, WIDTH), lambda bi, j, nr=n_row_blocks: (bi, jnp.maximum(j * nr - 1, 0), 0))
        in_specs += [tile(WIDTH)] * 3 + ([halo, halo] if has_halo else [])
        args += [q, k, v] + ([k, v] if has_halo else [])
        out_specs += [tile(WIDTH), tile(lw)]
        out_shapes += [jax.ShapeDtypeStruct((b, sub_len, dil * WIDTH), BF16),
                       jax.ShapeDtypeStruct((b, sub_len, dil * lw), F32)]
        tiles.append((n_row_blocks, n_col_blocks, has_halo))
    return in_specs, args, out_specs, out_shapes, tuple(tiles), n_steps


N_LEVELS = 7
ROW_CUM, ROW_REM = N_LEVELS, N_LEVELS + 1


def _hgrn_constants():
    r = np.arange(BLOCK)[:, None]
    s = np.arange(BLOCK)[None, :]
    mats = []
    for lvl in range(N_LEVELS):
        half = 1 << lvl
        mid = (r // (2 * half)) * 2 * half + half
        upper = (r % (2 * half)) >= half
        mats.append(np.where(upper, (s >= mid) & (s <= r), (s > r) & (s <= mid - 1)))
    mats.append(s <= r)
    mats.append(s > r)
    range_mat = np.concatenate(mats, axis=0).astype(np.float32)
    range_mat = np.concatenate([range_mat, range_mat], axis=1)
    x = r ^ s
    level = np.where(s < r, np.floor(np.log2(np.maximum(x, 1))).astype(np.int32), np.where(s == r, N_LEVELS, -1))
    return jnp.asarray(range_mat, BF16), jnp.asarray(level, jnp.int32)


HGRN_SEQS = 2


def _hgrn_intra(q_b, k_b, lf, rng, level):
    lf_hi = lf.astype(BF16)
    lf_lo = (lf - lf_hi.astype(F32)).astype(BF16)
    z = _dot(rng, jnp.concatenate([lf_hi, lf_lo], axis=0))
    q = q_b.astype(F32)
    kk = k_b.astype(F32)
    lhs, rhs = [], []
    for lvl in range(N_LEVELS):
        x = jnp.exp(z[lvl * BLOCK:(lvl + 1) * BLOCK])
        lhs.append((q * x).astype(BF16))
        rhs.append((kk * x).astype(BF16))
    lhs.append(q_b)
    rhs.append(k_b)
    b_cum = z[ROW_CUM * BLOCK:(ROW_CUM + 1) * BLOCK]
    q_in = (q * jnp.exp(b_cum)).astype(BF16)
    k_out = (kk * jnp.exp(z[ROW_REM * BLOCK:(ROW_REM + 1) * BLOCK])).astype(BF16)
    decay = jnp.exp(b_cum[BLOCK - 1:BLOCK, :])
    mats = []
    for h in range(HEADS):
        sl = slice(h * HEAD_DIM, (h + 1) * HEAD_DIM)
        a = jnp.zeros((BLOCK, BLOCK), F32)
        for lvl in range(N_LEVELS + 1):
            a = jnp.where(level == lvl, _dot_nt(lhs[lvl][:, sl], rhs[lvl][:, sl]), a)
        mats.append(a.astype(BF16))
    return mats, q_in, k_out, decay


def _hgrn_state(mats, q_in, k_out, decay, v_b, gain, st_ref, j):
    outs = []
    for h in range(HEADS):
        sl = slice(h * HEAD_DIM, (h + 1) * HEAD_DIM)
        v = v_b[:, sl]
        st = st_ref[j, h]
        o = _dot(mats[h], v) + _dot_nt(q_in[:, sl], st.astype(BF16))
        outs.append(_rms(o, gain[:, sl]).astype(BF16))
        v_t = v.astype(F32).T.astype(BF16)
        st_ref[j, h] = st * decay[:, sl] + _dot(v_t, k_out[:, sl])
    return jnp.concatenate(outs, axis=-1)


def _hgrn_body(q_ref, k_ref, lf_ref, v_ref, rng_ref, lvl_ref, gain_ref, o_ref, s_ref, st_ref, *, n_chunks):
    c = pl.program_id(1)

    @pl.when(c == 0)
    def _():
        st_ref[...] = jnp.zeros_like(st_ref)

    rng, level, gain = rng_ref[...], lvl_ref[...], gain_ref[...]
    intra = [_hgrn_intra(q_ref[j], k_ref[j], lf_ref[j], rng, level) for j in range(HGRN_SEQS)]
    for j in range(HGRN_SEQS):
        o_ref[j] = _hgrn_state(*intra[j], v_ref[j], gain, st_ref, j)

    @pl.when(c == n_chunks - 1)
    def _():
        for j in range(HGRN_SEQS):
            for h in range(HEADS):
                s_ref[j, h] = st_ref[j, h].T


def _hgrn_prompt(hq, hk, lf, hv, gain, b, t):
    n_chunks = t // BLOCK
    range_mat, level = _hgrn_constants()
    as3d = lambda a: a.reshape(b, t, WIDTH)
    spec = pl.BlockSpec((HGRN_SEQS, BLOCK, WIDTH), lambda bi, c: (bi, c, 0))
    state_spec = pl.BlockSpec((None, HGRN_SEQS, HEADS, HEAD_DIM, HEAD_DIM), lambda bi, c: (0, bi, 0, 0, 0))
    o, state = pl.pallas_call(
        functools.partial(_hgrn_body, n_chunks=n_chunks),
        grid=(b // HGRN_SEQS, n_chunks),
        in_specs=[spec, spec, spec, spec, _const_spec(range_mat.shape), _const_spec(level.shape),
                  _const_spec((1, WIDTH))],
        out_specs=[spec, state_spec],
        out_shape=[jax.ShapeDtypeStruct((b, t, WIDTH), BF16),
                   jax.ShapeDtypeStruct((1, b, HEADS, HEAD_DIM, HEAD_DIM), F32)],
        scratch_shapes=[pltpu.VMEM((HGRN_SEQS, HEADS, HEAD_DIM, HEAD_DIM), F32)],
        compiler_params=_params("parallel", "arbitrary"),
        name="hgrn_prompt",
    )(as3d(hq), as3d(hk), as3d(lf), as3d(hv), range_mat, level, gain)
    return o.reshape(b * t, WIDTH), state


def _memkv_body(m_ref, g_ref, wk_ref, wv_ref, pk_ref, pv_ref, kb_ref, vb_ref):
    h = _rms(m_ref[...], g_ref[...]).astype(BF16)
    k = _dot(h, wk_ref[...])
    v = _dot(h, wv_ref[...])
    _store_heads(pk_ref, k)
    _store_heads(pv_ref, v)
    kb_ref[...] = k.astype(BF16)
    vb_ref[...] = v.astype(BF16)


def _mem_kv(mem, g_mem, w_mk, w_mv):
    b = mem.shape[0]
    head_spec = pl.BlockSpec((None, None, MEM_LEN, HEADS, HEAD_DIM), lambda bi: (0, bi, 0, 0, 0))
    flat_spec = pl.BlockSpec((None, MEM_LEN, WIDTH), lambda bi: (bi, 0, 0))
    head_shape = jax.ShapeDtypeStruct((1, b, MEM_LEN, HEADS, HEAD_DIM), F32)
    flat_shape = jax.ShapeDtypeStruct((b, MEM_LEN, WIDTH), BF16)
    return pl.pallas_call(
        _memkv_body,
        grid=(b,),
        in_specs=[pl.BlockSpec((None, MEM_LEN, D_MODEL), lambda bi: (bi, 0, 0)),
                  _const_spec((1, D_MODEL)), _const_spec((D_MODEL, WIDTH)), _const_spec((D_MODEL, WIDTH))],
        out_specs=[head_spec, head_spec, flat_spec, flat_spec],
        out_shape=[head_shape, head_shape, flat_shape, flat_shape],
        compiler_params=_params("parallel"),
        name="mem_kv",
    )(mem, g_mem, w_mk, w_mv)


def _from_strided_view(src_ref, stage_ref, dil, width):
    if dil == 1:
        return src_ref[...].astype(F32)
    n = src_ref.shape[0]
    n_col = width // 128
    for r in range(dil):
        for c in range(n_col):
            stage_ref[c, pl.ds(r, n, stride=dil), :] = (
                src_ref[:, r * width + c * 128:r * width + (c + 1) * 128].astype(F32))
    return jnp.concatenate([stage_ref[c] for c in range(n_col)], axis=-1)


def _gated_merge(x, att, hg, mem, gate_ref, wa_ref, wb_ref, wc_ref, wo_ref):
    def gate(j):
        return gate_ref[:, j * D_MODEL:(j + 1) * D_MODEL].astype(F32)

    m = gate(0) * _dot(att, wa_ref[...]) + gate(1) * _dot(hg, wb_ref[...]) + gate(2) * _dot(mem, wc_ref[...])
    return x + _dot(m.astype(BF16), wo_ref[...])


def _mix_body(x_ref, o0_ref, o1_ref, o2_ref, s0_ref, s1_ref, s2_ref, mq_ref, mk_ref, mv_ref, hg_ref, gate_ref,
              wa_ref, wb_ref, wc_ref, wo_ref, out_ref, *stage_refs):
    def gate(j):
        return gate_ref[:, j * D_MODEL:(j + 1) * D_MODEL].astype(F32)

    heads = [slice(h * HEAD_DIM, (h + 1) * HEAD_DIM) for h in range(HEADS)]
    merged = gate(1) * _dot(hg_ref[...], wb_ref[...])
    scores = [_dot_nt(mq_ref[:, sl], mk_ref[:, sl]) for sl in heads]
    probs = []
    for s in scores:
        p = jnp.exp(s - jnp.max(s, axis=-1, keepdims=True))
        probs.append((p.astype(BF16), 1.0 / jnp.sum(p, axis=-1, keepdims=True)))
    mem = [(_dot(p, mv_ref[:, sl]) * inv).astype(BF16) for (p, inv), sl in zip(probs, heads)]
    merged = merged + gate(2) * _dot(jnp.concatenate(mem, axis=-1), wc_ref[...])

    accs = [_from_strided_view(r, stage_refs[2 * g], ATT_GROUPS[g][1], WIDTH)
            for g, r in enumerate((o0_ref, o1_ref, o2_ref))]
    stats = [_from_strided_view(r, stage_refs[2 * g + 1], ATT_GROUPS[g][1], HEADS * LSE_REP)
             for g, r in enumerate((s0_ref, s1_ref, s2_ref))]
    att = []
    for h, sl in enumerate(heads):
        ms = [x[:, h * LSE_REP:h * LSE_REP + 1] for x in stats]
        ls = [x[:, h * LSE_REP + STAT_REP:h * LSE_REP + STAT_REP + 1] for x in stats]
        m = jnp.maximum(jnp.maximum(ms[0], ms[1]), ms[2])
        e = [jnp.exp(x - m) for x in ms]
        inv = 1.0 / (e[0] * ls[0] + e[1] * ls[1] + e[2] * ls[2])
        att.append(sum((e[g] * inv) * accs[g][:, sl] for g in range(N_GROUPS)).astype(BF16))
    merged = merged + gate(0) * _dot(jnp.concatenate(att, axis=-1), wa_ref[...])
    out_ref[...] = x_ref[...] + _dot(merged.astype(BF16), wo_ref[...])


def _mix(x, accs, stats, mq, mk, mv, hg, gates, wa, wb, wc, wo, b, t, *, tm):
    n_tiles = t // tm
    lw = HEADS * LSE_REP
    rows = lambda w: pl.BlockSpec((tm, w), lambda bi, i: (bi * n_tiles + i, 0))
    o_specs = [pl.BlockSpec((None, tm // d, d * WIDTH), lambda bi, i: (bi, i, 0)) for _, d in ATT_GROUPS]
    s_specs = [pl.BlockSpec((None, tm // d, d * lw), lambda bi, i: (bi, i, 0)) for _, d in ATT_GROUPS]
    memspec = pl.BlockSpec((None, MEM_LEN, WIDTH), lambda bi, i: (bi, 0, 0))
    stages = []
    for _ in ATT_GROUPS:
        stages += [pltpu.VMEM((WIDTH // 128, tm, 128), F32), pltpu.VMEM((lw // 128, tm, 128), F32)]
    return pl.pallas_call(
        _mix_body,
        grid=(b, n_tiles),
        in_specs=[rows(D_MODEL)] + o_specs + s_specs + [rows(WIDTH), memspec, memspec, rows(WIDTH),
                                                         rows(N_BRANCH * D_MODEL),
                                                         _const_spec((WIDTH, D_MODEL)), _const_spec((WIDTH, D_MODEL)),
                                                         _const_spec((WIDTH, D_MODEL)), _const_spec((D_MODEL, D_MODEL))],
        out_specs=rows(D_MODEL),
        out_shape=jax.ShapeDtypeStruct((b * t, D_MODEL), F32),
        scratch_shapes=stages,
        compiler_params=_params("parallel", "parallel"),
        name="mix",
    )(x, *accs, *stats, mq, mk, mv, hg, gates, wa, wb, wc, wo)


def _merge_body(x_ref, att_ref, hg_ref, mem_ref, gate_ref, wa_ref, wb_ref, wc_ref, wo_ref, o_ref):
    o_ref[...] = _gated_merge(x_ref[...], att_ref[...].astype(BF16), hg_ref[...].astype(BF16),
                              mem_ref[...].astype(BF16), gate_ref, wa_ref, wb_ref, wc_ref, wo_ref)


def _merge(x, att, hg, mem, gates, wa, wb, wc, wo, *, tm):
    m = x.shape[0]
    row = pl.BlockSpec((tm, WIDTH), lambda i: (i, 0))
    wide = pl.BlockSpec((tm, D_MODEL), lambda i: (i, 0))
    return pl.pallas_call(
        _merge_body,
        grid=(m // tm,),
        in_specs=[wide, row, row, row, pl.BlockSpec((tm, N_BRANCH * D_MODEL), lambda i: (i, 0)),
                  _const_spec((WIDTH, D_MODEL)), _const_spec((WIDTH, D_MODEL)), _const_spec((WIDTH, D_MODEL)),
                  _const_spec((D_MODEL, D_MODEL))],
        out_specs=wide,
        out_shape=jax.ShapeDtypeStruct((m, D_MODEL), F32),
        compiler_params=_params("parallel"),
        name="merge",
    )(x, att, hg, mem, gates, wa, wb, wc, wo)


def _norm_matmul_body(x_ref, g_ref, w_ref, o_ref):
    o_ref[...] = _dot(_rms(x_ref[...], g_ref[...]).astype(BF16), w_ref[...])


def _sample_in_proj(x, g_mix, w_in):
    m = x.shape[0]
    n = w_in.shape[1]
    return pl.pallas_call(
        _norm_matmul_body,
        grid=(n // WIDTH,),
        in_specs=[_const_spec((m, D_MODEL)), _const_spec((1, D_MODEL)),
                  pl.BlockSpec((D_MODEL, WIDTH), lambda j: (0, j))],
        out_specs=pl.BlockSpec((m, WIDTH), lambda j: (0, j)),
        out_shape=jax.ShapeDtypeStruct((m, n), F32),
        compiler_params=_params("parallel"),
        name="sample_in_proj",
    )(x, g_mix, w_in)


def _one_query_attend(q, k_rows, v_rows, k_new=None, v_new=None):
    s = jnp.sum(k_rows * q, axis=-1, keepdims=True)
    m = jnp.max(s, axis=0, keepdims=True)
    if k_new is not None:
        s_new = jnp.sum(k_new * q, axis=-1, keepdims=True)
        m = jnp.maximum(m, s_new)
    p = jnp.exp(s - m)
    l = jnp.sum(p, axis=0, keepdims=True)
    acc = jnp.sum(p * v_rows, axis=0, keepdims=True)
    if k_new is not None:
        p_new = jnp.exp(s_new - m)
        l = l + p_new
        acc = acc + p_new * v_new
    return acc / l, m + jnp.log(l)


def _sample_att_body(q_ref, kn_ref, vn_ref, mq_ref, k1_ref, v1_ref, k4_ref, v4_ref, k16_ref, v16_ref,
                     mk_ref, mv_ref, att_ref, mem_ref):
    k_refs = (k1_ref, k4_ref, k16_ref)
    v_refs = (v1_ref, v4_ref, v16_ref)
    for h in range(HEADS):
        sl = slice(h * HEAD_DIM, (h + 1) * HEAD_DIM)
        outs, lses = [], []
        for g in range(N_GROUPS):
            gs = slice(g * WIDTH + h * HEAD_DIM, g * WIDTH + (h + 1) * HEAD_DIM)
            o, lse = _one_query_attend(q_ref[:, gs] * QK_SCALE, _load_head(k_refs[g], h), _load_head(v_refs[g], h),
                                       kn_ref[:, gs], vn_ref[:, gs])
            outs.append(o)
            lses.append(lse)
        m = jnp.maximum(jnp.maximum(lses[0], lses[1]), lses[2])
        e = [jnp.exp(x - m) for x in lses]
        inv = 1.0 / (e[0] + e[1] + e[2])
        att_ref[:, sl] = sum((e[g] * inv) * outs[g] for g in range(N_GROUPS))
        o, _ = _one_query_attend(mq_ref[:, sl] * QK_SCALE, _load_head(mk_ref, h), _load_head(mv_ref, h))
        mem_ref[:, sl] = o


def _sample_attend(z3, caches_k, caches_v, cache_mk, cache_mv):
    bd = z3.shape[0]
    zspec = lambda col, n: pl.BlockSpec((None, 1, n * WIDTH), lambda b: (b, 0, col // n))
    cache_specs, cache_args = [], []
    for (win, dil), ck, cv in zip(ATT_GROUPS, caches_k, caches_v):
        rows = ck.shape[2]
        assert rows == win and rows // dil == BLOCK
        spec = pl.BlockSpec((None, None, BLOCK, None, HEADS, HEAD_DIM), lambda b: (0, b, 0, 0, 0, 0))
        for c in (ck, cv):
            cache_specs.append(spec)
            cache_args.append(c.reshape(1, bd, BLOCK, dil, HEADS, HEAD_DIM))
    mem_spec = pl.BlockSpec((None, None, MEM_LEN, HEADS, HEAD_DIM), lambda b: (0, b, 0, 0, 0))
    out_spec = pl.BlockSpec((None, 1, WIDTH), lambda b: (b, 0, 0))
    out_shape = jax.ShapeDtypeStruct((bd, 1, WIDTH), F32)
    return pl.pallas_call(
        _sample_att_body,
        grid=(bd,),
        in_specs=[zspec(COL_Q, 3), zspec(COL_K, 3), zspec(COL_V, 3), zspec(COL_MQ, 1)] + cache_specs
                 + [mem_spec, mem_spec],
        out_specs=[out_spec, out_spec],
        out_shape=[out_shape, out_shape],
        compiler_params=_params("parallel"),
        name="sample_attend",
    )(z3, z3, z3, z3, *cache_args, cache_mk, cache_mv)


SAMPLE_GROUP = 8


def _sample_hgrn_body(zq_ref, zf_ref, zi_ref, lb_ref, gain_ref, s_ref, o_ref, so_ref):
    lb = _lower_bound(lb_ref)
    gain = gain_ref[...]
    q = jax.nn.sigmoid(zq_ref[...])
    kk = (1.0 - lb) * jax.nn.sigmoid(zf_ref[...])
    f = lb + kk
    v = zi_ref[...]
    pad = jnp.zeros((HEAD_DIM - SAMPLE_GROUP, HEAD_DIM), F32)

    def columns(a, h):
        return jnp.concatenate([a[:, h * HEAD_DIM:(h + 1) * HEAD_DIM], pad], axis=0).T

    for h in range(HEADS):
        sl = slice(h * HEAD_DIM, (h + 1) * HEAD_DIM)
        q_t, k_t, f_t = columns(q, h), columns(kk, h), columns(f, h)
        for j in range(SAMPLE_GROUP):
            s1 = f_t[:, j:j + 1] * s_ref[j, h] + k_t[:, j:j + 1] * v[j:j + 1, sl]
            so_ref[j, h] = s1
            o = jnp.sum(q_t[:, j:j + 1] * s1, axis=0, keepdims=True)
            o_ref[j:j + 1, sl] = _rms(o, gain[:, sl])


def _sample_hgrn(z, lb_logits, gain, state):
    bd = z.shape[0]
    zspec = lambda col: pl.BlockSpec((SAMPLE_GROUP, WIDTH), lambda i: (i, col))
    sspec = pl.BlockSpec((None, SAMPLE_GROUP, HEADS, HEAD_DIM, HEAD_DIM), lambda i: (0, i, 0, 0, 0))
    return pl.pallas_call(
        _sample_hgrn_body,
        grid=(bd // SAMPLE_GROUP,),
        in_specs=[zspec(COL_HQ), zspec(COL_HF), zspec(COL_HI), _const_spec(lb_logits.shape),
                  _const_spec((1, WIDTH)), sspec],
        out_specs=[pl.BlockSpec((SAMPLE_GROUP, WIDTH), lambda i: (i, 0)), sspec],
        out_shape=[jax.ShapeDtypeStruct((bd, WIDTH), F32), jax.ShapeDtypeStruct(state.shape, F32)],
        compiler_params=_params("parallel"),
        name="sample_hgrn",
    )(z, z, z, lb_logits, gain, state)


def _gate_body(z_ref, o_ref):
    o_ref[...] = jax.nn.sigmoid(z_ref[...]).astype(BF16)


def _sample_gates(z):
    bd = z.shape[0]
    n = N_BRANCH * D_MODEL
    return pl.pallas_call(
        _gate_body,
        grid=(n // WIDTH,),
        in_specs=[pl.BlockSpec((bd, WIDTH), lambda j: (0, COL_GATE + j))],
        out_specs=pl.BlockSpec((bd, WIDTH), lambda j: (0, j)),
        out_shape=jax.ShapeDtypeStruct((bd, n), BF16),
        name="sample_gates",
    )(z)


def kernel(x_prompt, x_sample, mem_prompt, cache_win1_k, cache_win1_v, cache_win4_k, cache_win4_v, cache_win16_k, cache_win16_v, cache_mem_k, cache_mem_v, state_hgrn, g_ff1, w_ff1_gate, w_ff1_up, w_ff1_down, g_mix, w_in, hg_lb_logits, g_hg_out, g_mem, w_mem_k, w_mem_v, w_branch_att, w_branch_hg, w_branch_mem, w_out, g_ff2, w_ff2_gate, w_ff2_up, w_ff2_down, g_final):
    b, t, _ = x_prompt.shape
    bd = x_sample.shape[0]
    assert g_ff1.shape[0] == 1 and x_sample.shape[1] == 1 and w_in.shape[2] == IN_WIDTH
    bf = lambda w: w[0].astype(BF16)
    ffn1 = (g_ff1, bf(w_ff1_gate), bf(w_ff1_up), bf(w_ff1_down), g_final.reshape(1, D_MODEL))
    ffn2 = (g_ff2, bf(w_ff2_gate), bf(w_ff2_up), bf(w_ff2_down), g_final.reshape(1, D_MODEL))
    w_in_b = bf(w_in)
    w_qkv, w_rest = w_in_b[:, :COL_HQ * WIDTH], w_in_b[:, COL_HQ * WIDTH:]
    branch_w = (bf(w_branch_att), bf(w_branch_hg), bf(w_branch_mem), bf(w_out))

    xs, _ = _ffn(x_sample.reshape(bd, D_MODEL), *ffn1, tm=bd, final=False)
    zs = _sample_in_proj(xs, g_mix, w_in_b)
    caches_k = (cache_win1_k, cache_win4_k, cache_win16_k)
    caches_v = (cache_win1_v, cache_win4_v, cache_win16_v)
    z3 = zs.reshape(bd, 1, IN_WIDTH)
    att_s, mem_s = _sample_attend(z3, caches_k, caches_v, cache_mem_k, cache_mem_v)
    hg_s, s_hgrn = _sample_hgrn(zs, hg_lb_logits, g_hg_out, state_hgrn)
    xs = _merge(xs, att_s.reshape(bd, WIDTH), hg_s, mem_s.reshape(bd, WIDTH), _sample_gates(zs), *branch_w, tm=bd)
    y_sample = _ffn(xs, *ffn2, tm=bd, final=True)[0].reshape(bd, 1, D_MODEL)

    tm = b * t // bd
    xp, (s_win16_k,) = _ffn(x_prompt.reshape(b * t, D_MODEL), *ffn1, tm=tm, final=False, ride=(cache_win16_k,))
    qkv = _qkv_proj(xp.reshape(b, t, D_MODEL), g_mix, w_qkv, tm=512)
    q_g, k_g, v_g, p_k, p_v = qkv[0:3], qkv[3:6], qkv[6:9], qkv[9:12], qkv[12:15]
    (hq, hk, lf, hv, mq, gates), (s_win1_k, s_win1_v, s_win4_k, s_win4_v), outs, lses = _rest_att(
        xp, g_mix, w_rest, hg_lb_logits, q_g, k_g, v_g, tm=tm,
        ride=(cache_win1_k, cache_win1_v, cache_win4_k, cache_win4_v))
    hg_p, p_hgrn = _hgrn_prompt(hq, hk, lf, hv, g_hg_out, b, t)
    p_mem_k, p_mem_v, mk_b, mv_b = _mem_kv(mem_prompt, g_mem, bf(w_mem_k), bf(w_mem_v))
    xp = _mix(xp, outs, lses, mq, mk_b, mv_b, hg_p, gates, *branch_w, b, t, tm=512)
    y_prompt, (s_win16_v,) = _ffn(xp, *ffn2, tm=tm, final=True, ride=(cache_win16_v,))
    s_win1_k, s_win1_v, s_win4_k, s_win4_v, s_win16_k, s_win16_v = _write_new_rows(
        z3, [s_win1_k, s_win1_v, s_win4_k, s_win4_v, s_win16_k, s_win16_v],
        ((0, False), (0, True), (1, False), (1, True), (2, False), (2, True)))

    return (y_prompt.reshape(b, t, D_MODEL), y_sample,
            p_k[0], p_v[0], p_k[1], p_v[1], p_k[2], p_v[2],
            p_mem_k, p_mem_v, p_hgrn,
            s_win1_k, s_win1_v, s_win4_k, s_win4_v, s_win16_k, s_win16_v,
            s_hgrn)
```

```python
import functools

import numpy as np
import jax
import jax.numpy as jnp
from jax import lax
from jax.experimental import pallas as pl
from jax.experimental.pallas import tpu as pltpu

F32 = jnp.float32
BF16 = jnp.bfloat16

D_MODEL = 1024
D_FF = 2816
EPS = 1e-6
HEADS = 4
HEAD_DIM = 128
WIDTH = HEADS * HEAD_DIM
ATT_GROUPS = ((128, 1), (512, 4), (2048, 16))
N_GROUPS = 3
BLOCK = 128
MEM_LEN = 256
N_BRANCH = 3
LSE_REP = 32
QK_SCALE = HEAD_DIM ** -0.5
VMEM_LIMIT_BYTES = 56 * 1024 * 1024

COL_Q, COL_K, COL_V, COL_HQ, COL_HF, COL_HI, COL_MQ, COL_GATE = 0, 3, 6, 9, 10, 11, 12, 13
IN_WIDTH = (COL_GATE + N_BRANCH * D_MODEL // WIDTH) * WIDTH


def _params(*sem):
    return pltpu.CompilerParams(dimension_semantics=sem, vmem_limit_bytes=VMEM_LIMIT_BYTES)


def _const_spec(shape):
    nd = len(shape)
    return pl.BlockSpec(shape, lambda *_: (0,) * nd, pipeline_mode=pl.Buffered(1))


def _rms(x, g):
    return x * lax.rsqrt(jnp.mean(x * x, axis=-1, keepdims=True) + EPS) * g


def _dot(a, b):
    return jnp.dot(a, b, preferred_element_type=F32)


def _dot_nt(a, b):
    return lax.dot_general(a, b, (((1,), (1,)), ((), ())), preferred_element_type=F32)


def _ride_specs(caches, n_steps):
    if not caches:
        return [], [], [], []
    assert all(c.shape[1] == n_steps for c in caches)
    in_specs = [pl.BlockSpec((1, 1, c.shape[2], HEADS, HEAD_DIM), lambda i: (0, i, 0, 0, 0)) for c in caches]
    out_specs = [pl.BlockSpec(memory_space=pl.ANY)] * len(caches)
    shapes = [jax.ShapeDtypeStruct(c.shape, c.dtype) for c in caches]
    return in_specs, out_specs, shapes, [pltpu.SemaphoreType.DMA((2 * len(caches),))]


def _ride_copies(in_refs, out_refs, sem):
    seq = pl.program_id(0)
    copies = []
    for j, (cache_ref, out_ref) in enumerate(zip(in_refs, out_refs)):
        rows = cache_ref.shape[2]
        copies.append(pltpu.make_async_copy(cache_ref.at[:, :, pl.ds(1, rows - 1)],
                                            out_ref.at[:, pl.ds(seq, 1), pl.ds(0, rows - 1)], sem.at[2 * j]))
        copies.append(pltpu.make_async_copy(cache_ref.at[:, :, pl.ds(rows - 1, 1)],
                                            out_ref.at[:, pl.ds(seq, 1), pl.ds(rows - 1, 1)], sem.at[2 * j + 1]))
    for cp in copies:
        cp.start()
    return copies


def _new_rows_body(kn_ref, vn_ref, *refs, meta):
    n = len(meta)
    out_refs, row_refs, sem = refs[n:2 * n], refs[2 * n:3 * n], refs[3 * n]
    copies = []
    for j, ((g, is_v), out_ref, row_ref) in enumerate(zip(meta, out_refs, row_refs)):
        bd, rows = out_ref.shape[1], out_ref.shape[2]
        new = (vn_ref if is_v else kn_ref)[:, 0, g * WIDTH:(g + 1) * WIDTH]
        flat = row_ref.reshape(bd * HEADS, HEAD_DIM)
        for h in range(HEADS):
            flat[pl.ds(h, bd, stride=HEADS), :] = new[:, h * HEAD_DIM:(h + 1) * HEAD_DIM]
        copies.append(pltpu.make_async_copy(row_ref, out_ref.at[:, :, pl.ds(rows - 1, 1)], sem.at[j]))
        copies[-1].start()
    for cp in copies:
        cp.wait()


def _write_new_rows(z3, shifted, meta):
    bd = z3.shape[0]
    n = len(shifted)
    zspec = lambda col: pl.BlockSpec((bd, 1, N_GROUPS * WIDTH), lambda i: (0, 0, col // N_GROUPS))
    any_spec = pl.BlockSpec(memory_space=pl.ANY)
    return pl.pallas_call(
        functools.partial(_new_rows_body, meta=meta),
        grid=(1,),
        in_specs=[zspec(COL_K), zspec(COL_V)] + [any_spec] * n,
        out_specs=[any_spec] * n,
        out_shape=[jax.ShapeDtypeStruct(c.shape, c.dtype) for c in shifted],
        input_output_aliases={2 + j: j for j in range(n)},
        scratch_shapes=[pltpu.VMEM((1, bd, 1, HEADS, HEAD_DIM), F32)] * n + [pltpu.SemaphoreType.DMA((n,))],
        name="write_new_rows",
    )(z3, z3, *shifted)


FF_CHUNKS = 1


def _ffn_body(x_ref, g_ref, wg_ref, wu_ref, wd_ref, gf_ref, *rest, final, n_ride):
    o_ref = rest[n_ride]
    copies = _ride_copies(rest[:n_ride], rest[n_ride + 1:2 * n_ride + 1], rest[-1]) if n_ride else []
    x = x_ref[...]
    h = _rms(x, g_ref[...]).astype(BF16)
    y = x
    width = D_FF // FF_CHUNKS
    for c in range(FF_CHUNKS):
        cols = slice(c * width, (c + 1) * width)
        a = _dot(h, wg_ref[:, cols])
        u = _dot(h, wu_ref[:, cols])
        act = (a * jax.nn.sigmoid(a) * u).astype(BF16)
        y = y + 0.5 * _dot(act, wd_ref[cols, :])
    if final:
        y = _rms(y, gf_ref[...])
    o_ref[...] = y
    for cp in copies:
        cp.wait()


def _ffn(x, g, wg, wu, wd, g_final, *, tm, final, ride=()):
    m = x.shape[0]
    r_in, r_out, r_shapes, r_scratch = _ride_specs(ride, m // tm)
    res = pl.pallas_call(
        functools.partial(_ffn_body, final=final, n_ride=len(ride)),
        grid=(m // tm,),
        in_specs=[pl.BlockSpec((tm, D_MODEL), lambda i: (i, 0)),
                  _const_spec((1, D_MODEL)),
                  _const_spec((D_MODEL, D_FF)), _const_spec((D_MODEL, D_FF)), _const_spec((D_FF, D_MODEL)),
                  _const_spec((1, D_MODEL))] + r_in,
        out_specs=[pl.BlockSpec((tm, D_MODEL), lambda i: (i, 0))] + r_out,
        out_shape=[jax.ShapeDtypeStruct((m, D_MODEL), F32)] + r_shapes,
        scratch_shapes=r_scratch,
        compiler_params=_params("parallel"),
        name="ffn_final" if final else "ffn",
    )(x, g, wg, wu, wd, g_final, *ride)
    return res[0], res[1:]


def _store_heads(ref, z):
    rows = ref.shape[0]
    flat = ref.reshape(rows * HEADS, HEAD_DIM)
    for h in range(HEADS):
        flat[pl.ds(h, rows, stride=HEADS), :] = z[:, h * HEAD_DIM:(h + 1) * HEAD_DIM]


def _load_head(ref, h):
    rows = ref.shape[0]
    return ref.reshape(rows * HEADS, HEAD_DIM)[pl.ds(h, rows, stride=HEADS), :]


def _to_strided_view(dst_ref, z, stage_ref, dil):
    if dil == 1:
        dst_ref[...] = z.astype(dst_ref.dtype)
        return
    n = z.shape[0] // dil
    for c in range(stage_ref.shape[0]):
        stage_ref[c] = z[:, c * 128:(c + 1) * 128]
    for r in range(dil):
        for c in range(stage_ref.shape[0]):
            dst_ref[:, r * WIDTH + c * 128:r * WIDTH + (c + 1) * 128] = (
                stage_ref[c, pl.ds(r, n, stride=dil), :].astype(dst_ref.dtype))


def _qkv_body(x_ref, g_ref, w_ref, *refs, tm, n_tiles):
    q_refs, k_refs, v_refs = refs[0:3], refs[3:6], refs[6:9]
    pk_refs, pv_refs, stage_refs = refs[9:12], refs[12:15], refs[15:]
    h = _rms(x_ref[...], g_ref[...]).astype(BF16)
    late = []
    for g, (win, dil) in enumerate(ATT_GROUPS):
        keep = min(win, tm * n_tiles)
        for kind, (col, dst_refs, win_refs) in enumerate(((COL_Q, q_refs, None), (COL_K, k_refs, pk_refs),
                                                          (COL_V, v_refs, pv_refs))):
            z = _dot(h, w_ref[:, (col + g) * WIDTH:(col + g + 1) * WIDTH])
            if win_refs is None:
                z = z * QK_SCALE
            stage = stage_refs[3 * (g - 1) + kind] if dil > 1 else None
            _to_strided_view(dst_refs[g], z, stage, dil)
            if win_refs is None:
                continue
            if keep == tm * n_tiles:
                _store_heads(win_refs[g], z)
            elif dil == 1:
                late.append((win_refs[g], z[tm - keep:, :], None))
            else:
                assert keep == tm
                late.append((win_refs[g], None, stage))

    @pl.when(pl.program_id(1) == n_tiles - 1)
    def _():
        for ref, rows, stage in late:
            if stage is None:
                _store_heads(ref, rows)
            else:
                _store_heads(ref, jnp.concatenate([stage[hd] for hd in range(HEADS)], axis=-1))


def _qkv_proj(x, g_mix, w_qkv, *, tm):
    b, t, _ = x.shape
    n_tiles = t // tm
    view_specs = [pl.BlockSpec((None, tm // d, d * WIDTH), lambda bi, i: (bi, i, 0)) for _, d in ATT_GROUPS]
    view_shapes = [jax.ShapeDtypeStruct((b, t // d, d * WIDTH), BF16) for _, d in ATT_GROUPS]
    win_specs, win_shapes = [], []
    for win, _ in ATT_GROUPS:
        keep = min(win, t)
        assert keep == t or keep <= tm
        if keep == t:
            win_specs.append(pl.BlockSpec((None, None, tm, HEADS, HEAD_DIM), lambda bi, i: (0, bi, i, 0, 0)))
        else:
            win_specs.append(pl.BlockSpec((None, None, keep, HEADS, HEAD_DIM), lambda bi, i: (0, bi, 0, 0, 0)))
        win_shapes.append(jax.ShapeDtypeStruct((1, b, keep, HEADS, HEAD_DIM), F32))
    return pl.pallas_call(
        functools.partial(_qkv_body, tm=tm, n_tiles=n_tiles),
        grid=(b, n_tiles),
        in_specs=[pl.BlockSpec((None, tm, D_MODEL), lambda bi, i: (bi, i, 0)),
                  _const_spec((1, D_MODEL)),
                  _const_spec((D_MODEL, 3 * N_GROUPS * WIDTH))],
        out_specs=view_specs * 3 + win_specs + win_specs,
        out_shape=view_shapes * 3 + win_shapes + win_shapes,
        scratch_shapes=[pltpu.VMEM((WIDTH // 128, tm, 128), F32)] * (3 * sum(d > 1 for _, d in ATT_GROUPS)),
        compiler_params=_params("parallel", "arbitrary"),
        name="qkv_proj",
    )(x, g_mix, w_qkv)


def _lower_bound(logits_ref):
    lg = logits_ref[...]
    e = jnp.exp(lg - jnp.max(lg, axis=0, keepdims=True))
    return e[0:1, :] / jnp.sum(e, axis=0, keepdims=True)


def _rest_steps(x_ref, g_ref, w_ref, lb_ref, hq_ref, hk_ref, lf_ref, hv_ref, mq_ref, gate_ref):
    h = _rms(x_ref[...], g_ref[...]).astype(BF16)
    lb = _lower_bound(lb_ref)
    proj = lambda c: _dot(h, w_ref[:, c * WIDTH:(c + 1) * WIDTH])

    def hgrn_query():
        hq_ref[...] = jax.nn.sigmoid(proj(0)).astype(BF16)

    def hgrn_key():
        kk = (1.0 - lb) * jax.nn.sigmoid(proj(1))
        hk_ref[...] = kk.astype(BF16)
        lf_ref[...] = jnp.log(lb + kk)

    def hgrn_value():
        hv_ref[...] = proj(2).astype(BF16)

    def mem_query():
        mq_ref[...] = (proj(3) * QK_SCALE).astype(BF16)

    def gate(c):
        gate_ref[:, c * WIDTH:(c + 1) * WIDTH] = jax.nn.sigmoid(proj(4 + c)).astype(BF16)

    return [hgrn_query, hgrn_key, hgrn_value, mem_query] + [functools.partial(gate, c)
                                                            for c in range(N_BRANCH * D_MODEL // WIDTH)]


def _rest_att_body(x_ref, g_ref, w_ref, lb_ref, *rest, n_ride, tiles, tiles_per_seq):
    n_att_in = sum(5 if has_halo else 3 for _, _, has_halo in tiles)
    caches, rest = rest[:n_ride], rest[n_ride:]
    att_in, rest = rest[:n_att_in], rest[n_att_in:]
    proj_out, rest = rest[:6], rest[6:]
    ride_out, rest = rest[:n_ride], rest[n_ride:]
    att_out, scratch = rest[:2 * len(tiles)], rest[2 * len(tiles):]
    copies = _ride_copies(caches, ride_out, scratch[-1]) if n_ride else []
    steps = _rest_steps(x_ref, g_ref, w_ref, lb_ref, *proj_out)
    gens = _att_tiles(att_in + att_out, tiles, pl.program_id(0) % tiles_per_seq)
    steps.pop(0)()
    for gen in gens:
        next(gen)
        steps.pop(0)()
        next(gen)
        steps.pop(0)()
        next(gen, None)
    for step in steps:
        step()
    for cp in copies:
        cp.wait()


def _rest_att(x, g_mix, w_rest, lb_logits, q_g, k_g, v_g, *, tm, ride=()):
    m = x.shape[0]
    spec = pl.BlockSpec((tm, WIDTH), lambda i: (i, 0))
    bf = jax.ShapeDtypeStruct((m, WIDTH), BF16)
    r_in, r_out, r_shapes, r_scratch = _ride_specs(ride, m // tm)
    tiles_per_seq = q_g[0].shape[1] // (ATT_UNITS * BLOCK)
    a_in, a_args, a_out, a_shapes, tiles, n_tiles = _attention_specs(
        q_g, k_g, v_g, lambda i: (i // tiles_per_seq, i % tiles_per_seq))
    assert n_tiles == tiles_per_seq and q_g[0].shape[0] * n_tiles == m // tm
    res = pl.pallas_call(
        functools.partial(_rest_att_body, n_ride=len(ride), tiles=tiles, tiles_per_seq=tiles_per_seq),
        grid=(m // tm,),
        in_specs=[pl.BlockSpec((tm, D_MODEL), lambda i: (i, 0)),
                  _const_spec((1, D_MODEL)),
                  _const_spec(w_rest.shape),
                  _const_spec(lb_logits.shape)] + r_in + a_in,
        out_specs=[spec, spec, spec, spec, spec, pl.BlockSpec((tm, N_BRANCH * D_MODEL), lambda i: (i, 0))]
                  + r_out + a_out,
        out_shape=[bf, bf, jax.ShapeDtypeStruct((m, WIDTH), F32), bf, bf,
                   jax.ShapeDtypeStruct((m, N_BRANCH * D_MODEL), BF16)] + r_shapes + a_shapes,
        scratch_shapes=r_scratch,
        compiler_params=_params("arbitrary"),
        name="rest_att",
    )(x, g_mix, w_rest, lb_logits, *ride, *a_args)
    n = 6 + len(ride)
    return res[:6], res[6:n], res[n::2], res[n + 1::2]


ATT_UNITS = 4
STAT_REP = LSE_REP // 2


def _att_tiles(refs, tiles, tile_index):
    n_in = [5 if has_halo else 3 for _, _, has_halo in tiles]
    outs = refs[sum(n_in):]
    gens, pos = [], 0
    for g, (n_row_blocks, n_col_blocks, has_halo) in enumerate(tiles):
        gens.append(_att_tile(*refs[pos:pos + n_in[g]], outs[2 * g], outs[2 * g + 1], n_row_blocks=n_row_blocks,
                              n_col_blocks=n_col_blocks, has_halo=has_halo, tile_index=tile_index))
        pos += n_in[g]
    return gens


def _att_tile(q_ref, k_ref, v_ref, *rest, n_row_blocks, n_col_blocks, has_halo, tile_index):
    if has_halo:
        kh_ref, vh_ref, o_ref, st_ref = rest
        halo_ok = tile_index > 0
    else:
        o_ref, st_ref = rest
    qi = lax.broadcasted_iota(jnp.int32, (BLOCK, 2 * BLOCK), 0)
    ki = lax.broadcasted_iota(jnp.int32, (BLOCK, 2 * BLOCK), 1)
    own_ok = (lax.broadcasted_iota(jnp.int32, (BLOCK, BLOCK), 1)
              <= lax.broadcasted_iota(jnp.int32, (BLOCK, BLOCK), 0))
    both_ok = jnp.where(ki < BLOCK, ki, qi + BLOCK) >= jnp.where(ki < BLOCK, qi, ki)
    if has_halo:
        first_valid = jnp.where(halo_ok, 0, BLOCK)
        halo_both_ok = jnp.logical_and(both_ok, ki >= first_valid)

    units = []
    for c in range(n_col_blocks):
        for u in range(n_row_blocks):
            for h in range(HEADS):
                rows = slice(u * BLOCK, (u + 1) * BLOCK)
                cols = slice(c * WIDTH + h * HEAD_DIM, c * WIDTH + (h + 1) * HEAD_DIM)
                lcols = (c * HEADS + h) * LSE_REP
                if u > 0:
                    with_prev = slice((u - 1) * BLOCK, (u + 1) * BLOCK)
                    keys = lambda ref, rr=with_prev, cc=cols: ref[rr, cc]
                    ok = both_ok
                elif has_halo:
                    hcols = slice(h * HEAD_DIM, (h + 1) * HEAD_DIM)
                    keys = lambda ref, rr=rows, cc=cols, hc=hcols: jnp.concatenate(
                        [(kh_ref if ref is k_ref else vh_ref)[:, hc], ref[rr, cc]], axis=0)
                    ok = halo_both_ok
                else:
                    keys = lambda ref, rr=rows, cc=cols: ref[rr, cc]
                    ok = own_ok
                units.append((rows, cols, lcols, keys, ok))

    scores = [jnp.where(ok, _dot_nt(q_ref[rows, cols], keys(k_ref)), -jnp.inf) for rows, cols, _, keys, ok in units]
    yield

    probs = []
    for s in scores:
        m = jnp.max(s, axis=-1, keepdims=True)
        p = jnp.exp(s - m)
        probs.append((p.astype(BF16), m, jnp.sum(p, axis=-1, keepdims=True)))
    yield

    for (rows, cols, lcols, keys, _), (p, m, l) in zip(units, probs):
        o_ref[rows, cols] = _dot(p, keys(v_ref)).astype(BF16)
        st_ref[rows, lcols:lcols + STAT_REP] = jnp.broadcast_to(m, (BLOCK, STAT_REP))
        st_ref[rows, lcols + STAT_REP:lcols + LSE_REP] = jnp.broadcast_to(l, (BLOCK, STAT_REP))


def _attention_specs(q_g, k_g, v_g, to_bj):
    b = q_g[0].shape[0]
    lw = HEADS * LSE_REP
    n_steps = None
    in_specs, args, out_specs, out_shapes, tiles = [], [], [], [], []

    def spec(shape, index):
        return pl.BlockSpec(shape, lambda *grid_idx: index(*to_bj(*grid_idx)))

    for (_, dil), q, k, v in zip(ATT_GROUPS, q_g, k_g, v_g):
        sub_len = q.shape[1]
        n_row_blocks = min(ATT_UNITS, sub_len // BLOCK)
        n_col_blocks = ATT_UNITS // n_row_blocks
        span = n_row_blocks * BLOCK
        assert sub_len % span == 0 and dil % n_col_blocks == 0
        has_halo = sub_len > span
        assert not has_halo or (dil == 1 and n_col_blocks == 1)
        n = (sub_len // span) * (dil // n_col_blocks)
        assert n_steps in (None, n)
        n_steps = n
        if has_halo:
            tile = lambda w, span=span: spec((None, span, w), lambda bi, j: (bi, j, 0))
        else:
            tile = lambda w, span=span, nc=n_col_blocks: spec((None, span, nc * w), lambda bi, j: (bi, 0, j))
        halo = spec((None, BLOCK, WIDTH), lambda bi, j, nr=n_row_blocks: (bi, jnp.maximum(j * nr - 1, 0), 0))
        in_specs += [tile(WIDTH)] * 3 + ([halo, halo] if has_halo else [])
        args += [q, k, v] + ([k, v] if has_halo else [])
        out_specs += [tile(WIDTH), tile(lw)]
        out_shapes += [jax.ShapeDtypeStruct((b, sub_len, dil * WIDTH), BF16),
                       jax.ShapeDtypeStruct((b, sub_len, dil * lw), F32)]
        tiles.append((n_row_blocks, n_col_blocks, has_halo))
    return in_specs, args, out_specs, out_shapes, tuple(tiles), n_steps


N_LEVELS = 7
ROW_CUM, ROW_REM = N_LEVELS, N_LEVELS + 1


def _hgrn_constants():
    r = np.arange(BLOCK)[:, None]
    s = np.arange(BLOCK)[None, :]
    mats = []
    for lvl in range(N_LEVELS):
        half = 1 << lvl
        mid = (r // (2 * half)) * 2 * half + half
        upper = (r % (2 * half)) >= half
        mats.append(np.where(upper, (s >= mid) & (s <= r), (s > r) & (s <= mid - 1)))
    mats.append(s <= r)
    mats.append(s > r)
    range_mat = np.concatenate(mats, axis=0).astype(np.float32)
    range_mat = np.concatenate([range_mat, range_mat], axis=1)
    x = r ^ s
    level = np.where(s < r, np.floor(np.log2(np.maximum(x, 1))).astype(np.int32), np.where(s == r, N_LEVELS, -1))
    return jnp.asarray(range_mat, BF16), jnp.asarray(level, jnp.int32)


HGRN_SEQS = 4


def _hgrn_intra(q_b, k_b, lf, rng, level):
    lf_hi = lf.astype(BF16)
    lf_lo = (lf - lf_hi.astype(F32)).astype(BF16)
    z = _dot(rng, jnp.concatenate([lf_hi, lf_lo], axis=0))
    q = q_b.astype(F32)
    kk = k_b.astype(F32)
    lhs, rhs = [], []
    for lvl in range(N_LEVELS):
        x = jnp.exp(z[lvl * BLOCK:(lvl + 1) * BLOCK])
        lhs.append((q * x).astype(BF16))
        rhs.append((kk * x).astype(BF16))
    lhs.append(q_b)
    rhs.append(k_b)
    b_cum = z[ROW_CUM * BLOCK:(ROW_CUM + 1) * BLOCK]
    q_in = (q * jnp.exp(b_cum)).astype(BF16)
    k_out = (kk * jnp.exp(z[ROW_REM * BLOCK:(ROW_REM + 1) * BLOCK])).astype(BF16)
    decay = jnp.exp(b_cum[BLOCK - 1:BLOCK, :])
    mats = []
    for h in range(HEADS):
        sl = slice(h * HEAD_DIM, (h + 1) * HEAD_DIM)
        a = jnp.zeros((BLOCK, BLOCK), F32)
        for lvl in range(N_LEVELS + 1):
            a = jnp.where(level == lvl, _dot_nt(lhs[lvl][:, sl], rhs[lvl][:, sl]), a)
        mats.append(a.astype(BF16))
    return mats, q_in, k_out, decay


def _hgrn_state(mats, q_in, k_out, decay, v_b, gain, st_ref, j):
    outs = []
    for h in range(HEADS):
        sl = slice(h * HEAD_DIM, (h + 1) * HEAD_DIM)
        v = v_b[:, sl]
        st = st_ref[j, h]
        o = _dot(mats[h], v) + _dot_nt(q_in[:, sl], st.astype(BF16))
        outs.append(_rms(o, gain[:, sl]).astype(BF16))
        v_t = v.astype(F32).T.astype(BF16)
        st_ref[j, h] = st * decay[:, sl] + _dot(v_t, k_out[:, sl])
    return jnp.concatenate(outs, axis=-1)


def _hgrn_body(q_ref, k_ref, lf_ref, v_ref, rng_ref, lvl_ref, gain_ref, o_ref, s_ref, st_ref, *, n_chunks):
    c = pl.program_id(1)

    @pl.when(c == 0)
    def _():
        st_ref[...] = jnp.zeros_like(st_ref)

    rng, level, gain = rng_ref[...], lvl_ref[...], gain_ref[...]
    intra = [_hgrn_intra(q_ref[j], k_ref[j], lf_ref[j], rng, level) for j in range(HGRN_SEQS)]
    for j in range(HGRN_SEQS):
        o_ref[j] = _hgrn_state(*intra[j], v_ref[j], gain, st_ref, j)

    @pl.when(c == n_chunks - 1)
    def _():
        for j in range(HGRN_SEQS):
            for h in range(HEADS):
                s_ref[j, h] = st_ref[j, h].T


def _hgrn_prompt(hq, hk, lf, hv, gain, b, t):
    n_chunks = t // BLOCK
    range_mat, level = _hgrn_constants()
    as3d = lambda a: a.reshape(b, t, WIDTH)
    spec = pl.BlockSpec((HGRN_SEQS, BLOCK, WIDTH), lambda bi, c: (bi, c, 0))
    state_spec = pl.BlockSpec((None, HGRN_SEQS, HEADS, HEAD_DIM, HEAD_DIM), lambda bi, c: (0, bi, 0, 0, 0))
    o, state = pl.pallas_call(
        functools.partial(_hgrn_body, n_chunks=n_chunks),
        grid=(b // HGRN_SEQS, n_chunks),
        in_specs=[spec, spec, spec, spec, _const_spec(range_mat.shape), _const_spec(level.shape),
                  _const_spec((1, WIDTH))],
        out_specs=[spec, state_spec],
        out_shape=[jax.ShapeDtypeStruct((b, t, WIDTH), BF16),
                   jax.ShapeDtypeStruct((1, b, HEADS, HEAD_DIM, HEAD_DIM), F32)],
        scratch_shapes=[pltpu.VMEM((HGRN_SEQS, HEADS, HEAD_DIM, HEAD_DIM), F32)],
        compiler_params=_params("parallel", "arbitrary"),
        name="hgrn_prompt",
    )(as3d(hq), as3d(hk), as3d(lf), as3d(hv), range_mat, level, gain)
    return o.reshape(b * t, WIDTH), state


def _memkv_body(m_ref, g_ref, wk_ref, wv_ref, pk_ref, pv_ref, kb_ref, vb_ref):
    h = _rms(m_ref[...], g_ref[...]).astype(BF16)
    k = _dot(h, wk_ref[...])
    v = _dot(h, wv_ref[...])
    _store_heads(pk_ref, k)
    _store_heads(pv_ref, v)
    kb_ref[...] = k.astype(BF16)
    vb_ref[...] = v.astype(BF16)


def _mem_kv(mem, g_mem, w_mk, w_mv):
    b = mem.shape[0]
    head_spec = pl.BlockSpec((None, None, MEM_LEN, HEADS, HEAD_DIM), lambda bi: (0, bi, 0, 0, 0))
    flat_spec = pl.BlockSpec((None, MEM_LEN, WIDTH), lambda bi: (bi, 0, 0))
    head_shape = jax.ShapeDtypeStruct((1, b, MEM_LEN, HEADS, HEAD_DIM), F32)
    flat_shape = jax.ShapeDtypeStruct((b, MEM_LEN, WIDTH), BF16)
    return pl.pallas_call(
        _memkv_body,
        grid=(b,),
        in_specs=[pl.BlockSpec((None, MEM_LEN, D_MODEL), lambda bi: (bi, 0, 0)),
                  _const_spec((1, D_MODEL)), _const_spec((D_MODEL, WIDTH)), _const_spec((D_MODEL, WIDTH))],
        out_specs=[head_spec, head_spec, flat_spec, flat_spec],
        out_shape=[head_shape, head_shape, flat_shape, flat_shape],
        compiler_params=_params("parallel"),
        name="mem_kv",
    )(mem, g_mem, w_mk, w_mv)


def _from_strided_view(src_ref, stage_ref, dil, width):
    if dil == 1:
        return src_ref[...].astype(F32)
    n = src_ref.shape[0]
    n_col = width // 128
    for r in range(dil):
        for c in range(n_col):
            stage_ref[c, pl.ds(r, n, stride=dil), :] = (
                src_ref[:, r * width + c * 128:r * width + (c + 1) * 128].astype(F32))
    return jnp.concatenate([stage_ref[c] for c in range(n_col)], axis=-1)


def _gated_merge(x, att, hg, mem, gate_ref, wa_ref, wb_ref, wc_ref, wo_ref):
    def gate(j):
        return gate_ref[:, j * D_MODEL:(j + 1) * D_MODEL].astype(F32)

    m = gate(0) * _dot(att, wa_ref[...]) + gate(1) * _dot(hg, wb_ref[...]) + gate(2) * _dot(mem, wc_ref[...])
    return x + _dot(m.astype(BF16), wo_ref[...])


def _mix_body(x_ref, o0_ref, o1_ref, o2_ref, s0_ref, s1_ref, s2_ref, mq_ref, mk_ref, mv_ref, hg_ref, gate_ref,
              wa_ref, wb_ref, wc_ref, wo_ref, out_ref, *stage_refs):
    def gate(j):
        return gate_ref[:, j * D_MODEL:(j + 1) * D_MODEL].astype(F32)

    heads = [slice(h * HEAD_DIM, (h + 1) * HEAD_DIM) for h in range(HEADS)]
    merged = gate(1) * _dot(hg_ref[...], wb_ref[...])
    scores = [_dot_nt(mq_ref[:, sl], mk_ref[:, sl]) for sl in heads]
    probs = []
    for s in scores:
        p = jnp.exp(s - jnp.max(s, axis=-1, keepdims=True))
        probs.append((p.astype(BF16), 1.0 / jnp.sum(p, axis=-1, keepdims=True)))
    mem = [(_dot(p, mv_ref[:, sl]) * inv).astype(BF16) for (p, inv), sl in zip(probs, heads)]
    merged = merged + gate(2) * _dot(jnp.concatenate(mem, axis=-1), wc_ref[...])

    accs = [_from_strided_view(r, stage_refs[2 * g], ATT_GROUPS[g][1], WIDTH)
            for g, r in enumerate((o0_ref, o1_ref, o2_ref))]
    stats = [_from_strided_view(r, stage_refs[2 * g + 1], ATT_GROUPS[g][1], HEADS * LSE_REP)
             for g, r in enumerate((s0_ref, s1_ref, s2_ref))]
    att = []
    for h, sl in enumerate(heads):
        ms = [x[:, h * LSE_REP:h * LSE_REP + 1] for x in stats]
        ls = [x[:, h * LSE_REP + STAT_REP:h * LSE_REP + STAT_REP + 1] for x in stats]
        m = jnp.maximum(jnp.maximum(ms[0], ms[1]), ms[2])
        e = [jnp.exp(x - m) for x in ms]
        inv = 1.0 / (e[0] * ls[0] + e[1] * ls[1] + e[2] * ls[2])
        att.append(sum((e[g] * inv) * accs[g][:, sl] for g in range(N_GROUPS)).astype(BF16))
    merged = merged + gate(0) * _dot(jnp.concatenate(att, axis=-1), wa_ref[...])
    out_ref[...] = x_ref[...] + _dot(merged.astype(BF16), wo_ref[...])


def _mix(x, accs, stats, mq, mk, mv, hg, gates, wa, wb, wc, wo, b, t, *, tm):
    n_tiles = t // tm
    lw = HEADS * LSE_REP
    rows = lambda w: pl.BlockSpec((tm, w), lambda bi, i: (bi * n_tiles + i, 0))
    o_specs = [pl.BlockSpec((None, tm // d, d * WIDTH), lambda bi, i: (bi, i, 0)) for _, d in ATT_GROUPS]
    s_specs = [pl.BlockSpec((None, tm // d, d * lw), lambda bi, i: (bi, i, 0)) for _, d in ATT_GROUPS]
    memspec = pl.BlockSpec((None, MEM_LEN, WIDTH), lambda bi, i: (bi, 0, 0))
    stages = []
    for _ in ATT_GROUPS:
        stages += [pltpu.VMEM((WIDTH // 128, tm, 128), F32), pltpu.VMEM((lw // 128, tm, 128), F32)]
    return pl.pallas_call(
        _mix_body,
        grid=(b, n_tiles),
        in_specs=[rows(D_MODEL)] + o_specs + s_specs + [rows(WIDTH), memspec, memspec, rows(WIDTH),
                                                         rows(N_BRANCH * D_MODEL),
                                                         _const_spec((WIDTH, D_MODEL)), _const_spec((WIDTH, D_MODEL)),
                                                         _const_spec((WIDTH, D_MODEL)), _const_spec((D_MODEL, D_MODEL))],
        out_specs=rows(D_MODEL),
        out_shape=jax.ShapeDtypeStruct((b * t, D_MODEL), F32),
        scratch_shapes=stages,
        compiler_params=_params("parallel", "parallel"),
        name="mix",
    )(x, *accs, *stats, mq, mk, mv, hg, gates, wa, wb, wc, wo)


def _merge_body(x_ref, att_ref, hg_ref, mem_ref, gate_ref, wa_ref, wb_ref, wc_ref, wo_ref, o_ref):
    o_ref[...] = _gated_merge(x_ref[...], att_ref[...].astype(BF16), hg_ref[...].astype(BF16),
                              mem_ref[...].astype(BF16), gate_ref, wa_ref, wb_ref, wc_ref, wo_ref)


def _merge(x, att, hg, mem, gates, wa, wb, wc, wo, *, tm):
    m = x.shape[0]
    row = pl.BlockSpec((tm, WIDTH), lambda i: (i, 0))
    wide = pl.BlockSpec((tm, D_MODEL), lambda i: (i, 0))
    return pl.pallas_call(
        _merge_body,
        grid=(m // tm,),
        in_specs=[wide, row, row, row, pl.BlockSpec((tm, N_BRANCH * D_MODEL), lambda i: (i, 0)),
                  _const_spec((WIDTH, D_MODEL)), _const_spec((WIDTH, D_MODEL)), _const_spec((WIDTH, D_MODEL)),
                  _const_spec((D_MODEL, D_MODEL))],
        out_specs=wide,
        out_shape=jax.ShapeDtypeStruct((m, D_MODEL), F32),
        compiler_params=_params("parallel"),
        name="merge",
    )(x, att, hg, mem, gates, wa, wb, wc, wo)


def _norm_matmul_body(x_ref, g_ref, wa_ref, wb_ref, o_ref, *, n_a):
    h = _rms(x_ref[...], g_ref[...]).astype(BF16)

    @pl.when(pl.program_id(0) < n_a)
    def _():
        o_ref[...] = _dot(h, wa_ref[...])

    @pl.when(pl.program_id(0) >= n_a)
    def _():
        o_ref[...] = _dot(h, wb_ref[...])


def _sample_in_proj(x, g_mix, w_a, w_b):
    m = x.shape[0]
    n_a, n_b = w_a.shape[1] // WIDTH, w_b.shape[1] // WIDTH
    return pl.pallas_call(
        functools.partial(_norm_matmul_body, n_a=n_a),
        grid=(n_a + n_b,),
        in_specs=[_const_spec((m, D_MODEL)), _const_spec((1, D_MODEL)),
                  pl.BlockSpec((D_MODEL, WIDTH), lambda j: (0, jnp.minimum(j, n_a - 1))),
                  pl.BlockSpec((D_MODEL, WIDTH), lambda j: (0, jnp.maximum(j - n_a, 0)))],
        out_specs=pl.BlockSpec((m, WIDTH), lambda j: (0, j)),
        out_shape=jax.ShapeDtypeStruct((m, (n_a + n_b) * WIDTH), F32),
        compiler_params=_params("arbitrary"),
        name="sample_in_proj",
    )(x, g_mix, w_a, w_b)


def _one_query_attend(q, k_rows, v_rows, k_new=None, v_new=None):
    s = jnp.sum(k_rows * q, axis=-1, keepdims=True)
    m = jnp.max(s, axis=0, keepdims=True)
    if k_new is not None:
        s_new = jnp.sum(k_new * q, axis=-1, keepdims=True)
        m = jnp.maximum(m, s_new)
    p = jnp.exp(s - m)
    l = jnp.sum(p, axis=0, keepdims=True)
    acc = jnp.sum(p * v_rows, axis=0, keepdims=True)
    if k_new is not None:
        p_new = jnp.exp(s_new - m)
        l = l + p_new
        acc = acc + p_new * v_new
    return acc / l, m + jnp.log(l)


def _sample_att_body(q_ref, kn_ref, vn_ref, mq_ref, k1_ref, v1_ref, k4_ref, v4_ref, k16_ref, v16_ref,
                     mk_ref, mv_ref, att_ref, mem_ref):
    k_refs = (k1_ref, k4_ref, k16_ref)
    v_refs = (v1_ref, v4_ref, v16_ref)
    for h in range(HEADS):
        sl = slice(h * HEAD_DIM, (h + 1) * HEAD_DIM)
        outs, lses = [], []
        for g in range(N_GROUPS):
            gs = slice(g * WIDTH + h * HEAD_DIM, g * WIDTH + (h + 1) * HEAD_DIM)
            o, lse = _one_query_attend(q_ref[:, gs] * QK_SCALE, _load_head(k_refs[g], h), _load_head(v_refs[g], h),
                                       kn_ref[:, gs], vn_ref[:, gs])
            outs.append(o)
            lses.append(lse)
        m = jnp.maximum(jnp.maximum(lses[0], lses[1]), lses[2])
        e = [jnp.exp(x - m) for x in lses]
        inv = 1.0 / (e[0] + e[1] + e[2])
        att_ref[:, sl] = sum((e[g] * inv) * outs[g] for g in range(N_GROUPS))
        o, _ = _one_query_attend(mq_ref[:, sl] * QK_SCALE, _load_head(mk_ref, h), _load_head(mv_ref, h))
        mem_ref[:, sl] = o


def _sample_attend(z3, caches_k, caches_v, cache_mk, cache_mv):
    bd = z3.shape[0]
    zspec = lambda col, n: pl.BlockSpec((None, 1, n * WIDTH), lambda b: (b, 0, col // n))
    cache_specs, cache_args = [], []
    for (win, dil), ck, cv in zip(ATT_GROUPS, caches_k, caches_v):
        rows = ck.shape[2]
        assert rows == win and rows // dil == BLOCK
        spec = pl.BlockSpec((None, None, BLOCK, None, HEADS, HEAD_DIM), lambda b: (0, b, 0, 0, 0, 0))
        for c in (ck, cv):
            cache_specs.append(spec)
            cache_args.append(c.reshape(1, bd, BLOCK, dil, HEADS, HEAD_DIM))
    mem_spec = pl.BlockSpec((None, None, MEM_LEN, HEADS, HEAD_DIM), lambda b: (0, b, 0, 0, 0))
    out_spec = pl.BlockSpec((None, 1, WIDTH), lambda b: (b, 0, 0))
    out_shape = jax.ShapeDtypeStruct((bd, 1, WIDTH), F32)
    return pl.pallas_call(
        _sample_att_body,
        grid=(bd,),
        in_specs=[zspec(COL_Q, 3), zspec(COL_K, 3), zspec(COL_V, 3), zspec(COL_MQ, 1)] + cache_specs
                 + [mem_spec, mem_spec],
        out_specs=[out_spec, out_spec],
        out_shape=[out_shape, out_shape],
        compiler_params=_params("parallel"),
        name="sample_attend",
    )(z3, z3, z3, z3, *cache_args, cache_mk, cache_mv)


SAMPLE_GROUP = 8


def _sample_hgrn_body(zq_ref, zf_ref, zi_ref, lb_ref, gain_ref, s_ref, o_ref, so_ref):
    lb = _lower_bound(lb_ref)
    gain = gain_ref[...]
    q = jax.nn.sigmoid(zq_ref[...])
    kk = (1.0 - lb) * jax.nn.sigmoid(zf_ref[...])
    f = lb + kk
    v = zi_ref[...]
    pad = jnp.zeros((HEAD_DIM - SAMPLE_GROUP, HEAD_DIM), F32)

    def columns(a, h):
        return jnp.concatenate([a[:, h * HEAD_DIM:(h + 1) * HEAD_DIM], pad], axis=0).T

    for h in range(HEADS):
        sl = slice(h * HEAD_DIM, (h + 1) * HEAD_DIM)
        q_t, k_t, f_t = columns(q, h), columns(kk, h), columns(f, h)
        for j in range(SAMPLE_GROUP):
            s1 = f_t[:, j:j + 1] * s_ref[j, h] + k_t[:, j:j + 1] * v[j:j + 1, sl]
            so_ref[j, h] = s1
            o = jnp.sum(q_t[:, j:j + 1] * s1, axis=0, keepdims=True)
            o_ref[j:j + 1, sl] = _rms(o, gain[:, sl])


def _sample_hgrn(z, lb_logits, gain, state):
    bd = z.shape[0]
    zspec = lambda col: pl.BlockSpec((SAMPLE_GROUP, WIDTH), lambda i: (i, col))
    sspec = pl.BlockSpec((None, SAMPLE_GROUP, HEADS, HEAD_DIM, HEAD_DIM), lambda i: (0, i, 0, 0, 0))
    return pl.pallas_call(
        _sample_hgrn_body,
        grid=(bd // SAMPLE_GROUP,),
        in_specs=[zspec(COL_HQ), zspec(COL_HF), zspec(COL_HI), _const_spec(lb_logits.shape),
                  _const_spec((1, WIDTH)), sspec],
        out_specs=[pl.BlockSpec((SAMPLE_GROUP, WIDTH), lambda i: (i, 0)), sspec],
        out_shape=[jax.ShapeDtypeStruct((bd, WIDTH), F32), jax.ShapeDtypeStruct(state.shape, F32)],
        compiler_params=_params("parallel"),
        name="sample_hgrn",
    )(z, z, z, lb_logits, gain, state)


def _gate_body(z_ref, o_ref):
    o_ref[...] = jax.nn.sigmoid(z_ref[...]).astype(BF16)


def _sample_gates(z):
    bd = z.shape[0]
    n = N_BRANCH * D_MODEL
    return pl.pallas_call(
        _gate_body,
        grid=(n // WIDTH,),
        in_specs=[pl.BlockSpec((bd, WIDTH), lambda j: (0, COL_GATE + j))],
        out_specs=pl.BlockSpec((bd, WIDTH), lambda j: (0, j)),
        out_shape=jax.ShapeDtypeStruct((bd, n), BF16),
        name="sample_gates",
    )(z)


def kernel(x_prompt, x_sample, mem_prompt, cache_win1_k, cache_win1_v, cache_win4_k, cache_win4_v, cache_win16_k, cache_win16_v, cache_mem_k, cache_mem_v, state_hgrn, g_ff1, w_ff1_gate, w_ff1_up, w_ff1_down, g_mix, w_in, hg_lb_logits, g_hg_out, g_mem, w_mem_k, w_mem_v, w_branch_att, w_branch_hg, w_branch_mem, w_out, g_ff2, w_ff2_gate, w_ff2_up, w_ff2_down, g_final):
    b, t, _ = x_prompt.shape
    bd = x_sample.shape[0]
    assert g_ff1.shape[0] == 1 and x_sample.shape[1] == 1 and w_in.shape[2] == IN_WIDTH
    bf = lambda w: w[0].astype(BF16)
    ffn1 = (g_ff1, bf(w_ff1_gate), bf(w_ff1_up), bf(w_ff1_down), g_final.reshape(1, D_MODEL))
    ffn2 = (g_ff2, bf(w_ff2_gate), bf(w_ff2_up), bf(w_ff2_down), g_final.reshape(1, D_MODEL))
    w_qkv = w_in[0, :, :COL_HQ * WIDTH].astype(BF16)
    w_rest = w_in[0, :, COL_HQ * WIDTH:].astype(BF16)
    branch_w = (bf(w_branch_att), bf(w_branch_hg), bf(w_branch_mem), bf(w_out))

    xs, _ = _ffn(x_sample.reshape(bd, D_MODEL), *ffn1, tm=bd, final=False)
    zs = _sample_in_proj(xs, g_mix, w_qkv, w_rest)
    caches_k = (cache_win1_k, cache_win4_k, cache_win16_k)
    caches_v = (cache_win1_v, cache_win4_v, cache_win16_v)
    z3 = zs.reshape(bd, 1, IN_WIDTH)
    att_s, mem_s = _sample_attend(z3, caches_k, caches_v, cache_mem_k, cache_mem_v)
    hg_s, s_hgrn = _sample_hgrn(zs, hg_lb_logits, g_hg_out, state_hgrn)
    xs = _merge(xs, att_s.reshape(bd, WIDTH), hg_s, mem_s.reshape(bd, WIDTH), _sample_gates(zs), *branch_w, tm=bd)
    y_sample = _ffn(xs, *ffn2, tm=bd, final=True)[0].reshape(bd, 1, D_MODEL)

    tm = b * t // bd
    xp, (s_win16_k,) = _ffn(x_prompt.reshape(b * t, D_MODEL), *ffn1, tm=tm, final=False, ride=(cache_win16_k,))
    qkv = _qkv_proj(xp.reshape(b, t, D_MODEL), g_mix, w_qkv, tm=512)
    q_g, k_g, v_g, p_k, p_v = qkv[0:3], qkv[3:6], qkv[6:9], qkv[9:12], qkv[12:15]
    (hq, hk, lf, hv, mq, gates), (s_win1_k, s_win1_v, s_win4_k, s_win4_v), outs, lses = _rest_att(
        xp, g_mix, w_rest, hg_lb_logits, q_g, k_g, v_g, tm=tm,
        ride=(cache_win1_k, cache_win1_v, cache_win4_k, cache_win4_v))
    hg_p, p_hgrn = _hgrn_prompt(hq, hk, lf, hv, g_hg_out, b, t)
    p_mem_k, p_mem_v, mk_b, mv_b = _mem_kv(mem_prompt, g_mem, bf(w_mem_k), bf(w_mem_v))
    xp = _mix(xp, outs, lses, mq, mk_b, mv_b, hg_p, gates, *branch_w, b, t, tm=512)
    y_prompt, (s_win16_v,) = _ffn(xp, *ffn2, tm=tm, final=True, ride=(cache_win16_v,))
    s_win1_k, s_win1_v, s_win4_k, s_win4_v, s_win16_k, s_win16_v = _write_new_rows(
        z3, [s_win1_k, s_win1_v, s_win4_k, s_win4_v, s_win16_k, s_win16_v],
        ((0, False), (0, True), (1, False), (1, True), (2, False), (2, True)))

    return (y_prompt.reshape(b, t, D_MODEL), y_sample,
            p_k[0], p_v[0], p_k[1], p_v[1], p_k[2], p_v[2],
            p_mem_k, p_mem_v, p_hgrn,
            s_win1_k, s_win1_v, s_win4_k, s_win4_v, s_win16_k, s_win16_v,
            s_hgrn)
```

```python
import functools

import numpy as np
import jax
import jax.numpy as jnp
from jax import lax
from jax.experimental import pallas as pl
from jax.experimental.pallas import tpu as pltpu

F32 = jnp.float32
BF16 = jnp.bfloat16

D_MODEL = 1024
D_FF = 2816
EPS = 1e-6
HEADS = 4
HEAD_DIM = 128
WIDTH = HEADS * HEAD_DIM
ATT_GROUPS = ((128, 1), (512, 4), (2048, 16))
N_GROUPS = 3
BLOCK = 128
MEM_LEN = 256
N_BRANCH = 3
LSE_REP = 32
QK_SCALE = HEAD_DIM ** -0.5
VMEM_LIMIT_BYTES = 56 * 1024 * 1024

COL_Q, COL_K, COL_V, COL_HQ, COL_HF, COL_HI, COL_MQ, COL_GATE = 0, 3, 6, 9, 10, 11, 12, 13
IN_WIDTH = (COL_GATE + N_BRANCH * D_MODEL // WIDTH) * WIDTH


def _params(*sem):
    return pltpu.CompilerParams(dimension_semantics=sem, vmem_limit_bytes=VMEM_LIMIT_BYTES)


def _const_spec(shape):
    nd = len(shape)
    return pl.BlockSpec(shape, lambda *_: (0,) * nd, pipeline_mode=pl.Buffered(1))


def _rms(x, g):
    return x * lax.rsqrt(jnp.mean(x * x, axis=-1, keepdims=True) + EPS) * g


def _dot(a, b):
    return jnp.dot(a, b, preferred_element_type=F32)


def _dot_nt(a, b):
    return lax.dot_general(a, b, (((1,), (1,)), ((), ())), preferred_element_type=F32)


def _ride_specs(caches, n_steps):
    if not caches:
        return [], [], [], []
    assert all(c.shape[1] == n_steps for c in caches)
    in_specs = [pl.BlockSpec((1, 1, c.shape[2], HEADS, HEAD_DIM), lambda i: (0, i, 0, 0, 0)) for c in caches]
    out_specs = [pl.BlockSpec(memory_space=pl.ANY)] * len(caches)
    shapes = [jax.ShapeDtypeStruct(c.shape, c.dtype) for c in caches]
    return in_specs, out_specs, shapes, [pltpu.SemaphoreType.DMA((2 * len(caches),))]


def _ride_copies(in_refs, out_refs, sem):
    seq = pl.program_id(0)
    copies = []
    for j, (cache_ref, out_ref) in enumerate(zip(in_refs, out_refs)):
        rows = cache_ref.shape[2]
        copies.append(pltpu.make_async_copy(cache_ref.at[:, :, pl.ds(1, rows - 1)],
                                            out_ref.at[:, pl.ds(seq, 1), pl.ds(0, rows - 1)], sem.at[2 * j]))
        copies.append(pltpu.make_async_copy(cache_ref.at[:, :, pl.ds(rows - 1, 1)],
                                            out_ref.at[:, pl.ds(seq, 1), pl.ds(rows - 1, 1)], sem.at[2 * j + 1]))
    for cp in copies:
        cp.start()
    return copies


def _new_rows_body(kn_ref, vn_ref, *refs, meta):
    n = len(meta)
    out_refs, row_refs, sem = refs[n:2 * n], refs[2 * n:3 * n], refs[3 * n]
    copies = []
    for j, ((g, is_v), out_ref, row_ref) in enumerate(zip(meta, out_refs, row_refs)):
        bd, rows = out_ref.shape[1], out_ref.shape[2]
        new = (vn_ref if is_v else kn_ref)[:, 0, g * WIDTH:(g + 1) * WIDTH]
        flat = row_ref.reshape(bd * HEADS, HEAD_DIM)
        for h in range(HEADS):
            flat[pl.ds(h, bd, stride=HEADS), :] = new[:, h * HEAD_DIM:(h + 1) * HEAD_DIM]
        copies.append(pltpu.make_async_copy(row_ref, out_ref.at[:, :, pl.ds(rows - 1, 1)], sem.at[j]))
        copies[-1].start()
    for cp in copies:
        cp.wait()


def _write_new_rows(z3, shifted, meta):
    bd = z3.shape[0]
    n = len(shifted)
    zspec = lambda col: pl.BlockSpec((bd, 1, N_GROUPS * WIDTH), lambda i: (0, 0, col // N_GROUPS))
    any_spec = pl.BlockSpec(memory_space=pl.ANY)
    return pl.pallas_call(
        functools.partial(_new_rows_body, meta=meta),
        grid=(1,),
        in_specs=[zspec(COL_K), zspec(COL_V)] + [any_spec] * n,
        out_specs=[any_spec] * n,
        out_shape=[jax.ShapeDtypeStruct(c.shape, c.dtype) for c in shifted],
        input_output_aliases={2 + j: j for j in range(n)},
        scratch_shapes=[pltpu.VMEM((1, bd, 1, HEADS, HEAD_DIM), F32)] * n + [pltpu.SemaphoreType.DMA((n,))],
        name="write_new_rows",
    )(z3, z3, *shifted)


WEIGHT_CHUNKS = 8


def _load_bf16(w_hbm, dst_ref, stage_ref, sem):
    chunk = stage_ref.shape[2]

    def copy(c):
        return pltpu.make_async_copy(w_hbm.at[:, pl.ds(c * chunk, chunk)], stage_ref.at[c % 2], sem.at[c % 2])

    copy(0).start()
    for c in range(WEIGHT_CHUNKS):
        if c + 1 < WEIGHT_CHUNKS:
            copy(c + 1).start()
        copy(c).wait()
        dst_ref[c * chunk:(c + 1) * chunk, :] = stage_ref[c % 2, 0].astype(BF16)


def _ffn_body(x_ref, g_ref, wg_hbm, wu_hbm, wd_hbm, gf_ref, *rest, final, n_ride):
    o_ref = rest[n_ride]
    wg_ref, wu_ref, wd_ref, stage_in, stage_out, w_sem = rest[2 * n_ride + 1:2 * n_ride + 7]
    copies = _ride_copies(rest[:n_ride], rest[n_ride + 1:2 * n_ride + 1], rest[-1]) if n_ride else []

    @pl.when(pl.program_id(0) == 0)
    def _():
        _load_bf16(wg_hbm, wg_ref, stage_in, w_sem)
        _load_bf16(wu_hbm, wu_ref, stage_in, w_sem)
        _load_bf16(wd_hbm, wd_ref, stage_out, w_sem)

    x = x_ref[...]
    h = _rms(x, g_ref[...]).astype(BF16)
    a = _dot(h, wg_ref[...])
    u = _dot(h, wu_ref[...])
    act = (a * jax.nn.sigmoid(a) * u).astype(BF16)
    y = x + 0.5 * _dot(act, wd_ref[...])
    if final:
        y = _rms(y, gf_ref[...])
    o_ref[...] = y
    for cp in copies:
        cp.wait()


def _ffn(x, g, wg, wu, wd, g_final, *, tm, final, ride=()):
    m = x.shape[0]
    r_in, r_out, r_shapes, r_scratch = _ride_specs(ride, m // tm)
    any_spec = pl.BlockSpec(memory_space=pl.ANY)
    res = pl.pallas_call(
        functools.partial(_ffn_body, final=final, n_ride=len(ride)),
        grid=(m // tm,),
        in_specs=[pl.BlockSpec((tm, D_MODEL), lambda i: (i, 0)),
                  _const_spec((1, D_MODEL)), any_spec, any_spec, any_spec,
                  _const_spec((1, D_MODEL))] + r_in,
        out_specs=[pl.BlockSpec((tm, D_MODEL), lambda i: (i, 0))] + r_out,
        out_shape=[jax.ShapeDtypeStruct((m, D_MODEL), F32)] + r_shapes,
        scratch_shapes=[pltpu.VMEM((D_MODEL, D_FF), BF16), pltpu.VMEM((D_MODEL, D_FF), BF16),
                        pltpu.VMEM((D_FF, D_MODEL), BF16),
                        pltpu.VMEM((2, 1, D_MODEL // WEIGHT_CHUNKS, D_FF), F32),
                        pltpu.VMEM((2, 1, D_FF // WEIGHT_CHUNKS, D_MODEL), F32),
                        pltpu.SemaphoreType.DMA((2,))] + r_scratch,
        compiler_params=_params("arbitrary"),
        name="ffn_final" if final else "ffn",
    )(x, g, wg, wu, wd, g_final, *ride)
    return res[0], res[1:]


def _store_heads(ref, z):
    rows = ref.shape[0]
    flat = ref.reshape(rows * HEADS, HEAD_DIM)
    for h in range(HEADS):
        flat[pl.ds(h, rows, stride=HEADS), :] = z[:, h * HEAD_DIM:(h + 1) * HEAD_DIM]


def _load_head(ref, h):
    rows = ref.shape[0]
    return ref.reshape(rows * HEADS, HEAD_DIM)[pl.ds(h, rows, stride=HEADS), :]


def _to_strided_view(dst_ref, z, stage_ref, dil):
    if dil == 1:
        dst_ref[...] = z.astype(dst_ref.dtype)
        return
    n = z.shape[0] // dil
    for c in range(stage_ref.shape[0]):
        stage_ref[c] = z[:, c * 128:(c + 1) * 128]
    for r in range(dil):
        for c in range(stage_ref.shape[0]):
            dst_ref[:, r * WIDTH + c * 128:r * WIDTH + (c + 1) * 128] = (
                stage_ref[c, pl.ds(r, n, stride=dil), :].astype(dst_ref.dtype))


def _qkv_body(x_ref, g_ref, w_ref, *refs, tm, n_tiles):
    q_refs, k_refs, v_refs = refs[0:3], refs[3:6], refs[6:9]
    pk_refs, pv_refs, stage_refs = refs[9:12], refs[12:15], refs[15:]
    h = _rms(x_ref[...], g_ref[...]).astype(BF16)
    late = []
    for g, (win, dil) in enumerate(ATT_GROUPS):
        keep = min(win, tm * n_tiles)
        for kind, (col, dst_refs, win_refs) in enumerate(((COL_Q, q_refs, None), (COL_K, k_refs, pk_refs),
                                                          (COL_V, v_refs, pv_refs))):
            z = _dot(h, w_ref[:, (col + g) * WIDTH:(col + g + 1) * WIDTH])
            if win_refs is None:
                z = z * QK_SCALE
            stage = stage_refs[3 * (g - 1) + kind] if dil > 1 else None
            _to_strided_view(dst_refs[g], z, stage, dil)
            if win_refs is None:
                continue
            if keep == tm * n_tiles:
                _store_heads(win_refs[g], z)
            elif dil == 1:
                late.append((win_refs[g], z[tm - keep:, :], None))
            else:
                assert keep == tm
                late.append((win_refs[g], None, stage))

    @pl.when(pl.program_id(1) == n_tiles - 1)
    def _():
        for ref, rows, stage in late:
            if stage is None:
                _store_heads(ref, rows)
            else:
                _store_heads(ref, jnp.concatenate([stage[hd] for hd in range(HEADS)], axis=-1))


def _qkv_proj(x, g_mix, w_qkv, *, tm):
    b, t, _ = x.shape
    n_tiles = t // tm
    view_specs = [pl.BlockSpec((None, tm // d, d * WIDTH), lambda bi, i: (bi, i, 0)) for _, d in ATT_GROUPS]
    view_shapes = [jax.ShapeDtypeStruct((b, t // d, d * WIDTH), BF16) for _, d in ATT_GROUPS]
    win_specs, win_shapes = [], []
    for win, _ in ATT_GROUPS:
        keep = min(win, t)
        assert keep == t or keep <= tm
        if keep == t:
            win_specs.append(pl.BlockSpec((None, None, tm, HEADS, HEAD_DIM), lambda bi, i: (0, bi, i, 0, 0)))
        else:
            win_specs.append(pl.BlockSpec((None, None, keep, HEADS, HEAD_DIM), lambda bi, i: (0, bi, 0, 0, 0)))
        win_shapes.append(jax.ShapeDtypeStruct((1, b, keep, HEADS, HEAD_DIM), F32))
    return pl.pallas_call(
        functools.partial(_qkv_body, tm=tm, n_tiles=n_tiles),
        grid=(b, n_tiles),
        in_specs=[pl.BlockSpec((None, tm, D_MODEL), lambda bi, i: (bi, i, 0)),
                  _const_spec((1, D_MODEL)),
                  _const_spec((D_MODEL, 3 * N_GROUPS * WIDTH))],
        out_specs=view_specs * 3 + win_specs + win_specs,
        out_shape=view_shapes * 3 + win_shapes + win_shapes,
        scratch_shapes=[pltpu.VMEM((WIDTH // 128, tm, 128), F32)] * (3 * sum(d > 1 for _, d in ATT_GROUPS)),
        compiler_params=_params("parallel", "arbitrary"),
        name="qkv_proj",
    )(x, g_mix, w_qkv)


def _lower_bound(logits_ref):
    lg = logits_ref[...]
    e = jnp.exp(lg - jnp.max(lg, axis=0, keepdims=True))
    return e[0:1, :] / jnp.sum(e, axis=0, keepdims=True)


def _rest_steps(x_ref, g_ref, w_ref, lb_ref, hq_ref, hk_ref, lf_ref, hv_ref, mq_ref, gate_ref):
    h = _rms(x_ref[...], g_ref[...]).astype(BF16)
    lb = _lower_bound(lb_ref)
    proj = lambda c: _dot(h, w_ref[:, c * WIDTH:(c + 1) * WIDTH])

    def hgrn_query():
        hq_ref[...] = jax.nn.sigmoid(proj(0)).astype(BF16)

    def hgrn_key():
        kk = (1.0 - lb) * jax.nn.sigmoid(proj(1))
        hk_ref[...] = kk.astype(BF16)
        lf_ref[...] = jnp.log(lb + kk)

    def hgrn_value():
        hv_ref[...] = proj(2).astype(BF16)

    def mem_query():
        mq_ref[...] = (proj(3) * QK_SCALE).astype(BF16)

    def gate(c):
        gate_ref[:, c * WIDTH:(c + 1) * WIDTH] = jax.nn.sigmoid(proj(4 + c)).astype(BF16)

    return [hgrn_query, hgrn_key, hgrn_value, mem_query] + [functools.partial(gate, c)
                                                            for c in range(N_BRANCH * D_MODEL // WIDTH)]


def _rest_att_body(x_ref, g_ref, w_ref, lb_ref, *rest, n_ride, tiles, tiles_per_seq):
    n_att_in = sum(5 if has_halo else 3 for _, _, has_halo in tiles)
    caches, rest = rest[:n_ride], rest[n_ride:]
    att_in, rest = rest[:n_att_in], rest[n_att_in:]
    proj_out, rest = rest[:6], rest[6:]
    ride_out, rest = rest[:n_ride], rest[n_ride:]
    att_out, scratch = rest[:2 * len(tiles)], rest[2 * len(tiles):]
    copies = _ride_copies(caches, ride_out, scratch[-1]) if n_ride else []
    steps = _rest_steps(x_ref, g_ref, w_ref, lb_ref, *proj_out)
    gens = _att_tiles(att_in + att_out, tiles, pl.program_id(0) % tiles_per_seq)
    steps.pop(0)()
    for gen in gens:
        next(gen)
        steps.pop(0)()
        next(gen)
        steps.pop(0)()
        next(gen, None)
    for step in steps:
        step()
    for cp in copies:
        cp.wait()


def _rest_att(x, g_mix, w_rest, lb_logits, q_g, k_g, v_g, *, tm, ride=()):
    m = x.shape[0]
    spec = pl.BlockSpec((tm, WIDTH), lambda i: (i, 0))
    bf = jax.ShapeDtypeStruct((m, WIDTH), BF16)
    r_in, r_out, r_shapes, r_scratch = _ride_specs(ride, m // tm)
    tiles_per_seq = q_g[0].shape[1] // (ATT_UNITS * BLOCK)
    a_in, a_args, a_out, a_shapes, tiles, n_tiles = _attention_specs(
        q_g, k_g, v_g, lambda i: (i // tiles_per_seq, i % tiles_per_seq))
    assert n_tiles == tiles_per_seq and q_g[0].shape[0] * n_tiles == m // tm
    res = pl.pallas_call(
        functools.partial(_rest_att_body, n_ride=len(ride), tiles=tiles, tiles_per_seq=tiles_per_seq),
        grid=(m // tm,),
        in_specs=[pl.BlockSpec((tm, D_MODEL), lambda i: (i, 0)),
                  _const_spec((1, D_MODEL)),
                  _const_spec(w_rest.shape),
                  _const_spec(lb_logits.shape)] + r_in + a_in,
        out_specs=[spec, spec, spec, spec, spec, pl.BlockSpec((tm, N_BRANCH * D_MODEL), lambda i: (i, 0))]
                  + r_out + a_out,
        out_shape=[bf, bf, jax.ShapeDtypeStruct((m, WIDTH), F32), bf, bf,
                   jax.ShapeDtypeStruct((m, N_BRANCH * D_MODEL), BF16)] + r_shapes + a_shapes,
        scratch_shapes=r_scratch,
        compiler_params=_params("arbitrary"),
        name="rest_att",
    )(x, g_mix, w_rest, lb_logits, *ride, *a_args)
    n = 6 + len(ride)
    return res[:6], res[6:n], res[n::2], res[n + 1::2]


ATT_UNITS = 4
STAT_REP = LSE_REP // 2


def _att_tiles(refs, tiles, tile_index):
    n_in = [5 if has_halo else 3 for _, _, has_halo in tiles]
    outs = refs[sum(n_in):]
    gens, pos = [], 0
    for g, (n_row_blocks, n_col_blocks, has_halo) in enumerate(tiles):
        gens.append(_att_tile(*refs[pos:pos + n_in[g]], outs[2 * g], outs[2 * g + 1], n_row_blocks=n_row_blocks,
                              n_col_blocks=n_col_blocks, has_halo=has_halo, tile_index=tile_index))
        pos += n_in[g]
    return gens


def _att_tile(q_ref, k_ref, v_ref, *rest, n_row_blocks, n_col_blocks, has_halo, tile_index):
    if has_halo:
        kh_ref, vh_ref, o_ref, st_ref = rest
        halo_ok = tile_index > 0
    else:
        o_ref, st_ref = rest
    qi = lax.broadcasted_iota(jnp.int32, (BLOCK, 2 * BLOCK), 0)
    ki = lax.broadcasted_iota(jnp.int32, (BLOCK, 2 * BLOCK), 1)
    own_ok = (lax.broadcasted_iota(jnp.int32, (BLOCK, BLOCK), 1)
              <= lax.broadcasted_iota(jnp.int32, (BLOCK, BLOCK), 0))
    both_ok = jnp.where(ki < BLOCK, ki, qi + BLOCK) >= jnp.where(ki < BLOCK, qi, ki)
    if has_halo:
        first_valid = jnp.where(halo_ok, 0, BLOCK)
        halo_both_ok = jnp.logical_and(both_ok, ki >= first_valid)

    units = []
    for c in range(n_col_blocks):
        for u in range(n_row_blocks):
            for h in range(HEADS):
                rows = slice(u * BLOCK, (u + 1) * BLOCK)
                cols = slice(c * WIDTH + h * HEAD_DIM, c * WIDTH + (h + 1) * HEAD_DIM)
                lcols = (c * HEADS + h) * LSE_REP
                if u > 0:
                    with_prev = slice((u - 1) * BLOCK, (u + 1) * BLOCK)
                    keys = lambda ref, rr=with_prev, cc=cols: ref[rr, cc]
                    ok = both_ok
                elif has_halo:
                    hcols = slice(h * HEAD_DIM, (h + 1) * HEAD_DIM)
                    keys = lambda ref, rr=rows, cc=cols, hc=hcols: jnp.concatenate(
                        [(kh_ref if ref is k_ref else vh_ref)[:, hc], ref[rr, cc]], axis=0)
                    ok = halo_both_ok
                else:
                    keys = lambda ref, rr=rows, cc=cols: ref[rr, cc]
                    ok = own_ok
                units.append((rows, cols, lcols, keys, ok))

    scores = [jnp.where(ok, _dot_nt(q_ref[rows, cols], keys(k_ref)), -jnp.inf) for rows, cols, _, keys, ok in units]
    yield

    probs = []
    for s in scores:
        m = jnp.max(s, axis=-1, keepdims=True)
        p = jnp.exp(s - m)
        probs.append((p.astype(BF16), m, jnp.sum(p, axis=-1, keepdims=True)))
    yield

    for (rows, cols, lcols, keys, _), (p, m, l) in zip(units, probs):
        o_ref[rows, cols] = _dot(p, keys(v_ref)).astype(BF16)
        st_ref[rows, lcols:lcols + STAT_REP] = jnp.broadcast_to(m, (BLOCK, STAT_REP))
        st_ref[rows, lcols + STAT_REP:lcols + LSE_REP] = jnp.broadcast_to(l, (BLOCK, STAT_REP))


def _attention_specs(q_g, k_g, v_g, to_bj):
    b = q_g[0].shape[0]
    lw = HEADS * LSE_REP
    n_steps = None
    in_specs, args, out_specs, out_shapes, tiles = [], [], [], [], []

    def spec(shape, index):
        return pl.BlockSpec(shape, lambda *grid_idx: index(*to_bj(*grid_idx)))

    for (_, dil), q, k, v in zip(ATT_GROUPS, q_g, k_g, v_g):
        sub_len = q.shape[1]
        n_row_blocks = min(ATT_UNITS, sub_len // BLOCK)
        n_col_blocks = ATT_UNITS // n_row_blocks
        span = n_row_blocks * BLOCK
        assert sub_len % span == 0 and dil % n_col_blocks == 0
        has_halo = sub_len > span
        assert not has_halo or (dil == 1 and n_col_blocks == 1)
        n = (sub_len // span) * (dil // n_col_blocks)
        assert n_steps in (None, n)
        n_steps = n
        if has_halo:
            tile = lambda w, span=span: spec((None, span, w), lambda bi, j: (bi, j, 0))
        else:
            tile = lambda w, span=span, nc=n_col_blocks: spec((None, span, nc * w), lambda bi, j: (bi, 0, j))
        halo = spec((None, BLOCK, WIDTH), lambda bi, j, nr=n_row_blocks: (bi, jnp.maximum(j * nr - 1, 0), 0))
        in_specs += [tile(WIDTH)] * 3 + ([halo, halo] if has_halo else [])
        args += [q, k, v] + ([k, v] if has_halo else [])
        out_specs += [tile(WIDTH), tile(lw)]
        out_shapes += [jax.ShapeDtypeStruct((b, sub_len, dil * WIDTH), BF16),
                       jax.ShapeDtypeStruct((b, sub_len, dil * lw), F32)]
        tiles.append((n_row_blocks, n_col_blocks, has_halo))
    return in_specs, args, out_specs, out_shapes, tuple(tiles), n_steps


N_LEVELS = 7
ROW_CUM, ROW_REM = N_LEVELS, N_LEVELS + 1


def _hgrn_constants():
    r = np.arange(BLOCK)[:, None]
    s = np.arange(BLOCK)[None, :]
    mats = []
    for lvl in range(N_LEVELS):
        half = 1 << lvl
        mid = (r // (2 * half)) * 2 * half + half
        upper = (r % (2 * half)) >= half
        mats.append(np.where(upper, (s >= mid) & (s <= r), (s > r) & (s <= mid - 1)))
    mats.append(s <= r)
    mats.append(s > r)
    range_mat = np.concatenate(mats, axis=0).astype(np.float32)
    range_mat = np.concatenate([range_mat, range_mat], axis=1)
    x = r ^ s
    level = np.where(s < r, np.floor(np.log2(np.maximum(x, 1))).astype(np.int32), np.where(s == r, N_LEVELS, -1))
    return jnp.asarray(range_mat, BF16), jnp.asarray(level, jnp.int32)


HGRN_SEQS = 4


def _hgrn_intra(q_b, k_b, lf, rng, level):
    lf_hi = lf.astype(BF16)
    lf_lo = (lf - lf_hi.astype(F32)).astype(BF16)
    z = _dot(rng, jnp.concatenate([lf_hi, lf_lo], axis=0))
    q = q_b.astype(F32)
    kk = k_b.astype(F32)
    lhs, rhs = [], []
    for lvl in range(N_LEVELS):
        x = jnp.exp(z[lvl * BLOCK:(lvl + 1) * BLOCK])
        lhs.append((q * x).astype(BF16))
        rhs.append((kk * x).astype(BF16))
    lhs.append(q_b)
    rhs.append(k_b)
    b_cum = z[ROW_CUM * BLOCK:(ROW_CUM + 1) * BLOCK]
    q_in = (q * jnp.exp(b_cum)).astype(BF16)
    k_out = (kk * jnp.exp(z[ROW_REM * BLOCK:(ROW_REM + 1) * BLOCK])).astype(BF16)
    decay = jnp.exp(b_cum[BLOCK - 1:BLOCK, :])
    mats = []
    for h in range(HEADS):
        sl = slice(h * HEAD_DIM, (h + 1) * HEAD_DIM)
        a = jnp.zeros((BLOCK, BLOCK), F32)
        for lvl in range(N_LEVELS + 1):
            a = jnp.where(level == lvl, _dot_nt(lhs[lvl][:, sl], rhs[lvl][:, sl]), a)
        mats.append(a.astype(BF16))
    return mats, q_in, k_out, decay


def _hgrn_state(mats, q_in, k_out, decay, v_b, gain, st_ref, j):
    outs = []
    for h in range(HEADS):
        sl = slice(h * HEAD_DIM, (h + 1) * HEAD_DIM)
        v = v_b[:, sl]
        st = st_ref[j, h]
        o = _dot(mats[h], v) + _dot_nt(q_in[:, sl], st.astype(BF16))
        outs.append(_rms(o, gain[:, sl]).astype(BF16))
        v_t = v.astype(F32).T.astype(BF16)
        st_ref[j, h] = st * decay[:, sl] + _dot(v_t, k_out[:, sl])
    return jnp.concatenate(outs, axis=-1)


def _hgrn_body(q_ref, k_ref, lf_ref, v_ref, rng_ref, lvl_ref, gain_ref, o_ref, s_ref, st_ref, *, n_chunks):
    c = pl.program_id(1)

    @pl.when(c == 0)
    def _():
        st_ref[...] = jnp.zeros_like(st_ref)

    rng, level, gain = rng_ref[...], lvl_ref[...], gain_ref[...]
    intra = [_hgrn_intra(q_ref[j], k_ref[j], lf_ref[j], rng, level) for j in range(HGRN_SEQS)]
    for j in range(HGRN_SEQS):
        o_ref[j] = _hgrn_state(*intra[j], v_ref[j], gain, st_ref, j)

    @pl.when(c == n_chunks - 1)
    def _():
        for j in range(HGRN_SEQS):
            for h in range(HEADS):
                s_ref[j, h] = st_ref[j, h].T


def _hgrn_prompt(hq, hk, lf, hv, gain, b, t):
    n_chunks = t // BLOCK
    range_mat, level = _hgrn_constants()
    as3d = lambda a: a.reshape(b, t, WIDTH)
    spec = pl.BlockSpec((HGRN_SEQS, BLOCK, WIDTH), lambda bi, c: (bi, c, 0))
    state_spec = pl.BlockSpec((None, HGRN_SEQS, HEADS, HEAD_DIM, HEAD_DIM), lambda bi, c: (0, bi, 0, 0, 0))
    o, state = pl.pallas_call(
        functools.partial(_hgrn_body, n_chunks=n_chunks),
        grid=(b // HGRN_SEQS, n_chunks),
        in_specs=[spec, spec, spec, spec, _const_spec(range_mat.shape), _const_spec(level.shape),
                  _const_spec((1, WIDTH))],
        out_specs=[spec, state_spec],
        out_shape=[jax.ShapeDtypeStruct((b, t, WIDTH), BF16),
                   jax.ShapeDtypeStruct((1, b, HEADS, HEAD_DIM, HEAD_DIM), F32)],
        scratch_shapes=[pltpu.VMEM((HGRN_SEQS, HEADS, HEAD_DIM, HEAD_DIM), F32)],
        compiler_params=_params("parallel", "arbitrary"),
        name="hgrn_prompt",
    )(as3d(hq), as3d(hk), as3d(lf), as3d(hv), range_mat, level, gain)
    return o.reshape(b * t, WIDTH), state


def _memkv_body(m_ref, g_ref, wk_ref, wv_ref, pk_ref, pv_ref, kb_ref, vb_ref):
    h = _rms(m_ref[...], g_ref[...]).astype(BF16)
    k = _dot(h, wk_ref[...])
    v = _dot(h, wv_ref[...])
    _store_heads(pk_ref, k)
    _store_heads(pv_ref, v)
    kb_ref[...] = k.astype(BF16)
    vb_ref[...] = v.astype(BF16)


def _mem_kv(mem, g_mem, w_mk, w_mv):
    b = mem.shape[0]
    head_spec = pl.BlockSpec((None, None, MEM_LEN, HEADS, HEAD_DIM), lambda bi: (0, bi, 0, 0, 0))
    flat_spec = pl.BlockSpec((None, MEM_LEN, WIDTH), lambda bi: (bi, 0, 0))
    head_shape = jax.ShapeDtypeStruct((1, b, MEM_LEN, HEADS, HEAD_DIM), F32)
    flat_shape = jax.ShapeDtypeStruct((b, MEM_LEN, WIDTH), BF16)
    return pl.pallas_call(
        _memkv_body,
        grid=(b,),
        in_specs=[pl.BlockSpec((None, MEM_LEN, D_MODEL), lambda bi: (bi, 0, 0)),
                  _const_spec((1, D_MODEL)), _const_spec((D_MODEL, WIDTH)), _const_spec((D_MODEL, WIDTH))],
        out_specs=[head_spec, head_spec, flat_spec, flat_spec],
        out_shape=[head_shape, head_shape, flat_shape, flat_shape],
        compiler_params=_params("parallel"),
        name="mem_kv",
    )(mem, g_mem, w_mk, w_mv)


def _from_strided_view(src_ref, stage_ref, dil, width):
    if dil == 1:
        return src_ref[...].astype(F32)
    n = src_ref.shape[0]
    n_col = width // 128
    for r in range(dil):
        for c in range(n_col):
            stage_ref[c, pl.ds(r, n, stride=dil), :] = (
                src_ref[:, r * width + c * 128:r * width + (c + 1) * 128].astype(F32))
    return jnp.concatenate([stage_ref[c] for c in range(n_col)], axis=-1)


def _gated_merge(x, att, hg, mem, gate_ref, wa_ref, wb_ref, wc_ref, wo_ref):
    def gate(j):
        return gate_ref[:, j * D_MODEL:(j + 1) * D_MODEL].astype(F32)

    m = gate(0) * _dot(att, wa_ref[...]) + gate(1) * _dot(hg, wb_ref[...]) + gate(2) * _dot(mem, wc_ref[...])
    return x + _dot(m.astype(BF16), wo_ref[...])


def _mix_body(x_ref, o0_ref, o1_ref, o2_ref, s0_ref, s1_ref, s2_ref, mq_ref, mk_ref, mv_ref, hg_ref, gate_ref,
              wa_ref, wb_ref, wc_ref, wo_ref, out_ref, *stage_refs):
    def gate(j):
        return gate_ref[:, j * D_MODEL:(j + 1) * D_MODEL].astype(F32)

    heads = [slice(h * HEAD_DIM, (h + 1) * HEAD_DIM) for h in range(HEADS)]
    merged = gate(1) * _dot(hg_ref[...], wb_ref[...])
    scores = [_dot_nt(mq_ref[:, sl], mk_ref[:, sl]) for sl in heads]
    probs = []
    for s in scores:
        p = jnp.exp(s - jnp.max(s, axis=-1, keepdims=True))
        probs.append((p.astype(BF16), 1.0 / jnp.sum(p, axis=-1, keepdims=True)))
    mem = [(_dot(p, mv_ref[:, sl]) * inv).astype(BF16) for (p, inv), sl in zip(probs, heads)]
    merged = merged + gate(2) * _dot(jnp.concatenate(mem, axis=-1), wc_ref[...])

    accs = [_from_strided_view(r, stage_refs[2 * g], ATT_GROUPS[g][1], WIDTH)
            for g, r in enumerate((o0_ref, o1_ref, o2_ref))]
    stats = [_from_strided_view(r, stage_refs[2 * g + 1], ATT_GROUPS[g][1], HEADS * LSE_REP)
             for g, r in enumerate((s0_ref, s1_ref, s2_ref))]
    att = []
    for h, sl in enumerate(heads):
        ms = [x[:, h * LSE_REP:h * LSE_REP + 1] for x in stats]
        ls = [x[:, h * LSE_REP + STAT_REP:h * LSE_REP + STAT_REP + 1] for x in stats]
        m = jnp.maximum(jnp.maximum(ms[0], ms[1]), ms[2])
        e = [jnp.exp(x - m) for x in ms]
        inv = 1.0 / (e[0] * ls[0] + e[1] * ls[1] + e[2] * ls[2])
        att.append(sum((e[g] * inv) * accs[g][:, sl] for g in range(N_GROUPS)).astype(BF16))
    merged = merged + gate(0) * _dot(jnp.concatenate(att, axis=-1), wa_ref[...])
    out_ref[...] = x_ref[...] + _dot(merged.astype(BF16), wo_ref[...])


def _mix(x, accs, stats, mq, mk, mv, hg, gates, wa, wb, wc, wo, b, t, *, tm):
    n_tiles = t // tm
    lw = HEADS * LSE_REP
    rows = lambda w: pl.BlockSpec((tm, w), lambda bi, i: (bi * n_tiles + i, 0))
    o_specs = [pl.BlockSpec((None, tm // d, d * WIDTH), lambda bi, i: (bi, i, 0)) for _, d in ATT_GROUPS]
    s_specs = [pl.BlockSpec((None, tm // d, d * lw), lambda bi, i: (bi, i, 0)) for _, d in ATT_GROUPS]
    memspec = pl.BlockSpec((None, MEM_LEN, WIDTH), lambda bi, i: (bi, 0, 0))
    stages = []
    for _ in ATT_GROUPS:
        stages += [pltpu.VMEM((WIDTH // 128, tm, 128), F32), pltpu.VMEM((lw // 128, tm, 128), F32)]
    return pl.pallas_call(
        _mix_body,
        grid=(b, n_tiles),
        in_specs=[rows(D_MODEL)] + o_specs + s_specs + [rows(WIDTH), memspec, memspec, rows(WIDTH),
                                                         rows(N_BRANCH * D_MODEL),
                                                         _const_spec((WIDTH, D_MODEL)), _const_spec((WIDTH, D_MODEL)),
                                                         _const_spec((WIDTH, D_MODEL)), _const_spec((D_MODEL, D_MODEL))],
        out_specs=rows(D_MODEL),
        out_shape=jax.ShapeDtypeStruct((b * t, D_MODEL), F32),
        scratch_shapes=stages,
        compiler_params=_params("parallel", "parallel"),
        name="mix",
    )(x, *accs, *stats, mq, mk, mv, hg, gates, wa, wb, wc, wo)


def _merge_body(x_ref, att_ref, hg_ref, mem_ref, gate_ref, wa_ref, wb_ref, wc_ref, wo_ref, o_ref):
    o_ref[...] = _gated_merge(x_ref[...], att_ref[...].astype(BF16), hg_ref[...].astype(BF16),
                              mem_ref[...].astype(BF16), gate_ref, wa_ref, wb_ref, wc_ref, wo_ref)


def _merge(x, att, hg, mem, gates, wa, wb, wc, wo, *, tm):
    m = x.shape[0]
    row = pl.BlockSpec((tm, WIDTH), lambda i: (i, 0))
    wide = pl.BlockSpec((tm, D_MODEL), lambda i: (i, 0))
    return pl.pallas_call(
        _merge_body,
        grid=(m // tm,),
        in_specs=[wide, row, row, row, pl.BlockSpec((tm, N_BRANCH * D_MODEL), lambda i: (i, 0)),
                  _const_spec((WIDTH, D_MODEL)), _const_spec((WIDTH, D_MODEL)), _const_spec((WIDTH, D_MODEL)),
                  _const_spec((D_MODEL, D_MODEL))],
        out_specs=wide,
        out_shape=jax.ShapeDtypeStruct((m, D_MODEL), F32),
        compiler_params=_params("parallel"),
        name="merge",
    )(x, att, hg, mem, gates, wa, wb, wc, wo)


def _norm_matmul_body(x_ref, g_ref, wa_ref, wb_ref, o_ref, *, n_a):
    h = _rms(x_ref[...], g_ref[...]).astype(BF16)

    @pl.when(pl.program_id(0) < n_a)
    def _():
        o_ref[...] = _dot(h, wa_ref[...])

    @pl.when(pl.program_id(0) >= n_a)
    def _():
        o_ref[...] = _dot(h, wb_ref[...])


def _sample_in_proj(x, g_mix, w_a, w_b):
    m = x.shape[0]
    n_a, n_b = w_a.shape[1] // WIDTH, w_b.shape[1] // WIDTH
    return pl.pallas_call(
        functools.partial(_norm_matmul_body, n_a=n_a),
        grid=(n_a + n_b,),
        in_specs=[_const_spec((m, D_MODEL)), _const_spec((1, D_MODEL)),
                  pl.BlockSpec((D_MODEL, WIDTH), lambda j: (0, jnp.minimum(j, n_a - 1))),
                  pl.BlockSpec((D_MODEL, WIDTH), lambda j: (0, jnp.maximum(j - n_a, 0)))],
        out_specs=pl.BlockSpec((m, WIDTH), lambda j: (0, j)),
        out_shape=jax.ShapeDtypeStruct((m, (n_a + n_b) * WIDTH), F32),
        compiler_params=_params("arbitrary"),
        name="sample_in_proj",
    )(x, g_mix, w_a, w_b)


def _one_query_attend(q, k_rows, v_rows, k_new=None, v_new=None):
    s = jnp.sum(k_rows * q, axis=-1, keepdims=True)
    m = jnp.max(s, axis=0, keepdims=True)
    if k_new is not None:
        s_new = jnp.sum(k_new * q, axis=-1, keepdims=True)
        m = jnp.maximum(m, s_new)
    p = jnp.exp(s - m)
    l = jnp.sum(p, axis=0, keepdims=True)
    acc = jnp.sum(p * v_rows, axis=0, keepdims=True)
    if k_new is not None:
        p_new = jnp.exp(s_new - m)
        l = l + p_new
        acc = acc + p_new * v_new
    return acc / l, m + jnp.log(l)


def _sample_att_body(q_ref, kn_ref, vn_ref, mq_ref, k1_ref, v1_ref, k4_ref, v4_ref, k16_ref, v16_ref,
                     mk_ref, mv_ref, att_ref, mem_ref):
    k_refs = (k1_ref, k4_ref, k16_ref)
    v_refs = (v1_ref, v4_ref, v16_ref)
    for h in range(HEADS):
        sl = slice(h * HEAD_DIM, (h + 1) * HEAD_DIM)
        outs, lses = [], []
        for g in range(N_GROUPS):
            gs = slice(g * WIDTH + h * HEAD_DIM, g * WIDTH + (h + 1) * HEAD_DIM)
            o, lse = _one_query_attend(q_ref[:, gs] * QK_SCALE, _load_head(k_refs[g], h), _load_head(v_refs[g], h),
                                       kn_ref[:, gs], vn_ref[:, gs])
            outs.append(o)
            lses.append(lse)
        m = jnp.maximum(jnp.maximum(lses[0], lses[1]), lses[2])
        e = [jnp.exp(x - m) for x in lses]
        inv = 1.0 / (e[0] + e[1] + e[2])
        att_ref[:, sl] = sum((e[g] * inv) * outs[g] for g in range(N_GROUPS))
        o, _ = _one_query_attend(mq_ref[:, sl] * QK_SCALE, _load_head(mk_ref, h), _load_head(mv_ref, h))
        mem_ref[:, sl] = o


def _sample_attend(z3, caches_k, caches_v, cache_mk, cache_mv):
    bd = z3.shape[0]
    zspec = lambda col, n: pl.BlockSpec((None, 1, n * WIDTH), lambda b: (b, 0, col // n))
    cache_specs, cache_args = [], []
    for (win, dil), ck, cv in zip(ATT_GROUPS, caches_k, caches_v):
        rows = ck.shape[2]
        assert rows == win and rows // dil == BLOCK
        spec = pl.BlockSpec((None, None, BLOCK, None, HEADS, HEAD_DIM), lambda b: (0, b, 0, 0, 0, 0))
        for c in (ck, cv):
            cache_specs.append(spec)
            cache_args.append(c.reshape(1, bd, BLOCK, dil, HEADS, HEAD_DIM))
    mem_spec = pl.BlockSpec((None, None, MEM_LEN, HEADS, HEAD_DIM), lambda b: (0, b, 0, 0, 0))
    out_spec = pl.BlockSpec((None, 1, WIDTH), lambda b: (b, 0, 0))
    out_shape = jax.ShapeDtypeStruct((bd, 1, WIDTH), F32)
    return pl.pallas_call(
        _sample_att_body,
        grid=(bd,),
        in_specs=[zspec(COL_Q, 3), zspec(COL_K, 3), zspec(COL_V, 3), zspec(COL_MQ, 1)] + cache_specs
                 + [mem_spec, mem_spec],
        out_specs=[out_spec, out_spec],
        out_shape=[out_shape, out_shape],
        compiler_params=_params("parallel"),
        name="sample_attend",
    )(z3, z3, z3, z3, *cache_args, cache_mk, cache_mv)


SAMPLE_GROUP = 8


def _sample_hgrn_body(zq_ref, zf_ref, zi_ref, lb_ref, gain_ref, s_ref, o_ref, so_ref):
    lb = _lower_bound(lb_ref)
    gain = gain_ref[...]
    q = jax.nn.sigmoid(zq_ref[...])
    kk = (1.0 - lb) * jax.nn.sigmoid(zf_ref[...])
    f = lb + kk
    v = zi_ref[...]
    pad = jnp.zeros((HEAD_DIM - SAMPLE_GROUP, HEAD_DIM), F32)

    def columns(a, h):
        return jnp.concatenate([a[:, h * HEAD_DIM:(h + 1) * HEAD_DIM], pad], axis=0).T

    for h in range(HEADS):
        sl = slice(h * HEAD_DIM, (h + 1) * HEAD_DIM)
        q_t, k_t, f_t = columns(q, h), columns(kk, h), columns(f, h)
        for j in range(SAMPLE_GROUP):
            s1 = f_t[:, j:j + 1] * s_ref[j, h] + k_t[:, j:j + 1] * v[j:j + 1, sl]
            so_ref[j, h] = s1
            o = jnp.sum(q_t[:, j:j + 1] * s1, axis=0, keepdims=True)
            o_ref[j:j + 1, sl] = _rms(o, gain[:, sl])


def _sample_hgrn(z, lb_logits, gain, state):
    bd = z.shape[0]
    zspec = lambda col: pl.BlockSpec((SAMPLE_GROUP, WIDTH), lambda i: (i, col))
    sspec = pl.BlockSpec((None, SAMPLE_GROUP, HEADS, HEAD_DIM, HEAD_DIM), lambda i: (0, i, 0, 0, 0))
    return pl.pallas_call(
        _sample_hgrn_body,
        grid=(bd // SAMPLE_GROUP,),
        in_specs=[zspec(COL_HQ), zspec(COL_HF), zspec(COL_HI), _const_spec(lb_logits.shape),
                  _const_spec((1, WIDTH)), sspec],
        out_specs=[pl.BlockSpec((SAMPLE_GROUP, WIDTH), lambda i: (i, 0)), sspec],
        out_shape=[jax.ShapeDtypeStruct((bd, WIDTH), F32), jax.ShapeDtypeStruct(state.shape, F32)],
        compiler_params=_params("parallel"),
        name="sample_hgrn",
    )(z, z, z, lb_logits, gain, state)


def _gate_body(z_ref, o_ref):
    o_ref[...] = jax.nn.sigmoid(z_ref[...]).astype(BF16)


def _sample_gates(z):
    bd = z.shape[0]
    n = N_BRANCH * D_MODEL
    return pl.pallas_call(
        _gate_body,
        grid=(n // WIDTH,),
        in_specs=[pl.BlockSpec((bd, WIDTH), lambda j: (0, COL_GATE + j))],
        out_specs=pl.BlockSpec((bd, WIDTH), lambda j: (0, j)),
        out_shape=jax.ShapeDtypeStruct((bd, n), BF16),
        name="sample_gates",
    )(z)


def kernel(x_prompt, x_sample, mem_prompt, cache_win1_k, cache_win1_v, cache_win4_k, cache_win4_v, cache_win16_k, cache_win16_v, cache_mem_k, cache_mem_v, state_hgrn, g_ff1, w_ff1_gate, w_ff1_up, w_ff1_down, g_mix, w_in, hg_lb_logits, g_hg_out, g_mem, w_mem_k, w_mem_v, w_branch_att, w_branch_hg, w_branch_mem, w_out, g_ff2, w_ff2_gate, w_ff2_up, w_ff2_down, g_final):
    b, t, _ = x_prompt.shape
    bd = x_sample.shape[0]
    assert g_ff1.shape[0] == 1 and x_sample.shape[1] == 1 and w_in.shape[2] == IN_WIDTH
    bf = lambda w: w[0].astype(BF16)
    ffn1 = (g_ff1, w_ff1_gate, w_ff1_up, w_ff1_down, g_final.reshape(1, D_MODEL))
    ffn2 = (g_ff2, w_ff2_gate, w_ff2_up, w_ff2_down, g_final.reshape(1, D_MODEL))
    w_qkv = w_in[0, :, :COL_HQ * WIDTH].astype(BF16)
    w_rest = w_in[0, :, COL_HQ * WIDTH:].astype(BF16)
    branch_w = (bf(w_branch_att), bf(w_branch_hg), bf(w_branch_mem), bf(w_out))

    xs, _ = _ffn(x_sample.reshape(bd, D_MODEL), *ffn1, tm=bd, final=False)
    zs = _sample_in_proj(xs, g_mix, w_qkv, w_rest)
    caches_k = (cache_win1_k, cache_win4_k, cache_win16_k)
    caches_v = (cache_win1_v, cache_win4_v, cache_win16_v)
    z3 = zs.reshape(bd, 1, IN_WIDTH)
    att_s, mem_s = _sample_attend(z3, caches_k, caches_v, cache_mem_k, cache_mem_v)
    hg_s, s_hgrn = _sample_hgrn(zs, hg_lb_logits, g_hg_out, state_hgrn)
    xs = _merge(xs, att_s.reshape(bd, WIDTH), hg_s, mem_s.reshape(bd, WIDTH), _sample_gates(zs), *branch_w, tm=bd)
    y_sample = _ffn(xs, *ffn2, tm=bd, final=True)[0].reshape(bd, 1, D_MODEL)

    tm = b * t // bd
    xp, (s_win16_k,) = _ffn(x_prompt.reshape(b * t, D_MODEL), *ffn1, tm=tm, final=False, ride=(cache_win16_k,))
    qkv = _qkv_proj(xp.reshape(b, t, D_MODEL), g_mix, w_qkv, tm=512)
    q_g, k_g, v_g, p_k, p_v = qkv[0:3], qkv[3:6], qkv[6:9], qkv[9:12], qkv[12:15]
    (hq, hk, lf, hv, mq, gates), (s_win1_k, s_win1_v, s_win4_k, s_win4_v), outs, lses = _rest_att(
        xp, g_mix, w_rest, hg_lb_logits, q_g, k_g, v_g, tm=tm,
        ride=(cache_win1_k, cache_win1_v, cache_win4_k, cache_win4_v))
    hg_p, p_hgrn = _hgrn_prompt(hq, hk, lf, hv, g_hg_out, b, t)
    p_mem_k, p_mem_v, mk_b, mv_b = _mem_kv(mem_prompt, g_mem, bf(w_mem_k), bf(w_mem_v))
    xp = _mix(xp, outs, lses, mq, mk_b, mv_b, hg_p, gates, *branch_w, b, t, tm=512)
    y_prompt, (s_win16_v,) = _ffn(xp, *ffn2, tm=tm, final=True, ride=(cache_win16_v,))
    s_win1_k, s_win1_v, s_win4_k, s_win4_v, s_win16_k, s_win16_v = _write_new_rows(
        z3, [s_win1_k, s_win1_v, s_win4_k, s_win4_v, s_win16_k, s_win16_v],
        ((0, False), (0, True), (1, False), (1, True), (2, False), (2, True)))

    return (y_prompt.reshape(b, t, D_MODEL), y_sample,
            p_k[0], p_v[0], p_k[1], p_v[1], p_k[2], p_v[2],
            p_mem_k, p_mem_v, p_hgrn,
            s_win1_k, s_win1_v, s_win4_k, s_win4_v, s_win16_k, s_win16_v,
            s_hgrn)
```

```python
import functools

import numpy as np
import jax
import jax.numpy as jnp
from jax import lax
from jax.experimental import pallas as pl
from jax.experimental.pallas import tpu as pltpu

F32 = jnp.float32
BF16 = jnp.bfloat16

D_MODEL = 1024
D_FF = 2816
EPS = 1e-6
HEADS = 4
HEAD_DIM = 128
WIDTH = HEADS * HEAD_DIM
ATT_GROUPS = ((128, 1), (512, 4), (2048, 16))
N_GROUPS = 3
BLOCK = 128
MEM_LEN = 256
N_BRANCH = 3
LSE_REP = 32
QK_SCALE = HEAD_DIM ** -0.5
VMEM_LIMIT_BYTES = 56 * 1024 * 1024

COL_Q, COL_K, COL_V, COL_HQ, COL_HF, COL_HI, COL_MQ, COL_GATE = 0, 3, 6, 9, 10, 11, 12, 13
IN_WIDTH = (COL_GATE + N_BRANCH * D_MODEL // WIDTH) * WIDTH


def _params(*sem):
    return pltpu.CompilerParams(dimension_semantics=sem, vmem_limit_bytes=VMEM_LIMIT_BYTES)


def _const_spec(shape):
    nd = len(shape)
    return pl.BlockSpec(shape, lambda *_: (0,) * nd, pipeline_mode=pl.Buffered(1))


def _rms(x, g):
    return x * lax.rsqrt(jnp.mean(x * x, axis=-1, keepdims=True) + EPS) * g


def _dot(a, b):
    return jnp.dot(a, b, preferred_element_type=F32)


def _dot_nt(a, b):
    return lax.dot_general(a, b, (((1,), (1,)), ((), ())), preferred_element_type=F32)


def _ride_specs(caches, n_steps):
    if not caches:
        return [], [], [], []
    assert all(c.shape[1] == n_steps for c in caches)
    in_specs = [pl.BlockSpec((1, 1, c.shape[2], HEADS, HEAD_DIM), lambda i: (0, i, 0, 0, 0)) for c in caches]
    out_specs = [pl.BlockSpec(memory_space=pl.ANY)] * len(caches)
    shapes = [jax.ShapeDtypeStruct(c.shape, c.dtype) for c in caches]
    return in_specs, out_specs, shapes, [pltpu.SemaphoreType.DMA((2 * len(caches),))]


def _ride_copies(in_refs, out_refs, sem):
    seq = pl.program_id(0)
    copies = []
    for j, (cache_ref, out_ref) in enumerate(zip(in_refs, out_refs)):
        rows = cache_ref.shape[2]
        copies.append(pltpu.make_async_copy(cache_ref.at[:, :, pl.ds(1, rows - 1)],
                                            out_ref.at[:, pl.ds(seq, 1), pl.ds(0, rows - 1)], sem.at[2 * j]))
        copies.append(pltpu.make_async_copy(cache_ref.at[:, :, pl.ds(rows - 1, 1)],
                                            out_ref.at[:, pl.ds(seq, 1), pl.ds(rows - 1, 1)], sem.at[2 * j + 1]))
    for cp in copies:
        cp.start()
    return copies


def _new_rows_body(kn_ref, vn_ref, *refs, meta):
    n = len(meta)
    out_refs, row_refs, sem = refs[n:2 * n], refs[2 * n:3 * n], refs[3 * n]
    copies = []
    for j, ((g, is_v), out_ref, row_ref) in enumerate(zip(meta, out_refs, row_refs)):
        bd, rows = out_ref.shape[1], out_ref.shape[2]
        new = (vn_ref if is_v else kn_ref)[:, 0, g * WIDTH:(g + 1) * WIDTH]
        flat = row_ref.reshape(bd * HEADS, HEAD_DIM)
        for h in range(HEADS):
            flat[pl.ds(h, bd, stride=HEADS), :] = new[:, h * HEAD_DIM:(h + 1) * HEAD_DIM]
        copies.append(pltpu.make_async_copy(row_ref, out_ref.at[:, :, pl.ds(rows - 1, 1)], sem.at[j]))
        copies[-1].start()
    for cp in copies:
        cp.wait()


def _write_new_rows(z3, shifted, meta):
    bd = z3.shape[0]
    n = len(shifted)
    zspec = lambda col: pl.BlockSpec((bd, 1, N_GROUPS * WIDTH), lambda i: (0, 0, col // N_GROUPS))
    any_spec = pl.BlockSpec(memory_space=pl.ANY)
    return pl.pallas_call(
        functools.partial(_new_rows_body, meta=meta),
        grid=(1,),
        in_specs=[zspec(COL_K), zspec(COL_V)] + [any_spec] * n,
        out_specs=[any_spec] * n,
        out_shape=[jax.ShapeDtypeStruct(c.shape, c.dtype) for c in shifted],
        input_output_aliases={2 + j: j for j in range(n)},
        scratch_shapes=[pltpu.VMEM((1, bd, 1, HEADS, HEAD_DIM), F32)] * n + [pltpu.SemaphoreType.DMA((n,))],
        name="write_new_rows",
    )(z3, z3, *shifted)


FF_CHUNKS = 1


def _ffn_body(x_ref, g_ref, wg_ref, wu_ref, wd_ref, gf_ref, *rest, final, n_ride):
    o_ref = rest[n_ride]
    copies = _ride_copies(rest[:n_ride], rest[n_ride + 1:2 * n_ride + 1], rest[-1]) if n_ride else []
    x = x_ref[...]
    h = _rms(x, g_ref[...]).astype(BF16)
    y = x
    width = D_FF // FF_CHUNKS
    for c in range(FF_CHUNKS):
        cols = slice(c * width, (c + 1) * width)
        a = _dot(h, wg_ref[:, cols])
        u = _dot(h, wu_ref[:, cols])
        act = (a * jax.nn.sigmoid(a) * u).astype(BF16)
        y = y + 0.5 * _dot(act, wd_ref[cols, :])
    if final:
        y = _rms(y, gf_ref[...])
    o_ref[...] = y
    for cp in copies:
        cp.wait()


def _ffn(x, g, wg, wu, wd, g_final, *, tm, final, ride=()):
    m = x.shape[0]
    r_in, r_out, r_shapes, r_scratch = _ride_specs(ride, m // tm)
    res = pl.pallas_call(
        functools.partial(_ffn_body, final=final, n_ride=len(ride)),
        grid=(m // tm,),
        in_specs=[pl.BlockSpec((tm, D_MODEL), lambda i: (i, 0)),
                  _const_spec((1, D_MODEL)),
                  _const_spec((D_MODEL, D_FF)), _const_spec((D_MODEL, D_FF)), _const_spec((D_FF, D_MODEL)),
                  _const_spec((1, D_MODEL))] + r_in,
        out_specs=[pl.BlockSpec((tm, D_MODEL), lambda i: (i, 0))] + r_out,
        out_shape=[jax.ShapeDtypeStruct((m, D_MODEL), F32)] + r_shapes,
        scratch_shapes=r_scratch,
        compiler_params=_params("parallel"),
        name="ffn_final" if final else "ffn",
    )(x, g, wg, wu, wd, g_final, *ride)
    return res[0], res[1:]


def _store_heads(ref, z):
    rows = ref.shape[0]
    flat = ref.reshape(rows * HEADS, HEAD_DIM)
    for h in range(HEADS):
        flat[pl.ds(h, rows, stride=HEADS), :] = z[:, h * HEAD_DIM:(h + 1) * HEAD_DIM]


def _load_head(ref, h):
    rows = ref.shape[0]
    return ref.reshape(rows * HEADS, HEAD_DIM)[pl.ds(h, rows, stride=HEADS), :]


def _to_strided_view(dst_ref, z, stage_ref, dil):
    if dil == 1:
        dst_ref[...] = z.astype(dst_ref.dtype)
        return
    n = z.shape[0] // dil
    for c in range(stage_ref.shape[0]):
        stage_ref[c] = z[:, c * 128:(c + 1) * 128]
    for r in range(dil):
        for c in range(stage_ref.shape[0]):
            dst_ref[:, r * WIDTH + c * 128:r * WIDTH + (c + 1) * 128] = (
                stage_ref[c, pl.ds(r, n, stride=dil), :].astype(dst_ref.dtype))


def _qkv_body(x_ref, g_ref, w_ref, *refs, tm, n_tiles):
    q_refs, k_refs, v_refs = refs[0:3], refs[3:6], refs[6:9]
    pk_refs, pv_refs, stage_refs = refs[9:12], refs[12:15], refs[15:]
    h = _rms(x_ref[...], g_ref[...]).astype(BF16)
    late = []
    for g, (win, dil) in enumerate(ATT_GROUPS):
        keep = min(win, tm * n_tiles)
        for kind, (col, dst_refs, win_refs) in enumerate(((COL_Q, q_refs, None), (COL_K, k_refs, pk_refs),
                                                          (COL_V, v_refs, pv_refs))):
            z = _dot(h, w_ref[:, (col + g) * WIDTH:(col + g + 1) * WIDTH])
            if win_refs is None:
                z = z * QK_SCALE
            stage = stage_refs[3 * (g - 1) + kind] if dil > 1 else None
            _to_strided_view(dst_refs[g], z, stage, dil)
            if win_refs is None:
                continue
            if keep == tm * n_tiles:
                _store_heads(win_refs[g], z)
            elif dil == 1:
                late.append((win_refs[g], z[tm - keep:, :], None))
            else:
                assert keep == tm
                late.append((win_refs[g], None, stage))

    @pl.when(pl.program_id(1) == n_tiles - 1)
    def _():
        for ref, rows, stage in late:
            if stage is None:
                _store_heads(ref, rows)
            else:
                _store_heads(ref, jnp.concatenate([stage[hd] for hd in range(HEADS)], axis=-1))


def _qkv_proj(x, g_mix, w_qkv, *, tm):
    b, t, _ = x.shape
    n_tiles = t // tm
    view_specs = [pl.BlockSpec((None, tm // d, d * WIDTH), lambda bi, i: (bi, i, 0)) for _, d in ATT_GROUPS]
    view_shapes = [jax.ShapeDtypeStruct((b, t // d, d * WIDTH), BF16) for _, d in ATT_GROUPS]
    win_specs, win_shapes = [], []
    for win, _ in ATT_GROUPS:
        keep = min(win, t)
        assert keep == t or keep <= tm
        if keep == t:
            win_specs.append(pl.BlockSpec((None, None, tm, HEADS, HEAD_DIM), lambda bi, i: (0, bi, i, 0, 0)))
        else:
            win_specs.append(pl.BlockSpec((None, None, keep, HEADS, HEAD_DIM), lambda bi, i: (0, bi, 0, 0, 0)))
        win_shapes.append(jax.ShapeDtypeStruct((1, b, keep, HEADS, HEAD_DIM), F32))
    return pl.pallas_call(
        functools.partial(_qkv_body, tm=tm, n_tiles=n_tiles),
        grid=(b, n_tiles),
        in_specs=[pl.BlockSpec((None, tm, D_MODEL), lambda bi, i: (bi, i, 0)),
                  _const_spec((1, D_MODEL)),
                  _const_spec((D_MODEL, 3 * N_GROUPS * WIDTH))],
        out_specs=view_specs * 3 + win_specs + win_specs,
        out_shape=view_shapes * 3 + win_shapes + win_shapes,
        scratch_shapes=[pltpu.VMEM((WIDTH // 128, tm, 128), F32)] * (3 * sum(d > 1 for _, d in ATT_GROUPS)),
        compiler_params=_params("parallel", "arbitrary"),
        name="qkv_proj",
    )(x, g_mix, w_qkv)


def _lower_bound(logits_ref):
    lg = logits_ref[...]
    e = jnp.exp(lg - jnp.max(lg, axis=0, keepdims=True))
    return e[0:1, :] / jnp.sum(e, axis=0, keepdims=True)


def _rest_steps(x_ref, g_ref, w_ref, lb_ref, hq_ref, hk_ref, lf_ref, hv_ref, mq_ref, gate_ref):
    h = _rms(x_ref[...], g_ref[...]).astype(BF16)
    lb = _lower_bound(lb_ref)
    proj = lambda c: _dot(h, w_ref[:, c * WIDTH:(c + 1) * WIDTH])

    def hgrn_query():
        hq_ref[...] = jax.nn.sigmoid(proj(0)).astype(BF16)

    def hgrn_key():
        kk = (1.0 - lb) * jax.nn.sigmoid(proj(1))
        hk_ref[...] = kk.astype(BF16)
        lf_ref[...] = jnp.log(lb + kk)

    def hgrn_value():
        hv_ref[...] = proj(2).astype(BF16)

    def mem_query():
        mq_ref[...] = (proj(3) * QK_SCALE).astype(BF16)

    def gate(c):
        gate_ref[:, c * WIDTH:(c + 1) * WIDTH] = jax.nn.sigmoid(proj(4 + c)).astype(BF16)

    return [hgrn_query, hgrn_key, hgrn_value, mem_query] + [functools.partial(gate, c)
                                                            for c in range(N_BRANCH * D_MODEL // WIDTH)]


def _rest_att_body(x_ref, g_ref, w_ref, lb_ref, *rest, n_ride, tiles, tiles_per_seq):
    n_att_in = sum(5 if has_halo else 3 for _, _, has_halo in tiles)
    caches, rest = rest[:n_ride], rest[n_ride:]
    att_in, rest = rest[:n_att_in], rest[n_att_in:]
    proj_out, rest = rest[:6], rest[6:]
    ride_out, rest = rest[:n_ride], rest[n_ride:]
    att_out, scratch = rest[:2 * len(tiles)], rest[2 * len(tiles):]
    copies = _ride_copies(caches, ride_out, scratch[-1]) if n_ride else []
    steps = _rest_steps(x_ref, g_ref, w_ref, lb_ref, *proj_out)
    gens = _att_tiles(att_in + att_out, tiles, pl.program_id(0) % tiles_per_seq)
    steps.pop(0)()
    for gen in gens:
        next(gen)
        steps.pop(0)()
        next(gen)
        steps.pop(0)()
        next(gen, None)
    for step in steps:
        step()
    for cp in copies:
        cp.wait()


def _rest_att(x, g_mix, w_rest, lb_logits, q_g, k_g, v_g, *, tm, ride=()):
    m = x.shape[0]
    spec = pl.BlockSpec((tm, WIDTH), lambda i: (i, 0))
    bf = jax.ShapeDtypeStruct((m, WIDTH), BF16)
    r_in, r_out, r_shapes, r_scratch = _ride_specs(ride, m // tm)
    tiles_per_seq = q_g[0].shape[1] // (ATT_UNITS * BLOCK)
    a_in, a_args, a_out, a_shapes, tiles, n_tiles = _attention_specs(
        q_g, k_g, v_g, lambda i: (i // tiles_per_seq, i % tiles_per_seq))
    assert n_tiles == tiles_per_seq and q_g[0].shape[0] * n_tiles == m // tm
    res = pl.pallas_call(
        functools.partial(_rest_att_body, n_ride=len(ride), tiles=tiles, tiles_per_seq=tiles_per_seq),
        grid=(m // tm,),
        in_specs=[pl.BlockSpec((tm, D_MODEL), lambda i: (i, 0)),
                  _const_spec((1, D_MODEL)),
                  _const_spec(w_rest.shape),
                  _const_spec(lb_logits.shape)] + r_in + a_in,
        out_specs=[spec, spec, spec, spec, spec, pl.BlockSpec((tm, N_BRANCH * D_MODEL), lambda i: (i, 0))]
                  + r_out + a_out,
        out_shape=[bf, bf, jax.ShapeDtypeStruct((m, WIDTH), F32), bf, bf,
                   jax.ShapeDtypeStruct((m, N_BRANCH * D_MODEL), BF16)] + r_shapes + a_shapes,
        scratch_shapes=r_scratch,
        compiler_params=_params("arbitrary"),
        name="rest_att",
    )(x, g_mix, w_rest, lb_logits, *ride, *a_args)
    n = 6 + len(ride)
    return res[:6], res[6:n], res[n::2], res[n + 1::2]


ATT_UNITS = 4
STAT_REP = LSE_REP // 2


def _att_tiles(refs, tiles, tile_index):
    n_in = [5 if has_halo else 3 for _, _, has_halo in tiles]
    outs = refs[sum(n_in):]
    gens, pos = [], 0
    for g, (n_row_blocks, n_col_blocks, has_halo) in enumerate(tiles):
        gens.append(_att_tile(*refs[pos:pos + n_in[g]], outs[2 * g], outs[2 * g + 1], n_row_blocks=n_row_blocks,
                              n_col_blocks=n_col_blocks, has_halo=has_halo, tile_index=tile_index))
        pos += n_in[g]
    return gens


def _att_tile(q_ref, k_ref, v_ref, *rest, n_row_blocks, n_col_blocks, has_halo, tile_index):
    if has_halo:
        kh_ref, vh_ref, o_ref, st_ref = rest
        halo_ok = tile_index > 0
    else:
        o_ref, st_ref = rest
    qi = lax.broadcasted_iota(jnp.int32, (BLOCK, 2 * BLOCK), 0)
    ki = lax.broadcasted_iota(jnp.int32, (BLOCK, 2 * BLOCK), 1)
    own_ok = (lax.broadcasted_iota(jnp.int32, (BLOCK, BLOCK), 1)
              <= lax.broadcasted_iota(jnp.int32, (BLOCK, BLOCK), 0))
    both_ok = jnp.where(ki < BLOCK, ki, qi + BLOCK) >= jnp.where(ki < BLOCK, qi, ki)
    if has_halo:
        first_valid = jnp.where(halo_ok, 0, BLOCK)
        halo_both_ok = jnp.logical_and(both_ok, ki >= first_valid)

    units = []
    for c in range(n_col_blocks):
        for u in range(n_row_blocks):
            for h in range(HEADS):
                rows = slice(u * BLOCK, (u + 1) * BLOCK)
                cols = slice(c * WIDTH + h * HEAD_DIM, c * WIDTH + (h + 1) * HEAD_DIM)
                lcols = (c * HEADS + h) * LSE_REP
                if u > 0:
                    with_prev = slice((u - 1) * BLOCK, (u + 1) * BLOCK)
                    keys = lambda ref, rr=with_prev, cc=cols: ref[rr, cc]
                    ok = both_ok
                elif has_halo:
                    hcols = slice(h * HEAD_DIM, (h + 1) * HEAD_DIM)
                    keys = lambda ref, rr=rows, cc=cols, hc=hcols: jnp.concatenate(
                        [(kh_ref if ref is k_ref else vh_ref)[:, hc], ref[rr, cc]], axis=0)
                    ok = halo_both_ok
                else:
                    keys = lambda ref, rr=rows, cc=cols: ref[rr, cc]
                    ok = own_ok
                units.append((rows, cols, lcols, keys, ok))

    scores = [jnp.where(ok, _dot_nt(q_ref[rows, cols], keys(k_ref)), -jnp.inf) for rows, cols, _, keys, ok in units]
    yield

    probs = []
    for s in scores:
        m = jnp.max(s, axis=-1, keepdims=True)
        p = jnp.exp(s - m)
        probs.append((p.astype(BF16), m, jnp.sum(p, axis=-1, keepdims=True)))
    yield

    for (rows, cols, lcols, keys, _), (p, m, l) in zip(units, probs):
        o_ref[rows, cols] = _dot(p, keys(v_ref)).astype(BF16)
        st_ref[rows, lcols:lcols + STAT_REP] = jnp.broadcast_to(m, (BLOCK, STAT_REP))
        st_ref[rows, lcols + STAT_REP:lcols + LSE_REP] = jnp.broadcast_to(l, (BLOCK, STAT_REP))


def _attention_specs(q_g, k_g, v_g, to_bj):
    b = q_g[0].shape[0]
    lw = HEADS * LSE_REP
    n_steps = None
    in_specs, args, out_specs, out_shapes, tiles = [], [], [], [], []

    def spec(shape, index):
        return pl.BlockSpec(shape, lambda *grid_idx: index(*to_bj(*grid_idx)))

    for (_, dil), q, k, v in zip(ATT_GROUPS, q_g, k_g, v_g):
        sub_len = q.shape[1]
        n_row_blocks = min(ATT_UNITS, sub_len // BLOCK)
        n_col_blocks = ATT_UNITS // n_row_blocks
        span = n_row_blocks * BLOCK
        assert sub_len % span == 0 and dil % n_col_blocks == 0
        has_halo = sub_len > span
        assert not has_halo or (dil == 1 and n_col_blocks == 1)
        n = (sub_len // span) * (dil // n_col_blocks)
        assert n_steps in (None, n)
        n_steps = n
        if has_halo:
            tile = lambda w, span=span: spec((None, span, w), lambda bi, j: (bi, j, 0))
        else:
            tile = lambda w, span=span, nc=n_col_blocks: spec((None, span, nc * w), lambda bi, j: (bi, 0, j))
        halo = spec((None, BLOCK, WIDTH), lambda bi, j, nr=n_row_blocks: (bi, jnp.maximum(j * nr - 1, 0), 0))
        in_specs += [tile(WIDTH)] * 3 + ([halo, halo] if has_halo else [])
        args += [q, k, v] + ([k, v] if has_halo else [])
        out_specs += [tile(WIDTH), tile(lw)]
        out_shapes += [jax.ShapeDtypeStruct((b, sub_len, dil * WIDTH), BF16),
                       jax.ShapeDtypeStruct((b, sub_len, dil * lw), F32)]
        tiles.append((n_row_blocks, n_col_blocks, has_halo))
    return in_specs, args, out_specs, out_shapes, tuple(tiles), n_steps


N_LEVELS = 7
ROW_CUM, ROW_REM = N_LEVELS, N_LEVELS + 1


def _hgrn_constants():
    r = np.arange(BLOCK)[:, None]
    s = np.arange(BLOCK)[None, :]
    mats = []
    for lvl in range(N_LEVELS):
        half = 1 << lvl
        mid = (r // (2 * half)) * 2 * half + half
        upper = (r % (2 * half)) >= half
        mats.append(np.where(upper, (s >= mid) & (s <= r), (s > r) & (s <= mid - 1)))
    mats.append(s <= r)
    mats.append(s > r)
    range_mat = np.concatenate(mats, axis=0).astype(np.float32)
    range_mat = np.concatenate([range_mat, range_mat], axis=1)
    x = r ^ s
    level = np.where(s < r, np.floor(np.log2(np.maximum(x, 1))).astype(np.int32), np.where(s == r, N_LEVELS, -1))
    return jnp.asarray(range_mat, BF16), jnp.asarray(level, jnp.int32)


HGRN_SEQS = 4


def _hgrn_intra(q_b, k_b, lf, rng, level):
    lf_hi = lf.astype(BF16)
    lf_lo = (lf - lf_hi.astype(F32)).astype(BF16)
    z = _dot(rng, jnp.concatenate([lf_hi, lf_lo], axis=0))
    q = q_b.astype(F32)
    kk = k_b.astype(F32)
    lhs, rhs = [], []
    for lvl in range(N_LEVELS):
        x = jnp.exp(z[lvl * BLOCK:(lvl + 1) * BLOCK])
        lhs.append((q * x).astype(BF16))
        rhs.append((kk * x).astype(BF16))
    lhs.append(q_b)
    rhs.append(k_b)
    b_cum = z[ROW_CUM * BLOCK:(ROW_CUM + 1) * BLOCK]
    q_in = (q * jnp.exp(b_cum)).astype(BF16)
    k_out = (kk * jnp.exp(z[ROW_REM * BLOCK:(ROW_REM + 1) * BLOCK])).astype(BF16)
    decay = jnp.exp(b_cum[BLOCK - 1:BLOCK, :])
    mats = []
    for h in range(HEADS):
        sl = slice(h * HEAD_DIM, (h + 1) * HEAD_DIM)
        a = jnp.zeros((BLOCK, BLOCK), F32)
        for lvl in range(N_LEVELS + 1):
            a = jnp.where(level == lvl, _dot_nt(lhs[lvl][:, sl], rhs[lvl][:, sl]), a)
        mats.append(a.astype(BF16))
    return mats, q_in, k_out, decay


def _hgrn_state(mats, q_in, k_out, decay, v_b, gain, st_ref, j):
    outs = []
    for h in range(HEADS):
        sl = slice(h * HEAD_DIM, (h + 1) * HEAD_DIM)
        v = v_b[:, sl]
        st = st_ref[j, h]
        o = _dot(mats[h], v) + _dot_nt(q_in[:, sl], st.astype(BF16))
        outs.append(_rms(o, gain[:, sl]).astype(BF16))
        v_t = v.astype(F32).T.astype(BF16)
        st_ref[j, h] = st * decay[:, sl] + _dot(v_t, k_out[:, sl])
    return jnp.concatenate(outs, axis=-1)


def _hgrn_body(q_ref, k_ref, lf_ref, v_ref, rng_ref, lvl_ref, gain_ref, o_ref, s_ref, st_ref, *, n_chunks):
    c = pl.program_id(1)

    @pl.when(c == 0)
    def _():
        st_ref[...] = jnp.zeros_like(st_ref)

    rng, level, gain = rng_ref[...], lvl_ref[...], gain_ref[...]
    intra = [_hgrn_intra(q_ref[j], k_ref[j], lf_ref[j], rng, level) for j in range(HGRN_SEQS)]
    for j in range(HGRN_SEQS):
        o_ref[j] = _hgrn_state(*intra[j], v_ref[j], gain, st_ref, j)

    @pl.when(c == n_chunks - 1)
    def _():
        for j in range(HGRN_SEQS):
            for h in range(HEADS):
                s_ref[j, h] = st_ref[j, h].T


def _hgrn_prompt(hq, hk, lf, hv, gain, b, t):
    n_chunks = t // BLOCK
    range_mat, level = _hgrn_constants()
    as3d = lambda a: a.reshape(b, t, WIDTH)
    spec = pl.BlockSpec((HGRN_SEQS, BLOCK, WIDTH), lambda bi, c: (bi, c, 0))
    state_spec = pl.BlockSpec((None, HGRN_SEQS, HEADS, HEAD_DIM, HEAD_DIM), lambda bi, c: (0, bi, 0, 0, 0))
    o, state = pl.pallas_call(
        functools.partial(_hgrn_body, n_chunks=n_chunks),
        grid=(b // HGRN_SEQS, n_chunks),
        in_specs=[spec, spec, spec, spec, _const_spec(range_mat.shape), _const_spec(level.shape),
                  _const_spec((1, WIDTH))],
        out_specs=[spec, state_spec],
        out_shape=[jax.ShapeDtypeStruct((b, t, WIDTH), BF16),
                   jax.ShapeDtypeStruct((1, b, HEADS, HEAD_DIM, HEAD_DIM), F32)],
        scratch_shapes=[pltpu.VMEM((HGRN_SEQS, HEADS, HEAD_DIM, HEAD_DIM), F32)],
        compiler_params=_params("parallel", "arbitrary"),
        name="hgrn_prompt",
    )(as3d(hq), as3d(hk), as3d(lf), as3d(hv), range_mat, level, gain)
    return o.reshape(b * t, WIDTH), state


def _memkv_body(m_ref, g_ref, wk_ref, wv_ref, pk_ref, pv_ref, kb_ref, vb_ref):
    h = _rms(m_ref[...], g_ref[...]).astype(BF16)
    k = _dot(h, wk_ref[...])
    v = _dot(h, wv_ref[...])
    _store_heads(pk_ref, k)
    _store_heads(pv_ref, v)
    kb_ref[...] = k.astype(BF16)
    vb_ref[...] = v.astype(BF16)


def _mem_kv(mem, g_mem, w_mk, w_mv):
    b = mem.shape[0]
    head_spec = pl.BlockSpec((None, None, MEM_LEN, HEADS, HEAD_DIM), lambda bi: (0, bi, 0, 0, 0))
    flat_spec = pl.BlockSpec((None, MEM_LEN, WIDTH), lambda bi: (bi, 0, 0))
    head_shape = jax.ShapeDtypeStruct((1, b, MEM_LEN, HEADS, HEAD_DIM), F32)
    flat_shape = jax.ShapeDtypeStruct((b, MEM_LEN, WIDTH), BF16)
    return pl.pallas_call(
        _memkv_body,
        grid=(b,),
        in_specs=[pl.BlockSpec((None, MEM_LEN, D_MODEL), lambda bi: (bi, 0, 0)),
                  _const_spec((1, D_MODEL)), _const_spec((D_MODEL, WIDTH)), _const_spec((D_MODEL, WIDTH))],
        out_specs=[head_spec, head_spec, flat_spec, flat_spec],
        out_shape=[head_shape, head_shape, flat_shape, flat_shape],
        compiler_params=_params("parallel"),
        name="mem_kv",
    )(mem, g_mem, w_mk, w_mv)


def _from_strided_view(src_ref, stage_ref, dil, width):
    if dil == 1:
        return src_ref[...].astype(F32)
    n = src_ref.shape[0]
    n_col = width // 128
    for r in range(dil):
        for c in range(n_col):
            stage_ref[c, pl.ds(r, n, stride=dil), :] = (
                src_ref[:, r * width + c * 128:r * width + (c + 1) * 128].astype(F32))
    return jnp.concatenate([stage_ref[c] for c in range(n_col)], axis=-1)


def _gated_merge(x, att, hg, mem, gate_ref, wa_ref, wb_ref, wc_ref, wo_ref):
    def gate(j):
        return gate_ref[:, j * D_MODEL:(j + 1) * D_MODEL].astype(F32)

    m = gate(0) * _dot(att, wa_ref[...]) + gate(1) * _dot(hg, wb_ref[...]) + gate(2) * _dot(mem, wc_ref[...])
    return x + _dot(m.astype(BF16), wo_ref[...])


def _mix_body(x_ref, o0_ref, o1_ref, o2_ref, s0_ref, s1_ref, s2_ref, mq_ref, mk_ref, mv_ref, hg_ref, gate_ref,
              wa_ref, wb_ref, wc_ref, wo_ref, out_ref, *stage_refs):
    def gate(j):
        return gate_ref[:, j * D_MODEL:(j + 1) * D_MODEL].astype(F32)

    heads = [slice(h * HEAD_DIM, (h + 1) * HEAD_DIM) for h in range(HEADS)]
    merged = gate(1) * _dot(hg_ref[...], wb_ref[...])
    scores = [_dot_nt(mq_ref[:, sl], mk_ref[:, sl]) for sl in heads]
    probs = []
    for s in scores:
        p = jnp.exp(s - jnp.max(s, axis=-1, keepdims=True))
        probs.append((p.astype(BF16), 1.0 / jnp.sum(p, axis=-1, keepdims=True)))
    mem = [(_dot(p, mv_ref[:, sl]) * inv).astype(BF16) for (p, inv), sl in zip(probs, heads)]
    merged = merged + gate(2) * _dot(jnp.concatenate(mem, axis=-1), wc_ref[...])

    accs = [_from_strided_view(r, stage_refs[2 * g], ATT_GROUPS[g][1], WIDTH)
            for g, r in enumerate((o0_ref, o1_ref, o2_ref))]
    stats = [_from_strided_view(r, stage_refs[2 * g + 1], ATT_GROUPS[g][1], HEADS * LSE_REP)
             for g, r in enumerate((s0_ref, s1_ref, s2_ref))]
    att = []
    for h, sl in enumerate(heads):
        ms = [x[:, h * LSE_REP:h * LSE_REP + 1] for x in stats]
        ls = [x[:, h * LSE_REP + STAT_REP:h * LSE_REP + STAT_REP + 1] for x in stats]
        m = jnp.maximum(jnp.maximum(ms[0], ms[1]), ms[2])
        e = [jnp.exp(x - m) for x in ms]
        inv = 1.0 / (e[0] * ls[0] + e[1] * ls[1] + e[2] * ls[2])
        att.append(sum((e[g] * inv) * accs[g][:, sl] for g in range(N_GROUPS)).astype(BF16))
    merged = merged + gate(0) * _dot(jnp.concatenate(att, axis=-1), wa_ref[...])
    out_ref[...] = x_ref[...] + _dot(merged.astype(BF16), wo_ref[...])


def _mix(x, accs, stats, mq, mk, mv, hg, gates, wa, wb, wc, wo, b, t, *, tm):
    n_tiles = t // tm
    lw = HEADS * LSE_REP
    rows = lambda w: pl.BlockSpec((tm, w), lambda bi, i: (bi * n_tiles + i, 0))
    o_specs = [pl.BlockSpec((None, tm // d, d * WIDTH), lambda bi, i: (bi, i, 0)) for _, d in ATT_GROUPS]
    s_specs = [pl.BlockSpec((None, tm // d, d * lw), lambda bi, i: (bi, i, 0)) for _, d in ATT_GROUPS]
    memspec = pl.BlockSpec((None, MEM_LEN, WIDTH), lambda bi, i: (bi, 0, 0))
    stages = []
    for _ in ATT_GROUPS:
        stages += [pltpu.VMEM((WIDTH // 128, tm, 128), F32), pltpu.VMEM((lw // 128, tm, 128), F32)]
    return pl.pallas_call(
        _mix_body,
        grid=(b, n_tiles),
        in_specs=[rows(D_MODEL)] + o_specs + s_specs + [rows(WIDTH), memspec, memspec, rows(WIDTH),
                                                         rows(N_BRANCH * D_MODEL),
                                                         _const_spec((WIDTH, D_MODEL)), _const_spec((WIDTH, D_MODEL)),
                                                         _const_spec((WIDTH, D_MODEL)), _const_spec((D_MODEL, D_MODEL))],
        out_specs=rows(D_MODEL),
        out_shape=jax.ShapeDtypeStruct((b * t, D_MODEL), F32),
        scratch_shapes=stages,
        compiler_params=_params("parallel", "parallel"),
        name="mix",
    )(x, *accs, *stats, mq, mk, mv, hg, gates, wa, wb, wc, wo)


def _merge_body(x_ref, att_ref, hg_ref, mem_ref, gate_ref, wa_ref, wb_ref, wc_ref, wo_ref, o_ref):
    o_ref[...] = _gated_merge(x_ref[...], att_ref[...].astype(BF16), hg_ref[...].astype(BF16),
                              mem_ref[...].astype(BF16), gate_ref, wa_ref, wb_ref, wc_ref, wo_ref)


def _merge(x, att, hg, mem, gates, wa, wb, wc, wo, *, tm):
    m = x.shape[0]
    row = pl.BlockSpec((tm, WIDTH), lambda i: (i, 0))
    wide = pl.BlockSpec((tm, D_MODEL), lambda i: (i, 0))
    return pl.pallas_call(
        _merge_body,
        grid=(m // tm,),
        in_specs=[wide, row, row, row, pl.BlockSpec((tm, N_BRANCH * D_MODEL), lambda i: (i, 0)),
                  _const_spec((WIDTH, D_MODEL)), _const_spec((WIDTH, D_MODEL)), _const_spec((WIDTH, D_MODEL)),
                  _const_spec((D_MODEL, D_MODEL))],
        out_specs=wide,
        out_shape=jax.ShapeDtypeStruct((m, D_MODEL), F32),
        compiler_params=_params("parallel"),
        name="merge",
    )(x, att, hg, mem, gates, wa, wb, wc, wo)


def _sample_in_proj_body(x_ref, g_ref, w_ref, o_ref, wa_ref, wb_ref, *, n_a):
    w = w_ref[...].astype(BF16)
    o_ref[...] = _dot(_rms(x_ref[...], g_ref[...]).astype(BF16), w)

    @pl.when(pl.program_id(0) < n_a)
    def _():
        wa_ref[...] = w

    @pl.when(pl.program_id(0) >= n_a)
    def _():
        wb_ref[...] = w


def _sample_in_proj(x, g_mix, w_in, n_a):
    m = x.shape[0]
    n = w_in.shape[2] // WIDTH
    return pl.pallas_call(
        functools.partial(_sample_in_proj_body, n_a=n_a),
        grid=(n,),
        in_specs=[_const_spec((m, D_MODEL)), _const_spec((1, D_MODEL)),
                  pl.BlockSpec((None, D_MODEL, WIDTH), lambda j: (0, 0, j))],
        out_specs=[pl.BlockSpec((m, WIDTH), lambda j: (0, j)),
                   pl.BlockSpec((D_MODEL, WIDTH), lambda j: (0, jnp.minimum(j, n_a - 1))),
                   pl.BlockSpec((D_MODEL, WIDTH), lambda j: (0, jnp.maximum(j - n_a, 0)))],
        out_shape=[jax.ShapeDtypeStruct((m, n * WIDTH), F32),
                   jax.ShapeDtypeStruct((D_MODEL, n_a * WIDTH), BF16),
                   jax.ShapeDtypeStruct((D_MODEL, (n - n_a) * WIDTH), BF16)],
        compiler_params=_params("arbitrary"),
        name="sample_in_proj",
    )(x, g_mix, w_in)


FF_TILE = 256


def _sample_ffn_body(x_ref, g_ref, wg_ref, wu_ref, wd_ref, gf_ref, o_ref, wgb_ref, wub_ref, wdb_ref, acc_ref, *,
                     final):
    j = pl.program_id(0)

    @pl.when(j == 0)
    def _():
        acc_ref[...] = jnp.zeros_like(acc_ref)

    wg, wu, wd = wg_ref[...].astype(BF16), wu_ref[...].astype(BF16), wd_ref[...].astype(BF16)
    wgb_ref[...], wub_ref[...], wdb_ref[...] = wg, wu, wd
    x = x_ref[...]
    h = _rms(x, g_ref[...]).astype(BF16)
    a = _dot(h, wg)
    act = (a * jax.nn.sigmoid(a) * _dot(h, wu)).astype(BF16)
    acc_ref[...] += _dot(act, wd)

    @pl.when(j == pl.num_programs(0) - 1)
    def _():
        y = x + 0.5 * acc_ref[...]
        o_ref[...] = _rms(y, gf_ref[...]) if final else y


def _sample_ffn(x, g, wg, wu, wd, g_final, *, final):
    m = x.shape[0]
    res = pl.pallas_call(
        functools.partial(_sample_ffn_body, final=final),
        grid=(D_FF // FF_TILE,),
        in_specs=[_const_spec((m, D_MODEL)), _const_spec((1, D_MODEL)),
                  pl.BlockSpec((None, D_MODEL, FF_TILE), lambda j: (0, 0, j)),
                  pl.BlockSpec((None, D_MODEL, FF_TILE), lambda j: (0, 0, j)),
                  pl.BlockSpec((None, FF_TILE, D_MODEL), lambda j: (0, j, 0)),
                  _const_spec((1, D_MODEL))],
        out_specs=[pl.BlockSpec((m, D_MODEL), lambda j: (0, 0)),
                   pl.BlockSpec((D_MODEL, FF_TILE), lambda j: (0, j)),
                   pl.BlockSpec((D_MODEL, FF_TILE), lambda j: (0, j)),
                   pl.BlockSpec((FF_TILE, D_MODEL), lambda j: (j, 0))],
        out_shape=[jax.ShapeDtypeStruct((m, D_MODEL), F32),
                   jax.ShapeDtypeStruct((D_MODEL, D_FF), BF16), jax.ShapeDtypeStruct((D_MODEL, D_FF), BF16),
                   jax.ShapeDtypeStruct((D_FF, D_MODEL), BF16)],
        scratch_shapes=[pltpu.VMEM((m, D_MODEL), F32)],
        compiler_params=_params("arbitrary"),
        name="sample_ffn_final" if final else "sample_ffn",
    )(x, g, wg, wu, wd, g_final)
    return res[0], res[1:]


def _one_query_attend(q, k_rows, v_rows, k_new=None, v_new=None):
    s = jnp.sum(k_rows * q, axis=-1, keepdims=True)
    m = jnp.max(s, axis=0, keepdims=True)
    if k_new is not None:
        s_new = jnp.sum(k_new * q, axis=-1, keepdims=True)
        m = jnp.maximum(m, s_new)
    p = jnp.exp(s - m)
    l = jnp.sum(p, axis=0, keepdims=True)
    acc = jnp.sum(p * v_rows, axis=0, keepdims=True)
    if k_new is not None:
        p_new = jnp.exp(s_new - m)
        l = l + p_new
        acc = acc + p_new * v_new
    return acc / l, m + jnp.log(l)


def _sample_att_body(q_ref, kn_ref, vn_ref, mq_ref, k1_ref, v1_ref, k4_ref, v4_ref, k16_ref, v16_ref,
                     mk_ref, mv_ref, att_ref, mem_ref):
    k_refs = (k1_ref, k4_ref, k16_ref)
    v_refs = (v1_ref, v4_ref, v16_ref)
    for h in range(HEADS):
        sl = slice(h * HEAD_DIM, (h + 1) * HEAD_DIM)
        outs, lses = [], []
        for g in range(N_GROUPS):
            gs = slice(g * WIDTH + h * HEAD_DIM, g * WIDTH + (h + 1) * HEAD_DIM)
            o, lse = _one_query_attend(q_ref[:, gs] * QK_SCALE, _load_head(k_refs[g], h), _load_head(v_refs[g], h),
                                       kn_ref[:, gs], vn_ref[:, gs])
            outs.append(o)
            lses.append(lse)
        m = jnp.maximum(jnp.maximum(lses[0], lses[1]), lses[2])
        e = [jnp.exp(x - m) for x in lses]
        inv = 1.0 / (e[0] + e[1] + e[2])
        att_ref[:, sl] = sum((e[g] * inv) * outs[g] for g in range(N_GROUPS))
        o, _ = _one_query_attend(mq_ref[:, sl] * QK_SCALE, _load_head(mk_ref, h), _load_head(mv_ref, h))
        mem_ref[:, sl] = o


def _sample_attend(z3, caches_k, caches_v, cache_mk, cache_mv):
    bd = z3.shape[0]
    zspec = lambda col, n: pl.BlockSpec((None, 1, n * WIDTH), lambda b: (b, 0, col // n))
    cache_specs, cache_args = [], []
    for (win, dil), ck, cv in zip(ATT_GROUPS, caches_k, caches_v):
        rows = ck.shape[2]
        assert rows == win and rows // dil == BLOCK
        spec = pl.BlockSpec((None, None, BLOCK, None, HEADS, HEAD_DIM), lambda b: (0, b, 0, 0, 0, 0))
        for c in (ck, cv):
            cache_specs.append(spec)
            cache_args.append(c.reshape(1, bd, BLOCK, dil, HEADS, HEAD_DIM))
    mem_spec = pl.BlockSpec((None, None, MEM_LEN, HEADS, HEAD_DIM), lambda b: (0, b, 0, 0, 0))
    out_spec = pl.BlockSpec((None, 1, WIDTH), lambda b: (b, 0, 0))
    out_shape = jax.ShapeDtypeStruct((bd, 1, WIDTH), F32)
    return pl.pallas_call(
        _sample_att_body,
        grid=(bd,),
        in_specs=[zspec(COL_Q, 3), zspec(COL_K, 3), zspec(COL_V, 3), zspec(COL_MQ, 1)] + cache_specs
                 + [mem_spec, mem_spec],
        out_specs=[out_spec, out_spec],
        out_shape=[out_shape, out_shape],
        compiler_params=_params("parallel"),
        name="sample_attend",
    )(z3, z3, z3, z3, *cache_args, cache_mk, cache_mv)


SAMPLE_GROUP = 8


def _sample_hgrn_body(zq_ref, zf_ref, zi_ref, lb_ref, gain_ref, s_ref, o_ref, so_ref):
    lb = _lower_bound(lb_ref)
    gain = gain_ref[...]
    q = jax.nn.sigmoid(zq_ref[...])
    kk = (1.0 - lb) * jax.nn.sigmoid(zf_ref[...])
    f = lb + kk
    v = zi_ref[...]
    pad = jnp.zeros((HEAD_DIM - SAMPLE_GROUP, HEAD_DIM), F32)

    def columns(a, h):
        return jnp.concatenate([a[:, h * HEAD_DIM:(h + 1) * HEAD_DIM], pad], axis=0).T

    for h in range(HEADS):
        sl = slice(h * HEAD_DIM, (h + 1) * HEAD_DIM)
        q_t, k_t, f_t = columns(q, h), columns(kk, h), columns(f, h)
        for j in range(SAMPLE_GROUP):
            s1 = f_t[:, j:j + 1] * s_ref[j, h] + k_t[:, j:j + 1] * v[j:j + 1, sl]
            so_ref[j, h] = s1
            o = jnp.sum(q_t[:, j:j + 1] * s1, axis=0, keepdims=True)
            o_ref[j:j + 1, sl] = _rms(o, gain[:, sl])


def _sample_hgrn(z, lb_logits, gain, state):
    bd = z.shape[0]
    zspec = lambda col: pl.BlockSpec((SAMPLE_GROUP, WIDTH), lambda i: (i, col))
    sspec = pl.BlockSpec((None, SAMPLE_GROUP, HEADS, HEAD_DIM, HEAD_DIM), lambda i: (0, i, 0, 0, 0))
    return pl.pallas_call(
        _sample_hgrn_body,
        grid=(bd // SAMPLE_GROUP,),
        in_specs=[zspec(COL_HQ), zspec(COL_HF), zspec(COL_HI), _const_spec(lb_logits.shape),
                  _const_spec((1, WIDTH)), sspec],
        out_specs=[pl.BlockSpec((SAMPLE_GROUP, WIDTH), lambda i: (i, 0)), sspec],
        out_shape=[jax.ShapeDtypeStruct((bd, WIDTH), F32), jax.ShapeDtypeStruct(state.shape, F32)],
        compiler_params=_params("parallel"),
        name="sample_hgrn",
    )(z, z, z, lb_logits, gain, state)


def _gate_body(z_ref, o_ref):
    o_ref[...] = jax.nn.sigmoid(z_ref[...]).astype(BF16)


def _sample_gates(z):
    bd = z.shape[0]
    n = N_BRANCH * D_MODEL
    return pl.pallas_call(
        _gate_body,
        grid=(n // WIDTH,),
        in_specs=[pl.BlockSpec((bd, WIDTH), lambda j: (0, COL_GATE + j))],
        out_specs=pl.BlockSpec((bd, WIDTH), lambda j: (0, j)),
        out_shape=jax.ShapeDtypeStruct((bd, n), BF16),
        name="sample_gates",
    )(z)


def kernel(x_prompt, x_sample, mem_prompt, cache_win1_k, cache_win1_v, cache_win4_k, cache_win4_v, cache_win16_k, cache_win16_v, cache_mem_k, cache_mem_v, state_hgrn, g_ff1, w_ff1_gate, w_ff1_up, w_ff1_down, g_mix, w_in, hg_lb_logits, g_hg_out, g_mem, w_mem_k, w_mem_v, w_branch_att, w_branch_hg, w_branch_mem, w_out, g_ff2, w_ff2_gate, w_ff2_up, w_ff2_down, g_final):
    b, t, _ = x_prompt.shape
    bd = x_sample.shape[0]
    assert g_ff1.shape[0] == 1 and x_sample.shape[1] == 1 and w_in.shape[2] == IN_WIDTH
    bf = lambda w: w[0].astype(BF16)
    g_fin = g_final.reshape(1, D_MODEL)
    branch_w = (bf(w_branch_att), bf(w_branch_hg), bf(w_branch_mem), bf(w_out))

    xs, w_ff1_b = _sample_ffn(x_sample.reshape(bd, D_MODEL), g_ff1, w_ff1_gate, w_ff1_up, w_ff1_down, g_fin,
                              final=False)
    zs, w_qkv, w_rest = _sample_in_proj(xs, g_mix, w_in, COL_HQ)
    caches_k = (cache_win1_k, cache_win4_k, cache_win16_k)
    caches_v = (cache_win1_v, cache_win4_v, cache_win16_v)
    z3 = zs.reshape(bd, 1, IN_WIDTH)
    att_s, mem_s = _sample_attend(z3, caches_k, caches_v, cache_mem_k, cache_mem_v)
    hg_s, s_hgrn = _sample_hgrn(zs, hg_lb_logits, g_hg_out, state_hgrn)
    xs = _merge(xs, att_s.reshape(bd, WIDTH), hg_s, mem_s.reshape(bd, WIDTH), _sample_gates(zs), *branch_w, tm=bd)
    y_sample, w_ff2_b = _sample_ffn(xs, g_ff2, w_ff2_gate, w_ff2_up, w_ff2_down, g_fin, final=True)
    y_sample = y_sample.reshape(bd, 1, D_MODEL)
    ffn1 = (g_ff1, *w_ff1_b, g_fin)
    ffn2 = (g_ff2, *w_ff2_b, g_fin)

    tm = b * t // bd
    xp, (s_win16_k,) = _ffn(x_prompt.reshape(b * t, D_MODEL), *ffn1, tm=tm, final=False, ride=(cache_win16_k,))
    qkv = _qkv_proj(xp.reshape(b, t, D_MODEL), g_mix, w_qkv, tm=512)
    q_g, k_g, v_g, p_k, p_v = qkv[0:3], qkv[3:6], qkv[6:9], qkv[9:12], qkv[12:15]
    (hq, hk, lf, hv, mq, gates), (s_win1_k, s_win1_v, s_win4_k, s_win4_v), outs, lses = _rest_att(
        xp, g_mix, w_rest, hg_lb_logits, q_g, k_g, v_g, tm=tm,
        ride=(cache_win1_k, cache_win1_v, cache_win4_k, cache_win4_v))
    hg_p, p_hgrn = _hgrn_prompt(hq, hk, lf, hv, g_hg_out, b, t)
    p_mem_k, p_mem_v, mk_b, mv_b = _mem_kv(mem_prompt, g_mem, bf(w_mem_k), bf(w_mem_v))
    xp = _mix(xp, outs, lses, mq, mk_b, mv_b, hg_p, gates, *branch_w, b, t, tm=512)
    y_prompt, (s_win16_v,) = _ffn(xp, *ffn2, tm=tm, final=True, ride=(cache_win16_v,))
    s_win1_k, s_win1_v, s_win4_k, s_win4_v, s_win16_k, s_win16_v = _write_new_rows(
        z3, [s_win1_k, s_win1_v, s_win4_k, s_win4_v, s_win16_k, s_win16_v],
        ((0, False), (0, True), (1, False), (1, True), (2, False), (2, True)))

    return (y_prompt.reshape(b, t, D_MODEL), y_sample,
            p_k[0], p_v[0], p_k[1], p_v[1], p_k[2], p_v[2],
            p_mem_k, p_mem_v, p_hgrn,
            s_win1_k, s_win1_v, s_win4_k, s_win4_v, s_win16_k, s_win16_v,
            s_hgrn)
```

```python
import functools

import numpy as np
import jax
import jax.numpy as jnp
from jax import lax
from jax.experimental import pallas as pl
from jax.experimental.pallas import tpu as pltpu

F32 = jnp.float32
BF16 = jnp.bfloat16

D_MODEL = 1024
D_FF = 2816
EPS = 1e-6
HEADS = 4
HEAD_DIM = 128
WIDTH = HEADS * HEAD_DIM
ATT_GROUPS = ((128, 1), (512, 4), (2048, 16))
N_GROUPS = 3
BLOCK = 128
MEM_LEN = 256
N_BRANCH = 3
LSE_REP = 32
QK_SCALE = HEAD_DIM ** -0.5
VMEM_LIMIT_BYTES = 56 * 1024 * 1024

COL_Q, COL_K, COL_V, COL_HQ, COL_HF, COL_HI, COL_MQ, COL_GATE = 0, 3, 6, 9, 10, 11, 12, 13
IN_WIDTH = (COL_GATE + N_BRANCH * D_MODEL // WIDTH) * WIDTH


def _params(*sem):
    return pltpu.CompilerParams(dimension_semantics=sem, vmem_limit_bytes=VMEM_LIMIT_BYTES)


def _const_spec(shape):
    nd = len(shape)
    return pl.BlockSpec(shape, lambda *_: (0,) * nd, pipeline_mode=pl.Buffered(1))


def _rms(x, g):
    return x * lax.rsqrt(jnp.mean(x * x, axis=-1, keepdims=True) + EPS) * g


def _dot(a, b):
    return jnp.dot(a, b, preferred_element_type=F32)


def _dot_nt(a, b):
    return lax.dot_general(a, b, (((1,), (1,)), ((), ())), preferred_element_type=F32)


def _ride_specs(caches, n_steps):
    if not caches:
        return [], [], [], []
    assert all(c.shape[1] == n_steps for c in caches)
    in_specs = [pl.BlockSpec((1, 1, c.shape[2], HEADS, HEAD_DIM), lambda i: (0, i, 0, 0, 0)) for c in caches]
    out_specs = [pl.BlockSpec(memory_space=pl.ANY)] * len(caches)
    shapes = [jax.ShapeDtypeStruct(c.shape, c.dtype) for c in caches]
    return in_specs, out_specs, shapes, [pltpu.SemaphoreType.DMA((2 * len(caches),))]


def _ride_copies(in_refs, out_refs, sem):
    seq = pl.program_id(0)
    copies = []
    for j, (cache_ref, out_ref) in enumerate(zip(in_refs, out_refs)):
        rows = cache_ref.shape[2]
        copies.append(pltpu.make_async_copy(cache_ref.at[:, :, pl.ds(1, rows - 1)],
                                            out_ref.at[:, pl.ds(seq, 1), pl.ds(0, rows - 1)], sem.at[2 * j]))
        copies.append(pltpu.make_async_copy(cache_ref.at[:, :, pl.ds(rows - 1, 1)],
                                            out_ref.at[:, pl.ds(seq, 1), pl.ds(rows - 1, 1)], sem.at[2 * j + 1]))
    for cp in copies:
        cp.start()
    return copies


def _new_rows_body(kn_ref, vn_ref, *refs, meta):
    n = len(meta)
    out_refs, row_refs, sem = refs[n:2 * n], refs[2 * n:3 * n], refs[3 * n]
    copies = []
    for j, ((g, is_v), out_ref, row_ref) in enumerate(zip(meta, out_refs, row_refs)):
        bd, rows = out_ref.shape[1], out_ref.shape[2]
        new = (vn_ref if is_v else kn_ref)[:, 0, g * WIDTH:(g + 1) * WIDTH]
        flat = row_ref.reshape(bd * HEADS, HEAD_DIM)
        for h in range(HEADS):
            flat[pl.ds(h, bd, stride=HEADS), :] = new[:, h * HEAD_DIM:(h + 1) * HEAD_DIM]
        copies.append(pltpu.make_async_copy(row_ref, out_ref.at[:, :, pl.ds(rows - 1, 1)], sem.at[j]))
        copies[-1].start()
    for cp in copies:
        cp.wait()


def _write_new_rows(z3, shifted, meta):
    bd = z3.shape[0]
    n = len(shifted)
    zspec = lambda col: pl.BlockSpec((bd, 1, N_GROUPS * WIDTH), lambda i: (0, 0, col // N_GROUPS))
    any_spec = pl.BlockSpec(memory_space=pl.ANY)
    return pl.pallas_call(
        functools.partial(_new_rows_body, meta=meta),
        grid=(1,),
        in_specs=[zspec(COL_K), zspec(COL_V)] + [any_spec] * n,
        out_specs=[any_spec] * n,
        out_shape=[jax.ShapeDtypeStruct(c.shape, c.dtype) for c in shifted],
        input_output_aliases={2 + j: j for j in range(n)},
        scratch_shapes=[pltpu.VMEM((1, bd, 1, HEADS, HEAD_DIM), F32)] * n + [pltpu.SemaphoreType.DMA((n,))],
        name="write_new_rows",
    )(z3, z3, *shifted)


FF_CHUNKS = 1


def _ffn_body(x_ref, g_ref, wg_ref, wu_ref, wd_ref, gf_ref, *rest, final, n_ride):
    o_ref = rest[n_ride]
    copies = _ride_copies(rest[:n_ride], rest[n_ride + 1:2 * n_ride + 1], rest[-1]) if n_ride else []
    x = x_ref[...]
    h = _rms(x, g_ref[...]).astype(BF16)
    y = x
    width = D_FF // FF_CHUNKS
    for c in range(FF_CHUNKS):
        cols = slice(c * width, (c + 1) * width)
        a = _dot(h, wg_ref[:, cols])
        u = _dot(h, wu_ref[:, cols])
        act = (a * jax.nn.sigmoid(a) * u).astype(BF16)
        y = y + 0.5 * _dot(act, wd_ref[cols, :])
    if final:
        y = _rms(y, gf_ref[...])
    o_ref[...] = y
    for cp in copies:
        cp.wait()


def _ffn(x, g, wg, wu, wd, g_final, *, tm, final, ride=()):
    m = x.shape[0]
    r_in, r_out, r_shapes, r_scratch = _ride_specs(ride, m // tm)
    res = pl.pallas_call(
        functools.partial(_ffn_body, final=final, n_ride=len(ride)),
        grid=(m // tm,),
        in_specs=[pl.BlockSpec((tm, D_MODEL), lambda i: (i, 0)),
                  _const_spec((1, D_MODEL)),
                  _const_spec((D_MODEL, D_FF)), _const_spec((D_MODEL, D_FF)), _const_spec((D_FF, D_MODEL)),
                  _const_spec((1, D_MODEL))] + r_in,
        out_specs=[pl.BlockSpec((tm, D_MODEL), lambda i: (i, 0))] + r_out,
        out_shape=[jax.ShapeDtypeStruct((m, D_MODEL), F32)] + r_shapes,
        scratch_shapes=r_scratch,
        compiler_params=_params("parallel"),
        name="ffn_final" if final else "ffn",
    )(x, g, wg, wu, wd, g_final, *ride)
    return res[0], res[1:]


def _store_heads(ref, z):
    rows = ref.shape[0]
    flat = ref.reshape(rows * HEADS, HEAD_DIM)
    for h in range(HEADS):
        flat[pl.ds(h, rows, stride=HEADS), :] = z[:, h * HEAD_DIM:(h + 1) * HEAD_DIM]


def _load_head(ref, h):
    rows = ref.shape[0]
    return ref.reshape(rows * HEADS, HEAD_DIM)[pl.ds(h, rows, stride=HEADS), :]


SUBLANES = 8


def _stage_pitch(dil):
    return dil + SUBLANES // 2 if dil % (2 * SUBLANES) == 0 else dil


def _stage_shape(rows, dil, width):
    return (width // 128, rows // dil * _stage_pitch(dil), 128)


def _to_strided_view(dst_ref, z, stage_ref, dil):
    if dil == 1:
        dst_ref[...] = z.astype(dst_ref.dtype)
        return
    n = z.shape[0] // dil
    pitch = _stage_pitch(dil)
    for c in range(stage_ref.shape[0]):
        if pitch == dil:
            stage_ref[c] = z[:, c * 128:(c + 1) * 128]
        else:
            for l in range(n):
                stage_ref[c, l * pitch:l * pitch + dil, :] = z[l * dil:(l + 1) * dil, c * 128:(c + 1) * 128]
    for r in range(dil):
        for c in range(stage_ref.shape[0]):
            dst_ref[:, r * WIDTH + c * 128:r * WIDTH + (c + 1) * 128] = (
                stage_ref[c, pl.ds(r, n, stride=pitch), :].astype(dst_ref.dtype))


def _qkv_body(x_ref, g_ref, w_ref, *refs, tm, n_tiles):
    q_refs, k_refs, v_refs = refs[0:3], refs[3:6], refs[6:9]
    pk_refs, pv_refs, stage_refs = refs[9:12], refs[12:15], refs[15:]
    h = _rms(x_ref[...], g_ref[...]).astype(BF16)
    late = []
    for g, (win, dil) in reversed(list(enumerate(ATT_GROUPS))):
        keep = min(win, tm * n_tiles)
        for kind, (col, dst_refs, win_refs) in enumerate(((COL_Q, q_refs, None), (COL_K, k_refs, pk_refs),
                                                          (COL_V, v_refs, pv_refs))):
            z = _dot(h, w_ref[:, (col + g) * WIDTH:(col + g + 1) * WIDTH])
            if win_refs is None:
                z = z * QK_SCALE
            stage = stage_refs[3 * (g - 1) + kind] if dil > 1 else None
            _to_strided_view(dst_refs[g], z, stage, dil)
            if win_refs is None:
                continue
            if keep == tm * n_tiles:
                _store_heads(win_refs[g], z)
            elif dil == 1:
                late.append((win_refs[g], z[tm - keep:, :], None))
            else:
                assert keep == tm
                late.append((win_refs[g], None, stage))

    @pl.when(pl.program_id(1) == n_tiles - 1)
    def _():
        for ref, rows, stage in late:
            if stage is None:
                _store_heads(ref, rows)
            else:
                _store_heads(ref, jnp.concatenate([stage[hd] for hd in range(HEADS)], axis=-1))


def _qkv_proj(x, g_mix, w_qkv, *, tm):
    b, t, _ = x.shape
    n_tiles = t // tm
    view_specs = [pl.BlockSpec((None, tm // d, d * WIDTH), lambda bi, i: (bi, i, 0)) for _, d in ATT_GROUPS]
    view_shapes = [jax.ShapeDtypeStruct((b, t // d, d * WIDTH), BF16) for _, d in ATT_GROUPS]
    win_specs, win_shapes = [], []
    for win, _ in ATT_GROUPS:
        keep = min(win, t)
        assert keep == t or keep <= tm
        if keep == t:
            win_specs.append(pl.BlockSpec((None, None, tm, HEADS, HEAD_DIM), lambda bi, i: (0, bi, i, 0, 0)))
        else:
            win_specs.append(pl.BlockSpec((None, None, keep, HEADS, HEAD_DIM), lambda bi, i: (0, bi, 0, 0, 0)))
        win_shapes.append(jax.ShapeDtypeStruct((1, b, keep, HEADS, HEAD_DIM), F32))
    return pl.pallas_call(
        functools.partial(_qkv_body, tm=tm, n_tiles=n_tiles),
        grid=(b, n_tiles),
        in_specs=[pl.BlockSpec((None, tm, D_MODEL), lambda bi, i: (bi, i, 0)),
                  _const_spec((1, D_MODEL)),
                  _const_spec((D_MODEL, 3 * N_GROUPS * WIDTH))],
        out_specs=view_specs * 3 + win_specs + win_specs,
        out_shape=view_shapes * 3 + win_shapes + win_shapes,
        scratch_shapes=[pltpu.VMEM(_stage_shape(tm, d, WIDTH), F32) for _, d in ATT_GROUPS if d > 1 for _ in range(3)],
        compiler_params=_params("parallel", "arbitrary"),
        name="qkv_proj",
    )(x, g_mix, w_qkv)


def _lower_bound(logits_ref):
    lg = logits_ref[...]
    e = jnp.exp(lg - jnp.max(lg, axis=0, keepdims=True))
    return e[0:1, :] / jnp.sum(e, axis=0, keepdims=True)


def _rest_steps(x_ref, g_ref, w_ref, lb_ref, hq_ref, hk_ref, lf_ref, hv_ref, mq_ref, gate_ref):
    h = _rms(x_ref[...], g_ref[...]).astype(BF16)
    lb = _lower_bound(lb_ref)
    proj = lambda c: _dot(h, w_ref[:, c * WIDTH:(c + 1) * WIDTH])

    def hgrn_query():
        hq_ref[...] = jax.nn.sigmoid(proj(0)).astype(BF16)

    def hgrn_key():
        kk = (1.0 - lb) * jax.nn.sigmoid(proj(1))
        hk_ref[...] = kk.astype(BF16)
        lf_ref[...] = jnp.log(lb + kk)

    def hgrn_value():
        hv_ref[...] = proj(2).astype(BF16)

    def mem_query():
        mq_ref[...] = (proj(3) * QK_SCALE).astype(BF16)

    def gate(c):
        gate_ref[:, c * WIDTH:(c + 1) * WIDTH] = jax.nn.sigmoid(proj(4 + c)).astype(BF16)

    return [hgrn_query, hgrn_key, hgrn_value, mem_query] + [functools.partial(gate, c)
                                                            for c in range(N_BRANCH * D_MODEL // WIDTH)]


def _rest_att_body(x_ref, g_ref, w_ref, lb_ref, *rest, n_ride, tiles, tiles_per_seq):
    n_att_in = sum(5 if has_halo else 3 for _, _, has_halo in tiles)
    caches, rest = rest[:n_ride], rest[n_ride:]
    att_in, rest = rest[:n_att_in], rest[n_att_in:]
    proj_out, rest = rest[:6], rest[6:]
    ride_out, rest = rest[:n_ride], rest[n_ride:]
    att_out, scratch = rest[:2 * len(tiles)], rest[2 * len(tiles):]
    copies = _ride_copies(caches, ride_out, scratch[-1]) if n_ride else []
    steps = _rest_steps(x_ref, g_ref, w_ref, lb_ref, *proj_out)
    gens = _att_tiles(att_in + att_out, tiles, pl.program_id(0) % tiles_per_seq)
    steps.pop(0)()
    for gen in gens:
        next(gen)
        steps.pop(0)()
        next(gen)
        steps.pop(0)()
        next(gen, None)
    for step in steps:
        step()
    for cp in copies:
        cp.wait()


def _rest_att(x, g_mix, w_rest, lb_logits, q_g, k_g, v_g, *, tm, ride=()):
    m = x.shape[0]
    spec = pl.BlockSpec((tm, WIDTH), lambda i: (i, 0))
    bf = jax.ShapeDtypeStruct((m, WIDTH), BF16)
    r_in, r_out, r_shapes, r_scratch = _ride_specs(ride, m // tm)
    tiles_per_seq = q_g[0].shape[1] // (ATT_UNITS * BLOCK)
    a_in, a_args, a_out, a_shapes, tiles, n_tiles = _attention_specs(
        q_g, k_g, v_g, lambda i: (i // tiles_per_seq, i % tiles_per_seq))
    assert n_tiles == tiles_per_seq and q_g[0].shape[0] * n_tiles == m // tm
    res = pl.pallas_call(
        functools.partial(_rest_att_body, n_ride=len(ride), tiles=tiles, tiles_per_seq=tiles_per_seq),
        grid=(m // tm,),
        in_specs=[pl.BlockSpec((tm, D_MODEL), lambda i: (i, 0)),
                  _const_spec((1, D_MODEL)),
                  _const_spec(w_rest.shape),
                  _const_spec(lb_logits.shape)] + r_in + a_in,
        out_specs=[spec, spec, spec, spec, spec, pl.BlockSpec((tm, N_BRANCH * D_MODEL), lambda i: (i, 0))]
                  + r_out + a_out,
        out_shape=[bf, bf, jax.ShapeDtypeStruct((m, WIDTH), F32), bf, bf,
                   jax.ShapeDtypeStruct((m, N_BRANCH * D_MODEL), BF16)] + r_shapes + a_shapes,
        scratch_shapes=r_scratch,
        compiler_params=_params("arbitrary"),
        name="rest_att",
    )(x, g_mix, w_rest, lb_logits, *ride, *a_args)
    n = 6 + len(ride)
    return res[:6], res[6:n], res[n::2], res[n + 1::2]


ATT_UNITS = 4
STAT_REP = LSE_REP // 2


def _att_tiles(refs, tiles, tile_index):
    n_in = [5 if has_halo else 3 for _, _, has_halo in tiles]
    outs = refs[sum(n_in):]
    gens, pos = [], 0
    for g, (n_row_blocks, n_col_blocks, has_halo) in enumerate(tiles):
        gens.append(_att_tile(*refs[pos:pos + n_in[g]], outs[2 * g], outs[2 * g + 1], n_row_blocks=n_row_blocks,
                              n_col_blocks=n_col_blocks, has_halo=has_halo, tile_index=tile_index))
        pos += n_in[g]
    return gens


def _att_tile(q_ref, k_ref, v_ref, *rest, n_row_blocks, n_col_blocks, has_halo, tile_index):
    if has_halo:
        kh_ref, vh_ref, o_ref, st_ref = rest
        halo_ok = tile_index > 0
    else:
        o_ref, st_ref = rest
    qi = lax.broadcasted_iota(jnp.int32, (BLOCK, 2 * BLOCK), 0)
    ki = lax.broadcasted_iota(jnp.int32, (BLOCK, 2 * BLOCK), 1)
    own_ok = (lax.broadcasted_iota(jnp.int32, (BLOCK, BLOCK), 1)
              <= lax.broadcasted_iota(jnp.int32, (BLOCK, BLOCK), 0))
    both_ok = jnp.where(ki < BLOCK, ki, qi + BLOCK) >= jnp.where(ki < BLOCK, qi, ki)
    if has_halo:
        first_valid = jnp.where(halo_ok, 0, BLOCK)
        halo_both_ok = jnp.logical_and(both_ok, ki >= first_valid)

    units = []
    for c in range(n_col_blocks):
        for u in range(n_row_blocks):
            for h in range(HEADS):
                rows = slice(u * BLOCK, (u + 1) * BLOCK)
                cols = slice(c * WIDTH + h * HEAD_DIM, c * WIDTH + (h + 1) * HEAD_DIM)
                lcols = (c * HEADS + h) * LSE_REP
                if u > 0:
                    with_prev = slice((u - 1) * BLOCK, (u + 1) * BLOCK)
                    keys = lambda ref, rr=with_prev, cc=cols: ref[rr, cc]
                    ok = both_ok
                elif has_halo:
                    hcols = slice(h * HEAD_DIM, (h + 1) * HEAD_DIM)
                    keys = lambda ref, rr=rows, cc=cols, hc=hcols: jnp.concatenate(
                        [(kh_ref if ref is k_ref else vh_ref)[:, hc], ref[rr, cc]], axis=0)
                    ok = halo_both_ok
                else:
                    keys = lambda ref, rr=rows, cc=cols: ref[rr, cc]
                    ok = own_ok
                units.append((rows, cols, lcols, keys, ok))

    scores = [jnp.where(ok, _dot_nt(q_ref[rows, cols], keys(k_ref)), -jnp.inf) for rows, cols, _, keys, ok in units]
    yield

    probs = []
    for s in scores:
        m = jnp.max(s, axis=-1, keepdims=True)
        p = jnp.exp(s - m)
        probs.append((p.astype(BF16), m, jnp.sum(p, axis=-1, keepdims=True)))
    yield

    for (rows, cols, lcols, keys, _), (p, m, l) in zip(units, probs):
        o_ref[rows, cols] = _dot(p, keys(v_ref)).astype(BF16)
        st_ref[rows, lcols:lcols + STAT_REP] = jnp.broadcast_to(m, (BLOCK, STAT_REP))
        st_ref[rows, lcols + STAT_REP:lcols + LSE_REP] = jnp.broadcast_to(l, (BLOCK, STAT_REP))


def _attention_specs(q_g, k_g, v_g, to_bj):
    b = q_g[0].shape[0]
    lw = HEADS * LSE_REP
    n_steps = None
    in_specs, args, out_specs, out_shapes, tiles = [], [], [], [], []

    def spec(shape, index):
        return pl.BlockSpec(shape, lambda *grid_idx: index(*to_bj(*grid_idx)))

    for (_, dil), q, k, v in zip(ATT_GROUPS, q_g, k_g, v_g):
        sub_len = q.shape[1]
        n_row_blocks = min(ATT_UNITS, sub_len // BLOCK)
        n_col_blocks = ATT_UNITS // n_row_blocks
        span = n_row_blocks * BLOCK
        assert sub_len % span == 0 and dil % n_col_blocks == 0
        has_halo = sub_len > span
        assert not has_halo or (dil == 1 and n_col_blocks == 1)
        n = (sub_len // span) * (dil // n_col_blocks)
        assert n_steps in (None, n)
        n_steps = n
        if has_halo:
            tile = lambda w, span=span: spec((None, span, w), lambda bi, j: (bi, j, 0))
        else:
            tile = lambda w, span=span, nc=n_col_blocks: spec((None, span, nc * w), lambda bi, j: (bi, 0, j))
        halo = spec((None, BLOCK, WIDTH), lambda bi, j, nr=n_row_blocks: (bi, jnp.maximum(j * nr - 1, 0), 0))
        in_specs += [tile(WIDTH)] * 3 + ([halo, halo] if has_halo else [])
        args += [q, k, v] + ([k, v] if has_halo else [])
        out_specs += [tile(WIDTH), tile(lw)]
        out_shapes += [jax.ShapeDtypeStruct((b, sub_len, dil * WIDTH), BF16),
                       jax.ShapeDtypeStruct((b, sub_len, dil * lw), F32)]
        tiles.append((n_row_blocks, n_col_blocks, has_halo))
    return in_specs, args, out_specs, out_shapes, tuple(tiles), n_steps


N_LEVELS = 7
ROW_CUM, ROW_REM = N_LEVELS, N_LEVELS + 1


def _hgrn_constants():
    r = np.arange(BLOCK)[:, None]
    s = np.arange(BLOCK)[None, :]
    mats = []
    for lvl in range(N_LEVELS):
        half = 1 << lvl
        mid = (r // (2 * half)) * 2 * half + half
        upper = (r % (2 * half)) >= half
        mats.append(np.where(upper, (s >= mid) & (s <= r), (s > r) & (s <= mid - 1)))
    mats.append(s <= r)
    mats.append(s > r)
    range_mat = np.concatenate(mats, axis=0).astype(np.float32)
    range_mat = np.concatenate([range_mat, range_mat], axis=1)
    x = r ^ s
    level = np.where(s < r, np.floor(np.log2(np.maximum(x, 1))).astype(np.int32), np.where(s == r, N_LEVELS, -1))
    return jnp.asarray(range_mat, BF16), jnp.asarray(level, jnp.int32)


HGRN_SEQS = 4


def _hgrn_intra(q_b, k_b, lf, rng, level):
    lf_hi = lf.astype(BF16)
    lf_lo = (lf - lf_hi.astype(F32)).astype(BF16)
    z = _dot(rng, jnp.concatenate([lf_hi, lf_lo], axis=0))
    q = q_b.astype(F32)
    kk = k_b.astype(F32)
    lhs, rhs = [], []
    for lvl in range(N_LEVELS):
        x = jnp.exp(z[lvl * BLOCK:(lvl + 1) * BLOCK]).astype(BF16)
        lhs.append(q_b * x)
        rhs.append(k_b * x)
    lhs.append(q_b)
    rhs.append(k_b)
    b_cum = z[ROW_CUM * BLOCK:(ROW_CUM + 1) * BLOCK]
    q_in = (q * jnp.exp(b_cum)).astype(BF16)
    k_out = (kk * jnp.exp(z[ROW_REM * BLOCK:(ROW_REM + 1) * BLOCK])).astype(BF16)
    decay = jnp.exp(b_cum[BLOCK - 1:BLOCK, :])
    mats = []
    for h in range(HEADS):
        sl = slice(h * HEAD_DIM, (h + 1) * HEAD_DIM)
        a = jnp.zeros((BLOCK, BLOCK), F32)
        for lvl in range(N_LEVELS + 1):
            a = jnp.where(level == lvl, _dot_nt(lhs[lvl][:, sl], rhs[lvl][:, sl]), a)
        mats.append(a.astype(BF16))
    return mats, q_in, k_out, decay


def _hgrn_state(mats, q_in, k_out, decay, v_b, gain, st_ref, j):
    outs = []
    for h in range(HEADS):
        sl = slice(h * HEAD_DIM, (h + 1) * HEAD_DIM)
        v = v_b[:, sl]
        st = st_ref[j, h]
        o = _dot(mats[h], v) + _dot_nt(q_in[:, sl], st.astype(BF16))
        outs.append(_rms(o, gain[:, sl]).astype(BF16))
        v_t = v.astype(F32).T.astype(BF16)
        st_ref[j, h] = st * decay[:, sl] + _dot(v_t, k_out[:, sl])
    return jnp.concatenate(outs, axis=-1)


def _hgrn_body(q_ref, k_ref, lf_ref, v_ref, rng_ref, lvl_ref, gain_ref, o_ref, s_ref, st_ref, *, n_chunks):
    c = pl.program_id(1)

    @pl.when(c == 0)
    def _():
        st_ref[...] = jnp.zeros_like(st_ref)

    rng, level, gain = rng_ref[...], lvl_ref[...], gain_ref[...]
    intra = [_hgrn_intra(q_ref[j], k_ref[j], lf_ref[j], rng, level) for j in range(HGRN_SEQS)]
    for j in range(HGRN_SEQS):
        o_ref[j] = _hgrn_state(*intra[j], v_ref[j], gain, st_ref, j)

    @pl.when(c == n_chunks - 1)
    def _():
        for j in range(HGRN_SEQS):
            for h in range(HEADS):
                s_ref[j, h] = st_ref[j, h].T


def _hgrn_prompt(hq, hk, lf, hv, gain, b, t):
    n_chunks = t // BLOCK
    range_mat, level = _hgrn_constants()
    as3d = lambda a: a.reshape(b, t, WIDTH)
    spec = pl.BlockSpec((HGRN_SEQS, BLOCK, WIDTH), lambda bi, c: (bi, c, 0))
    state_spec = pl.BlockSpec((None, HGRN_SEQS, HEADS, HEAD_DIM, HEAD_DIM), lambda bi, c: (0, bi, 0, 0, 0))
    o, state = pl.pallas_call(
        functools.partial(_hgrn_body, n_chunks=n_chunks),
        grid=(b // HGRN_SEQS, n_chunks),
        in_specs=[spec, spec, spec, spec, _const_spec(range_mat.shape), _const_spec(level.shape),
                  _const_spec((1, WIDTH))],
        out_specs=[spec, state_spec],
        out_shape=[jax.ShapeDtypeStruct((b, t, WIDTH), BF16),
                   jax.ShapeDtypeStruct((1, b, HEADS, HEAD_DIM, HEAD_DIM), F32)],
        scratch_shapes=[pltpu.VMEM((HGRN_SEQS, HEADS, HEAD_DIM, HEAD_DIM), F32)],
        compiler_params=_params("parallel", "arbitrary"),
        name="hgrn_prompt",
    )(as3d(hq), as3d(hk), as3d(lf), as3d(hv), range_mat, level, gain)
    return o.reshape(b * t, WIDTH), state


def _memkv_body(m_ref, g_ref, wk_ref, wv_ref, pk_ref, pv_ref, kb_ref, vb_ref):
    h = _rms(m_ref[...], g_ref[...]).astype(BF16)
    k = _dot(h, wk_ref[...].astype(BF16))
    v = _dot(h, wv_ref[...].astype(BF16))
    _store_heads(pk_ref, k)
    _store_heads(pv_ref, v)
    kb_ref[...] = k.astype(BF16)
    vb_ref[...] = v.astype(BF16)


def _mem_kv(mem, g_mem, w_mk, w_mv):
    b = mem.shape[0]
    head_spec = pl.BlockSpec((None, None, MEM_LEN, HEADS, HEAD_DIM), lambda bi: (0, bi, 0, 0, 0))
    flat_spec = pl.BlockSpec((None, MEM_LEN, WIDTH), lambda bi: (bi, 0, 0))
    head_shape = jax.ShapeDtypeStruct((1, b, MEM_LEN, HEADS, HEAD_DIM), F32)
    flat_shape = jax.ShapeDtypeStruct((b, MEM_LEN, WIDTH), BF16)
    return pl.pallas_call(
        _memkv_body,
        grid=(b,),
        in_specs=[pl.BlockSpec((None, MEM_LEN, D_MODEL), lambda bi: (bi, 0, 0)),
                  _const_spec((1, D_MODEL)),
                  pl.BlockSpec((None, D_MODEL, WIDTH), lambda bi: (0, 0, 0), pipeline_mode=pl.Buffered(1)),
                  pl.BlockSpec((None, D_MODEL, WIDTH), lambda bi: (0, 0, 0), pipeline_mode=pl.Buffered(1))],
        out_specs=[head_spec, head_spec, flat_spec, flat_spec],
        out_shape=[head_shape, head_shape, flat_shape, flat_shape],
        compiler_params=_params("parallel"),
        name="mem_kv",
    )(mem, g_mem, w_mk, w_mv)


def _from_strided_view(src_ref, stage_ref, dil, width):
    if dil == 1:
        return src_ref[...].astype(F32)
    n = src_ref.shape[0]
    n_col = width // 128
    pitch = _stage_pitch(dil)
    for r in range(dil):
        for c in range(n_col):
            stage_ref[c, pl.ds(r, n, stride=pitch), :] = (
                src_ref[:, r * width + c * 128:r * width + (c + 1) * 128].astype(F32))

    def natural(c):
        if pitch == dil:
            return stage_ref[c]
        return jnp.concatenate([stage_ref[c, l * pitch:l * pitch + dil, :] for l in range(n)], axis=0)

    return jnp.concatenate([natural(c) for c in range(n_col)], axis=-1)


def _mix_body(x_ref, o0_ref, o1_ref, o2_ref, s0_ref, s1_ref, s2_ref, mq_ref, mk_ref, mv_ref, hg_ref, gate_ref,
              wa_ref, wb_ref, wc_ref, wo_ref, out_ref, *stage_refs):
    def gate(j):
        return gate_ref[:, j * D_MODEL:(j + 1) * D_MODEL].astype(F32)

    heads = [slice(h * HEAD_DIM, (h + 1) * HEAD_DIM) for h in range(HEADS)]
    merged = gate(1) * _dot(hg_ref[...], wb_ref[...])
    scores = [_dot_nt(mq_ref[:, sl], mk_ref[:, sl]) for sl in heads]
    probs = []
    for s in scores:
        p = jnp.exp(s - jnp.max(s, axis=-1, keepdims=True))
        probs.append((p.astype(BF16), 1.0 / jnp.sum(p, axis=-1, keepdims=True)))
    mem = [(_dot(p, mv_ref[:, sl]) * inv).astype(BF16) for (p, inv), sl in zip(probs, heads)]
    merged = merged + gate(2) * _dot(jnp.concatenate(mem, axis=-1), wc_ref[...])

    accs = [_from_strided_view(r, stage_refs[2 * g], ATT_GROUPS[g][1], WIDTH)
            for g, r in enumerate((o0_ref, o1_ref, o2_ref))]
    stats = [_from_strided_view(r, stage_refs[2 * g + 1], ATT_GROUPS[g][1], HEADS * LSE_REP)
             for g, r in enumerate((s0_ref, s1_ref, s2_ref))]
    att = []
    for h, sl in enumerate(heads):
        ms = [x[:, h * LSE_REP:h * LSE_REP + 1] for x in stats]
        ls = [x[:, h * LSE_REP + STAT_REP:h * LSE_REP + STAT_REP + 1] for x in stats]
        m = jnp.maximum(jnp.maximum(ms[0], ms[1]), ms[2])
        e = [jnp.exp(x - m) for x in ms]
        inv = 1.0 / (e[0] * ls[0] + e[1] * ls[1] + e[2] * ls[2])
        att.append(sum((e[g] * inv) * accs[g][:, sl] for g in range(N_GROUPS)).astype(BF16))
    merged = merged + gate(0) * _dot(jnp.concatenate(att, axis=-1), wa_ref[...])
    out_ref[...] = x_ref[...] + _dot(merged.astype(BF16), wo_ref[...])


def _mix(x, accs, stats, mq, mk, mv, hg, gates, wa, wb, wc, wo, b, t, *, tm):
    n_tiles = t // tm
    lw = HEADS * LSE_REP
    rows = lambda w: pl.BlockSpec((tm, w), lambda bi, i: (bi * n_tiles + i, 0))
    o_specs = [pl.BlockSpec((None, tm // d, d * WIDTH), lambda bi, i: (bi, i, 0)) for _, d in ATT_GROUPS]
    s_specs = [pl.BlockSpec((None, tm // d, d * lw), lambda bi, i: (bi, i, 0)) for _, d in ATT_GROUPS]
    memspec = pl.BlockSpec((None, MEM_LEN, WIDTH), lambda bi, i: (bi, 0, 0))
    stages = []
    for _, d in ATT_GROUPS:
        stages += [pltpu.VMEM(_stage_shape(tm, d, WIDTH), F32), pltpu.VMEM(_stage_shape(tm, d, lw), F32)]
    return pl.pallas_call(
        _mix_body,
        grid=(b, n_tiles),
        in_specs=[rows(D_MODEL)] + o_specs + s_specs + [rows(WIDTH), memspec, memspec, rows(WIDTH),
                                                         rows(N_BRANCH * D_MODEL),
                                                         _const_spec((WIDTH, D_MODEL)), _const_spec((WIDTH, D_MODEL)),
                                                         _const_spec((WIDTH, D_MODEL)), _const_spec((D_MODEL, D_MODEL))],
        out_specs=rows(D_MODEL),
        out_shape=jax.ShapeDtypeStruct((b * t, D_MODEL), F32),
        scratch_shapes=stages,
        compiler_params=_params("parallel", "parallel"),
        name="mix",
    )(x, *accs, *stats, mq, mk, mv, hg, gates, wa, wb, wc, wo)


def _sample_merge_body(x_ref, att_ref, hg_ref, mem_ref, z_ref, wa_ref, wb_ref, wc_ref, wo_ref,
                       o_ref, wab_ref, wbb_ref, wcb_ref, wob_ref):
    wa, wb, wc, wo = (r[...].astype(BF16) for r in (wa_ref, wb_ref, wc_ref, wo_ref))
    wab_ref[...], wbb_ref[...], wcb_ref[...], wob_ref[...] = wa, wb, wc, wo
    gate = lambda j: jax.nn.sigmoid(z_ref[:, COL_GATE * WIDTH + j * D_MODEL:COL_GATE * WIDTH + (j + 1) * D_MODEL])
    m = (gate(0) * _dot(att_ref[...].astype(BF16), wa) + gate(1) * _dot(hg_ref[...].astype(BF16), wb)
         + gate(2) * _dot(mem_ref[...].astype(BF16), wc))
    o_ref[...] = x_ref[...] + _dot(m.astype(BF16), wo)


def _sample_merge(x, att, hg, mem, z, wa, wb, wc, wo):
    m = x.shape[0]
    full = lambda a: _const_spec(a.shape)
    w_specs = [pl.BlockSpec((None,) + w.shape[1:], lambda i: (0, 0, 0)) for w in (wa, wb, wc, wo)]
    b_specs = [pl.BlockSpec(w.shape[1:], lambda i: (0, 0)) for w in (wa, wb, wc, wo)]
    res = pl.pallas_call(
        _sample_merge_body,
        grid=(1,),
        in_specs=[full(x), full(att), full(hg), full(mem), full(z)] + w_specs,
        out_specs=[pl.BlockSpec((m, D_MODEL), lambda i: (0, 0))] + b_specs,
        out_shape=[jax.ShapeDtypeStruct((m, D_MODEL), F32)]
                  + [jax.ShapeDtypeStruct(w.shape[1:], BF16) for w in (wa, wb, wc, wo)],
        compiler_params=_params("arbitrary"),
        name="sample_merge",
    )(x, att, hg, mem, z, wa, wb, wc, wo)
    return res[0], res[1:]


def _sample_in_proj_body(x_ref, g_ref, w_ref, o_ref, wa_ref, wb_ref, *, n_a):
    w = w_ref[...].astype(BF16)
    o_ref[...] = _dot(_rms(x_ref[...], g_ref[...]).astype(BF16), w)

    @pl.when(pl.program_id(0) < n_a)
    def _():
        wa_ref[...] = w

    @pl.when(pl.program_id(0) >= n_a)
    def _():
        wb_ref[...] = w


def _sample_in_proj(x, g_mix, w_in, n_a):
    m = x.shape[0]
    n = w_in.shape[2] // WIDTH
    return pl.pallas_call(
        functools.partial(_sample_in_proj_body, n_a=n_a),
        grid=(n,),
        in_specs=[_const_spec((m, D_MODEL)), _const_spec((1, D_MODEL)),
                  pl.BlockSpec((None, D_MODEL, WIDTH), lambda j: (0, 0, j))],
        out_specs=[pl.BlockSpec((m, WIDTH), lambda j: (0, j)),
                   pl.BlockSpec((D_MODEL, WIDTH), lambda j: (0, jnp.minimum(j, n_a - 1))),
                   pl.BlockSpec((D_MODEL, WIDTH), lambda j: (0, jnp.maximum(j - n_a, 0)))],
        out_shape=[jax.ShapeDtypeStruct((m, n * WIDTH), F32),
                   jax.ShapeDtypeStruct((D_MODEL, n_a * WIDTH), BF16),
                   jax.ShapeDtypeStruct((D_MODEL, (n - n_a) * WIDTH), BF16)],
        compiler_params=_params("arbitrary"),
        name="sample_in_proj",
    )(x, g_mix, w_in)


FF_TILE = 256


def _sample_ffn_body(x_ref, g_ref, wg_ref, wu_ref, wd_ref, gf_ref, o_ref, wgb_ref, wub_ref, wdb_ref, acc_ref, *,
                     final):
    j = pl.program_id(0)

    @pl.when(j == 0)
    def _():
        acc_ref[...] = jnp.zeros_like(acc_ref)

    wg, wu, wd = wg_ref[...].astype(BF16), wu_ref[...].astype(BF16), wd_ref[...].astype(BF16)
    wgb_ref[...], wub_ref[...], wdb_ref[...] = wg, wu, wd
    x = x_ref[...]
    h = _rms(x, g_ref[...]).astype(BF16)
    a = _dot(h, wg)
    act = (a * jax.nn.sigmoid(a) * _dot(h, wu)).astype(BF16)
    acc_ref[...] += _dot(act, wd)

    @pl.when(j == pl.num_programs(0) - 1)
    def _():
        y = x + 0.5 * acc_ref[...]
        o_ref[...] = _rms(y, gf_ref[...]) if final else y


def _sample_ffn(x, g, wg, wu, wd, g_final, *, final):
    m = x.shape[0]
    res = pl.pallas_call(
        functools.partial(_sample_ffn_body, final=final),
        grid=(D_FF // FF_TILE,),
        in_specs=[_const_spec((m, D_MODEL)), _const_spec((1, D_MODEL)),
                  pl.BlockSpec((None, D_MODEL, FF_TILE), lambda j: (0, 0, j)),
                  pl.BlockSpec((None, D_MODEL, FF_TILE), lambda j: (0, 0, j)),
                  pl.BlockSpec((None, FF_TILE, D_MODEL), lambda j: (0, j, 0)),
                  _const_spec((1, D_MODEL))],
        out_specs=[pl.BlockSpec((m, D_MODEL), lambda j: (0, 0)),
                   pl.BlockSpec((D_MODEL, FF_TILE), lambda j: (0, j)),
                   pl.BlockSpec((D_MODEL, FF_TILE), lambda j: (0, j)),
                   pl.BlockSpec((FF_TILE, D_MODEL), lambda j: (j, 0))],
        out_shape=[jax.ShapeDtypeStruct((m, D_MODEL), F32),
                   jax.ShapeDtypeStruct((D_MODEL, D_FF), BF16), jax.ShapeDtypeStruct((D_MODEL, D_FF), BF16),
                   jax.ShapeDtypeStruct((D_FF, D_MODEL), BF16)],
        scratch_shapes=[pltpu.VMEM((m, D_MODEL), F32)],
        compiler_params=_params("arbitrary"),
        name="sample_ffn_final" if final else "sample_ffn",
    )(x, g, wg, wu, wd, g_final)
    return res[0], res[1:]


def _one_query_attend(q, k_rows, v_rows, k_new=None, v_new=None):
    s = jnp.sum(k_rows * q, axis=-1, keepdims=True)
    m = jnp.max(s, axis=0, keepdims=True)
    if k_new is not None:
        s_new = jnp.sum(k_new * q, axis=-1, keepdims=True)
        m = jnp.maximum(m, s_new)
    p = jnp.exp(s - m)
    l = jnp.sum(p, axis=0, keepdims=True)
    acc = jnp.sum(p * v_rows, axis=0, keepdims=True)
    if k_new is not None:
        p_new = jnp.exp(s_new - m)
        l = l + p_new
        acc = acc + p_new * v_new
    return acc / l, m + jnp.log(l)


def _sample_att_body(q_ref, kn_ref, vn_ref, mq_ref, k1_ref, v1_ref, k4_ref, v4_ref, k16_ref, v16_ref,
                     mk_ref, mv_ref, att_ref, mem_ref):
    k_refs = (k1_ref, k4_ref, k16_ref)
    v_refs = (v1_ref, v4_ref, v16_ref)
    for h in range(HEADS):
        sl = slice(h * HEAD_DIM, (h + 1) * HEAD_DIM)
        outs, lses = [], []
        for g in range(N_GROUPS):
            gs = slice(g * WIDTH + h * HEAD_DIM, g * WIDTH + (h + 1) * HEAD_DIM)
            o, lse = _one_query_attend(q_ref[:, gs] * QK_SCALE, _load_head(k_refs[g], h), _load_head(v_refs[g], h),
                                       kn_ref[:, gs], vn_ref[:, gs])
            outs.append(o)
            lses.append(lse)
        m = jnp.maximum(jnp.maximum(lses[0], lses[1]), lses[2])
        e = [jnp.exp(x - m) for x in lses]
        inv = 1.0 / (e[0] + e[1] + e[2])
        att_ref[:, sl] = sum((e[g] * inv) * outs[g] for g in range(N_GROUPS))
        o, _ = _one_query_attend(mq_ref[:, sl] * QK_SCALE, _load_head(mk_ref, h), _load_head(mv_ref, h))
        mem_ref[:, sl] = o


def _sample_attend(z3, caches_k, caches_v, cache_mk, cache_mv):
    bd = z3.shape[0]
    zspec = lambda col, n: pl.BlockSpec((None, 1, n * WIDTH), lambda b: (b, 0, col // n))
    cache_specs, cache_args = [], []
    for (win, dil), ck, cv in zip(ATT_GROUPS, caches_k, caches_v):
        rows = ck.shape[2]
        assert rows == win and rows // dil == BLOCK
        spec = pl.BlockSpec((None, None, BLOCK, None, HEADS, HEAD_DIM), lambda b: (0, b, 0, 0, 0, 0))
        for c in (ck, cv):
            cache_specs.append(spec)
            cache_args.append(c.reshape(1, bd, BLOCK, dil, HEADS, HEAD_DIM))
    mem_spec = pl.BlockSpec((None, None, MEM_LEN, HEADS, HEAD_DIM), lambda b: (0, b, 0, 0, 0))
    out_spec = pl.BlockSpec((None, 1, WIDTH), lambda b: (b, 0, 0))
    out_shape = jax.ShapeDtypeStruct((bd, 1, WIDTH), F32)
    return pl.pallas_call(
        _sample_att_body,
        grid=(bd,),
        in_specs=[zspec(COL_Q, 3), zspec(COL_K, 3), zspec(COL_V, 3), zspec(COL_MQ, 1)] + cache_specs
                 + [mem_spec, mem_spec],
        out_specs=[out_spec, out_spec],
        out_shape=[out_shape, out_shape],
        compiler_params=_params("parallel"),
        name="sample_attend",
    )(z3, z3, z3, z3, *cache_args, cache_mk, cache_mv)


SAMPLE_GROUP = 8


def _sample_hgrn_body(zq_ref, zf_ref, zi_ref, lb_ref, gain_ref, s_ref, o_ref, so_ref):
    lb = _lower_bound(lb_ref)
    gain = gain_ref[...]
    q = jax.nn.sigmoid(zq_ref[...])
    kk = (1.0 - lb) * jax.nn.sigmoid(zf_ref[...])
    f = lb + kk
    v = zi_ref[...]
    pad = jnp.zeros((HEAD_DIM - SAMPLE_GROUP, HEAD_DIM), F32)

    def columns(a, h):
        return jnp.concatenate([a[:, h * HEAD_DIM:(h + 1) * HEAD_DIM], pad], axis=0).T

    for h in range(HEADS):
        sl = slice(h * HEAD_DIM, (h + 1) * HEAD_DIM)
        q_t, k_t, f_t = columns(q, h), columns(kk, h), columns(f, h)
        for j in range(SAMPLE_GROUP):
            s1 = f_t[:, j:j + 1] * s_ref[j, h] + k_t[:, j:j + 1] * v[j:j + 1, sl]
            so_ref[j, h] = s1
            o = jnp.sum(q_t[:, j:j + 1] * s1, axis=0, keepdims=True)
            o_ref[j:j + 1, sl] = _rms(o, gain[:, sl])


def _sample_hgrn(z, lb_logits, gain, state):
    bd = z.shape[0]
    zspec = lambda col: pl.BlockSpec((SAMPLE_GROUP, WIDTH), lambda i: (i, col))
    sspec = pl.BlockSpec((None, SAMPLE_GROUP, HEADS, HEAD_DIM, HEAD_DIM), lambda i: (0, i, 0, 0, 0))
    return pl.pallas_call(
        _sample_hgrn_body,
        grid=(bd // SAMPLE_GROUP,),
        in_specs=[zspec(COL_HQ), zspec(COL_HF), zspec(COL_HI), _const_spec(lb_logits.shape),
                  _const_spec((1, WIDTH)), sspec],
        out_specs=[pl.BlockSpec((SAMPLE_GROUP, WIDTH), lambda i: (i, 0)), sspec],
        out_shape=[jax.ShapeDtypeStruct((bd, WIDTH), F32), jax.ShapeDtypeStruct(state.shape, F32)],
        compiler_params=_params("parallel"),
        name="sample_hgrn",
    )(z, z, z, lb_logits, gain, state)


def kernel(x_prompt, x_sample, mem_prompt, cache_win1_k, cache_win1_v, cache_win4_k, cache_win4_v, cache_win16_k, cache_win16_v, cache_mem_k, cache_mem_v, state_hgrn, g_ff1, w_ff1_gate, w_ff1_up, w_ff1_down, g_mix, w_in, hg_lb_logits, g_hg_out, g_mem, w_mem_k, w_mem_v, w_branch_att, w_branch_hg, w_branch_mem, w_out, g_ff2, w_ff2_gate, w_ff2_up, w_ff2_down, g_final):
    b, t, _ = x_prompt.shape
    bd = x_sample.shape[0]
    assert g_ff1.shape[0] == 1 and x_sample.shape[1] == 1 and w_in.shape[2] == IN_WIDTH
    g_fin = g_final.reshape(1, D_MODEL)

    xs, w_ff1_b = _sample_ffn(x_sample.reshape(bd, D_MODEL), g_ff1, w_ff1_gate, w_ff1_up, w_ff1_down, g_fin,
                              final=False)
    zs, w_qkv, w_rest = _sample_in_proj(xs, g_mix, w_in, COL_HQ)
    caches_k = (cache_win1_k, cache_win4_k, cache_win16_k)
    caches_v = (cache_win1_v, cache_win4_v, cache_win16_v)
    z3 = zs.reshape(bd, 1, IN_WIDTH)
    att_s, mem_s = _sample_attend(z3, caches_k, caches_v, cache_mem_k, cache_mem_v)
    hg_s, s_hgrn = _sample_hgrn(zs, hg_lb_logits, g_hg_out, state_hgrn)
    xs, branch_w = _sample_merge(xs, att_s.reshape(bd, WIDTH), hg_s, mem_s.reshape(bd, WIDTH), zs,
                                 w_branch_att, w_branch_hg, w_branch_mem, w_out)
    y_sample, w_ff2_b = _sample_ffn(xs, g_ff2, w_ff2_gate, w_ff2_up, w_ff2_down, g_fin, final=True)
    y_sample = y_sample.reshape(bd, 1, D_MODEL)
    ffn1 = (g_ff1, *w_ff1_b, g_fin)
    ffn2 = (g_ff2, *w_ff2_b, g_fin)

    tm = b * t // bd
    xp, (s_win16_k,) = _ffn(x_prompt.reshape(b * t, D_MODEL), *ffn1, tm=tm, final=False, ride=(cache_win16_k,))
    qkv = _qkv_proj(xp.reshape(b, t, D_MODEL), g_mix, w_qkv, tm=512)
    q_g, k_g, v_g, p_k, p_v = qkv[0:3], qkv[3:6], qkv[6:9], qkv[9:12], qkv[12:15]
    (hq, hk, lf, hv, mq, gates), (s_win1_k, s_win1_v, s_win4_k, s_win4_v), outs, lses = _rest_att(
        xp, g_mix, w_rest, hg_lb_logits, q_g, k_g, v_g, tm=tm,
        ride=(cache_win1_k, cache_win1_v, cache_win4_k, cache_win4_v))
    hg_p, p_hgrn = _hgrn_prompt(hq, hk, lf, hv, g_hg_out, b, t)
    p_mem_k, p_mem_v, mk_b, mv_b = _mem_kv(mem_prompt, g_mem, w_mem_k, w_mem_v)
    xp = _mix(xp, outs, lses, mq, mk_b, mv_b, hg_p, gates, *branch_w, b, t, tm=512)
    y_prompt, (s_win16_v,) = _ffn(xp, *ffn2, tm=tm, final=True, ride=(cache_win16_v,))
    s_win1_k, s_win1_v, s_win4_k, s_win4_v, s_win16_k, s_win16_v = _write_new_rows(
        z3, [s_win1_k, s_win1_v, s_win4_k, s_win4_v, s_win16_k, s_win16_v],
        ((0, False), (0, True), (1, False), (1, True), (2, False), (2, True)))

    return (y_prompt.reshape(b, t, D_MODEL), y_sample,
            p_k[0], p_v[0], p_k[1], p_v[1], p_k[2], p_v[2],
            p_mem_k, p_mem_v, p_hgrn,
            s_win1_k, s_win1_v, s_win4_k, s_win4_v, s_win16_k, s_win16_v,
            s_hgrn)
```

```python
import functools

import numpy as np
import jax
import jax.numpy as jnp
from jax import lax
from jax.experimental import pallas as pl
from jax.experimental.pallas import tpu as pltpu

F32 = jnp.float32
BF16 = jnp.bfloat16

D_MODEL = 1024
D_FF = 2816
EPS = 1e-6
HEADS = 4
HEAD_DIM = 128
WIDTH = HEADS * HEAD_DIM
ATT_GROUPS = ((128, 1), (512, 4), (2048, 16))
N_GROUPS = 3
BLOCK = 128
MEM_LEN = 256
N_BRANCH = 3
LSE_REP = 32
PROMPT_TILE = 512
QK_SCALE = HEAD_DIM ** -0.5
VMEM_LIMIT_BYTES = 56 * 1024 * 1024

COL_Q, COL_K, COL_V, COL_HQ, COL_HF, COL_HI, COL_MQ, COL_GATE = 0, 3, 6, 9, 10, 11, 12, 13
IN_WIDTH = (COL_GATE + N_BRANCH * D_MODEL // WIDTH) * WIDTH


def _params(*sem):
    return pltpu.CompilerParams(dimension_semantics=sem, vmem_limit_bytes=VMEM_LIMIT_BYTES)


def _const_spec(shape):
    nd = len(shape)
    return pl.BlockSpec(shape, lambda *_: (0,) * nd, pipeline_mode=pl.Buffered(1))


def _rms(x, g):
    return x * lax.rsqrt(jnp.mean(x * x, axis=-1, keepdims=True) + EPS) * g


def _dot(a, b):
    return jnp.dot(a, b, preferred_element_type=F32)


def _dot_nt(a, b):
    return lax.dot_general(a, b, (((1,), (1,)), ((), ())), preferred_element_type=F32)


def _ride_specs(caches, n_steps):
    if not caches:
        return [], [], [], []
    assert all(c.shape[1] == n_steps for c in caches)
    in_specs = [pl.BlockSpec((1, 1, c.shape[2], HEADS, HEAD_DIM), lambda i: (0, i, 0, 0, 0)) for c in caches]
    out_specs = [pl.BlockSpec(memory_space=pl.ANY)] * len(caches)
    shapes = [jax.ShapeDtypeStruct(c.shape, c.dtype) for c in caches]
    return in_specs, out_specs, shapes, [pltpu.SemaphoreType.DMA((2 * len(caches),))]


def _ride_copies(in_refs, out_refs, sem):
    seq = pl.program_id(0)
    copies = []
    for j, (cache_ref, out_ref) in enumerate(zip(in_refs, out_refs)):
        rows = cache_ref.shape[2]
        copies.append(pltpu.make_async_copy(cache_ref.at[:, :, pl.ds(1, rows - 1)],
                                            out_ref.at[:, pl.ds(seq, 1), pl.ds(0, rows - 1)], sem.at[2 * j]))
        copies.append(pltpu.make_async_copy(cache_ref.at[:, :, pl.ds(rows - 1, 1)],
                                            out_ref.at[:, pl.ds(seq, 1), pl.ds(rows - 1, 1)], sem.at[2 * j + 1]))
    for cp in copies:
        cp.start()
    return copies


def _new_rows_body(kn_ref, vn_ref, *refs, meta):
    n = len(meta)
    out_refs, row_refs, sem = refs[n:2 * n], refs[2 * n:3 * n], refs[3 * n]
    copies = []
    for j, ((g, is_v), out_ref, row_ref) in enumerate(zip(meta, out_refs, row_refs)):
        bd, rows = out_ref.shape[1], out_ref.shape[2]
        new = (vn_ref if is_v else kn_ref)[:, 0, g * WIDTH:(g + 1) * WIDTH]
        flat = row_ref.reshape(bd * HEADS, HEAD_DIM)
        for h in range(HEADS):
            flat[pl.ds(h, bd, stride=HEADS), :] = new[:, h * HEAD_DIM:(h + 1) * HEAD_DIM]
        copies.append(pltpu.make_async_copy(row_ref, out_ref.at[:, :, pl.ds(rows - 1, 1)], sem.at[j]))
        copies[-1].start()
    for cp in copies:
        cp.wait()


def _write_new_rows(z3, shifted, meta):
    bd = z3.shape[0]
    n = len(shifted)
    zspec = lambda col: pl.BlockSpec((bd, 1, N_GROUPS * WIDTH), lambda i: (0, 0, col // N_GROUPS))
    any_spec = pl.BlockSpec(memory_space=pl.ANY)
    return pl.pallas_call(
        functools.partial(_new_rows_body, meta=meta),
        grid=(1,),
        in_specs=[zspec(COL_K), zspec(COL_V)] + [any_spec] * n,
        out_specs=[any_spec] * n,
        out_shape=[jax.ShapeDtypeStruct(c.shape, c.dtype) for c in shifted],
        input_output_aliases={2 + j: j for j in range(n)},
        scratch_shapes=[pltpu.VMEM((1, bd, 1, HEADS, HEAD_DIM), F32)] * n + [pltpu.SemaphoreType.DMA((n,))],
        name="write_new_rows",
    )(z3, z3, *shifted)


def _ffn_body(x_ref, g_ref, wg_ref, wu_ref, wd_ref, gf_ref, *rest, final, n_ride):
    o_ref = rest[n_ride]
    copies = _ride_copies(rest[:n_ride], rest[n_ride + 1:2 * n_ride + 1], rest[-1]) if n_ride else []
    x = x_ref[...]
    h = _rms(x, g_ref[...]).astype(BF16)
    a = _dot(h, wg_ref[...])
    act = (a * jax.nn.sigmoid(a) * _dot(h, wu_ref[...])).astype(BF16)
    y = x + 0.5 * _dot(act, wd_ref[...])
    if final:
        y = _rms(y, gf_ref[...])
    o_ref[...] = y
    for cp in copies:
        cp.wait()


def _ffn(x, g, wg, wu, wd, g_final, *, tm, final, ride=()):
    m = x.shape[0]
    r_in, r_out, r_shapes, r_scratch = _ride_specs(ride, m // tm)
    res = pl.pallas_call(
        functools.partial(_ffn_body, final=final, n_ride=len(ride)),
        grid=(m // tm,),
        in_specs=[pl.BlockSpec((tm, D_MODEL), lambda i: (i, 0)),
                  _const_spec((1, D_MODEL)),
                  _const_spec((D_MODEL, D_FF)), _const_spec((D_MODEL, D_FF)), _const_spec((D_FF, D_MODEL)),
                  _const_spec((1, D_MODEL))] + r_in,
        out_specs=[pl.BlockSpec((tm, D_MODEL), lambda i: (i, 0))] + r_out,
        out_shape=[jax.ShapeDtypeStruct((m, D_MODEL), F32)] + r_shapes,
        scratch_shapes=r_scratch,
        compiler_params=_params("parallel"),
        name="ffn_final" if final else "ffn",
    )(x, g, wg, wu, wd, g_final, *ride)
    return res[0], res[1:]


def _store_heads(ref, z):
    rows = ref.shape[0]
    flat = ref.reshape(rows * HEADS, HEAD_DIM)
    for h in range(HEADS):
        flat[pl.ds(h, rows, stride=HEADS), :] = z[:, h * HEAD_DIM:(h + 1) * HEAD_DIM]


def _load_head(ref, h):
    rows = ref.shape[0]
    return ref.reshape(rows * HEADS, HEAD_DIM)[pl.ds(h, rows, stride=HEADS), :]


SUBLANES = 8


def _stage_pitch(dil):
    return dil + SUBLANES // 2 if dil % (2 * SUBLANES) == 0 else dil


def _stage_shape(rows, dil, width):
    return (width // 128, rows // dil * _stage_pitch(dil), 128)


def _to_strided_view(dst_ref, z, stage_ref, dil):
    if dil == 1:
        dst_ref[...] = z.astype(dst_ref.dtype)
        return
    n = z.shape[0] // dil
    pitch = _stage_pitch(dil)
    for c in range(stage_ref.shape[0]):
        if pitch == dil:
            stage_ref[c] = z[:, c * 128:(c + 1) * 128]
        else:
            for l in range(n):
                stage_ref[c, l * pitch:l * pitch + dil, :] = z[l * dil:(l + 1) * dil, c * 128:(c + 1) * 128]
    for r in range(dil):
        for c in range(stage_ref.shape[0]):
            dst_ref[:, r * WIDTH + c * 128:r * WIDTH + (c + 1) * 128] = (
                stage_ref[c, pl.ds(r, n, stride=pitch), :].astype(dst_ref.dtype))


def _qkv_body(x_ref, g_ref, w_ref, *refs, tm, n_tiles):
    q_refs, k_refs, v_refs = refs[0:3], refs[3:6], refs[6:9]
    pk_refs, pv_refs, stage_refs = refs[9:12], refs[12:15], refs[15:]
    h = _rms(x_ref[...], g_ref[...]).astype(BF16)
    late = []
    for g, (win, dil) in reversed(list(enumerate(ATT_GROUPS))):
        keep = min(win, tm * n_tiles)
        for kind, (col, dst_refs, win_refs) in enumerate(((COL_Q, q_refs, None), (COL_K, k_refs, pk_refs),
                                                          (COL_V, v_refs, pv_refs))):
            z = _dot(h, w_ref[:, (col + g) * WIDTH:(col + g + 1) * WIDTH])
            if win_refs is None:
                z = z * QK_SCALE
            stage = stage_refs[3 * (g - 1) + kind] if dil > 1 else None
            _to_strided_view(dst_refs[g], z, stage, dil)
            if win_refs is None:
                continue
            if keep == tm * n_tiles:
                _store_heads(win_refs[g], z)
            elif dil == 1:
                late.append((win_refs[g], z[tm - keep:, :], None))
            else:
                assert keep == tm
                late.append((win_refs[g], None, stage))

    @pl.when(pl.program_id(1) == n_tiles - 1)
    def _():
        for ref, rows, stage in late:
            if stage is None:
                _store_heads(ref, rows)
            else:
                _store_heads(ref, jnp.concatenate([stage[hd] for hd in range(HEADS)], axis=-1))


def _qkv_proj(x, g_mix, w_qkv, *, tm):
    b, t, _ = x.shape
    n_tiles = t // tm
    view_specs = [pl.BlockSpec((None, tm // d, d * WIDTH), lambda bi, i: (bi, i, 0)) for _, d in ATT_GROUPS]
    view_shapes = [jax.ShapeDtypeStruct((b, t // d, d * WIDTH), BF16) for _, d in ATT_GROUPS]
    win_specs, win_shapes = [], []
    for win, _ in ATT_GROUPS:
        keep = min(win, t)
        assert keep == t or keep <= tm
        if keep == t:
            win_specs.append(pl.BlockSpec((None, None, tm, HEADS, HEAD_DIM), lambda bi, i: (0, bi, i, 0, 0)))
        else:
            win_specs.append(pl.BlockSpec((None, None, keep, HEADS, HEAD_DIM), lambda bi, i: (0, bi, 0, 0, 0)))
        win_shapes.append(jax.ShapeDtypeStruct((1, b, keep, HEADS, HEAD_DIM), F32))
    return pl.pallas_call(
        functools.partial(_qkv_body, tm=tm, n_tiles=n_tiles),
        grid=(b, n_tiles),
        in_specs=[pl.BlockSpec((None, tm, D_MODEL), lambda bi, i: (bi, i, 0)),
                  _const_spec((1, D_MODEL)),
                  _const_spec((D_MODEL, 3 * N_GROUPS * WIDTH))],
        out_specs=view_specs * 3 + win_specs + win_specs,
        out_shape=view_shapes * 3 + win_shapes + win_shapes,
        scratch_shapes=[pltpu.VMEM(_stage_shape(tm, d, WIDTH), F32) for _, d in ATT_GROUPS if d > 1 for _ in range(3)],
        compiler_params=_params("parallel", "arbitrary"),
        name="qkv_proj",
    )(x, g_mix, w_qkv)


def _lower_bound(logits_ref):
    lg = logits_ref[...]
    e = jnp.exp(lg - jnp.max(lg, axis=0, keepdims=True))
    return e[0:1, :] / jnp.sum(e, axis=0, keepdims=True)


def _rest_steps(x_ref, g_ref, w_ref, lb_ref, hq_ref, hk_ref, lf_ref, hv_ref, mq_ref):
    h = _rms(x_ref[...], g_ref[...]).astype(BF16)
    lb = _lower_bound(lb_ref)
    proj = lambda c: _dot(h, w_ref[:, c * WIDTH:(c + 1) * WIDTH])

    def hgrn_query():
        hq_ref[...] = jax.nn.sigmoid(proj(0)).astype(BF16)

    def hgrn_key():
        kk = (1.0 - lb) * jax.nn.sigmoid(proj(1))
        hk_ref[...] = kk.astype(BF16)
        lf_ref[...] = jnp.log(lb + kk)

    def hgrn_value():
        hv_ref[...] = proj(2).astype(BF16)

    def mem_query():
        mq_ref[...] = (proj(3) * QK_SCALE).astype(BF16)

    return [hgrn_query, hgrn_key, hgrn_value, mem_query]


def _rest_att_body(x_ref, g_ref, w_ref, lb_ref, *rest, n_ride, tiles, tiles_per_seq):
    n_att_in = sum(5 if has_halo else 3 for _, _, has_halo in tiles)
    caches, rest = rest[:n_ride], rest[n_ride:]
    att_in, rest = rest[:n_att_in], rest[n_att_in:]
    proj_out, rest = rest[:N_REST_OUT], rest[N_REST_OUT:]
    ride_out, rest = rest[:n_ride], rest[n_ride:]
    att_out, scratch = rest[:2 * len(tiles)], rest[2 * len(tiles):]
    copies = _ride_copies(caches, ride_out, scratch[-1]) if n_ride else []
    steps = _rest_steps(x_ref, g_ref, w_ref, lb_ref, *proj_out)
    gens = _att_tiles(att_in + att_out, tiles, pl.program_id(0) % tiles_per_seq)
    steps.pop(0)()
    for gen in gens:
        next(gen)
        if steps:
            steps.pop(0)()
        next(gen)
        next(gen, None)
    for step in steps:
        step()
    for cp in copies:
        cp.wait()


N_REST_OUT = 5


def _rest_att(x, g_mix, w_rest, lb_logits, q_g, k_g, v_g, *, tm, ride=()):
    m = x.shape[0]
    spec = pl.BlockSpec((tm, WIDTH), lambda i: (i, 0))
    bf = jax.ShapeDtypeStruct((m, WIDTH), BF16)
    r_in, r_out, r_shapes, r_scratch = _ride_specs(ride, m // tm)
    tiles_per_seq = q_g[0].shape[1] // (ATT_UNITS * BLOCK)
    a_in, a_args, a_out, a_shapes, tiles, n_tiles = _attention_specs(
        q_g, k_g, v_g, lambda i: (i // tiles_per_seq, i % tiles_per_seq))
    assert n_tiles == tiles_per_seq and q_g[0].shape[0] * n_tiles == m // tm
    res = pl.pallas_call(
        functools.partial(_rest_att_body, n_ride=len(ride), tiles=tiles, tiles_per_seq=tiles_per_seq),
        grid=(m // tm,),
        in_specs=[pl.BlockSpec((tm, D_MODEL), lambda i: (i, 0)),
                  _const_spec((1, D_MODEL)),
                  pl.BlockSpec((D_MODEL, (COL_GATE - COL_HQ) * WIDTH), lambda i: (0, 0), pipeline_mode=pl.Buffered(1)),
                  _const_spec(lb_logits.shape)] + r_in + a_in,
        out_specs=[spec] * N_REST_OUT + r_out + a_out,
        out_shape=[bf, bf, jax.ShapeDtypeStruct((m, WIDTH), F32), bf, bf] + r_shapes + a_shapes,
        scratch_shapes=r_scratch,
        compiler_params=_params("arbitrary"),
        name="rest_att",
    )(x, g_mix, w_rest, lb_logits, *ride, *a_args)
    n = N_REST_OUT + len(ride)
    return res[:N_REST_OUT], res[N_REST_OUT:n], res[n::2], res[n + 1::2]


ATT_UNITS = 4
STAT_REP = LSE_REP // 2


def _att_tiles(refs, tiles, tile_index):
    n_in = [5 if has_halo else 3 for _, _, has_halo in tiles]
    outs = refs[sum(n_in):]
    gens, pos = [], 0
    for g, (n_row_blocks, n_col_blocks, has_halo) in enumerate(tiles):
        gens.append(_att_tile(*refs[pos:pos + n_in[g]], outs[2 * g], outs[2 * g + 1], n_row_blocks=n_row_blocks,
                              n_col_blocks=n_col_blocks, has_halo=has_halo, tile_index=tile_index))
        pos += n_in[g]
    return gens


def _att_tile(q_ref, k_ref, v_ref, *rest, n_row_blocks, n_col_blocks, has_halo, tile_index):
    if has_halo:
        kh_ref, vh_ref, o_ref, st_ref = rest
        halo_ok = tile_index > 0
    else:
        o_ref, st_ref = rest
    qi = lax.broadcasted_iota(jnp.int32, (BLOCK, 2 * BLOCK), 0)
    ki = lax.broadcasted_iota(jnp.int32, (BLOCK, 2 * BLOCK), 1)
    own_ok = (lax.broadcasted_iota(jnp.int32, (BLOCK, BLOCK), 1)
              <= lax.broadcasted_iota(jnp.int32, (BLOCK, BLOCK), 0))
    both_ok = jnp.where(ki < BLOCK, ki, qi + BLOCK) >= jnp.where(ki < BLOCK, qi, ki)
    if has_halo:
        first_valid = jnp.where(halo_ok, 0, BLOCK)
        halo_both_ok = jnp.logical_and(both_ok, ki >= first_valid)

    units = []
    for c in range(n_col_blocks):
        for u in range(n_row_blocks):
            for h in range(HEADS):
                rows = slice(u * BLOCK, (u + 1) * BLOCK)
                cols = slice(c * WIDTH + h * HEAD_DIM, c * WIDTH + (h + 1) * HEAD_DIM)
                lcols = (c * HEADS + h) * LSE_REP
                if u > 0:
                    with_prev = slice((u - 1) * BLOCK, (u + 1) * BLOCK)
                    keys = lambda ref, rr=with_prev, cc=cols: ref[rr, cc]
                    ok = both_ok
                elif has_halo:
                    hcols = slice(h * HEAD_DIM, (h + 1) * HEAD_DIM)
                    keys = lambda ref, rr=rows, cc=cols, hc=hcols: jnp.concatenate(
                        [(kh_ref if ref is k_ref else vh_ref)[:, hc], ref[rr, cc]], axis=0)
                    ok = halo_both_ok
                else:
                    keys = lambda ref, rr=rows, cc=cols: ref[rr, cc]
                    ok = own_ok
                units.append((rows, cols, lcols, keys, ok))

    scores = [jnp.where(ok, _dot_nt(q_ref[rows, cols], keys(k_ref)), -jnp.inf) for rows, cols, _, keys, ok in units]
    yield

    probs = []
    for s in scores:
        m = jnp.max(s, axis=-1, keepdims=True)
        p = jnp.exp(s - m)
        probs.append((p.astype(BF16), m, jnp.sum(p, axis=-1, keepdims=True)))
    yield

    for (rows, cols, lcols, keys, _), (p, m, l) in zip(units, probs):
        o_ref[rows, cols] = _dot(p, keys(v_ref)).astype(BF16)
        st_ref[rows, lcols:lcols + STAT_REP] = jnp.broadcast_to(m, (BLOCK, STAT_REP))
        st_ref[rows, lcols + STAT_REP:lcols + LSE_REP] = jnp.broadcast_to(l, (BLOCK, STAT_REP))


def _attention_specs(q_g, k_g, v_g, to_bj):
    b = q_g[0].shape[0]
    lw = HEADS * LSE_REP
    n_steps = None
    in_specs, args, out_specs, out_shapes, tiles = [], [], [], [], []

    def spec(shape, index):
        return pl.BlockSpec(shape, lambda *grid_idx: index(*to_bj(*grid_idx)))

    for (_, dil), q, k, v in zip(ATT_GROUPS, q_g, k_g, v_g):
        sub_len = q.shape[1]
        n_row_blocks = min(ATT_UNITS, sub_len // BLOCK)
        n_col_blocks = ATT_UNITS // n_row_blocks
        span = n_row_blocks * BLOCK
        assert sub_len % span == 0 and dil % n_col_blocks == 0
        has_halo = sub_len > span
        assert not has_halo or (dil == 1 and n_col_blocks == 1)
        n = (sub_len // span) * (dil // n_col_blocks)
        assert n_steps in (None, n)
        n_steps = n
        if has_halo:
            tile = lambda w, span=span: spec((None, span, w), lambda bi, j: (bi, j, 0))
        else:
            tile = lambda w, span=span, nc=n_col_blocks: spec((None, span, nc * w), lambda bi, j: (bi, 0, j))
        halo = spec((None, BLOCK, WIDTH), lambda bi, j, nr=n_row_blocks: (bi, jnp.maximum(j * nr - 1, 0), 0))
        in_specs += [tile(WIDTH)] * 3 + ([halo, halo] if has_halo else [])
        args += [q, k, v] + ([k, v] if has_halo else [])
        out_specs += [tile(WIDTH), tile(lw)]
        out_shapes += [jax.ShapeDtypeStruct((b, sub_len, dil * WIDTH), BF16),
                       jax.ShapeDtypeStruct((b, sub_len, dil * lw), F32)]
        tiles.append((n_row_blocks, n_col_blocks, has_halo))
    return in_specs, args, out_specs, out_shapes, tuple(tiles), n_steps


N_LEVELS = 7
ROW_CUM, ROW_REM = N_LEVELS, N_LEVELS + 1


def _hgrn_constants():
    r = np.arange(BLOCK)[:, None]
    s = np.arange(BLOCK)[None, :]
    mats = []
    for lvl in range(N_LEVELS):
        half = 1 << lvl
        mid = (r // (2 * half)) * 2 * half + half
        upper = (r % (2 * half)) >= half
        mats.append(np.where(upper, (s >= mid) & (s <= r), (s > r) & (s <= mid - 1)))
    mats.append(s <= r)
    mats.append(s > r)
    range_mat = np.concatenate(mats, axis=0).astype(np.float32)
    range_mat = np.concatenate([range_mat, range_mat], axis=1)
    x = r ^ s
    level = np.where(s < r, np.floor(np.log2(np.maximum(x, 1))).astype(np.int32), np.where(s == r, N_LEVELS, -1))
    return jnp.asarray(range_mat, BF16), jnp.asarray(level, jnp.int32)


HGRN_SEQS = 8


def _hgrn_intra(q_b, k_b, lf, rng, level):
    lf_hi = lf.astype(BF16)
    lf_lo = (lf - lf_hi.astype(F32)).astype(BF16)
    z = _dot(rng, jnp.concatenate([lf_hi, lf_lo], axis=0))
    q = q_b.astype(F32)
    kk = k_b.astype(F32)
    lhs, rhs = [], []
    for lvl in range(N_LEVELS):
        x = jnp.exp(z[lvl * BLOCK:(lvl + 1) * BLOCK]).astype(BF16)
        lhs.append(q_b * x)
        rhs.append(k_b * x)
    lhs.append(q_b)
    rhs.append(k_b)
    b_cum = z[ROW_CUM * BLOCK:(ROW_CUM + 1) * BLOCK]
    q_in = (q * jnp.exp(b_cum)).astype(BF16)
    k_out = (kk * jnp.exp(z[ROW_REM * BLOCK:(ROW_REM + 1) * BLOCK])).astype(BF16)
    decay = jnp.exp(b_cum[BLOCK - 1:BLOCK, :])
    mats = []
    for h in range(HEADS):
        sl = slice(h * HEAD_DIM, (h + 1) * HEAD_DIM)
        a = jnp.zeros((BLOCK, BLOCK), F32)
        for lvl in range(N_LEVELS + 1):
            a = jnp.where(level == lvl, _dot_nt(lhs[lvl][:, sl], rhs[lvl][:, sl]), a)
        mats.append(a.astype(BF16))
    return mats, q_in, k_out, decay


def _hgrn_state(mats, q_in, k_out, decay, v_b, gain, st_ref, j):
    outs = []
    for h in range(HEADS):
        sl = slice(h * HEAD_DIM, (h + 1) * HEAD_DIM)
        v = v_b[:, sl]
        st = st_ref[j, h]
        o = _dot(mats[h], v) + _dot_nt(q_in[:, sl], st.astype(BF16))
        outs.append(_rms(o, gain[:, sl]).astype(BF16))
        v_t = v.astype(F32).T.astype(BF16)
        st_ref[j, h] = st * decay[:, sl] + _dot(v_t, k_out[:, sl])
    return jnp.concatenate(outs, axis=-1)


def _hgrn_body(q_ref, k_ref, lf_ref, v_ref, rng_ref, lvl_ref, gain_ref, o_ref, s_ref, st_ref, *, n_chunks):
    c = pl.program_id(1)

    @pl.when(c == 0)
    def _():
        st_ref[...] = jnp.zeros_like(st_ref)

    rng, level, gain = rng_ref[...], lvl_ref[...], gain_ref[...]
    intra = [_hgrn_intra(q_ref[j], k_ref[j], lf_ref[j], rng, level) for j in range(HGRN_SEQS)]
    for j in range(HGRN_SEQS):
        o_ref[j] = _hgrn_state(*intra[j], v_ref[j], gain, st_ref, j)

    @pl.when(c == n_chunks - 1)
    def _():
        for j in range(HGRN_SEQS):
            for h in range(HEADS):
                s_ref[j, h] = st_ref[j, h].T


def _hgrn_prompt(hq, hk, lf, hv, gain, b, t):
    n_chunks = t // BLOCK
    range_mat, level = _hgrn_constants()
    as3d = lambda a: a.reshape(b, t, WIDTH)
    spec = pl.BlockSpec((HGRN_SEQS, BLOCK, WIDTH), lambda bi, c: (bi, c, 0))
    state_spec = pl.BlockSpec((None, HGRN_SEQS, HEADS, HEAD_DIM, HEAD_DIM), lambda bi, c: (0, bi, 0, 0, 0))
    o, state = pl.pallas_call(
        functools.partial(_hgrn_body, n_chunks=n_chunks),
        grid=(b // HGRN_SEQS, n_chunks),
        in_specs=[spec, spec, spec, spec, _const_spec(range_mat.shape), _const_spec(level.shape),
                  _const_spec((1, WIDTH))],
        out_specs=[spec, state_spec],
        out_shape=[jax.ShapeDtypeStruct((b, t, WIDTH), BF16),
                   jax.ShapeDtypeStruct((1, b, HEADS, HEAD_DIM, HEAD_DIM), F32)],
        scratch_shapes=[pltpu.VMEM((HGRN_SEQS, HEADS, HEAD_DIM, HEAD_DIM), F32)],
        compiler_params=_params("parallel", "arbitrary"),
        name="hgrn_prompt",
    )(as3d(hq), as3d(hk), as3d(lf), as3d(hv), range_mat, level, gain)
    return o.reshape(b * t, WIDTH), state


def _memkv_body(m_ref, g_ref, wk_ref, wv_ref, pk_ref, pv_ref, kb_ref, vb_ref):
    h = _rms(m_ref[...], g_ref[...]).astype(BF16)
    k = _dot(h, wk_ref[...].astype(BF16))
    v = _dot(h, wv_ref[...].astype(BF16))
    _store_heads(pk_ref, k)
    _store_heads(pv_ref, v)
    kb_ref[...] = k.astype(BF16)
    vb_ref[...] = v.astype(BF16)


def _mem_kv(mem, g_mem, w_mk, w_mv):
    b = mem.shape[0]
    head_spec = pl.BlockSpec((None, None, MEM_LEN, HEADS, HEAD_DIM), lambda bi: (0, bi, 0, 0, 0))
    flat_spec = pl.BlockSpec((None, MEM_LEN, WIDTH), lambda bi: (bi, 0, 0))
    head_shape = jax.ShapeDtypeStruct((1, b, MEM_LEN, HEADS, HEAD_DIM), F32)
    flat_shape = jax.ShapeDtypeStruct((b, MEM_LEN, WIDTH), BF16)
    return pl.pallas_call(
        _memkv_body,
        grid=(b,),
        in_specs=[pl.BlockSpec((None, MEM_LEN, D_MODEL), lambda bi: (bi, 0, 0)),
                  _const_spec((1, D_MODEL)),
                  pl.BlockSpec((None, D_MODEL, WIDTH), lambda bi: (0, 0, 0), pipeline_mode=pl.Buffered(1)),
                  pl.BlockSpec((None, D_MODEL, WIDTH), lambda bi: (0, 0, 0), pipeline_mode=pl.Buffered(1))],
        out_specs=[head_spec, head_spec, flat_spec, flat_spec],
        out_shape=[head_shape, head_shape, flat_shape, flat_shape],
        compiler_params=_params("parallel"),
        name="mem_kv",
    )(mem, g_mem, w_mk, w_mv)


def _from_strided_view(src_ref, stage_ref, dil, width):
    if dil == 1:
        return src_ref[...].astype(F32)
    n = src_ref.shape[0]
    n_col = width // 128
    pitch = _stage_pitch(dil)
    for r in range(dil):
        for c in range(n_col):
            stage_ref[c, pl.ds(r, n, stride=pitch), :] = (
                src_ref[:, r * width + c * 128:r * width + (c + 1) * 128].astype(F32))

    def natural(c):
        if pitch == dil:
            return stage_ref[c]
        return jnp.concatenate([stage_ref[c, l * pitch:l * pitch + dil, :] for l in range(n)], axis=0)

    return jnp.concatenate([natural(c) for c in range(n_col)], axis=-1)


def _mix_body(x_ref, o0_ref, o1_ref, o2_ref, s0_ref, s1_ref, s2_ref, mq_ref, mk_ref, mv_ref, hg_ref, g_ref,
              wg0_ref, wg1_ref, wg2_ref, wa_ref, wb_ref, wc_ref, wo_ref, out_ref, *stage_refs):
    heads = [slice(h * HEAD_DIM, (h + 1) * HEAD_DIM) for h in range(HEADS)]
    hn = _rms(x_ref[...], g_ref[...]).astype(BF16)
    gate = lambda w_ref: jax.nn.sigmoid(_dot(hn, w_ref[...]))
    gate_att, gate_hg = gate(wg0_ref), gate(wg1_ref)

    accs = [_from_strided_view(r, stage_refs[2 * g], ATT_GROUPS[g][1], WIDTH)
            for g, r in enumerate((o0_ref, o1_ref, o2_ref))]
    stats = [_from_strided_view(r, stage_refs[2 * g + 1], ATT_GROUPS[g][1], HEADS * LSE_REP)
             for g, r in enumerate((s0_ref, s1_ref, s2_ref))]
    att = []
    for h, sl in enumerate(heads):
        ms = [x[:, h * LSE_REP:h * LSE_REP + 1] for x in stats]
        ls = [x[:, h * LSE_REP + STAT_REP:h * LSE_REP + STAT_REP + 1] for x in stats]
        m = jnp.maximum(jnp.maximum(ms[0], ms[1]), ms[2])
        e = [jnp.exp(x - m) for x in ms]
        inv = 1.0 / (e[0] * ls[0] + e[1] * ls[1] + e[2] * ls[2])
        att.append(sum((e[g] * inv) * accs[g][:, sl] for g in range(N_GROUPS)).astype(BF16))
    merged = gate_att * _dot(jnp.concatenate(att, axis=-1), wa_ref[...])

    scores = [_dot_nt(mq_ref[:, sl], mk_ref[:, sl]) for sl in heads]
    gate_mem = gate(wg2_ref)
    merged = merged + gate_hg * _dot(hg_ref[...], wb_ref[...])
    probs = []
    for s in scores:
        p = jnp.exp(s - jnp.max(s, axis=-1, keepdims=True))
        probs.append((p.astype(BF16), 1.0 / jnp.sum(p, axis=-1, keepdims=True)))
    mem = [(_dot(p, mv_ref[:, sl]) * inv).astype(BF16) for (p, inv), sl in zip(probs, heads)]
    merged = merged + gate_mem * _dot(jnp.concatenate(mem, axis=-1), wc_ref[...])
    out_ref[...] = x_ref[...] + _dot(merged.astype(BF16), wo_ref[...])


def _mix(x, accs, stats, mq, mk, mv, hg, g_mix, w_rest, wa, wb, wc, wo, b, t, *, tm):
    n_tiles = t // tm
    first_gate = (COL_GATE - COL_HQ) * WIDTH // D_MODEL
    assert first_gate * D_MODEL == (COL_GATE - COL_HQ) * WIDTH
    gate_w = [pl.BlockSpec((D_MODEL, D_MODEL), lambda bi, i, j=j: (0, first_gate + j), pipeline_mode=pl.Buffered(1))
              for j in range(N_BRANCH)]
    lw = HEADS * LSE_REP
    rows = lambda w: pl.BlockSpec((tm, w), lambda bi, i: (bi * n_tiles + i, 0))
    o_specs = [pl.BlockSpec((None, tm // d, d * WIDTH), lambda bi, i: (bi, i, 0)) for _, d in ATT_GROUPS]
    s_specs = [pl.BlockSpec((None, tm // d, d * lw), lambda bi, i: (bi, i, 0)) for _, d in ATT_GROUPS]
    memspec = pl.BlockSpec((None, MEM_LEN, WIDTH), lambda bi, i: (bi, 0, 0))
    stages = []
    for _, d in ATT_GROUPS:
        stages += [pltpu.VMEM(_stage_shape(tm, d, WIDTH), F32), pltpu.VMEM(_stage_shape(tm, d, lw), F32)]
    return pl.pallas_call(
        _mix_body,
        grid=(b, n_tiles),
        in_specs=[rows(D_MODEL)] + o_specs + s_specs + [rows(WIDTH), memspec, memspec, rows(WIDTH),
                                                         _const_spec((1, D_MODEL))] + gate_w
                 + [_const_spec((WIDTH, D_MODEL)), _const_spec((WIDTH, D_MODEL)), _const_spec((WIDTH, D_MODEL)),
                    _const_spec((D_MODEL, D_MODEL))],
        out_specs=rows(D_MODEL),
        out_shape=jax.ShapeDtypeStruct((b * t, D_MODEL), F32),
        scratch_shapes=stages,
        compiler_params=_params("parallel", "parallel"),
        name="mix",
    )(x, *accs, *stats, mq, mk, mv, hg, g_mix, w_rest, w_rest, w_rest, wa, wb, wc, wo)


def _sample_merge_body(x_ref, att_ref, hg_ref, mem_ref, z_ref, wa_ref, wb_ref, wc_ref, wo_ref,
                       o_ref, wab_ref, wbb_ref, wcb_ref, wob_ref):
    wa, wb, wc, wo = (r[...].astype(BF16) for r in (wa_ref, wb_ref, wc_ref, wo_ref))
    wab_ref[...], wbb_ref[...], wcb_ref[...], wob_ref[...] = wa, wb, wc, wo
    gate = lambda j: jax.nn.sigmoid(z_ref[:, COL_GATE * WIDTH + j * D_MODEL:COL_GATE * WIDTH + (j + 1) * D_MODEL])
    m = (gate(0) * _dot(att_ref[...].astype(BF16), wa) + gate(1) * _dot(hg_ref[...].astype(BF16), wb)
         + gate(2) * _dot(mem_ref[...].astype(BF16), wc))
    o_ref[...] = x_ref[...] + _dot(m.astype(BF16), wo)


def _sample_merge(x, att, hg, mem, z, wa, wb, wc, wo):
    m = x.shape[0]
    full = lambda a: _const_spec(a.shape)
    w_specs = [pl.BlockSpec((None,) + w.shape[1:], lambda i: (0, 0, 0)) for w in (wa, wb, wc, wo)]
    b_specs = [pl.BlockSpec(w.shape[1:], lambda i: (0, 0)) for w in (wa, wb, wc, wo)]
    res = pl.pallas_call(
        _sample_merge_body,
        grid=(1,),
        in_specs=[full(x), full(att), full(hg), full(mem), full(z)] + w_specs,
        out_specs=[pl.BlockSpec((m, D_MODEL), lambda i: (0, 0))] + b_specs,
        out_shape=[jax.ShapeDtypeStruct((m, D_MODEL), F32)]
                  + [jax.ShapeDtypeStruct(w.shape[1:], BF16) for w in (wa, wb, wc, wo)],
        compiler_params=_params("arbitrary"),
        name="sample_merge",
    )(x, att, hg, mem, z, wa, wb, wc, wo)
    return res[0], res[1:]


def _sample_in_proj_body(x_ref, g_ref, w_ref, o_ref, wa_ref, wb_ref, *, n_a):
    w = w_ref[...].astype(BF16)
    o_ref[...] = _dot(_rms(x_ref[...], g_ref[...]).astype(BF16), w)

    @pl.when(pl.program_id(0) < n_a)
    def _():
        wa_ref[...] = w

    @pl.when(pl.program_id(0) >= n_a)
    def _():
        wb_ref[...] = w


def _sample_in_proj(x, g_mix, w_in, n_a):
    m = x.shape[0]
    n = w_in.shape[2] // WIDTH
    return pl.pallas_call(
        functools.partial(_sample_in_proj_body, n_a=n_a),
        grid=(n,),
        in_specs=[_const_spec((m, D_MODEL)), _const_spec((1, D_MODEL)),
                  pl.BlockSpec((None, D_MODEL, WIDTH), lambda j: (0, 0, j))],
        out_specs=[pl.BlockSpec((m, WIDTH), lambda j: (0, j)),
                   pl.BlockSpec((D_MODEL, WIDTH), lambda j: (0, jnp.minimum(j, n_a - 1))),
                   pl.BlockSpec((D_MODEL, WIDTH), lambda j: (0, jnp.maximum(j - n_a, 0)))],
        out_shape=[jax.ShapeDtypeStruct((m, n * WIDTH), F32),
                   jax.ShapeDtypeStruct((D_MODEL, n_a * WIDTH), BF16),
                   jax.ShapeDtypeStruct((D_MODEL, (n - n_a) * WIDTH), BF16)],
        compiler_params=_params("arbitrary"),
        name="sample_in_proj",
    )(x, g_mix, w_in)


FF_TILE = 256


def _sample_ffn_body(x_ref, g_ref, wg_ref, wu_ref, wd_ref, gf_ref, o_ref, wgb_ref, wub_ref, wdb_ref, acc_ref, *,
                     final):
    j = pl.program_id(0)

    @pl.when(j == 0)
    def _():
        acc_ref[...] = jnp.zeros_like(acc_ref)

    wg, wu, wd = wg_ref[...].astype(BF16), wu_ref[...].astype(BF16), wd_ref[...].astype(BF16)
    wgb_ref[...], wub_ref[...], wdb_ref[...] = wg, wu, wd
    x = x_ref[...]
    h = _rms(x, g_ref[...]).astype(BF16)
    a = _dot(h, wg)
    act = (a * jax.nn.sigmoid(a) * _dot(h, wu)).astype(BF16)
    acc_ref[...] += _dot(act, wd)

    @pl.when(j == pl.num_programs(0) - 1)
    def _():
        y = x + 0.5 * acc_ref[...]
        o_ref[...] = _rms(y, gf_ref[...]) if final else y


def _sample_ffn(x, g, wg, wu, wd, g_final, *, final):
    m = x.shape[0]
    res = pl.pallas_call(
        functools.partial(_sample_ffn_body, final=final),
        grid=(D_FF // FF_TILE,),
        in_specs=[_const_spec((m, D_MODEL)), _const_spec((1, D_MODEL)),
                  pl.BlockSpec((None, D_MODEL, FF_TILE), lambda j: (0, 0, j)),
                  pl.BlockSpec((None, D_MODEL, FF_TILE), lambda j: (0, 0, j)),
                  pl.BlockSpec((None, FF_TILE, D_MODEL), lambda j: (0, j, 0)),
                  _const_spec((1, D_MODEL))],
        out_specs=[pl.BlockSpec((m, D_MODEL), lambda j: (0, 0)),
                   pl.BlockSpec((D_MODEL, FF_TILE), lambda j: (0, j)),
                   pl.BlockSpec((D_MODEL, FF_TILE), lambda j: (0, j)),
                   pl.BlockSpec((FF_TILE, D_MODEL), lambda j: (j, 0))],
        out_shape=[jax.ShapeDtypeStruct((m, D_MODEL), F32),
                   jax.ShapeDtypeStruct((D_MODEL, D_FF), BF16), jax.ShapeDtypeStruct((D_MODEL, D_FF), BF16),
                   jax.ShapeDtypeStruct((D_FF, D_MODEL), BF16)],
        scratch_shapes=[pltpu.VMEM((m, D_MODEL), F32)],
        compiler_params=_params("arbitrary"),
        name="sample_ffn_final" if final else "sample_ffn",
    )(x, g, wg, wu, wd, g_final)
    return res[0], res[1:]


def _one_query_attend(q, k_rows, v_rows, k_new=None, v_new=None):
    s = jnp.sum(k_rows * q, axis=-1, keepdims=True)
    m = jnp.max(s, axis=0, keepdims=True)
    if k_new is not None:
        s_new = jnp.sum(k_new * q, axis=-1, keepdims=True)
        m = jnp.maximum(m, s_new)
    p = jnp.exp(s - m)
    l = jnp.sum(p, axis=0, keepdims=True)
    acc = jnp.sum(p * v_rows, axis=0, keepdims=True)
    if k_new is not None:
        p_new = jnp.exp(s_new - m)
        l = l + p_new
        acc = acc + p_new * v_new
    return acc / l, m + jnp.log(l)


SAMPLE_ATT_SEQS = 4


def _load_seq_head(ref, s, h):
    seqs, rows = ref.shape[0], ref.shape[1]
    return ref.reshape(seqs * rows * HEADS, HEAD_DIM)[pl.ds(s * rows * HEADS + h, rows, stride=HEADS), :]


def _sample_att_body(q_ref, kn_ref, vn_ref, mq_ref, k1_ref, v1_ref, k4_ref, v4_ref, k16_ref, v16_ref,
                     mk_ref, mv_ref, att_ref, mem_ref):
    k_refs = (k1_ref, k4_ref, k16_ref)
    v_refs = (v1_ref, v4_ref, v16_ref)
    for s in range(SAMPLE_ATT_SEQS):
        for h in range(HEADS):
            sl = slice(h * HEAD_DIM, (h + 1) * HEAD_DIM)
            outs, lses = [], []
            for g in range(N_GROUPS):
                gs = slice(g * WIDTH + h * HEAD_DIM, g * WIDTH + (h + 1) * HEAD_DIM)
                o, lse = _one_query_attend(q_ref[s, :, gs] * QK_SCALE, _load_seq_head(k_refs[g], s, h),
                                           _load_seq_head(v_refs[g], s, h), kn_ref[s, :, gs], vn_ref[s, :, gs])
                outs.append(o)
                lses.append(lse)
            m = jnp.maximum(jnp.maximum(lses[0], lses[1]), lses[2])
            e = [jnp.exp(x - m) for x in lses]
            inv = 1.0 / (e[0] + e[1] + e[2])
            att_ref[s, :, sl] = sum((e[g] * inv) * outs[g] for g in range(N_GROUPS))
            o, _ = _one_query_attend(mq_ref[s, :, sl] * QK_SCALE, _load_seq_head(mk_ref, s, h),
                                     _load_seq_head(mv_ref, s, h))
            mem_ref[s, :, sl] = o


def _sample_attend(z3, caches_k, caches_v, cache_mk, cache_mv):
    bd = z3.shape[0]
    seqs = SAMPLE_ATT_SEQS
    zspec = lambda col, n: pl.BlockSpec((seqs, 1, n * WIDTH), lambda b: (b, 0, col // n))
    cache_specs, cache_args = [], []
    for (win, dil), ck, cv in zip(ATT_GROUPS, caches_k, caches_v):
        rows = ck.shape[2]
        assert rows == win and rows // dil == BLOCK
        spec = pl.BlockSpec((None, seqs, BLOCK, None, HEADS, HEAD_DIM), lambda b: (0, b, 0, 0, 0, 0))
        for c in (ck, cv):
            cache_specs.append(spec)
            cache_args.append(c.reshape(1, bd, BLOCK, dil, HEADS, HEAD_DIM))
    mem_spec = pl.BlockSpec((None, seqs, MEM_LEN, HEADS, HEAD_DIM), lambda b: (0, b, 0, 0, 0))
    out_spec = pl.BlockSpec((seqs, 1, WIDTH), lambda b: (b, 0, 0))
    out_shape = jax.ShapeDtypeStruct((bd, 1, WIDTH), F32)
    return pl.pallas_call(
        _sample_att_body,
        grid=(bd // seqs,),
        in_specs=[zspec(COL_Q, 3), zspec(COL_K, 3), zspec(COL_V, 3), zspec(COL_MQ, 1)] + cache_specs
                 + [mem_spec, mem_spec],
        out_specs=[out_spec, out_spec],
        out_shape=[out_shape, out_shape],
        compiler_params=_params("parallel"),
        name="sample_attend",
    )(z3, z3, z3, z3, *cache_args, cache_mk, cache_mv)


SAMPLE_GROUP = 8


def _sample_hgrn_body(zq_ref, zf_ref, zi_ref, lb_ref, gain_ref, s_ref, o_ref, so_ref):
    lb = _lower_bound(lb_ref)
    gain = gain_ref[...]
    q = jax.nn.sigmoid(zq_ref[...])
    kk = (1.0 - lb) * jax.nn.sigmoid(zf_ref[...])
    f = lb + kk
    v = zi_ref[...]
    pad = jnp.zeros((HEAD_DIM - SAMPLE_GROUP, HEAD_DIM), F32)

    def columns(a, h):
        return jnp.concatenate([a[:, h * HEAD_DIM:(h + 1) * HEAD_DIM], pad], axis=0).T

    for h in range(HEADS):
        sl = slice(h * HEAD_DIM, (h + 1) * HEAD_DIM)
        q_t, k_t, f_t = columns(q, h), columns(kk, h), columns(f, h)
        for j in range(SAMPLE_GROUP):
            s1 = f_t[:, j:j + 1] * s_ref[j, h] + k_t[:, j:j + 1] * v[j:j + 1, sl]
            so_ref[j, h] = s1
            o = jnp.sum(q_t[:, j:j + 1] * s1, axis=0, keepdims=True)
            o_ref[j:j + 1, sl] = _rms(o, gain[:, sl])


def _sample_hgrn(z, lb_logits, gain, state):
    bd = z.shape[0]
    zspec = lambda col: pl.BlockSpec((SAMPLE_GROUP, WIDTH), lambda i: (i, col))
    sspec = pl.BlockSpec((None, SAMPLE_GROUP, HEADS, HEAD_DIM, HEAD_DIM), lambda i: (0, i, 0, 0, 0))
    return pl.pallas_call(
        _sample_hgrn_body,
        grid=(bd // SAMPLE_GROUP,),
        in_specs=[zspec(COL_HQ), zspec(COL_HF), zspec(COL_HI), _const_spec(lb_logits.shape),
                  _const_spec((1, WIDTH)), sspec],
        out_specs=[pl.BlockSpec((SAMPLE_GROUP, WIDTH), lambda i: (i, 0)), sspec],
        out_shape=[jax.ShapeDtypeStruct((bd, WIDTH), F32), jax.ShapeDtypeStruct(state.shape, F32)],
        compiler_params=_params("parallel"),
        name="sample_hgrn",
    )(z, z, z, lb_logits, gain, state)


def kernel(x_prompt, x_sample, mem_prompt, cache_win1_k, cache_win1_v, cache_win4_k, cache_win4_v, cache_win16_k, cache_win16_v, cache_mem_k, cache_mem_v, state_hgrn, g_ff1, w_ff1_gate, w_ff1_up, w_ff1_down, g_mix, w_in, hg_lb_logits, g_hg_out, g_mem, w_mem_k, w_mem_v, w_branch_att, w_branch_hg, w_branch_mem, w_out, g_ff2, w_ff2_gate, w_ff2_up, w_ff2_down, g_final):
    b, t, _ = x_prompt.shape
    bd = x_sample.shape[0]
    assert g_ff1.shape[0] == 1 and x_sample.shape[1] == 1 and w_in.shape[2] == IN_WIDTH
    g_fin = g_final.reshape(1, D_MODEL)

    xs, w_ff1_b = _sample_ffn(x_sample.reshape(bd, D_MODEL), g_ff1, w_ff1_gate, w_ff1_up, w_ff1_down, g_fin,
                              final=False)
    zs, w_qkv, w_rest = _sample_in_proj(xs, g_mix, w_in, COL_HQ)
    caches_k = (cache_win1_k, cache_win4_k, cache_win16_k)
    caches_v = (cache_win1_v, cache_win4_v, cache_win16_v)
    z3 = zs.reshape(bd, 1, IN_WIDTH)
    att_s, mem_s = _sample_attend(z3, caches_k, caches_v, cache_mem_k, cache_mem_v)
    hg_s, s_hgrn = _sample_hgrn(zs, hg_lb_logits, g_hg_out, state_hgrn)
    xs, branch_w = _sample_merge(xs, att_s.reshape(bd, WIDTH), hg_s, mem_s.reshape(bd, WIDTH), zs,
                                 w_branch_att, w_branch_hg, w_branch_mem, w_out)
    y_sample, w_ff2_b = _sample_ffn(xs, g_ff2, w_ff2_gate, w_ff2_up, w_ff2_down, g_fin, final=True)
    y_sample = y_sample.reshape(bd, 1, D_MODEL)
    ffn1 = (g_ff1, *w_ff1_b, g_fin)
    ffn2 = (g_ff2, *w_ff2_b, g_fin)

    tm = PROMPT_TILE
    assert b * t // tm == bd
    xp, (s_win16_k,) = _ffn(x_prompt.reshape(b * t, D_MODEL), *ffn1, tm=tm, final=False, ride=(cache_win16_k,))
    qkv = _qkv_proj(xp.reshape(b, t, D_MODEL), g_mix, w_qkv, tm=tm)
    q_g, k_g, v_g, p_k, p_v = qkv[0:3], qkv[3:6], qkv[6:9], qkv[9:12], qkv[12:15]
    (hq, hk, lf, hv, mq), (s_win1_k, s_win1_v, s_win4_k, s_win4_v), outs, lses = _rest_att(
        xp, g_mix, w_rest, hg_lb_logits, q_g, k_g, v_g, tm=tm,
        ride=(cache_win1_k, cache_win1_v, cache_win4_k, cache_win4_v))
    hg_p, p_hgrn = _hgrn_prompt(hq, hk, lf, hv, g_hg_out, b, t)
    p_mem_k, p_mem_v, mk_b, mv_b = _mem_kv(mem_prompt, g_mem, w_mem_k, w_mem_v)
    xp = _mix(xp, outs, lses, mq, mk_b, mv_b, hg_p, g_mix, w_rest, *branch_w, b, t, tm=tm)
    y_prompt, (s_win16_v,) = _ffn(xp, *ffn2, tm=tm, final=True, ride=(cache_win16_v,))
    s_win1_k, s_win1_v, s_win4_k, s_win4_v, s_win16_k, s_win16_v = _write_new_rows(
        z3, [s_win1_k, s_win1_v, s_win4_k, s_win4_v, s_win16_k, s_win16_v],
        ((0, False), (0, True), (1, False), (1, True), (2, False), (2, True)))

    return (y_prompt.reshape(b, t, D_MODEL), y_sample,
            p_k[0], p_v[0], p_k[1], p_v[1], p_k[2], p_v[2],
            p_mem_k, p_mem_v, p_hgrn,
            s_win1_k, s_win1_v, s_win4_k, s_win4_v, s_win16_k, s_win16_v,
            s_hgrn)
```

```python
import functools

import numpy as np
import jax
import jax.numpy as jnp
from jax import lax
from jax.experimental import pallas as pl
from jax.experimental.pallas import tpu as pltpu

F32 = jnp.float32
BF16 = jnp.bfloat16

D_MODEL = 1024
D_FF = 2816
EPS = 1e-6
HEADS = 4
HEAD_DIM = 128
WIDTH = HEADS * HEAD_DIM
ATT_GROUPS = ((128, 1), (512, 4), (2048, 16))
N_GROUPS = 3
BLOCK = 128
MEM_LEN = 256
N_BRANCH = 3
LSE_REP = 32
PROMPT_TILE = 512
QK_SCALE = HEAD_DIM ** -0.5
VMEM_LIMIT_BYTES = 56 * 1024 * 1024

COL_Q, COL_K, COL_V, COL_HQ, COL_HF, COL_HI, COL_MQ, COL_GATE = 0, 3, 6, 9, 10, 11, 12, 13
IN_WIDTH = (COL_GATE + N_BRANCH * D_MODEL // WIDTH) * WIDTH


def _params(*sem):
    return pltpu.CompilerParams(dimension_semantics=sem, vmem_limit_bytes=VMEM_LIMIT_BYTES)


def _const_spec(shape):
    nd = len(shape)
    return pl.BlockSpec(shape, lambda *_: (0,) * nd, pipeline_mode=pl.Buffered(1))


def _rms(x, g):
    return x * lax.rsqrt(jnp.mean(x * x, axis=-1, keepdims=True) + EPS) * g


def _dot(a, b):
    return jnp.dot(a, b, preferred_element_type=F32)


def _dot_nt(a, b):
    return lax.dot_general(a, b, (((1,), (1,)), ((), ())), preferred_element_type=F32)


def _ride_specs(caches, n_steps, to_seq=lambda i: i):
    if not caches:
        return [], [], [], []
    assert all(c.shape[1] == n_steps for c in caches)
    in_specs = [pl.BlockSpec((1, 1, c.shape[2], HEADS, HEAD_DIM), lambda *idx: (0, to_seq(*idx), 0, 0, 0))
                for c in caches]
    out_specs = [pl.BlockSpec(memory_space=pl.ANY)] * len(caches)
    shapes = [jax.ShapeDtypeStruct(c.shape, c.dtype) for c in caches]
    return in_specs, out_specs, shapes, [pltpu.SemaphoreType.DMA((2 * len(caches),))]


def _ride_copies(in_refs, out_refs, sem, seq=None):
    seq = pl.program_id(0) if seq is None else seq
    copies = []
    for j, (cache_ref, out_ref) in enumerate(zip(in_refs, out_refs)):
        rows = cache_ref.shape[2]
        copies.append(pltpu.make_async_copy(cache_ref.at[:, :, pl.ds(1, rows - 1)],
                                            out_ref.at[:, pl.ds(seq, 1), pl.ds(0, rows - 1)], sem.at[2 * j]))
        copies.append(pltpu.make_async_copy(cache_ref.at[:, :, pl.ds(rows - 1, 1)],
                                            out_ref.at[:, pl.ds(seq, 1), pl.ds(rows - 1, 1)], sem.at[2 * j + 1]))
    for cp in copies:
        cp.start()
    return copies


def _new_rows_body(kn_ref, vn_ref, *refs, meta):
    n = len(meta)
    out_refs, row_refs, sem = refs[n:2 * n], refs[2 * n:3 * n], refs[3 * n]
    copies = []
    for j, ((g, is_v), out_ref, row_ref) in enumerate(zip(meta, out_refs, row_refs)):
        bd, rows = out_ref.shape[1], out_ref.shape[2]
        new = (vn_ref if is_v else kn_ref)[:, 0, g * WIDTH:(g + 1) * WIDTH]
        flat = row_ref.reshape(bd * HEADS, HEAD_DIM)
        for h in range(HEADS):
            flat[pl.ds(h, bd, stride=HEADS), :] = new[:, h * HEAD_DIM:(h + 1) * HEAD_DIM]
        copies.append(pltpu.make_async_copy(row_ref, out_ref.at[:, :, pl.ds(rows - 1, 1)], sem.at[j]))
        copies[-1].start()
    for cp in copies:
        cp.wait()


def _write_new_rows(z3, shifted, meta):
    bd = z3.shape[0]
    n = len(shifted)
    zspec = lambda col: pl.BlockSpec((bd, 1, N_GROUPS * WIDTH), lambda i: (0, 0, col // N_GROUPS))
    any_spec = pl.BlockSpec(memory_space=pl.ANY)
    return pl.pallas_call(
        functools.partial(_new_rows_body, meta=meta),
        grid=(1,),
        in_specs=[zspec(COL_K), zspec(COL_V)] + [any_spec] * n,
        out_specs=[any_spec] * n,
        out_shape=[jax.ShapeDtypeStruct(c.shape, c.dtype) for c in shifted],
        input_output_aliases={2 + j: j for j in range(n)},
        scratch_shapes=[pltpu.VMEM((1, bd, 1, HEADS, HEAD_DIM), F32)] * n + [pltpu.SemaphoreType.DMA((n,))],
        name="write_new_rows",
    )(z3, z3, *shifted)


def _ffn_body(x_ref, g_ref, wg_ref, wu_ref, wd_ref, gf_ref, *rest, final, n_ride):
    o_ref = rest[n_ride]
    copies = _ride_copies(rest[:n_ride], rest[n_ride + 1:2 * n_ride + 1], rest[-1]) if n_ride else []
    x = x_ref[...]
    h = _rms(x, g_ref[...]).astype(BF16)
    a = _dot(h, wg_ref[...])
    act = (a * jax.nn.sigmoid(a) * _dot(h, wu_ref[...])).astype(BF16)
    y = x + 0.5 * _dot(act, wd_ref[...])
    if final:
        y = _rms(y, gf_ref[...])
    o_ref[...] = y
    for cp in copies:
        cp.wait()


def _ffn(x, g, wg, wu, wd, g_final, *, tm, final, ride=()):
    m = x.shape[0]
    r_in, r_out, r_shapes, r_scratch = _ride_specs(ride, m // tm)
    res = pl.pallas_call(
        functools.partial(_ffn_body, final=final, n_ride=len(ride)),
        grid=(m // tm,),
        in_specs=[pl.BlockSpec((tm, D_MODEL), lambda i: (i, 0)),
                  _const_spec((1, D_MODEL)),
                  _const_spec((D_MODEL, D_FF)), _const_spec((D_MODEL, D_FF)), _const_spec((D_FF, D_MODEL)),
                  _const_spec((1, D_MODEL))] + r_in,
        out_specs=[pl.BlockSpec((tm, D_MODEL), lambda i: (i, 0))] + r_out,
        out_shape=[jax.ShapeDtypeStruct((m, D_MODEL), F32)] + r_shapes,
        scratch_shapes=r_scratch,
        compiler_params=_params("parallel"),
        name="ffn_final" if final else "ffn",
    )(x, g, wg, wu, wd, g_final, *ride)
    return res[0], res[1:]


def _store_heads(ref, z):
    rows = ref.shape[0]
    flat = ref.reshape(rows * HEADS, HEAD_DIM)
    for h in range(HEADS):
        flat[pl.ds(h, rows, stride=HEADS), :] = z[:, h * HEAD_DIM:(h + 1) * HEAD_DIM]


def _load_head(ref, h):
    rows = ref.shape[0]
    return ref.reshape(rows * HEADS, HEAD_DIM)[pl.ds(h, rows, stride=HEADS), :]


SUBLANES = 8


def _stage_pitch(dil):
    return dil + SUBLANES // 2 if dil % (2 * SUBLANES) == 0 else dil


def _stage_shape(rows, dil, width):
    return (width // 128, rows // dil * _stage_pitch(dil), 128)


def _to_strided_view(dst_ref, z, stage_ref, dil):
    if dil == 1:
        dst_ref[...] = z.astype(dst_ref.dtype)
        return
    n = z.shape[0] // dil
    pitch = _stage_pitch(dil)
    for c in range(stage_ref.shape[0]):
        if pitch == dil:
            stage_ref[c] = z[:, c * 128:(c + 1) * 128]
        else:
            for l in range(n):
                stage_ref[c, l * pitch:l * pitch + dil, :] = z[l * dil:(l + 1) * dil, c * 128:(c + 1) * 128]
    for r in range(dil):
        for c in range(stage_ref.shape[0]):
            dst_ref[:, r * WIDTH + c * 128:r * WIDTH + (c + 1) * 128] = (
                stage_ref[c, pl.ds(r, n, stride=pitch), :].astype(dst_ref.dtype))


def _qkv_body(x_ref, g_ref, w_ref, *refs, tm, n_tiles):
    q_refs, k_refs, v_refs = refs[0:3], refs[3:6], refs[6:9]
    pk_refs, pv_refs, stage_refs = refs[9:12], refs[12:15], refs[15:]
    h = _rms(x_ref[...], g_ref[...]).astype(BF16)
    late = []
    for g, (win, dil) in reversed(list(enumerate(ATT_GROUPS))):
        keep = min(win, tm * n_tiles)
        for kind, (col, dst_refs, win_refs) in enumerate(((COL_Q, q_refs, None), (COL_K, k_refs, pk_refs),
                                                          (COL_V, v_refs, pv_refs))):
            z = _dot(h, w_ref[:, (col + g) * WIDTH:(col + g + 1) * WIDTH])
            if win_refs is None:
                z = z * QK_SCALE
            stage = stage_refs[3 * (g - 1) + kind] if dil > 1 else None
            _to_strided_view(dst_refs[g], z, stage, dil)
            if win_refs is None:
                continue
            if keep == tm * n_tiles:
                _store_heads(win_refs[g], z)
            elif dil == 1:
                late.append((win_refs[g], z[tm - keep:, :], None))
            else:
                assert keep == tm
                late.append((win_refs[g], None, stage))

    @pl.when(pl.program_id(1) == n_tiles - 1)
    def _():
        for ref, rows, stage in late:
            if stage is None:
                _store_heads(ref, rows)
            else:
                _store_heads(ref, jnp.concatenate([stage[hd] for hd in range(HEADS)], axis=-1))


def _qkv_proj(x, g_mix, w_qkv, *, tm):
    b, t, _ = x.shape
    n_tiles = t // tm
    view_specs = [pl.BlockSpec((None, tm // d, d * WIDTH), lambda bi, i: (bi, i, 0)) for _, d in ATT_GROUPS]
    view_shapes = [jax.ShapeDtypeStruct((b, t // d, d * WIDTH), BF16) for _, d in ATT_GROUPS]
    win_specs, win_shapes = [], []
    for win, _ in ATT_GROUPS:
        keep = min(win, t)
        assert keep == t or keep <= tm
        if keep == t:
            win_specs.append(pl.BlockSpec((None, None, tm, HEADS, HEAD_DIM), lambda bi, i: (0, bi, i, 0, 0)))
        else:
            win_specs.append(pl.BlockSpec((None, None, keep, HEADS, HEAD_DIM), lambda bi, i: (0, bi, 0, 0, 0)))
        win_shapes.append(jax.ShapeDtypeStruct((1, b, keep, HEADS, HEAD_DIM), F32))
    return pl.pallas_call(
        functools.partial(_qkv_body, tm=tm, n_tiles=n_tiles),
        grid=(b, n_tiles),
        in_specs=[pl.BlockSpec((None, tm, D_MODEL), lambda bi, i: (bi, i, 0)),
                  _const_spec((1, D_MODEL)),
                  _const_spec((D_MODEL, 3 * N_GROUPS * WIDTH))],
        out_specs=view_specs * 3 + win_specs + win_specs,
        out_shape=view_shapes * 3 + win_shapes + win_shapes,
        scratch_shapes=[pltpu.VMEM(_stage_shape(tm, d, WIDTH), F32) for _, d in ATT_GROUPS if d > 1 for _ in range(3)],
        compiler_params=_params("parallel", "arbitrary"),
        name="qkv_proj",
    )(x, g_mix, w_qkv)


def _lower_bound(logits_ref):
    lg = logits_ref[...]
    e = jnp.exp(lg - jnp.max(lg, axis=0, keepdims=True))
    return e[0:1, :] / jnp.sum(e, axis=0, keepdims=True)


def _rest_steps(x_ref, g_ref, w_ref, lb_ref, hq_ref, hk_ref, lf_ref, hv_ref, mq_ref):
    h = _rms(x_ref[...], g_ref[...]).astype(BF16)
    lb = _lower_bound(lb_ref)
    proj = lambda c: _dot(h, w_ref[:, c * WIDTH:(c + 1) * WIDTH])

    def hgrn_query():
        hq_ref[...] = jax.nn.sigmoid(proj(0)).astype(BF16)

    def hgrn_key():
        kk = (1.0 - lb) * jax.nn.sigmoid(proj(1))
        hk_ref[...] = kk.astype(BF16)
        lf_ref[...] = jnp.log(lb + kk)

    def hgrn_value():
        hv_ref[...] = proj(2).astype(BF16)

    def mem_query():
        mq_ref[...] = (proj(3) * QK_SCALE).astype(BF16)

    return [hgrn_query, hgrn_key, hgrn_value, mem_query]


def _rest_att_body(x_ref, g_ref, w_ref, lb_ref, *rest, n_ride, tiles, tiles_per_seq):
    n_att_in = sum(5 if has_halo else 3 for _, _, has_halo in tiles)
    caches, rest = rest[:n_ride], rest[n_ride:]
    att_in, rest = rest[:n_att_in], rest[n_att_in:]
    proj_out, rest = rest[:N_REST_OUT], rest[N_REST_OUT:]
    ride_out, rest = rest[:n_ride], rest[n_ride:]
    att_out, scratch = rest[:2 * len(tiles)], rest[2 * len(tiles):]
    copies = _ride_copies(caches, ride_out, scratch[-1]) if n_ride else []
    steps = _rest_steps(x_ref, g_ref, w_ref, lb_ref, *proj_out)
    gens = _att_tiles(att_in + att_out, tiles, pl.program_id(0) % tiles_per_seq)
    steps.pop(0)()
    for gen in gens:
        next(gen)
        if steps:
            steps.pop(0)()
        next(gen)
        next(gen, None)
    for step in steps:
        step()
    for cp in copies:
        cp.wait()


N_REST_OUT = 5


def _rest_att(x, g_mix, w_rest, lb_logits, q_g, k_g, v_g, *, tm, ride=()):
    m = x.shape[0]
    spec = pl.BlockSpec((tm, WIDTH), lambda i: (i, 0))
    bf = jax.ShapeDtypeStruct((m, WIDTH), BF16)
    r_in, r_out, r_shapes, r_scratch = _ride_specs(ride, m // tm)
    tiles_per_seq = q_g[0].shape[1] // (ATT_UNITS * BLOCK)
    a_in, a_args, a_out, a_shapes, tiles, n_tiles = _attention_specs(
        q_g, k_g, v_g, lambda i: (i // tiles_per_seq, i % tiles_per_seq))
    assert n_tiles == tiles_per_seq and q_g[0].shape[0] * n_tiles == m // tm
    res = pl.pallas_call(
        functools.partial(_rest_att_body, n_ride=len(ride), tiles=tiles, tiles_per_seq=tiles_per_seq),
        grid=(m // tm,),
        in_specs=[pl.BlockSpec((tm, D_MODEL), lambda i: (i, 0)),
                  _const_spec((1, D_MODEL)),
                  pl.BlockSpec((D_MODEL, (COL_GATE - COL_HQ) * WIDTH), lambda i: (0, 0), pipeline_mode=pl.Buffered(1)),
                  _const_spec(lb_logits.shape)] + r_in + a_in,
        out_specs=[spec] * N_REST_OUT + r_out + a_out,
        out_shape=[bf, bf, jax.ShapeDtypeStruct((m, WIDTH), F32), bf, bf] + r_shapes + a_shapes,
        scratch_shapes=r_scratch,
        compiler_params=_params("arbitrary"),
        name="rest_att",
    )(x, g_mix, w_rest, lb_logits, *ride, *a_args)
    n = N_REST_OUT + len(ride)
    return res[:N_REST_OUT], res[N_REST_OUT:n], res[n::2], res[n + 1::2]


ATT_UNITS = 4
STAT_REP = LSE_REP // 2


def _att_tiles(refs, tiles, tile_index):
    n_in = [5 if has_halo else 3 for _, _, has_halo in tiles]
    outs = refs[sum(n_in):]
    gens, pos = [], 0
    for g, (n_row_blocks, n_col_blocks, has_halo) in enumerate(tiles):
        gens.append(_att_tile(*refs[pos:pos + n_in[g]], outs[2 * g], outs[2 * g + 1], n_row_blocks=n_row_blocks,
                              n_col_blocks=n_col_blocks, has_halo=has_halo, tile_index=tile_index))
        pos += n_in[g]
    return gens


def _att_tile(q_ref, k_ref, v_ref, *rest, n_row_blocks, n_col_blocks, has_halo, tile_index):
    if has_halo:
        kh_ref, vh_ref, o_ref, st_ref = rest
        halo_ok = tile_index > 0
    else:
        o_ref, st_ref = rest
    qi = lax.broadcasted_iota(jnp.int32, (BLOCK, 2 * BLOCK), 0)
    ki = lax.broadcasted_iota(jnp.int32, (BLOCK, 2 * BLOCK), 1)
    own_ok = (lax.broadcasted_iota(jnp.int32, (BLOCK, BLOCK), 1)
              <= lax.broadcasted_iota(jnp.int32, (BLOCK, BLOCK), 0))
    both_ok = jnp.where(ki < BLOCK, ki, qi + BLOCK) >= jnp.where(ki < BLOCK, qi, ki)
    if has_halo:
        first_valid = jnp.where(halo_ok, 0, BLOCK)
        halo_both_ok = jnp.logical_and(both_ok, ki >= first_valid)

    units = []
    for c in range(n_col_blocks):
        for u in range(n_row_blocks):
            for h in range(HEADS):
                rows = slice(u * BLOCK, (u + 1) * BLOCK)
                cols = slice(c * WIDTH + h * HEAD_DIM, c * WIDTH + (h + 1) * HEAD_DIM)
                lcols = (c * HEADS + h) * LSE_REP
                if u > 0:
                    with_prev = slice((u - 1) * BLOCK, (u + 1) * BLOCK)
                    keys = lambda ref, rr=with_prev, cc=cols: ref[rr, cc]
                    ok = both_ok
                elif has_halo:
                    hcols = slice(h * HEAD_DIM, (h + 1) * HEAD_DIM)
                    keys = lambda ref, rr=rows, cc=cols, hc=hcols: jnp.concatenate(
                        [(kh_ref if ref is k_ref else vh_ref)[:, hc], ref[rr, cc]], axis=0)
                    ok = halo_both_ok
                else:
                    keys = lambda ref, rr=rows, cc=cols: ref[rr, cc]
                    ok = own_ok
                units.append((rows, cols, lcols, keys, ok))

    scores = [jnp.where(ok, _dot_nt(q_ref[rows, cols], keys(k_ref)), -jnp.inf) for rows, cols, _, keys, ok in units]
    yield

    probs = []
    for s in scores:
        m = jnp.max(s, axis=-1, keepdims=True)
        p = jnp.exp(s - m)
        probs.append((p.astype(BF16), m, jnp.sum(p, axis=-1, keepdims=True)))
    yield

    for (rows, cols, lcols, keys, _), (p, m, l) in zip(units, probs):
        o_ref[rows, cols] = _dot(p, keys(v_ref)).astype(BF16)
        st_ref[rows, lcols:lcols + STAT_REP] = jnp.broadcast_to(m, (BLOCK, STAT_REP))
        st_ref[rows, lcols + STAT_REP:lcols + LSE_REP] = jnp.broadcast_to(l, (BLOCK, STAT_REP))


def _attention_specs(q_g, k_g, v_g, to_bj):
    b = q_g[0].shape[0]
    lw = HEADS * LSE_REP
    n_steps = None
    in_specs, args, out_specs, out_shapes, tiles = [], [], [], [], []

    def spec(shape, index):
        return pl.BlockSpec(shape, lambda *grid_idx: index(*to_bj(*grid_idx)))

    for (_, dil), q, k, v in zip(ATT_GROUPS, q_g, k_g, v_g):
        sub_len = q.shape[1]
        n_row_blocks = min(ATT_UNITS, sub_len // BLOCK)
        n_col_blocks = ATT_UNITS // n_row_blocks
        span = n_row_blocks * BLOCK
        assert sub_len % span == 0 and dil % n_col_blocks == 0
        has_halo = sub_len > span
        assert not has_halo or (dil == 1 and n_col_blocks == 1)
        n = (sub_len // span) * (dil // n_col_blocks)
        assert n_steps in (None, n)
        n_steps = n
        if has_halo:
            tile = lambda w, span=span: spec((None, span, w), lambda bi, j: (bi, j, 0))
        else:
            tile = lambda w, span=span, nc=n_col_blocks: spec((None, span, nc * w), lambda bi, j: (bi, 0, j))
        halo = spec((None, BLOCK, WIDTH), lambda bi, j, nr=n_row_blocks: (bi, jnp.maximum(j * nr - 1, 0), 0))
        in_specs += [tile(WIDTH)] * 3 + ([halo, halo] if has_halo else [])
        args += [q, k, v] + ([k, v] if has_halo else [])
        out_specs += [tile(WIDTH), tile(lw)]
        out_shapes += [jax.ShapeDtypeStruct((b, sub_len, dil * WIDTH), BF16),
                       jax.ShapeDtypeStruct((b, sub_len, dil * lw), F32)]
        tiles.append((n_row_blocks, n_col_blocks, has_halo))
    return in_specs, args, out_specs, out_shapes, tuple(tiles), n_steps


N_LEVELS = 7
ROW_CUM, ROW_REM = N_LEVELS, N_LEVELS + 1


def _hgrn_constants():
    r = np.arange(BLOCK)[:, None]
    s = np.arange(BLOCK)[None, :]
    mats = []
    for lvl in range(N_LEVELS):
        half = 1 << lvl
        mid = (r // (2 * half)) * 2 * half + half
        upper = (r % (2 * half)) >= half
        mats.append(np.where(upper, (s >= mid) & (s <= r), (s > r) & (s <= mid - 1)))
    mats.append(s <= r)
    mats.append(s > r)
    range_mat = np.concatenate(mats, axis=0).astype(np.float32)
    range_mat = np.concatenate([range_mat, range_mat], axis=1)
    x = r ^ s
    level = np.where(s < r, np.floor(np.log2(np.maximum(x, 1))).astype(np.int32), np.where(s == r, N_LEVELS, -1))
    return jnp.asarray(range_mat, BF16), jnp.asarray(level, jnp.int32)


HGRN_SEQS = 8


def _hgrn_intra(q_b, k_b, lf, rng, level):
    lf_hi = lf.astype(BF16)
    lf_lo = (lf - lf_hi.astype(F32)).astype(BF16)
    z = _dot(rng, jnp.concatenate([lf_hi, lf_lo], axis=0))
    q = q_b.astype(F32)
    kk = k_b.astype(F32)
    lhs, rhs = [], []
    for lvl in range(N_LEVELS):
        x = jnp.exp(z[lvl * BLOCK:(lvl + 1) * BLOCK]).astype(BF16)
        lhs.append(q_b * x)
        rhs.append(k_b * x)
    lhs.append(q_b)
    rhs.append(k_b)
    b_cum = z[ROW_CUM * BLOCK:(ROW_CUM + 1) * BLOCK]
    q_in = (q * jnp.exp(b_cum)).astype(BF16)
    k_out = (kk * jnp.exp(z[ROW_REM * BLOCK:(ROW_REM + 1) * BLOCK])).astype(BF16)
    decay = jnp.exp(b_cum[BLOCK - 1:BLOCK, :])
    mats = []
    for h in range(HEADS):
        sl = slice(h * HEAD_DIM, (h + 1) * HEAD_DIM)
        a = jnp.zeros((BLOCK, BLOCK), F32)
        for lvl in range(N_LEVELS + 1):
            a = jnp.where(level == lvl, _dot_nt(lhs[lvl][:, sl], rhs[lvl][:, sl]), a)
        mats.append(a.astype(BF16))
    return mats, q_in, k_out, decay


def _hgrn_state(mats, q_in, k_out, decay, v_b, gain, st_ref, j):
    outs = []
    for h in range(HEADS):
        sl = slice(h * HEAD_DIM, (h + 1) * HEAD_DIM)
        v = v_b[:, sl]
        st = st_ref[j, h]
        o = _dot(mats[h], v) + _dot_nt(q_in[:, sl], st.astype(BF16))
        outs.append(_rms(o, gain[:, sl]).astype(BF16))
        v_t = v.astype(F32).T.astype(BF16)
        st_ref[j, h] = st * decay[:, sl] + _dot(v_t, k_out[:, sl])
    return jnp.concatenate(outs, axis=-1)


def _hgrn_body(q_ref, k_ref, lf_ref, v_ref, rng_ref, lvl_ref, gain_ref, o_ref, s_ref, st_ref, *, n_chunks):
    c = pl.program_id(1)

    @pl.when(c == 0)
    def _():
        st_ref[...] = jnp.zeros_like(st_ref)

    rng, level, gain = rng_ref[...], lvl_ref[...], gain_ref[...]
    intra = [_hgrn_intra(q_ref[j], k_ref[j], lf_ref[j], rng, level) for j in range(HGRN_SEQS)]
    for j in range(HGRN_SEQS):
        o_ref[j] = _hgrn_state(*intra[j], v_ref[j], gain, st_ref, j)

    @pl.when(c == n_chunks - 1)
    def _():
        for j in range(HGRN_SEQS):
            for h in range(HEADS):
                s_ref[j, h] = st_ref[j, h].T


def _hgrn_prompt(hq, hk, lf, hv, gain, b, t):
    n_chunks = t // BLOCK
    range_mat, level = _hgrn_constants()
    as3d = lambda a: a.reshape(b, t, WIDTH)
    spec = pl.BlockSpec((HGRN_SEQS, BLOCK, WIDTH), lambda bi, c: (bi, c, 0))
    state_spec = pl.BlockSpec((None, HGRN_SEQS, HEADS, HEAD_DIM, HEAD_DIM), lambda bi, c: (0, bi, 0, 0, 0))
    o, state = pl.pallas_call(
        functools.partial(_hgrn_body, n_chunks=n_chunks),
        grid=(b // HGRN_SEQS, n_chunks),
        in_specs=[spec, spec, spec, spec, _const_spec(range_mat.shape), _const_spec(level.shape),
                  _const_spec((1, WIDTH))],
        out_specs=[spec, state_spec],
        out_shape=[jax.ShapeDtypeStruct((b, t, WIDTH), BF16),
                   jax.ShapeDtypeStruct((1, b, HEADS, HEAD_DIM, HEAD_DIM), F32)],
        scratch_shapes=[pltpu.VMEM((HGRN_SEQS, HEADS, HEAD_DIM, HEAD_DIM), F32)],
        compiler_params=_params("parallel", "arbitrary"),
        name="hgrn_prompt",
    )(as3d(hq), as3d(hk), as3d(lf), as3d(hv), range_mat, level, gain)
    return o.reshape(b * t, WIDTH), state


def _memkv_body(m_ref, g_ref, wk_ref, wv_ref, pk_ref, pv_ref, kb_ref, vb_ref):
    h = _rms(m_ref[...], g_ref[...]).astype(BF16)
    k = _dot(h, wk_ref[...].astype(BF16))
    v = _dot(h, wv_ref[...].astype(BF16))
    _store_heads(pk_ref, k)
    _store_heads(pv_ref, v)
    kb_ref[...] = k.astype(BF16)
    vb_ref[...] = v.astype(BF16)


def _mem_kv(mem, g_mem, w_mk, w_mv):
    b = mem.shape[0]
    head_spec = pl.BlockSpec((None, None, MEM_LEN, HEADS, HEAD_DIM), lambda bi: (0, bi, 0, 0, 0))
    flat_spec = pl.BlockSpec((None, MEM_LEN, WIDTH), lambda bi: (bi, 0, 0))
    head_shape = jax.ShapeDtypeStruct((1, b, MEM_LEN, HEADS, HEAD_DIM), F32)
    flat_shape = jax.ShapeDtypeStruct((b, MEM_LEN, WIDTH), BF16)
    return pl.pallas_call(
        _memkv_body,
        grid=(b,),
        in_specs=[pl.BlockSpec((None, MEM_LEN, D_MODEL), lambda bi: (bi, 0, 0)),
                  _const_spec((1, D_MODEL)),
                  pl.BlockSpec((None, D_MODEL, WIDTH), lambda bi: (0, 0, 0), pipeline_mode=pl.Buffered(1)),
                  pl.BlockSpec((None, D_MODEL, WIDTH), lambda bi: (0, 0, 0), pipeline_mode=pl.Buffered(1))],
        out_specs=[head_spec, head_spec, flat_spec, flat_spec],
        out_shape=[head_shape, head_shape, flat_shape, flat_shape],
        compiler_params=_params("parallel"),
        name="mem_kv",
    )(mem, g_mem, w_mk, w_mv)


def _from_strided_view(src_ref, stage_ref, dil, width):
    if dil == 1:
        return src_ref[...].astype(F32)
    n = src_ref.shape[0]
    n_col = width // 128
    pitch = _stage_pitch(dil)
    for r in range(dil):
        for c in range(n_col):
            stage_ref[c, pl.ds(r, n, stride=pitch), :] = (
                src_ref[:, r * width + c * 128:r * width + (c + 1) * 128].astype(F32))

    def natural(c):
        if pitch == dil:
            return stage_ref[c]
        return jnp.concatenate([stage_ref[c, l * pitch:l * pitch + dil, :] for l in range(n)], axis=0)

    return jnp.concatenate([natural(c) for c in range(n_col)], axis=-1)


def _mix_body(x_ref, o0_ref, o1_ref, o2_ref, s0_ref, s1_ref, s2_ref, mq_ref, mk_ref, mv_ref, hg_ref, g_ref,
              wg0_ref, wg1_ref, wg2_ref, wa_ref, wb_ref, wc_ref, wo_ref, *rest, n_ride):
    out_ref = rest[n_ride]
    stage_refs = rest[2 * n_ride + 1:2 * n_ride + 1 + 2 * N_GROUPS]
    copies = []
    if n_ride:
        seq = pl.program_id(0) * pl.num_programs(1) + pl.program_id(1)
        copies = _ride_copies(rest[:n_ride], rest[n_ride + 1:2 * n_ride + 1], rest[-1], seq)
    _mix_tile(x_ref, o0_ref, o1_ref, o2_ref, s0_ref, s1_ref, s2_ref, mq_ref, mk_ref, mv_ref, hg_ref, g_ref,
              wg0_ref, wg1_ref, wg2_ref, wa_ref, wb_ref, wc_ref, wo_ref, out_ref, *stage_refs)
    for cp in copies:
        cp.wait()


def _mix_tile(x_ref, o0_ref, o1_ref, o2_ref, s0_ref, s1_ref, s2_ref, mq_ref, mk_ref, mv_ref, hg_ref, g_ref,
              wg0_ref, wg1_ref, wg2_ref, wa_ref, wb_ref, wc_ref, wo_ref, out_ref, *stage_refs):
    heads = [slice(h * HEAD_DIM, (h + 1) * HEAD_DIM) for h in range(HEADS)]
    hn = _rms(x_ref[...], g_ref[...]).astype(BF16)
    gate = lambda w_ref: jax.nn.sigmoid(_dot(hn, w_ref[...]))
    gate_att, gate_hg = gate(wg0_ref), gate(wg1_ref)

    accs = [_from_strided_view(r, stage_refs[2 * g], ATT_GROUPS[g][1], WIDTH)
            for g, r in enumerate((o0_ref, o1_ref, o2_ref))]
    stats = [_from_strided_view(r, stage_refs[2 * g + 1], ATT_GROUPS[g][1], HEADS * LSE_REP)
             for g, r in enumerate((s0_ref, s1_ref, s2_ref))]
    att = []
    for h, sl in enumerate(heads):
        ms = [x[:, h * LSE_REP:h * LSE_REP + 1] for x in stats]
        ls = [x[:, h * LSE_REP + STAT_REP:h * LSE_REP + STAT_REP + 1] for x in stats]
        m = jnp.maximum(jnp.maximum(ms[0], ms[1]), ms[2])
        e = [jnp.exp(x - m) for x in ms]
        inv = 1.0 / (e[0] * ls[0] + e[1] * ls[1] + e[2] * ls[2])
        att.append(sum((e[g] * inv) * accs[g][:, sl] for g in range(N_GROUPS)).astype(BF16))
    merged = gate_att * _dot(jnp.concatenate(att, axis=-1), wa_ref[...])

    scores = [_dot_nt(mq_ref[:, sl], mk_ref[:, sl]) for sl in heads]
    gate_mem = gate(wg2_ref)
    merged = merged + gate_hg * _dot(hg_ref[...], wb_ref[...])
    probs = []
    for s in scores:
        p = jnp.exp(s - jnp.max(s, axis=-1, keepdims=True))
        probs.append((p.astype(BF16), 1.0 / jnp.sum(p, axis=-1, keepdims=True)))
    mem = [(_dot(p, mv_ref[:, sl]) * inv).astype(BF16) for (p, inv), sl in zip(probs, heads)]
    merged = merged + gate_mem * _dot(jnp.concatenate(mem, axis=-1), wc_ref[...])
    out_ref[...] = x_ref[...] + _dot(merged.astype(BF16), wo_ref[...])


def _mix(x, accs, stats, mq, mk, mv, hg, g_mix, w_rest, wa, wb, wc, wo, b, t, *, tm, ride=()):
    n_tiles = t // tm
    first_gate = (COL_GATE - COL_HQ) * WIDTH // D_MODEL
    assert first_gate * D_MODEL == (COL_GATE - COL_HQ) * WIDTH
    gate_w = [pl.BlockSpec((D_MODEL, D_MODEL), lambda bi, i, j=j: (0, first_gate + j), pipeline_mode=pl.Buffered(1))
              for j in range(N_BRANCH)]
    lw = HEADS * LSE_REP
    rows = lambda w: pl.BlockSpec((tm, w), lambda bi, i: (bi * n_tiles + i, 0))
    o_specs = [pl.BlockSpec((None, tm // d, d * WIDTH), lambda bi, i: (bi, i, 0)) for _, d in ATT_GROUPS]
    s_specs = [pl.BlockSpec((None, tm // d, d * lw), lambda bi, i: (bi, i, 0)) for _, d in ATT_GROUPS]
    memspec = pl.BlockSpec((None, MEM_LEN, WIDTH), lambda bi, i: (bi, 0, 0))
    stages = []
    for _, d in ATT_GROUPS:
        stages += [pltpu.VMEM(_stage_shape(tm, d, WIDTH), F32), pltpu.VMEM(_stage_shape(tm, d, lw), F32)]
    r_in, r_out, r_shapes, r_scratch = _ride_specs(ride, b * n_tiles, lambda bi, i: bi * n_tiles + i)
    res = pl.pallas_call(
        functools.partial(_mix_body, n_ride=len(ride)),
        grid=(b, n_tiles),
        in_specs=[rows(D_MODEL)] + o_specs + s_specs + [rows(WIDTH), memspec, memspec, rows(WIDTH),
                                                         _const_spec((1, D_MODEL))] + gate_w
                 + [_const_spec((WIDTH, D_MODEL)), _const_spec((WIDTH, D_MODEL)), _const_spec((WIDTH, D_MODEL)),
                    _const_spec((D_MODEL, D_MODEL))] + r_in,
        out_specs=[rows(D_MODEL)] + r_out,
        out_shape=[jax.ShapeDtypeStruct((b * t, D_MODEL), F32)] + r_shapes,
        scratch_shapes=stages + r_scratch,
        compiler_params=_params("arbitrary", "arbitrary"),
        name="mix",
    )(x, *accs, *stats, mq, mk, mv, hg, g_mix, w_rest, w_rest, w_rest, wa, wb, wc, wo, *ride)
    return res[0], res[1:]


def _sample_merge_body(x_ref, att_ref, hg_ref, mem_ref, z_ref, wa_ref, wb_ref, wc_ref, wo_ref,
                       o_ref, wab_ref, wbb_ref, wcb_ref, wob_ref):
    wa, wb, wc, wo = (r[...].astype(BF16) for r in (wa_ref, wb_ref, wc_ref, wo_ref))
    wab_ref[...], wbb_ref[...], wcb_ref[...], wob_ref[...] = wa, wb, wc, wo
    gate = lambda j: jax.nn.sigmoid(z_ref[:, COL_GATE * WIDTH + j * D_MODEL:COL_GATE * WIDTH + (j + 1) * D_MODEL])
    m = (gate(0) * _dot(att_ref[...].astype(BF16), wa) + gate(1) * _dot(hg_ref[...].astype(BF16), wb)
         + gate(2) * _dot(mem_ref[...].astype(BF16), wc))
    o_ref[...] = x_ref[...] + _dot(m.astype(BF16), wo)


def _sample_merge(x, att, hg, mem, z, wa, wb, wc, wo):
    m = x.shape[0]
    full = lambda a: _const_spec(a.shape)
    w_specs = [pl.BlockSpec((None,) + w.shape[1:], lambda i: (0, 0, 0)) for w in (wa, wb, wc, wo)]
    b_specs = [pl.BlockSpec(w.shape[1:], lambda i: (0, 0)) for w in (wa, wb, wc, wo)]
    res = pl.pallas_call(
        _sample_merge_body,
        grid=(1,),
        in_specs=[full(x), full(att), full(hg), full(mem), full(z)] + w_specs,
        out_specs=[pl.BlockSpec((m, D_MODEL), lambda i: (0, 0))] + b_specs,
        out_shape=[jax.ShapeDtypeStruct((m, D_MODEL), F32)]
                  + [jax.ShapeDtypeStruct(w.shape[1:], BF16) for w in (wa, wb, wc, wo)],
        compiler_params=_params("arbitrary"),
        name="sample_merge",
    )(x, att, hg, mem, z, wa, wb, wc, wo)
    return res[0], res[1:]


def _sample_in_proj_body(x_ref, g_ref, w_ref, o_ref, wa_ref, wb_ref, *, n_a):
    w = w_ref[...].astype(BF16)
    o_ref[...] = _dot(_rms(x_ref[...], g_ref[...]).astype(BF16), w)

    @pl.when(pl.program_id(0) < n_a)
    def _():
        wa_ref[...] = w

    @pl.when(pl.program_id(0) >= n_a)
    def _():
        wb_ref[...] = w


def _sample_in_proj(x, g_mix, w_in, n_a):
    m = x.shape[0]
    n = w_in.shape[2] // WIDTH
    return pl.pallas_call(
        functools.partial(_sample_in_proj_body, n_a=n_a),
        grid=(n,),
        in_specs=[_const_spec((m, D_MODEL)), _const_spec((1, D_MODEL)),
                  pl.BlockSpec((None, D_MODEL, WIDTH), lambda j: (0, 0, j))],
        out_specs=[pl.BlockSpec((m, WIDTH), lambda j: (0, j)),
                   pl.BlockSpec((D_MODEL, WIDTH), lambda j: (0, jnp.minimum(j, n_a - 1))),
                   pl.BlockSpec((D_MODEL, WIDTH), lambda j: (0, jnp.maximum(j - n_a, 0)))],
        out_shape=[jax.ShapeDtypeStruct((m, n * WIDTH), F32),
                   jax.ShapeDtypeStruct((D_MODEL, n_a * WIDTH), BF16),
                   jax.ShapeDtypeStruct((D_MODEL, (n - n_a) * WIDTH), BF16)],
        compiler_params=_params("arbitrary"),
        name="sample_in_proj",
    )(x, g_mix, w_in)


FF_TILE = 256


def _sample_ffn_body(x_ref, g_ref, wg_ref, wu_ref, wd_ref, gf_ref, o_ref, wgb_ref, wub_ref, wdb_ref, acc_ref, *,
                     final):
    j = pl.program_id(0)

    @pl.when(j == 0)
    def _():
        acc_ref[...] = jnp.zeros_like(acc_ref)

    wg, wu, wd = wg_ref[...].astype(BF16), wu_ref[...].astype(BF16), wd_ref[...].astype(BF16)
    wgb_ref[...], wub_ref[...], wdb_ref[...] = wg, wu, wd
    x = x_ref[...]
    h = _rms(x, g_ref[...]).astype(BF16)
    a = _dot(h, wg)
    act = (a * jax.nn.sigmoid(a) * _dot(h, wu)).astype(BF16)
    acc_ref[...] += _dot(act, wd)

    @pl.when(j == pl.num_programs(0) - 1)
    def _():
        y = x + 0.5 * acc_ref[...]
        o_ref[...] = _rms(y, gf_ref[...]) if final else y


def _sample_ffn(x, g, wg, wu, wd, g_final, *, final):
    m = x.shape[0]
    res = pl.pallas_call(
        functools.partial(_sample_ffn_body, final=final),
        grid=(D_FF // FF_TILE,),
        in_specs=[_const_spec((m, D_MODEL)), _const_spec((1, D_MODEL)),
                  pl.BlockSpec((None, D_MODEL, FF_TILE), lambda j: (0, 0, j)),
                  pl.BlockSpec((None, D_MODEL, FF_TILE), lambda j: (0, 0, j)),
                  pl.BlockSpec((None, FF_TILE, D_MODEL), lambda j: (0, j, 0)),
                  _const_spec((1, D_MODEL))],
        out_specs=[pl.BlockSpec((m, D_MODEL), lambda j: (0, 0)),
                   pl.BlockSpec((D_MODEL, FF_TILE), lambda j: (0, j)),
                   pl.BlockSpec((D_MODEL, FF_TILE), lambda j: (0, j)),
                   pl.BlockSpec((FF_TILE, D_MODEL), lambda j: (j, 0))],
        out_shape=[jax.ShapeDtypeStruct((m, D_MODEL), F32),
                   jax.ShapeDtypeStruct((D_MODEL, D_FF), BF16), jax.ShapeDtypeStruct((D_MODEL, D_FF), BF16),
                   jax.ShapeDtypeStruct((D_FF, D_MODEL), BF16)],
        scratch_shapes=[pltpu.VMEM((m, D_MODEL), F32)],
        compiler_params=_params("arbitrary"),
        name="sample_ffn_final" if final else "sample_ffn",
    )(x, g, wg, wu, wd, g_final)
    return res[0], res[1:]


def _one_query_attend(q, k_rows, v_rows, k_new=None, v_new=None):
    s = jnp.sum(k_rows * q, axis=-1, keepdims=True)
    m = jnp.max(s, axis=0, keepdims=True)
    if k_new is not None:
        s_new = jnp.sum(k_new * q, axis=-1, keepdims=True)
        m = jnp.maximum(m, s_new)
    p = jnp.exp(s - m)
    l = jnp.sum(p, axis=0, keepdims=True)
    acc = jnp.sum(p * v_rows, axis=0, keepdims=True)
    if k_new is not None:
        p_new = jnp.exp(s_new - m)
        l = l + p_new
        acc = acc + p_new * v_new
    return acc / l, m + jnp.log(l)


SAMPLE_ATT_SEQS = 4


def _load_seq_head(ref, s, h):
    seqs, rows = ref.shape[0], ref.shape[1]
    return ref.reshape(seqs * rows * HEADS, HEAD_DIM)[pl.ds(s * rows * HEADS + h, rows, stride=HEADS), :]


def _sample_att_body(q_ref, kn_ref, vn_ref, mq_ref, k1_ref, v1_ref, k4_ref, v4_ref, k16_ref, v16_ref,
                     mk_ref, mv_ref, att_ref, mem_ref):
    k_refs = (k1_ref, k4_ref, k16_ref)
    v_refs = (v1_ref, v4_ref, v16_ref)
    for s in range(SAMPLE_ATT_SEQS):
        for h in range(HEADS):
            sl = slice(h * HEAD_DIM, (h + 1) * HEAD_DIM)
            outs, lses = [], []
            for g in range(N_GROUPS):
                gs = slice(g * WIDTH + h * HEAD_DIM, g * WIDTH + (h + 1) * HEAD_DIM)
                o, lse = _one_query_attend(q_ref[s, :, gs] * QK_SCALE, _load_seq_head(k_refs[g], s, h),
                                           _load_seq_head(v_refs[g], s, h), kn_ref[s, :, gs], vn_ref[s, :, gs])
                outs.append(o)
                lses.append(lse)
            m = jnp.maximum(jnp.maximum(lses[0], lses[1]), lses[2])
            e = [jnp.exp(x - m) for x in lses]
            inv = 1.0 / (e[0] + e[1] + e[2])
            att_ref[s, :, sl] = sum((e[g] * inv) * outs[g] for g in range(N_GROUPS))
            o, _ = _one_query_attend(mq_ref[s, :, sl] * QK_SCALE, _load_seq_head(mk_ref, s, h),
                                     _load_seq_head(mv_ref, s, h))
            mem_ref[s, :, sl] = o


def _sample_attend(z3, caches_k, caches_v, cache_mk, cache_mv):
    bd = z3.shape[0]
    seqs = SAMPLE_ATT_SEQS
    zspec = lambda col, n: pl.BlockSpec((seqs, 1, n * WIDTH), lambda b: (b, 0, col // n))
    cache_specs, cache_args = [], []
    for (win, dil), ck, cv in zip(ATT_GROUPS, caches_k, caches_v):
        rows = ck.shape[2]
        assert rows == win and rows // dil == BLOCK
        spec = pl.BlockSpec((None, seqs, BLOCK, None, HEADS, HEAD_DIM), lambda b: (0, b, 0, 0, 0, 0))
        for c in (ck, cv):
            cache_specs.append(spec)
            cache_args.append(c.reshape(1, bd, BLOCK, dil, HEADS, HEAD_DIM))
    mem_spec = pl.BlockSpec((None, seqs, MEM_LEN, HEADS, HEAD_DIM), lambda b: (0, b, 0, 0, 0))
    out_spec = pl.BlockSpec((seqs, 1, WIDTH), lambda b: (b, 0, 0))
    out_shape = jax.ShapeDtypeStruct((bd, 1, WIDTH), F32)
    return pl.pallas_call(
        _sample_att_body,
        grid=(bd // seqs,),
        in_specs=[zspec(COL_Q, 3), zspec(COL_K, 3), zspec(COL_V, 3), zspec(COL_MQ, 1)] + cache_specs
                 + [mem_spec, mem_spec],
        out_specs=[out_spec, out_spec],
        out_shape=[out_shape, out_shape],
        compiler_params=_params("parallel"),
        name="sample_attend",
    )(z3, z3, z3, z3, *cache_args, cache_mk, cache_mv)


SAMPLE_GROUP = 8


def _sample_hgrn_body(zq_ref, zf_ref, zi_ref, lb_ref, gain_ref, s_ref, o_ref, so_ref):
    lb = _lower_bound(lb_ref)
    gain = gain_ref[...]
    q = jax.nn.sigmoid(zq_ref[...])
    kk = (1.0 - lb) * jax.nn.sigmoid(zf_ref[...])
    f = lb + kk
    v = zi_ref[...]
    pad = jnp.zeros((HEAD_DIM - SAMPLE_GROUP, HEAD_DIM), F32)

    def columns(a, h):
        return jnp.concatenate([a[:, h * HEAD_DIM:(h + 1) * HEAD_DIM], pad], axis=0).T

    for h in range(HEADS):
        sl = slice(h * HEAD_DIM, (h + 1) * HEAD_DIM)
        q_t, k_t, f_t = columns(q, h), columns(kk, h), columns(f, h)
        for j in range(SAMPLE_GROUP):
            s1 = f_t[:, j:j + 1] * s_ref[j, h] + k_t[:, j:j + 1] * v[j:j + 1, sl]
            so_ref[j, h] = s1
            o = jnp.sum(q_t[:, j:j + 1] * s1, axis=0, keepdims=True)
            o_ref[j:j + 1, sl] = _rms(o, gain[:, sl])


def _sample_hgrn(z, lb_logits, gain, state):
    bd = z.shape[0]
    zspec = lambda col: pl.BlockSpec((SAMPLE_GROUP, WIDTH), lambda i: (i, col))
    sspec = pl.BlockSpec((None, SAMPLE_GROUP, HEADS, HEAD_DIM, HEAD_DIM), lambda i: (0, i, 0, 0, 0))
    return pl.pallas_call(
        _sample_hgrn_body,
        grid=(bd // SAMPLE_GROUP,),
        in_specs=[zspec(COL_HQ), zspec(COL_HF), zspec(COL_HI), _const_spec(lb_logits.shape),
                  _const_spec((1, WIDTH)), sspec],
        out_specs=[pl.BlockSpec((SAMPLE_GROUP, WIDTH), lambda i: (i, 0)), sspec],
        out_shape=[jax.ShapeDtypeStruct((bd, WIDTH), F32), jax.ShapeDtypeStruct(state.shape, F32)],
        compiler_params=_params("parallel"),
        name="sample_hgrn",
    )(z, z, z, lb_logits, gain, state)


def kernel(x_prompt, x_sample, mem_prompt, cache_win1_k, cache_win1_v, cache_win4_k, cache_win4_v, cache_win16_k, cache_win16_v, cache_mem_k, cache_mem_v, state_hgrn, g_ff1, w_ff1_gate, w_ff1_up, w_ff1_down, g_mix, w_in, hg_lb_logits, g_hg_out, g_mem, w_mem_k, w_mem_v, w_branch_att, w_branch_hg, w_branch_mem, w_out, g_ff2, w_ff2_gate, w_ff2_up, w_ff2_down, g_final):
    b, t, _ = x_prompt.shape
    bd = x_sample.shape[0]
    assert g_ff1.shape[0] == 1 and x_sample.shape[1] == 1 and w_in.shape[2] == IN_WIDTH
    g_fin = g_final.reshape(1, D_MODEL)

    xs, w_ff1_b = _sample_ffn(x_sample.reshape(bd, D_MODEL), g_ff1, w_ff1_gate, w_ff1_up, w_ff1_down, g_fin,
                              final=False)
    zs, w_qkv, w_rest = _sample_in_proj(xs, g_mix, w_in, COL_HQ)
    caches_k = (cache_win1_k, cache_win4_k, cache_win16_k)
    caches_v = (cache_win1_v, cache_win4_v, cache_win16_v)
    z3 = zs.reshape(bd, 1, IN_WIDTH)
    att_s, mem_s = _sample_attend(z3, caches_k, caches_v, cache_mem_k, cache_mem_v)
    hg_s, s_hgrn = _sample_hgrn(zs, hg_lb_logits, g_hg_out, state_hgrn)
    xs, branch_w = _sample_merge(xs, att_s.reshape(bd, WIDTH), hg_s, mem_s.reshape(bd, WIDTH), zs,
                                 w_branch_att, w_branch_hg, w_branch_mem, w_out)
    y_sample, w_ff2_b = _sample_ffn(xs, g_ff2, w_ff2_gate, w_ff2_up, w_ff2_down, g_fin, final=True)
    y_sample = y_sample.reshape(bd, 1, D_MODEL)
    ffn1 = (g_ff1, *w_ff1_b, g_fin)
    ffn2 = (g_ff2, *w_ff2_b, g_fin)

    tm = PROMPT_TILE
    assert b * t // tm == bd
    xp, (s_win16_k,) = _ffn(x_prompt.reshape(b * t, D_MODEL), *ffn1, tm=tm, final=False, ride=(cache_win16_k,))
    qkv = _qkv_proj(xp.reshape(b, t, D_MODEL), g_mix, w_qkv, tm=tm)
    q_g, k_g, v_g, p_k, p_v = qkv[0:3], qkv[3:6], qkv[6:9], qkv[9:12], qkv[12:15]
    (hq, hk, lf, hv, mq), _, outs, lses = _rest_att(xp, g_mix, w_rest, hg_lb_logits, q_g, k_g, v_g, tm=tm)
    hg_p, p_hgrn = _hgrn_prompt(hq, hk, lf, hv, g_hg_out, b, t)
    p_mem_k, p_mem_v, mk_b, mv_b = _mem_kv(mem_prompt, g_mem, w_mem_k, w_mem_v)
    xp, (s_win1_k, s_win1_v, s_win4_k, s_win4_v) = _mix(
        xp, outs, lses, mq, mk_b, mv_b, hg_p, g_mix, w_rest, *branch_w, b, t, tm=tm,
        ride=(cache_win1_k, cache_win1_v, cache_win4_k, cache_win4_v))
    y_prompt, (s_win16_v,) = _ffn(xp, *ffn2, tm=tm, final=True, ride=(cache_win16_v,))
    s_win1_k, s_win1_v, s_win4_k, s_win4_v, s_win16_k, s_win16_v = _write_new_rows(
        z3, [s_win1_k, s_win1_v, s_win4_k, s_win4_v, s_win16_k, s_win16_v],
        ((0, False), (0, True), (1, False), (1, True), (2, False), (2, True)))

    return (y_prompt.reshape(b, t, D_MODEL), y_sample,
            p_k[0], p_v[0], p_k[1], p_v[1], p_k[2], p_v[2],
            p_mem_k, p_mem_v, p_hgrn,
            s_win1_k, s_win1_v, s_win4_k, s_win4_v, s_win16_k, s_win16_v,
            s_hgrn)
```

```python
import functools

import numpy as np
import jax
import jax.numpy as jnp
from jax import lax
from jax.experimental import pallas as pl
from jax.experimental.pallas import tpu as pltpu

F32 = jnp.float32
BF16 = jnp.bfloat16

D_MODEL = 1024
D_FF = 2816
EPS = 1e-6
HEADS = 4
HEAD_DIM = 128
WIDTH = HEADS * HEAD_DIM
ATT_GROUPS = ((128, 1), (512, 4), (2048, 16))
N_GROUPS = 3
BLOCK = 128
MEM_LEN = 256
N_BRANCH = 3
LSE_REP = 32
PROMPT_TILE = 512
QK_SCALE = HEAD_DIM ** -0.5
VMEM_LIMIT_BYTES = 56 * 1024 * 1024

COL_Q, COL_K, COL_V, COL_HQ, COL_HF, COL_HI, COL_MQ, COL_GATE = 0, 3, 6, 9, 10, 11, 12, 13
IN_WIDTH = (COL_GATE + N_BRANCH * D_MODEL // WIDTH) * WIDTH


def _params(*sem):
    return pltpu.CompilerParams(dimension_semantics=sem, vmem_limit_bytes=VMEM_LIMIT_BYTES)


def _const_spec(shape):
    nd = len(shape)
    return pl.BlockSpec(shape, lambda *_: (0,) * nd, pipeline_mode=pl.Buffered(1))


def _rms(x, g):
    return x * lax.rsqrt(jnp.mean(x * x, axis=-1, keepdims=True) + EPS) * g


def _dot(a, b):
    return jnp.dot(a, b, preferred_element_type=F32)


def _dot_nt(a, b):
    return lax.dot_general(a, b, (((1,), (1,)), ((), ())), preferred_element_type=F32)


def _ride_specs(caches, n_steps, to_seq=lambda i: i):
    if not caches:
        return [], [], [], []
    assert all(c.shape[1] == n_steps for c in caches)
    in_specs = [pl.BlockSpec((1, 1, c.shape[2], HEADS, HEAD_DIM), lambda *idx: (0, to_seq(*idx), 0, 0, 0))
                for c in caches]
    out_specs = [pl.BlockSpec(memory_space=pl.ANY)] * len(caches)
    shapes = [jax.ShapeDtypeStruct(c.shape, c.dtype) for c in caches]
    return in_specs, out_specs, shapes, [pltpu.SemaphoreType.DMA((2 * len(caches),))]


def _ride_copies(in_refs, out_refs, sem, seq=None):
    seq = pl.program_id(0) if seq is None else seq
    copies = []
    for j, (cache_ref, out_ref) in enumerate(zip(in_refs, out_refs)):
        rows = cache_ref.shape[2]
        copies.append(pltpu.make_async_copy(cache_ref.at[:, :, pl.ds(1, rows - 1)],
                                            out_ref.at[:, pl.ds(seq, 1), pl.ds(0, rows - 1)], sem.at[2 * j]))
        copies.append(pltpu.make_async_copy(cache_ref.at[:, :, pl.ds(rows - 1, 1)],
                                            out_ref.at[:, pl.ds(seq, 1), pl.ds(rows - 1, 1)], sem.at[2 * j + 1]))
    for cp in copies:
        cp.start()
    return copies


def _new_rows_body(kn_ref, vn_ref, *refs, meta):
    n = len(meta)
    out_refs, row_refs, sem = refs[n:2 * n], refs[2 * n:3 * n], refs[3 * n]
    copies = []
    for j, ((g, is_v), out_ref, row_ref) in enumerate(zip(meta, out_refs, row_refs)):
        bd, rows = out_ref.shape[1], out_ref.shape[2]
        new = (vn_ref if is_v else kn_ref)[:, 0, g * WIDTH:(g + 1) * WIDTH]
        flat = row_ref.reshape(bd * HEADS, HEAD_DIM)
        for h in range(HEADS):
            flat[pl.ds(h, bd, stride=HEADS), :] = new[:, h * HEAD_DIM:(h + 1) * HEAD_DIM]
        copies.append(pltpu.make_async_copy(row_ref, out_ref.at[:, :, pl.ds(rows - 1, 1)], sem.at[j]))
        copies[-1].start()
    for cp in copies:
        cp.wait()


def _write_new_rows(z3, shifted, meta):
    bd = z3.shape[0]
    n = len(shifted)
    zspec = lambda col: pl.BlockSpec((bd, 1, N_GROUPS * WIDTH), lambda i: (0, 0, col // N_GROUPS))
    any_spec = pl.BlockSpec(memory_space=pl.ANY)
    return pl.pallas_call(
        functools.partial(_new_rows_body, meta=meta),
        grid=(1,),
        in_specs=[zspec(COL_K), zspec(COL_V)] + [any_spec] * n,
        out_specs=[any_spec] * n,
        out_shape=[jax.ShapeDtypeStruct(c.shape, c.dtype) for c in shifted],
        input_output_aliases={2 + j: j for j in range(n)},
        scratch_shapes=[pltpu.VMEM((1, bd, 1, HEADS, HEAD_DIM), F32)] * n + [pltpu.SemaphoreType.DMA((n,))],
        name="write_new_rows",
    )(z3, z3, *shifted)


def _ffn_body(x_ref, g_ref, wg_ref, wu_ref, wd_ref, gf_ref, *rest, final, n_ride):
    o_ref = rest[n_ride]
    copies = _ride_copies(rest[:n_ride], rest[n_ride + 1:2 * n_ride + 1], rest[-1]) if n_ride else []
    x = x_ref[...]
    h = _rms(x, g_ref[...]).astype(BF16)
    a = _dot(h, wg_ref[...])
    act = (a * jax.nn.sigmoid(a) * _dot(h, wu_ref[...])).astype(BF16)
    y = x + 0.5 * _dot(act, wd_ref[...])
    if final:
        y = _rms(y, gf_ref[...])
    o_ref[...] = y
    for cp in copies:
        cp.wait()


def _ffn(x, g, wg, wu, wd, g_final, *, tm, final, ride=()):
    m = x.shape[0]
    r_in, r_out, r_shapes, r_scratch = _ride_specs(ride, m // tm)
    res = pl.pallas_call(
        functools.partial(_ffn_body, final=final, n_ride=len(ride)),
        grid=(m // tm,),
        in_specs=[pl.BlockSpec((tm, D_MODEL), lambda i: (i, 0)),
                  _const_spec((1, D_MODEL)),
                  _const_spec((D_MODEL, D_FF)), _const_spec((D_MODEL, D_FF)), _const_spec((D_FF, D_MODEL)),
                  _const_spec((1, D_MODEL))] + r_in,
        out_specs=[pl.BlockSpec((tm, D_MODEL), lambda i: (i, 0))] + r_out,
        out_shape=[jax.ShapeDtypeStruct((m, D_MODEL), F32)] + r_shapes,
        scratch_shapes=r_scratch,
        compiler_params=_params("parallel"),
        name="ffn_final" if final else "ffn",
    )(x, g, wg, wu, wd, g_final, *ride)
    return res[0], res[1:]


def _store_heads(ref, z):
    rows = ref.shape[0]
    flat = ref.reshape(rows * HEADS, HEAD_DIM)
    for h in range(HEADS):
        flat[pl.ds(h, rows, stride=HEADS), :] = z[:, h * HEAD_DIM:(h + 1) * HEAD_DIM]


def _load_head(ref, h):
    rows = ref.shape[0]
    return ref.reshape(rows * HEADS, HEAD_DIM)[pl.ds(h, rows, stride=HEADS), :]


SUBLANES = 8


def _stage_pitch(dil):
    return dil + SUBLANES // 2 if dil % (2 * SUBLANES) == 0 else dil


def _stage_shape(rows, dil, width):
    return (width // 128, rows // dil * _stage_pitch(dil), 128)


def _to_strided_view(dst_ref, z, stage_ref, dil):
    if dil == 1:
        dst_ref[...] = z.astype(dst_ref.dtype)
        return
    n = z.shape[0] // dil
    pitch = _stage_pitch(dil)
    for c in range(stage_ref.shape[0]):
        if pitch == dil:
            stage_ref[c] = z[:, c * 128:(c + 1) * 128]
        else:
            for l in range(n):
                stage_ref[c, l * pitch:l * pitch + dil, :] = z[l * dil:(l + 1) * dil, c * 128:(c + 1) * 128]
    for r in range(dil):
        for c in range(stage_ref.shape[0]):
            dst_ref[:, r * WIDTH + c * 128:r * WIDTH + (c + 1) * 128] = (
                stage_ref[c, pl.ds(r, n, stride=pitch), :].astype(dst_ref.dtype))


def _qkv_body(x_ref, g_ref, w_ref, *refs, tm, n_tiles):
    q_refs, k_refs, v_refs = refs[0:3], refs[3:6], refs[6:9]
    pk_refs, pv_refs, stage_refs = refs[9:12], refs[12:15], refs[15:]
    h = _rms(x_ref[...], g_ref[...]).astype(BF16)
    late = []
    for g, (win, dil) in reversed(list(enumerate(ATT_GROUPS))):
        keep = min(win, tm * n_tiles)
        for kind, (col, dst_refs, win_refs) in enumerate(((COL_Q, q_refs, None), (COL_K, k_refs, pk_refs),
                                                          (COL_V, v_refs, pv_refs))):
            z = _dot(h, w_ref[:, (col + g) * WIDTH:(col + g + 1) * WIDTH])
            if win_refs is None:
                z = z * QK_SCALE
            stage = stage_refs[3 * (g - 1) + kind] if dil > 1 else None
            _to_strided_view(dst_refs[g], z, stage, dil)
            if win_refs is None:
                continue
            if keep == tm * n_tiles:
                _store_heads(win_refs[g], z)
            elif dil == 1:
                late.append((win_refs[g], z[tm - keep:, :], None))
            else:
                assert keep == tm
                late.append((win_refs[g], None, stage))

    @pl.when(pl.program_id(1) == n_tiles - 1)
    def _():
        for ref, rows, stage in late:
            if stage is None:
                _store_heads(ref, rows)
            else:
                _store_heads(ref, jnp.concatenate([stage[hd] for hd in range(HEADS)], axis=-1))


def _qkv_proj(x, g_mix, w_qkv, *, tm):
    b, t, _ = x.shape
    n_tiles = t // tm
    view_specs = [pl.BlockSpec((None, tm // d, d * WIDTH), lambda bi, i: (bi, i, 0)) for _, d in ATT_GROUPS]
    view_shapes = [jax.ShapeDtypeStruct((b, t // d, d * WIDTH), BF16) for _, d in ATT_GROUPS]
    win_specs, win_shapes = [], []
    for win, _ in ATT_GROUPS:
        keep = min(win, t)
        assert keep == t or keep <= tm
        if keep == t:
            win_specs.append(pl.BlockSpec((None, None, tm, HEADS, HEAD_DIM), lambda bi, i: (0, bi, i, 0, 0)))
        else:
            win_specs.append(pl.BlockSpec((None, None, keep, HEADS, HEAD_DIM), lambda bi, i: (0, bi, 0, 0, 0)))
        win_shapes.append(jax.ShapeDtypeStruct((1, b, keep, HEADS, HEAD_DIM), F32))
    return pl.pallas_call(
        functools.partial(_qkv_body, tm=tm, n_tiles=n_tiles),
        grid=(b, n_tiles),
        in_specs=[pl.BlockSpec((None, tm, D_MODEL), lambda bi, i: (bi, i, 0)),
                  _const_spec((1, D_MODEL)),
                  _const_spec((D_MODEL, 3 * N_GROUPS * WIDTH))],
        out_specs=view_specs * 3 + win_specs + win_specs,
        out_shape=view_shapes * 3 + win_shapes + win_shapes,
        scratch_shapes=[pltpu.VMEM(_stage_shape(tm, d, WIDTH), F32) for _, d in ATT_GROUPS if d > 1 for _ in range(3)],
        compiler_params=_params("parallel", "arbitrary"),
        name="qkv_proj",
    )(x, g_mix, w_qkv)


def _lower_bound(logits_ref):
    lg = logits_ref[...]
    e = jnp.exp(lg - jnp.max(lg, axis=0, keepdims=True))
    return e[0:1, :] / jnp.sum(e, axis=0, keepdims=True)


def _rest_steps(x_ref, g_ref, w_ref, wgm_ref, lb_ref, hq_ref, hk_ref, lf_ref, hv_ref, mq_ref, gm_ref):
    h = _rms(x_ref[...], g_ref[...]).astype(BF16)
    lb = _lower_bound(lb_ref)
    proj = lambda c: _dot(h, w_ref[:, c * WIDTH:(c + 1) * WIDTH])

    def hgrn_query():
        hq_ref[...] = jax.nn.sigmoid(proj(0)).astype(BF16)

    def hgrn_key():
        kk = (1.0 - lb) * jax.nn.sigmoid(proj(1))
        hk_ref[...] = kk.astype(BF16)
        lf_ref[...] = jnp.log(lb + kk)

    def hgrn_value():
        hv_ref[...] = proj(2).astype(BF16)

    def mem_query():
        mq_ref[...] = (proj(3) * QK_SCALE).astype(BF16)

    def mem_gate(c):
        cols = slice(c * WIDTH, (c + 1) * WIDTH)
        gm_ref[:, cols] = jax.nn.sigmoid(_dot(h, wgm_ref[:, cols])).astype(BF16)

    return [hgrn_query, hgrn_key, hgrn_value, mem_query] + [functools.partial(mem_gate, c)
                                                            for c in range(D_MODEL // WIDTH)]


def _rest_att_body(x_ref, g_ref, w_ref, wgm_ref, lb_ref, *rest, n_ride, tiles, tiles_per_seq):
    n_att_in = sum(5 if has_halo else 3 for _, _, has_halo in tiles)
    caches, rest = rest[:n_ride], rest[n_ride:]
    att_in, rest = rest[:n_att_in], rest[n_att_in:]
    proj_out, rest = rest[:N_REST_OUT], rest[N_REST_OUT:]
    ride_out, rest = rest[:n_ride], rest[n_ride:]
    att_out, scratch = rest[:2 * len(tiles)], rest[2 * len(tiles):]
    copies = _ride_copies(caches, ride_out, scratch[-1]) if n_ride else []
    steps = _rest_steps(x_ref, g_ref, w_ref, wgm_ref, lb_ref, *proj_out)
    gens = _att_tiles(att_in + att_out, tiles, pl.program_id(0) % tiles_per_seq)
    steps.pop(0)()
    for gen in gens:
        next(gen)
        if steps:
            steps.pop(0)()
        next(gen)
        next(gen, None)
    for step in steps:
        step()
    for cp in copies:
        cp.wait()


N_REST_OUT = 6


def _rest_att(x, g_mix, w_rest, lb_logits, q_g, k_g, v_g, *, tm, ride=()):
    m = x.shape[0]
    spec = pl.BlockSpec((tm, WIDTH), lambda i: (i, 0))
    bf = jax.ShapeDtypeStruct((m, WIDTH), BF16)
    r_in, r_out, r_shapes, r_scratch = _ride_specs(ride, m // tm)
    tiles_per_seq = q_g[0].shape[1] // (ATT_UNITS * BLOCK)
    a_in, a_args, a_out, a_shapes, tiles, n_tiles = _attention_specs(
        q_g, k_g, v_g, lambda i: (i // tiles_per_seq, i % tiles_per_seq))
    assert n_tiles == tiles_per_seq and q_g[0].shape[0] * n_tiles == m // tm
    res = pl.pallas_call(
        functools.partial(_rest_att_body, n_ride=len(ride), tiles=tiles, tiles_per_seq=tiles_per_seq),
        grid=(m // tm,),
        in_specs=[pl.BlockSpec((tm, D_MODEL), lambda i: (i, 0)),
                  _const_spec((1, D_MODEL)),
                  pl.BlockSpec((D_MODEL, (COL_GATE - COL_HQ) * WIDTH), lambda i: (0, 0), pipeline_mode=pl.Buffered(1)),
                  pl.BlockSpec((D_MODEL, D_MODEL), lambda i: (0, w_rest.shape[1] // D_MODEL - 1),
                               pipeline_mode=pl.Buffered(1)),
                  _const_spec(lb_logits.shape)] + r_in + a_in,
        out_specs=[spec] * (N_REST_OUT - 1) + [pl.BlockSpec((tm, D_MODEL), lambda i: (i, 0))] + r_out + a_out,
        out_shape=[bf, bf, jax.ShapeDtypeStruct((m, WIDTH), F32), bf, bf, jax.ShapeDtypeStruct((m, D_MODEL), BF16)]
                  + r_shapes + a_shapes,
        scratch_shapes=r_scratch,
        compiler_params=_params("arbitrary"),
        name="rest_att",
    )(x, g_mix, w_rest, w_rest, lb_logits, *ride, *a_args)
    n = N_REST_OUT + len(ride)
    return res[:N_REST_OUT], res[N_REST_OUT:n], res[n::2], res[n + 1::2]


ATT_UNITS = 4
STAT_REP = LSE_REP // 2


def _att_tiles(refs, tiles, tile_index):
    n_in = [5 if has_halo else 3 for _, _, has_halo in tiles]
    outs = refs[sum(n_in):]
    gens, pos = [], 0
    for g, (n_row_blocks, n_col_blocks, has_halo) in enumerate(tiles):
        gens.append(_att_tile(*refs[pos:pos + n_in[g]], outs[2 * g], outs[2 * g + 1], n_row_blocks=n_row_blocks,
                              n_col_blocks=n_col_blocks, has_halo=has_halo, tile_index=tile_index))
        pos += n_in[g]
    return gens


def _att_tile(q_ref, k_ref, v_ref, *rest, n_row_blocks, n_col_blocks, has_halo, tile_index):
    if has_halo:
        kh_ref, vh_ref, o_ref, st_ref = rest
        halo_ok = tile_index > 0
    else:
        o_ref, st_ref = rest
    qi = lax.broadcasted_iota(jnp.int32, (BLOCK, 2 * BLOCK), 0)
    ki = lax.broadcasted_iota(jnp.int32, (BLOCK, 2 * BLOCK), 1)
    own_ok = (lax.broadcasted_iota(jnp.int32, (BLOCK, BLOCK), 1)
              <= lax.broadcasted_iota(jnp.int32, (BLOCK, BLOCK), 0))
    both_ok = jnp.where(ki < BLOCK, ki, qi + BLOCK) >= jnp.where(ki < BLOCK, qi, ki)
    if has_halo:
        first_valid = jnp.where(halo_ok, 0, BLOCK)
        halo_both_ok = jnp.logical_and(both_ok, ki >= first_valid)

    units = []
    for c in range(n_col_blocks):
        for u in range(n_row_blocks):
            for h in range(HEADS):
                rows = slice(u * BLOCK, (u + 1) * BLOCK)
                cols = slice(c * WIDTH + h * HEAD_DIM, c * WIDTH + (h + 1) * HEAD_DIM)
                lcols = (c * HEADS + h) * LSE_REP
                if u > 0:
                    with_prev = slice((u - 1) * BLOCK, (u + 1) * BLOCK)
                    keys = lambda ref, rr=with_prev, cc=cols: ref[rr, cc]
                    ok = both_ok
                elif has_halo:
                    hcols = slice(h * HEAD_DIM, (h + 1) * HEAD_DIM)
                    keys = lambda ref, rr=rows, cc=cols, hc=hcols: jnp.concatenate(
                        [(kh_ref if ref is k_ref else vh_ref)[:, hc], ref[rr, cc]], axis=0)
                    ok = halo_both_ok
                else:
                    keys = lambda ref, rr=rows, cc=cols: ref[rr, cc]
                    ok = own_ok
                units.append((rows, cols, lcols, keys, ok))

    scores = [jnp.where(ok, _dot_nt(q_ref[rows, cols], keys(k_ref)), -jnp.inf) for rows, cols, _, keys, ok in units]
    yield

    probs = []
    for s in scores:
        m = jnp.max(s, axis=-1, keepdims=True)
        p = jnp.exp(s - m)
        probs.append((p.astype(BF16), m, jnp.sum(p, axis=-1, keepdims=True)))
    yield

    for (rows, cols, lcols, keys, _), (p, m, l) in zip(units, probs):
        o_ref[rows, cols] = _dot(p, keys(v_ref)).astype(BF16)
        st_ref[rows, lcols:lcols + STAT_REP] = jnp.broadcast_to(m, (BLOCK, STAT_REP))
        st_ref[rows, lcols + STAT_REP:lcols + LSE_REP] = jnp.broadcast_to(l, (BLOCK, STAT_REP))


def _attention_specs(q_g, k_g, v_g, to_bj):
    b = q_g[0].shape[0]
    lw = HEADS * LSE_REP
    n_steps = None
    in_specs, args, out_specs, out_shapes, tiles = [], [], [], [], []

    def spec(shape, index):
        return pl.BlockSpec(shape, lambda *grid_idx: index(*to_bj(*grid_idx)))

    for (_, dil), q, k, v in zip(ATT_GROUPS, q_g, k_g, v_g):
        sub_len = q.shape[1]
        n_row_blocks = min(ATT_UNITS, sub_len // BLOCK)
        n_col_blocks = ATT_UNITS // n_row_blocks
        span = n_row_blocks * BLOCK
        assert sub_len % span == 0 and dil % n_col_blocks == 0
        has_halo = sub_len > span
        assert not has_halo or (dil == 1 and n_col_blocks == 1)
        n = (sub_len // span) * (dil // n_col_blocks)
        assert n_steps in (None, n)
        n_steps = n
        if has_halo:
            tile = lambda w, span=span: spec((None, span, w), lambda bi, j: (bi, j, 0))
        else:
            tile = lambda w, span=span, nc=n_col_blocks: spec((None, span, nc * w), lambda bi, j: (bi, 0, j))
        halo = spec((None, BLOCK, WIDTH), lambda bi, j, nr=n_row_blocks: (bi, jnp.maximum(j * nr - 1, 0), 0))
        in_specs += [tile(WIDTH)] * 3 + ([halo, halo] if has_halo else [])
        args += [q, k, v] + ([k, v] if has_halo else [])
        out_specs += [tile(WIDTH), tile(lw)]
        out_shapes += [jax.ShapeDtypeStruct((b, sub_len, dil * WIDTH), BF16),
                       jax.ShapeDtypeStruct((b, sub_len, dil * lw), F32)]
        tiles.append((n_row_blocks, n_col_blocks, has_halo))
    return in_specs, args, out_specs, out_shapes, tuple(tiles), n_steps


N_LEVELS = 7
ROW_CUM, ROW_REM = N_LEVELS, N_LEVELS + 1


def _hgrn_constants():
    r = np.arange(BLOCK)[:, None]
    s = np.arange(BLOCK)[None, :]
    mats = []
    for lvl in range(N_LEVELS):
        half = 1 << lvl
        mid = (r // (2 * half)) * 2 * half + half
        upper = (r % (2 * half)) >= half
        mats.append(np.where(upper, (s >= mid) & (s <= r), (s > r) & (s <= mid - 1)))
    mats.append(s <= r)
    mats.append(s > r)
    range_mat = np.concatenate(mats, axis=0).astype(np.float32)
    range_mat = np.concatenate([range_mat, range_mat], axis=1)
    x = r ^ s
    level = np.where(s < r, np.floor(np.log2(np.maximum(x, 1))).astype(np.int32), np.where(s == r, N_LEVELS, -1))
    return jnp.asarray(range_mat, BF16), jnp.asarray(level, jnp.int32)


HGRN_SEQS = 8


def _hgrn_intra(q_b, k_b, lf, rng, level):
    lf_hi = lf.astype(BF16)
    lf_lo = (lf - lf_hi.astype(F32)).astype(BF16)
    z = _dot(rng, jnp.concatenate([lf_hi, lf_lo], axis=0))
    q = q_b.astype(F32)
    kk = k_b.astype(F32)
    lhs, rhs = [], []
    for lvl in range(N_LEVELS):
        x = jnp.exp(z[lvl * BLOCK:(lvl + 1) * BLOCK]).astype(BF16)
        lhs.append(q_b * x)
        rhs.append(k_b * x)
    lhs.append(q_b)
    rhs.append(k_b)
    b_cum = z[ROW_CUM * BLOCK:(ROW_CUM + 1) * BLOCK]
    q_in = (q * jnp.exp(b_cum)).astype(BF16)
    k_out = (kk * jnp.exp(z[ROW_REM * BLOCK:(ROW_REM + 1) * BLOCK])).astype(BF16)
    decay = jnp.exp(b_cum[BLOCK - 1:BLOCK, :])
    mats = []
    for h in range(HEADS):
        sl = slice(h * HEAD_DIM, (h + 1) * HEAD_DIM)
        a = jnp.zeros((BLOCK, BLOCK), F32)
        for lvl in range(N_LEVELS + 1):
            a = jnp.where(level == lvl, _dot_nt(lhs[lvl][:, sl], rhs[lvl][:, sl]), a)
        mats.append(a.astype(BF16))
    return mats, q_in, k_out, decay


def _hgrn_state(mats, q_in, k_out, decay, v_b, gain, st_ref, j):
    outs = []
    for h in range(HEADS):
        sl = slice(h * HEAD_DIM, (h + 1) * HEAD_DIM)
        v = v_b[:, sl]
        st = st_ref[j, h]
        o = _dot(mats[h], v) + _dot_nt(q_in[:, sl], st.astype(BF16))
        outs.append(_rms(o, gain[:, sl]).astype(BF16))
        v_t = v.astype(F32).T.astype(BF16)
        st_ref[j, h] = st * decay[:, sl] + _dot(v_t, k_out[:, sl])
    return jnp.concatenate(outs, axis=-1)


def _hgrn_body(q_ref, k_ref, lf_ref, v_ref, rng_ref, lvl_ref, gain_ref, o_ref, s_ref, st_ref, *, n_chunks):
    c = pl.program_id(1)

    @pl.when(c == 0)
    def _():
        st_ref[...] = jnp.zeros_like(st_ref)

    rng, level, gain = rng_ref[...], lvl_ref[...], gain_ref[...]
    intra = [_hgrn_intra(q_ref[j], k_ref[j], lf_ref[j], rng, level) for j in range(HGRN_SEQS)]
    for j in range(HGRN_SEQS):
        o_ref[j] = _hgrn_state(*intra[j], v_ref[j], gain, st_ref, j)

    @pl.when(c == n_chunks - 1)
    def _():
        for j in range(HGRN_SEQS):
            for h in range(HEADS):
                s_ref[j, h] = st_ref[j, h].T


def _hgrn_prompt(hq, hk, lf, hv, gain, b, t):
    n_chunks = t // BLOCK
    range_mat, level = _hgrn_constants()
    as3d = lambda a: a.reshape(b, t, WIDTH)
    spec = pl.BlockSpec((HGRN_SEQS, BLOCK, WIDTH), lambda bi, c: (bi, c, 0))
    state_spec = pl.BlockSpec((None, HGRN_SEQS, HEADS, HEAD_DIM, HEAD_DIM), lambda bi, c: (0, bi, 0, 0, 0))
    o, state = pl.pallas_call(
        functools.partial(_hgrn_body, n_chunks=n_chunks),
        grid=(b // HGRN_SEQS, n_chunks),
        in_specs=[spec, spec, spec, spec, _const_spec(range_mat.shape), _const_spec(level.shape),
                  _const_spec((1, WIDTH))],
        out_specs=[spec, state_spec],
        out_shape=[jax.ShapeDtypeStruct((b, t, WIDTH), BF16),
                   jax.ShapeDtypeStruct((1, b, HEADS, HEAD_DIM, HEAD_DIM), F32)],
        scratch_shapes=[pltpu.VMEM((HGRN_SEQS, HEADS, HEAD_DIM, HEAD_DIM), F32)],
        compiler_params=_params("parallel", "arbitrary"),
        name="hgrn_prompt",
    )(as3d(hq), as3d(hk), as3d(lf), as3d(hv), range_mat, level, gain)
    return o.reshape(b * t, WIDTH), state


def _memkv_body(m_ref, g_ref, wk_ref, wv_ref, pk_ref, pv_ref, kb_ref, vb_ref):
    h = _rms(m_ref[...], g_ref[...]).astype(BF16)
    k = _dot(h, wk_ref[...].astype(BF16))
    v = _dot(h, wv_ref[...].astype(BF16))
    _store_heads(pk_ref, k)
    _store_heads(pv_ref, v)
    kb_ref[...] = k.astype(BF16)
    vb_ref[...] = v.astype(BF16)


def _mem_kv(mem, g_mem, w_mk, w_mv):
    b = mem.shape[0]
    head_spec = pl.BlockSpec((None, None, MEM_LEN, HEADS, HEAD_DIM), lambda bi: (0, bi, 0, 0, 0))
    flat_spec = pl.BlockSpec((None, MEM_LEN, WIDTH), lambda bi: (bi, 0, 0))
    head_shape = jax.ShapeDtypeStruct((1, b, MEM_LEN, HEADS, HEAD_DIM), F32)
    flat_shape = jax.ShapeDtypeStruct((b, MEM_LEN, WIDTH), BF16)
    return pl.pallas_call(
        _memkv_body,
        grid=(b,),
        in_specs=[pl.BlockSpec((None, MEM_LEN, D_MODEL), lambda bi: (bi, 0, 0)),
                  _const_spec((1, D_MODEL)),
                  pl.BlockSpec((None, D_MODEL, WIDTH), lambda bi: (0, 0, 0), pipeline_mode=pl.Buffered(1)),
                  pl.BlockSpec((None, D_MODEL, WIDTH), lambda bi: (0, 0, 0), pipeline_mode=pl.Buffered(1))],
        out_specs=[head_spec, head_spec, flat_spec, flat_spec],
        out_shape=[head_shape, head_shape, flat_shape, flat_shape],
        compiler_params=_params("parallel"),
        name="mem_kv",
    )(mem, g_mem, w_mk, w_mv)


def _from_strided_view(src_ref, stage_ref, dil, width):
    if dil == 1:
        return src_ref[...].astype(F32)
    n = src_ref.shape[0]
    n_col = width // 128
    pitch = _stage_pitch(dil)
    for r in range(dil):
        for c in range(n_col):
            stage_ref[c, pl.ds(r, n, stride=pitch), :] = (
                src_ref[:, r * width + c * 128:r * width + (c + 1) * 128].astype(F32))

    def natural(c):
        if pitch == dil:
            return stage_ref[c]
        return jnp.concatenate([stage_ref[c, l * pitch:l * pitch + dil, :] for l in range(n)], axis=0)

    return jnp.concatenate([natural(c) for c in range(n_col)], axis=-1)


def _mix_body(x_ref, o0_ref, o1_ref, o2_ref, s0_ref, s1_ref, s2_ref, mq_ref, mk_ref, mv_ref, hg_ref, g_ref,
              wg0_ref, wg1_ref, gm_ref, wa_ref, wb_ref, wc_ref, wo_ref, *rest, n_ride):
    out_ref = rest[n_ride]
    stage_refs = rest[2 * n_ride + 1:2 * n_ride + 1 + 2 * N_GROUPS]
    copies = []
    if n_ride:
        seq = pl.program_id(0) * pl.num_programs(1) + pl.program_id(1)
        copies = _ride_copies(rest[:n_ride], rest[n_ride + 1:2 * n_ride + 1], rest[-1], seq)
    _mix_tile(x_ref, o0_ref, o1_ref, o2_ref, s0_ref, s1_ref, s2_ref, mq_ref, mk_ref, mv_ref, hg_ref, g_ref,
              wg0_ref, wg1_ref, gm_ref, wa_ref, wb_ref, wc_ref, wo_ref, out_ref, *stage_refs)
    for cp in copies:
        cp.wait()


def _mix_tile(x_ref, o0_ref, o1_ref, o2_ref, s0_ref, s1_ref, s2_ref, mq_ref, mk_ref, mv_ref, hg_ref, g_ref,
              wg0_ref, wg1_ref, gm_ref, wa_ref, wb_ref, wc_ref, wo_ref, out_ref, *stage_refs):
    heads = [slice(h * HEAD_DIM, (h + 1) * HEAD_DIM) for h in range(HEADS)]
    hn = _rms(x_ref[...], g_ref[...]).astype(BF16)
    gate = lambda w_ref: jax.nn.sigmoid(_dot(hn, w_ref[...]))
    gate_att, gate_hg = gate(wg0_ref), gate(wg1_ref)

    accs = [_from_strided_view(r, stage_refs[2 * g], ATT_GROUPS[g][1], WIDTH)
            for g, r in enumerate((o0_ref, o1_ref, o2_ref))]
    stats = [_from_strided_view(r, stage_refs[2 * g + 1], ATT_GROUPS[g][1], HEADS * LSE_REP)
             for g, r in enumerate((s0_ref, s1_ref, s2_ref))]
    att = []
    for h, sl in enumerate(heads):
        ms = [x[:, h * LSE_REP:h * LSE_REP + 1] for x in stats]
        ls = [x[:, h * LSE_REP + STAT_REP:h * LSE_REP + STAT_REP + 1] for x in stats]
        m = jnp.maximum(jnp.maximum(ms[0], ms[1]), ms[2])
        e = [jnp.exp(x - m) for x in ms]
        inv = 1.0 / (e[0] * ls[0] + e[1] * ls[1] + e[2] * ls[2])
        att.append(sum((e[g] * inv) * accs[g][:, sl] for g in range(N_GROUPS)).astype(BF16))
    merged = gate_att * _dot(jnp.concatenate(att, axis=-1), wa_ref[...])

    scores = [_dot_nt(mq_ref[:, sl], mk_ref[:, sl]) for sl in heads]
    gate_mem = gm_ref[...].astype(F32)
    merged = merged + gate_hg * _dot(hg_ref[...], wb_ref[...])
    probs = []
    for s in scores:
        p = jnp.exp(s - jnp.max(s, axis=-1, keepdims=True))
        probs.append((p.astype(BF16), 1.0 / jnp.sum(p, axis=-1, keepdims=True)))
    mem = [(_dot(p, mv_ref[:, sl]) * inv).astype(BF16) for (p, inv), sl in zip(probs, heads)]
    merged = merged + gate_mem * _dot(jnp.concatenate(mem, axis=-1), wc_ref[...])
    out_ref[...] = x_ref[...] + _dot(merged.astype(BF16), wo_ref[...])


def _mix(x, accs, stats, mq, mk, mv, hg, g_mix, w_rest, gate_mem, wa, wb, wc, wo, b, t, *, tm, ride=()):
    n_tiles = t // tm
    first_gate = (COL_GATE - COL_HQ) * WIDTH // D_MODEL
    assert first_gate * D_MODEL == (COL_GATE - COL_HQ) * WIDTH
    gate_w = [pl.BlockSpec((D_MODEL, D_MODEL), lambda bi, i, j=j: (0, first_gate + j), pipeline_mode=pl.Buffered(1))
              for j in range(N_BRANCH - 1)]
    lw = HEADS * LSE_REP
    rows = lambda w: pl.BlockSpec((tm, w), lambda bi, i: (bi * n_tiles + i, 0))
    o_specs = [pl.BlockSpec((None, tm // d, d * WIDTH), lambda bi, i: (bi, i, 0)) for _, d in ATT_GROUPS]
    s_specs = [pl.BlockSpec((None, tm // d, d * lw), lambda bi, i: (bi, i, 0)) for _, d in ATT_GROUPS]
    memspec = pl.BlockSpec((None, MEM_LEN, WIDTH), lambda bi, i: (bi, 0, 0))
    stages = []
    for _, d in ATT_GROUPS:
        stages += [pltpu.VMEM(_stage_shape(tm, d, WIDTH), F32), pltpu.VMEM(_stage_shape(tm, d, lw), F32)]
    r_in, r_out, r_shapes, r_scratch = _ride_specs(ride, b * n_tiles, lambda bi, i: bi * n_tiles + i)
    res = pl.pallas_call(
        functools.partial(_mix_body, n_ride=len(ride)),
        grid=(b, n_tiles),
        in_specs=[rows(D_MODEL)] + o_specs + s_specs + [rows(WIDTH), memspec, memspec, rows(WIDTH),
                                                         _const_spec((1, D_MODEL))] + gate_w + [rows(D_MODEL)]
                 + [_const_spec((WIDTH, D_MODEL)), _const_spec((WIDTH, D_MODEL)), _const_spec((WIDTH, D_MODEL)),
                    _const_spec((D_MODEL, D_MODEL))] + r_in,
        out_specs=[rows(D_MODEL)] + r_out,
        out_shape=[jax.ShapeDtypeStruct((b * t, D_MODEL), F32)] + r_shapes,
        scratch_shapes=stages + r_scratch,
        compiler_params=_params("arbitrary", "arbitrary"),
        name="mix",
    )(x, *accs, *stats, mq, mk, mv, hg, g_mix, w_rest, w_rest, gate_mem, wa, wb, wc, wo, *ride)
    return res[0], res[1:]


def _sample_merge_body(x_ref, att_ref, hg_ref, mem_ref, z_ref, wa_ref, wb_ref, wc_ref, wo_ref,
                       o_ref, wab_ref, wbb_ref, wcb_ref, wob_ref):
    wa, wb, wc, wo = (r[...].astype(BF16) for r in (wa_ref, wb_ref, wc_ref, wo_ref))
    wab_ref[...], wbb_ref[...], wcb_ref[...], wob_ref[...] = wa, wb, wc, wo
    gate = lambda j: jax.nn.sigmoid(z_ref[:, COL_GATE * WIDTH + j * D_MODEL:COL_GATE * WIDTH + (j + 1) * D_MODEL])
    m = (gate(0) * _dot(att_ref[...].astype(BF16), wa) + gate(1) * _dot(hg_ref[...].astype(BF16), wb)
         + gate(2) * _dot(mem_ref[...].astype(BF16), wc))
    o_ref[...] = x_ref[...] + _dot(m.astype(BF16), wo)


def _sample_merge(x, att, hg, mem, z, wa, wb, wc, wo):
    m = x.shape[0]
    full = lambda a: _const_spec(a.shape)
    w_specs = [pl.BlockSpec((None,) + w.shape[1:], lambda i: (0, 0, 0)) for w in (wa, wb, wc, wo)]
    b_specs = [pl.BlockSpec(w.shape[1:], lambda i: (0, 0)) for w in (wa, wb, wc, wo)]
    res = pl.pallas_call(
        _sample_merge_body,
        grid=(1,),
        in_specs=[full(x), full(att), full(hg), full(mem), full(z)] + w_specs,
        out_specs=[pl.BlockSpec((m, D_MODEL), lambda i: (0, 0))] + b_specs,
        out_shape=[jax.ShapeDtypeStruct((m, D_MODEL), F32)]
                  + [jax.ShapeDtypeStruct(w.shape[1:], BF16) for w in (wa, wb, wc, wo)],
        compiler_params=_params("arbitrary"),
        name="sample_merge",
    )(x, att, hg, mem, z, wa, wb, wc, wo)
    return res[0], res[1:]


def _sample_in_proj_body(x_ref, g_ref, w_ref, o_ref, wa_ref, wb_ref, *, n_a):
    w = w_ref[...].astype(BF16)
    o_ref[...] = _dot(_rms(x_ref[...], g_ref[...]).astype(BF16), w)

    @pl.when(pl.program_id(0) < n_a)
    def _():
        wa_ref[...] = w

    @pl.when(pl.program_id(0) >= n_a)
    def _():
        wb_ref[...] = w


def _sample_in_proj(x, g_mix, w_in, n_a):
    m = x.shape[0]
    n = w_in.shape[2] // WIDTH
    return pl.pallas_call(
        functools.partial(_sample_in_proj_body, n_a=n_a),
        grid=(n,),
        in_specs=[_const_spec((m, D_MODEL)), _const_spec((1, D_MODEL)),
                  pl.BlockSpec((None, D_MODEL, WIDTH), lambda j: (0, 0, j))],
        out_specs=[pl.BlockSpec((m, WIDTH), lambda j: (0, j)),
                   pl.BlockSpec((D_MODEL, WIDTH), lambda j: (0, jnp.minimum(j, n_a - 1))),
                   pl.BlockSpec((D_MODEL, WIDTH), lambda j: (0, jnp.maximum(j - n_a, 0)))],
        out_shape=[jax.ShapeDtypeStruct((m, n * WIDTH), F32),
                   jax.ShapeDtypeStruct((D_MODEL, n_a * WIDTH), BF16),
                   jax.ShapeDtypeStruct((D_MODEL, (n - n_a) * WIDTH), BF16)],
        compiler_params=_params("arbitrary"),
        name="sample_in_proj",
    )(x, g_mix, w_in)


FF_TILE = 256


def _sample_ffn_body(x_ref, g_ref, wg_ref, wu_ref, wd_ref, gf_ref, o_ref, wgb_ref, wub_ref, wdb_ref, acc_ref, *,
                     final):
    j = pl.program_id(0)

    @pl.when(j == 0)
    def _():
        acc_ref[...] = jnp.zeros_like(acc_ref)

    wg, wu, wd = wg_ref[...].astype(BF16), wu_ref[...].astype(BF16), wd_ref[...].astype(BF16)
    wgb_ref[...], wub_ref[...], wdb_ref[...] = wg, wu, wd
    x = x_ref[...]
    h = _rms(x, g_ref[...]).astype(BF16)
    a = _dot(h, wg)
    act = (a * jax.nn.sigmoid(a) * _dot(h, wu)).astype(BF16)
    acc_ref[...] += _dot(act, wd)

    @pl.when(j == pl.num_programs(0) - 1)
    def _():
        y = x + 0.5 * acc_ref[...]
        o_ref[...] = _rms(y, gf_ref[...]) if final else y


def _sample_ffn(x, g, wg, wu, wd, g_final, *, final):
    m = x.shape[0]
    res = pl.pallas_call(
        functools.partial(_sample_ffn_body, final=final),
        grid=(D_FF // FF_TILE,),
        in_specs=[_const_spec((m, D_MODEL)), _const_spec((1, D_MODEL)),
                  pl.BlockSpec((None, D_MODEL, FF_TILE), lambda j: (0, 0, j)),
                  pl.BlockSpec((None, D_MODEL, FF_TILE), lambda j: (0, 0, j)),
                  pl.BlockSpec((None, FF_TILE, D_MODEL), lambda j: (0, j, 0)),
                  _const_spec((1, D_MODEL))],
        out_specs=[pl.BlockSpec((m, D_MODEL), lambda j: (0, 0)),
                   pl.BlockSpec((D_MODEL, FF_TILE), lambda j: (0, j)),
                   pl.BlockSpec((D_MODEL, FF_TILE), lambda j: (0, j)),
                   pl.BlockSpec((FF_TILE, D_MODEL), lambda j: (j, 0))],
        out_shape=[jax.ShapeDtypeStruct((m, D_MODEL), F32),
                   jax.ShapeDtypeStruct((D_MODEL, D_FF), BF16), jax.ShapeDtypeStruct((D_MODEL, D_FF), BF16),
                   jax.ShapeDtypeStruct((D_FF, D_MODEL), BF16)],
        scratch_shapes=[pltpu.VMEM((m, D_MODEL), F32)],
        compiler_params=_params("arbitrary"),
        name="sample_ffn_final" if final else "sample_ffn",
    )(x, g, wg, wu, wd, g_final)
    return res[0], res[1:]


def _one_query_attend(q, k_rows, v_rows, k_new=None, v_new=None):
    s = jnp.sum(k_rows * q, axis=-1, keepdims=True)
    m = jnp.max(s, axis=0, keepdims=True)
    if k_new is not None:
        s_new = jnp.sum(k_new * q, axis=-1, keepdims=True)
        m = jnp.maximum(m, s_new)
    p = jnp.exp(s - m)
    l = jnp.sum(p, axis=0, keepdims=True)
    acc = jnp.sum(p * v_rows, axis=0, keepdims=True)
    if k_new is not None:
        p_new = jnp.exp(s_new - m)
        l = l + p_new
        acc = acc + p_new * v_new
    return acc / l, m + jnp.log(l)


SAMPLE_ATT_SEQS = 4


def _load_seq_head(ref, s, h):
    seqs, rows = ref.shape[0], ref.shape[1]
    return ref.reshape(seqs * rows * HEADS, HEAD_DIM)[pl.ds(s * rows * HEADS + h, rows, stride=HEADS), :]


def _sample_att_body(q_ref, kn_ref, vn_ref, mq_ref, k1_ref, v1_ref, k4_ref, v4_ref, k16_ref, v16_ref,
                     mk_ref, mv_ref, att_ref, mem_ref):
    k_refs = (k1_ref, k4_ref, k16_ref)
    v_refs = (v1_ref, v4_ref, v16_ref)
    for s in range(SAMPLE_ATT_SEQS):
        for h in range(HEADS):
            sl = slice(h * HEAD_DIM, (h + 1) * HEAD_DIM)
            outs, lses = [], []
            for g in range(N_GROUPS):
                gs = slice(g * WIDTH + h * HEAD_DIM, g * WIDTH + (h + 1) * HEAD_DIM)
                o, lse = _one_query_attend(q_ref[s, :, gs] * QK_SCALE, _load_seq_head(k_refs[g], s, h),
                                           _load_seq_head(v_refs[g], s, h), kn_ref[s, :, gs], vn_ref[s, :, gs])
                outs.append(o)
                lses.append(lse)
            m = jnp.maximum(jnp.maximum(lses[0], lses[1]), lses[2])
            e = [jnp.exp(x - m) for x in lses]
            inv = 1.0 / (e[0] + e[1] + e[2])
            att_ref[s, :, sl] = sum((e[g] * inv) * outs[g] for g in range(N_GROUPS))
            o, _ = _one_query_attend(mq_ref[s, :, sl] * QK_SCALE, _load_seq_head(mk_ref, s, h),
                                     _load_seq_head(mv_ref, s, h))
            mem_ref[s, :, sl] = o


def _sample_attend(z3, caches_k, caches_v, cache_mk, cache_mv):
    bd = z3.shape[0]
    seqs = SAMPLE_ATT_SEQS
    zspec = lambda col, n: pl.BlockSpec((seqs, 1, n * WIDTH), lambda b: (b, 0, col // n))
    cache_specs, cache_args = [], []
    for (win, dil), ck, cv in zip(ATT_GROUPS, caches_k, caches_v):
        rows = ck.shape[2]
        assert rows == win and rows // dil == BLOCK
        spec = pl.BlockSpec((None, seqs, BLOCK, None, HEADS, HEAD_DIM), lambda b: (0, b, 0, 0, 0, 0))
        for c in (ck, cv):
            cache_specs.append(spec)
            cache_args.append(c.reshape(1, bd, BLOCK, dil, HEADS, HEAD_DIM))
    mem_spec = pl.BlockSpec((None, seqs, MEM_LEN, HEADS, HEAD_DIM), lambda b: (0, b, 0, 0, 0))
    out_spec = pl.BlockSpec((seqs, 1, WIDTH), lambda b: (b, 0, 0))
    out_shape = jax.ShapeDtypeStruct((bd, 1, WIDTH), F32)
    return pl.pallas_call(
        _sample_att_body,
        grid=(bd // seqs,),
        in_specs=[zspec(COL_Q, 3), zspec(COL_K, 3), zspec(COL_V, 3), zspec(COL_MQ, 1)] + cache_specs
                 + [mem_spec, mem_spec],
        out_specs=[out_spec, out_spec],
        out_shape=[out_shape, out_shape],
        compiler_params=_params("parallel"),
        name="sample_attend",
    )(z3, z3, z3, z3, *cache_args, cache_mk, cache_mv)


SAMPLE_GROUP = 8


def _sample_hgrn_body(zq_ref, zf_ref, zi_ref, lb_ref, gain_ref, s_ref, o_ref, so_ref):
    lb = _lower_bound(lb_ref)
    gain = gain_ref[...]
    q = jax.nn.sigmoid(zq_ref[...])
    kk = (1.0 - lb) * jax.nn.sigmoid(zf_ref[...])
    f = lb + kk
    v = zi_ref[...]
    pad = jnp.zeros((HEAD_DIM - SAMPLE_GROUP, HEAD_DIM), F32)

    def columns(a, h):
        return jnp.concatenate([a[:, h * HEAD_DIM:(h + 1) * HEAD_DIM], pad], axis=0).T

    for h in range(HEADS):
        sl = slice(h * HEAD_DIM, (h + 1) * HEAD_DIM)
        q_t, k_t, f_t = columns(q, h), columns(kk, h), columns(f, h)
        for j in range(SAMPLE_GROUP):
            s1 = f_t[:, j:j + 1] * s_ref[j, h] + k_t[:, j:j + 1] * v[j:j + 1, sl]
            so_ref[j, h] = s1
            o = jnp.sum(q_t[:, j:j + 1] * s1, axis=0, keepdims=True)
            o_ref[j:j + 1, sl] = _rms(o, gain[:, sl])


def _sample_hgrn(z, lb_logits, gain, state):
    bd = z.shape[0]
    zspec = lambda col: pl.BlockSpec((SAMPLE_GROUP, WIDTH), lambda i: (i, col))
    sspec = pl.BlockSpec((None, SAMPLE_GROUP, HEADS, HEAD_DIM, HEAD_DIM), lambda i: (0, i, 0, 0, 0))
    return pl.pallas_call(
        _sample_hgrn_body,
        grid=(bd // SAMPLE_GROUP,),
        in_specs=[zspec(COL_HQ), zspec(COL_HF), zspec(COL_HI), _const_spec(lb_logits.shape),
                  _const_spec((1, WIDTH)), sspec],
        out_specs=[pl.BlockSpec((SAMPLE_GROUP, WIDTH), lambda i: (i, 0)), sspec],
        out_shape=[jax.ShapeDtypeStruct((bd, WIDTH), F32), jax.ShapeDtypeStruct(state.shape, F32)],
        compiler_params=_params("parallel"),
        name="sample_hgrn",
    )(z, z, z, lb_logits, gain, state)


def kernel(x_prompt, x_sample, mem_prompt, cache_win1_k, cache_win1_v, cache_win4_k, cache_win4_v, cache_win16_k, cache_win16_v, cache_mem_k, cache_mem_v, state_hgrn, g_ff1, w_ff1_gate, w_ff1_up, w_ff1_down, g_mix, w_in, hg_lb_logits, g_hg_out, g_mem, w_mem_k, w_mem_v, w_branch_att, w_branch_hg, w_branch_mem, w_out, g_ff2, w_ff2_gate, w_ff2_up, w_ff2_down, g_final):
    b, t, _ = x_prompt.shape
    bd = x_sample.shape[0]
    assert g_ff1.shape[0] == 1 and x_sample.shape[1] == 1 and w_in.shape[2] == IN_WIDTH
    g_fin = g_final.reshape(1, D_MODEL)

    xs, w_ff1_b = _sample_ffn(x_sample.reshape(bd, D_MODEL), g_ff1, w_ff1_gate, w_ff1_up, w_ff1_down, g_fin,
                              final=False)
    zs, w_qkv, w_rest = _sample_in_proj(xs, g_mix, w_in, COL_HQ)
    caches_k = (cache_win1_k, cache_win4_k, cache_win16_k)
    caches_v = (cache_win1_v, cache_win4_v, cache_win16_v)
    z3 = zs.reshape(bd, 1, IN_WIDTH)
    att_s, mem_s = _sample_attend(z3, caches_k, caches_v, cache_mem_k, cache_mem_v)
    hg_s, s_hgrn = _sample_hgrn(zs, hg_lb_logits, g_hg_out, state_hgrn)
    xs, branch_w = _sample_merge(xs, att_s.reshape(bd, WIDTH), hg_s, mem_s.reshape(bd, WIDTH), zs,
                                 w_branch_att, w_branch_hg, w_branch_mem, w_out)
    y_sample, w_ff2_b = _sample_ffn(xs, g_ff2, w_ff2_gate, w_ff2_up, w_ff2_down, g_fin, final=True)
    y_sample = y_sample.reshape(bd, 1, D_MODEL)
    ffn1 = (g_ff1, *w_ff1_b, g_fin)
    ffn2 = (g_ff2, *w_ff2_b, g_fin)

    tm = PROMPT_TILE
    assert b * t // tm == bd
    xp, (s_win16_k,) = _ffn(x_prompt.reshape(b * t, D_MODEL), *ffn1, tm=tm, final=False, ride=(cache_win16_k,))
    qkv = _qkv_proj(xp.reshape(b, t, D_MODEL), g_mix, w_qkv, tm=tm)
    q_g, k_g, v_g, p_k, p_v = qkv[0:3], qkv[3:6], qkv[6:9], qkv[9:12], qkv[12:15]
    (hq, hk, lf, hv, mq, gate_mem), _, outs, lses = _rest_att(xp, g_mix, w_rest, hg_lb_logits, q_g, k_g, v_g, tm=tm)
    hg_p, p_hgrn = _hgrn_prompt(hq, hk, lf, hv, g_hg_out, b, t)
    p_mem_k, p_mem_v, mk_b, mv_b = _mem_kv(mem_prompt, g_mem, w_mem_k, w_mem_v)
    xp, (s_win1_k, s_win1_v, s_win4_k, s_win4_v) = _mix(
        xp, outs, lses, mq, mk_b, mv_b, hg_p, g_mix, w_rest, gate_mem, *branch_w, b, t, tm=tm,
        ride=(cache_win1_k, cache_win1_v, cache_win4_k, cache_win4_v))
    y_prompt, (s_win16_v,) = _ffn(xp, *ffn2, tm=tm, final=True, ride=(cache_win16_v,))
    s_win1_k, s_win1_v, s_win4_k, s_win4_v, s_win16_k, s_win16_v = _write_new_rows(
        z3, [s_win1_k, s_win1_v, s_win4_k, s_win4_v, s_win16_k, s_win16_v],
        ((0, False), (0, True), (1, False), (1, True), (2, False), (2, True)))

    return (y_prompt.reshape(b, t, D_MODEL), y_sample,
            p_k[0], p_v[0], p_k[1], p_v[1], p_k[2], p_v[2],
            p_mem_k, p_mem_v, p_hgrn,
            s_win1_k, s_win1_v, s_win4_k, s_win4_v, s_win16_k, s_win16_v,
            s_hgrn)
```
